```python
import jax
import jax.numpy as jnp
from jax import lax
import numpy as np

D_MODEL = 1024
BATCH = 32
SEQ = 256
DEPTH = 4
DEC_BATCH = 2
DEC_SEQ = 1024
PAST_LEN = 256

GRID_W = 64
N_MIXERS = 4
EPS = 1e-6
NEG = -1e30
ROPE_THETA = 10000.0
N_MOD = 6

GLA_HEADS = 4
GLA_DK = D_MODEL // 2 // GLA_HEADS
GLA_DV = D_MODEL // GLA_HEADS
GLA_RANK = 16
GLA_NORMALIZER = 16.0
GLA_CHUNK = 64
GLA_DIRS = 2

SWA_HEADS = 16
SWA_KV_HEADS = 4
SWA_GROUP = SWA_HEADS // SWA_KV_HEADS
SWA_HD = D_MODEL // SWA_HEADS
WINDOW = 128
BLOCK = 128

CONV_W = 3

MLA_HEADS = 8
Q_LORA = 384
KV_LORA = 256
QK_NOPE = 128
QK_ROPE = 64
V_HEAD = 128
MLA_SCALE = (QK_NOPE + QK_ROPE) ** -0.5

D_FF = 2816
N_EXPERTS = 8
TOP_K = 2
D_FF_EXPERT = 3584
N_DENSE = (DEPTH + 1) // 2
N_MOE = DEPTH // 2

kernel_name = 'hybrid_diffusion_trunk_step'


def rms(x):
    xf = x.astype(jnp.float32)
    return (xf * lax.rsqrt(jnp.mean(xf * xf, axis=-1, keepdims=True) + EPS)).astype(x.dtype)


def adaln(cvec, w, b):
    m = jax.nn.silu(cvec) @ w + b
    return jnp.split(m[:, None, :], N_MOD, axis=-1)


def modulate(x, shift, scale):
    return rms(x) * (1 + scale) + shift


def softmax_with_sink(scores, sink):
    m = jnp.maximum(jnp.max(scores, axis=-1, keepdims=True), sink)
    e = jnp.exp(scores - m)
    return e / (jnp.sum(e, axis=-1, keepdims=True) + jnp.exp(sink - m))


def axial_rope_table(L, rot_dim):
    rows = L // GRID_W
    nf = rot_dim // 4
    inv = ROPE_THETA ** (-jnp.arange(nf, dtype=jnp.float32) / nf)
    row = jnp.repeat(jnp.arange(rows, dtype=jnp.float32), GRID_W)
    col = jnp.tile(jnp.arange(GRID_W, dtype=jnp.float32), rows)
    ang = jnp.stack([row, col], axis=-1)[:, :, None] * inv
    return jnp.cos(ang), jnp.sin(ang)


def apply_axial_rope(x, cos, sin):
    nf = x.shape[-1] // 4
    xr = x.astype(jnp.float32).reshape(x.shape[:-1] + (2, 2, nf))
    shp = (1, cos.shape[0]) + (1,) * (x.ndim - 3) + (2, nf)
    cs, sn = cos.reshape(shp), sin.reshape(shp)
    x0, x1 = xr[..., 0, :], xr[..., 1, :]
    out = jnp.stack([x0 * cs - x1 * sn, x1 * cs + x0 * sn], axis=-2)
    return out.reshape(x.shape).astype(x.dtype)


def gla_chunk_scan(q, k, v, gk, h0):
    B, L, H, DK = q.shape
    DV = v.shape[-1]
    C = GLA_CHUNK
    N = L // C

    def chunks(a):
        return a.astype(jnp.float32).reshape(B, N, C, H, a.shape[-1]).transpose(1, 0, 3, 2, 4)

    causal = jnp.tril(jnp.ones((C, C), dtype=bool))

    def step(h, inp):
        qc, kc, vc, gc = inp
        b = jnp.cumsum(gc, axis=2)
        b_last = b[:, :, -1:, :]
        qt = qc * jnp.exp(b)
        kt = kc * jnp.exp(-b)
        att = jnp.where(causal, jnp.einsum('bhid,bhjd->bhij', qt, kt), 0.0)
        o = jnp.einsum('bhcd,bhde->bhce', qt, h) + jnp.einsum('bhij,bhje->bhie', att, vc)
        h_new = jnp.exp(b_last[:, :, 0, :])[..., None] * h + jnp.einsum('bhcd,bhce->bhde', kc * jnp.exp(b_last - b), vc)
        return h_new, o

    h_fin, o = lax.scan(step, h0.astype(jnp.float32), (chunks(q), chunks(k), chunks(v), chunks(gk)))
    return o.transpose(1, 0, 3, 2, 4).reshape(B, L, H, DV), h_fin


def gla_bidirectional(p, h, h0_fwd, h0_bwd):
    wq, wk, wv, wg, wgk1, wgk2, bgk, gnorm, wo = p
    B, L, _ = h.shape
    q = (h @ wq).reshape(B, L, GLA_HEADS, GLA_DK) * (GLA_DK ** -0.5)
    k = (h @ wk).reshape(B, L, GLA_HEADS, GLA_DK)
    v = (h @ wv).reshape(B, L, GLA_HEADS, GLA_DV)

    def decay(r):
        z = ((h @ wgk1[r]) @ wgk2[r] + bgk[r]).astype(jnp.float32)
        return (jax.nn.log_sigmoid(z) / GLA_NORMALIZER).reshape(B, L, GLA_HEADS, GLA_DK)

    def flip(a):
        return jnp.flip(a, axis=1)

    o_f, h_f = gla_chunk_scan(q, k, v, decay(0), h0_fwd)
    o_b, h_b = gla_chunk_scan(flip(q), flip(k), flip(v), flip(decay(1)), h0_bwd)
    o = rms((o_f + flip(o_b)).astype(h.dtype)) * gnorm
    out = (o.reshape(B, L, GLA_HEADS * GLA_DV) * jax.nn.silu(h @ wg)) @ wo
    return out, h_f, h_b


def gla_context(p, h):
    z = jnp.zeros((h.shape[0], GLA_HEADS, GLA_DK, GLA_DV), jnp.float32)
    out, h_f, h_b = gla_bidirectional(p, h, z, z)
    return out, (jnp.stack([h_f, h_b], axis=1).astype(h.dtype),)


def gla_latent(p, h, cache):
    (state,) = cache
    out, _, _ = gla_bidirectional(p, h, state[:, 0], state[:, 1])
    return out


def swa_project(p, h):
    B, L, _ = h.shape
    nq, nk = SWA_HEADS * SWA_HD, SWA_KV_HEADS * SWA_HD
    qkv = h @ p[0]
    q = qkv[..., :nq].reshape(B, L, SWA_KV_HEADS, SWA_GROUP, SWA_HD) * (SWA_HD ** -0.5)
    k = qkv[..., nq:nq + nk].reshape(B, L, SWA_KV_HEADS, SWA_HD)
    v = qkv[..., nq + nk:].reshape(B, L, SWA_KV_HEADS, SWA_HD)
    return q, k, v


def swa_context(p, h):
    _, sink, wo = p
    B, L, _ = h.shape
    q, k, v = swa_project(p, h)
    s = jnp.einsum('bqkgd,bskd->bkgqs', q, k, preferred_element_type=jnp.float32)
    probs = softmax_with_sink(s, sink.reshape(SWA_KV_HEADS, SWA_GROUP)[None, :, :, None, None].astype(jnp.float32))
    o = jnp.einsum('bkgqs,bskd->bqkgd', probs.astype(v.dtype), v)
    return o.reshape(B, L, SWA_HEADS * SWA_HD) @ wo, (k, v)


def swa_latent(p, h, cache):
    _, sink, wo = p
    kc, vc = cache
    B, L, _ = h.shape
    nblk = L // BLOCK
    q, k, v = swa_project(p, h)
    cos, sin = axial_rope_table(L, SWA_HD)
    q = apply_axial_rope(q, cos, sin)
    k = apply_axial_rope(k, cos, sin)
    pad = ((0, 0), (BLOCK, BLOCK), (0, 0), (0, 0))
    kb = jnp.pad(k, pad).reshape(B, nblk + 2, BLOCK, SWA_KV_HEADS, SWA_HD)
    vb = jnp.pad(v, pad).reshape(B, nblk + 2, BLOCK, SWA_KV_HEADS, SWA_HD)
    kw = jnp.concatenate([kb[:, :-2], kb[:, 1:-1], kb[:, 2:]], axis=2)
    vw = jnp.concatenate([vb[:, :-2], vb[:, 1:-1], vb[:, 2:]], axis=2)
    qb = q.reshape(B, nblk, BLOCK, SWA_KV_HEADS, SWA_GROUP, SWA_HD)
    s_loc = jnp.einsum('bnqkgd,bnskd->bnkgqs', qb, kw, preferred_element_type=jnp.float32)
    qpos = jnp.arange(nblk)[:, None, None] * BLOCK + jnp.arange(BLOCK)[None, :, None]
    kpos = jnp.arange(nblk)[:, None, None] * BLOCK - BLOCK + jnp.arange(3 * BLOCK)[None, None, :]
    valid = (jnp.abs(kpos - qpos) <= WINDOW) & (kpos >= 0) & (kpos < L)
    s_loc = jnp.where(valid[None, :, None, None], s_loc, NEG)
    s_ctx = jnp.einsum('bnqkgd,bskd->bnkgqs', qb, kc, preferred_element_type=jnp.float32)
    probs = softmax_with_sink(jnp.concatenate([s_loc, s_ctx], axis=-1),
                              sink.reshape(SWA_KV_HEADS, SWA_GROUP)[None, None, :, :, None, None].astype(jnp.float32))
    probs = probs.astype(v.dtype)
    o = (jnp.einsum('bnkgqs,bnskd->bnqkgd', probs[..., :3 * BLOCK], vw)
         + jnp.einsum('bnkgqs,bskd->bnqkgd', probs[..., 3 * BLOCK:], vc))
    return o.reshape(B, L, SWA_HEADS * SWA_HD) @ wo


def dwconv3(u, w):
    up = jnp.pad(u, ((0, 0), (1, 1), (0, 0)))
    return up[:, :-2] * w[0] + up[:, 1:-1] * w[1] + up[:, 2:] * w[2]


def conv_mix(p, h):
    win, w, wout = p
    b_g, c_g, u = jnp.split(h @ win, 3, axis=-1)
    return (b_g * dwconv3(c_g * u, w)) @ wout


def conv_context(p, h):
    return conv_mix(p, h), ()


def conv_latent(p, h, cache):
    return conv_mix(p, h)


def mla_project(p, h):
    wdq, qnorm, wuq, wdkv, kvnorm = p[:5]
    B, L, _ = h.shape
    q = ((rms(h @ wdq) * qnorm) @ wuq).reshape(B, L, MLA_HEADS, QK_NOPE + QK_ROPE)
    kv_c = h @ wdkv
    ckv = rms(kv_c[..., :KV_LORA]) * kvnorm
    return q[..., :QK_NOPE], q[..., QK_NOPE:], ckv, kv_c[..., KV_LORA:]


def mla_expand(p, ckv):
    B, L, _ = ckv.shape
    kv = (ckv @ p[5]).reshape(B, L, MLA_HEADS, QK_NOPE + V_HEAD)
    return kv[..., :QK_NOPE], kv[..., QK_NOPE:]


def mla_scores(q_nope, q_rope, k_nope, k_rope):
    s = jnp.einsum('bqhd,bkhd->bhqk', q_nope, k_nope, preferred_element_type=jnp.float32)
    s = s + jnp.einsum('bqhr,bkr->bhqk', q_rope, k_rope, preferred_element_type=jnp.float32)
    return s * MLA_SCALE


def mla_context(p, h):
    B, L, _ = h.shape
    q_nope, q_rope, ckv, k_rope = mla_project(p, h)
    k_nope, v = mla_expand(p, ckv)
    probs = jax.nn.softmax(mla_scores(q_nope, q_rope, k_nope, k_rope), axis=-1)
    o = jnp.einsum('bhqk,bkhd->bqhd', probs.astype(v.dtype), v)
    return o.reshape(B, L, MLA_HEADS * V_HEAD) @ p[6], (ckv, k_rope)


def mla_latent(p, h, cache):
    ckv_c, k_rope_c = cache
    B, L, _ = h.shape
    q_nope, q_rope, ckv, k_rope = mla_project(p, h)
    cos, sin = axial_rope_table(L, QK_ROPE)
    q_rope = apply_axial_rope(q_rope, cos, sin)
    k_rope = apply_axial_rope(k_rope, cos, sin)
    k_nope, v = mla_expand(p, ckv)
    k_nope_c, v_c = mla_expand(p, ckv_c)
    s = jnp.concatenate([mla_scores(q_nope, q_rope, k_nope, k_rope),
                         mla_scores(q_nope, q_rope, k_nope_c, k_rope_c)], axis=-1)
    probs = jax.nn.softmax(s, axis=-1).astype(v.dtype)
    o = (jnp.einsum('bhqk,bkhd->bqhd', probs[..., :L], v)
         + jnp.einsum('bhqk,bkhd->bqhd', probs[..., L:], v_c))
    return o.reshape(B, L, MLA_HEADS * V_HEAD) @ p[6]


def swiglu(x, wg, wu, wd):
    return (jax.nn.silu(x @ wg) * (x @ wu)) @ wd


def moe_swiglu(x, router, wg, wu, wd):
    shp = x.shape
    t = x.reshape(-1, shp[-1])
    logits = jnp.einsum('td,de->te', t, router, preferred_element_type=jnp.float32)
    top_val, top_idx = lax.top_k(logits, TOP_K)
    w = jax.nn.softmax(top_val, axis=-1)
    gates = jnp.sum(jax.nn.one_hot(top_idx, N_EXPERTS, dtype=jnp.float32) * w[..., None], axis=1).astype(t.dtype)
    y = jnp.zeros_like(t)
    for e in range(N_EXPERTS):
        y = y + gates[:, e:e + 1] * swiglu(t, wg[e], wu[e], wd[e])
    return y.reshape(shp)


def setup_inputs(seed: int = 0) -> dict:
    key = jax.random.key(seed)
    keys = iter(jax.random.split(key, 64))

    def nrm(shape, scale):
        return jax.random.normal(next(keys), shape, jnp.float32) * scale

    def gain(shape):
        return 1.0 + nrm(shape, 0.05)

    D = D_MODEL
    return {
        'x_prompt': nrm((BATCH, SEQ, D), 1.0),
        'x_sample': nrm((DEC_BATCH, DEC_SEQ, D), 1.0),
        'state_l0_gla': nrm((DEC_BATCH, GLA_DIRS, GLA_HEADS, GLA_DK, GLA_DV), 1.0),
        'cache_l1_k': nrm((DEC_BATCH, PAST_LEN, SWA_KV_HEADS, SWA_HD), 1.0),
        'cache_l1_v': nrm((DEC_BATCH, PAST_LEN, SWA_KV_HEADS, SWA_HD), 1.0),
        'cache_l3_ckv': nrm((DEC_BATCH, PAST_LEN, KV_LORA), 1.0),
        'cache_l3_krope': nrm((DEC_BATCH, PAST_LEN, QK_ROPE), 1.0),
        'c': nrm((DEC_BATCH, D), 1.0),
        'c_ctx': nrm((D,), 1.0),
        'w_mod': nrm((DEPTH, D, N_MOD * D), 0.5 * D ** -0.5),
        'b_mod': nrm((DEPTH, N_MOD * D), 0.01),
        'gla_wq': nrm((D, GLA_HEADS * GLA_DK), D ** -0.5),
        'gla_wk': nrm((D, GLA_HEADS * GLA_DK), D ** -0.5),
        'gla_wv': nrm((D, GLA_HEADS * GLA_DV), D ** -0.5),
        'gla_wg': nrm((D, GLA_HEADS * GLA_DV), D ** -0.5),
        'gla_wgk1': nrm((GLA_DIRS, D, GLA_RANK), D ** -0.5),
        'gla_wgk2': nrm((GLA_DIRS, GLA_RANK, GLA_HEADS * GLA_DK), GLA_RANK ** -0.5),
        'gla_bgk': nrm((GLA_DIRS, GLA_HEADS * GLA_DK), 0.01),
        'gla_norm': gain((GLA_DV,)),
        'gla_wo': nrm((GLA_HEADS * GLA_DV, D), (GLA_HEADS * GLA_DV) ** -0.5),
        'swa_wqkv': nrm((D, (SWA_HEADS + 2 * SWA_KV_HEADS) * SWA_HD), D ** -0.5),
        'swa_sink': nrm((SWA_HEADS,), 0.5),
        'swa_wo': nrm((SWA_HEADS * SWA_HD, D), (SWA_HEADS * SWA_HD) ** -0.5),
        'conv_win': nrm((D, 3 * D), D ** -0.5),
        'conv_w': nrm((CONV_W, D), CONV_W ** -0.5),
        'conv_wout': nrm((D, D), D ** -0.5),
        'mla_wdq': nrm((D, Q_LORA), D ** -0.5),
        'mla_qnorm': gain((Q_LORA,)),
        'mla_wuq': nrm((Q_LORA, MLA_HEADS * (QK_NOPE + QK_ROPE)), Q_LORA ** -0.5),
        'mla_wdkv': nrm((D, KV_LORA + QK_ROPE), D ** -0.5),
        'mla_kvnorm': gain((KV_LORA,)),
        'mla_wukv': nrm((KV_LORA, MLA_HEADS * (QK_NOPE + V_HEAD)), KV_LORA ** -0.5),
        'mla_wo': nrm((MLA_HEADS * V_HEAD, D), (MLA_HEADS * V_HEAD) ** -0.5),
        'dense_w_gate': nrm((N_DENSE, D, D_FF), D ** -0.5),
        'dense_w_up': nrm((N_DENSE, D, D_FF), D ** -0.5),
        'dense_w_down': nrm((N_DENSE, D_FF, D), D_FF ** -0.5),
        'moe_router': nrm((N_MOE, D, N_EXPERTS), D ** -0.5),
        'moe_w_gate': nrm((N_MOE, N_EXPERTS, D, D_FF_EXPERT), D ** -0.5),
        'moe_w_up': nrm((N_MOE, N_EXPERTS, D, D_FF_EXPERT), D ** -0.5),
        'moe_w_down': nrm((N_MOE, N_EXPERTS, D_FF_EXPERT, D), D_FF_EXPERT ** -0.5),
        'final_norm': gain((D,)),
    }


def reference(x_prompt, x_sample, state_l0_gla, cache_l1_k, cache_l1_v, cache_l3_ckv, cache_l3_krope,
              c, c_ctx, w_mod, b_mod,
              gla_wq, gla_wk, gla_wv, gla_wg, gla_wgk1, gla_wgk2, gla_bgk, gla_norm, gla_wo,
              swa_wqkv, swa_sink, swa_wo,
              conv_win, conv_w, conv_wout,
              mla_wdq, mla_qnorm, mla_wuq, mla_wdkv, mla_kvnorm, mla_wukv, mla_wo,
              dense_w_gate, dense_w_up, dense_w_down,
              moe_router, moe_w_gate, moe_w_up, moe_w_down,
              final_norm):
    mixer_params = (
        (gla_wq, gla_wk, gla_wv, gla_wg, gla_wgk1, gla_wgk2, gla_bgk, gla_norm, gla_wo),
        (swa_wqkv, swa_sink, swa_wo),
        (conv_win, conv_w, conv_wout),
        (mla_wdq, mla_qnorm, mla_wuq, mla_wdkv, mla_kvnorm, mla_wukv, mla_wo),
    )
    ctx_fns = (gla_context, swa_context, conv_context, mla_context)
    lat_fns = (gla_latent, swa_latent, conv_latent, mla_latent)
    layer_caches = ((state_l0_gla,), (cache_l1_k, cache_l1_v), (), (cache_l3_ckv, cache_l3_krope))

    def channel_mixer(i, h):
        j = i // 2
        if i % 2 == 0:
            return swiglu(h, dense_w_gate[j], dense_w_up[j], dense_w_down[j])
        return moe_swiglu(h, moe_router[j], moe_w_gate[j], moe_w_up[j], moe_w_down[j])

    xp, xs = x_prompt, x_sample
    new_states = []
    for i in range(DEPTH):
        mix = i % N_MIXERS
        sh1p, sc1p, g1p, sh2p, sc2p, g2p = adaln(c_ctx[None, :], w_mod[i], b_mod[i])
        sh1s, sc1s, g1s, sh2s, sc2s, g2s = adaln(c, w_mod[i], b_mod[i])
        out_p, ctx_state = ctx_fns[mix](mixer_params[mix], modulate(xp, sh1p, sc1p))
        xp = xp + g1p * out_p
        xp = xp + g2p * channel_mixer(i, modulate(xp, sh2p, sc2p))
        new_states.extend(ctx_state)
        out_s = lat_fns[mix](mixer_params[mix], modulate(xs, sh1s, sc1s), layer_caches[i])
        xs = xs + g1s * out_s
        xs = xs + g2s * channel_mixer(i, modulate(xs, sh2s, sc2s))

    y_prompt = rms(xp) * final_norm
    y_sample = rms(xs) * final_norm
    new_state_l0_gla, new_cache_l1_k, new_cache_l1_v, new_cache_l3_ckv, new_cache_l3_krope = new_states
    return (y_prompt, y_sample, new_state_l0_gla, new_cache_l1_k, new_cache_l1_v, new_cache_l3_ckv, new_cache_l3_krope)
```

```python
import functools

import numpy as np
import jax
import jax.numpy as jnp
from jax import lax
from jax.experimental import pallas as pl
from jax.experimental.pallas import tpu as pltpu

BF = jnp.bfloat16
F32 = jnp.float32

D = 1024
EPS = 1e-6
NEG = -1e30
ROPE_THETA = 10000.0
GRID_W = 64

GLA_H, GLA_DK, GLA_DV, GLA_RANK, GLA_CHUNK = 4, 128, 256, 16, 64
GLA_SCALE = GLA_DK ** -0.5
GLA_INV_NORMALIZER = 1.0 / 16.0

SWA_H, SWA_KVH, SWA_HD, SWA_WINDOW, SWA_BLOCK = 16, 4, 64, 128, 128
SWA_SCALE = SWA_HD ** -0.5
SWA_NQ = SWA_H * SWA_HD
SWA_NK = SWA_KVH * SWA_HD

MLA_H, Q_LORA, KV_LORA, QK_NOPE, QK_ROPE, V_HEAD = 8, 384, 256, 128, 64, 128
MLA_SCALE = (QK_NOPE + QK_ROPE) ** -0.5
MLA_QW = 256

N_EXPERTS = 8
MOE_TM = 512
MOE_TF = 512
DENSE_TM = 1024
DENSE_TF = 256

VMEM_LIMIT = 56 * 1024 * 1024


def _cp(*sem):
    return pltpu.CompilerParams(dimension_semantics=sem, vmem_limit_bytes=VMEM_LIMIT)


def _mm(a, b):
    return jnp.dot(a, b, preferred_element_type=F32)


def _mm_nt(a, b):
    return lax.dot_general(a, b, (((1,), (1,)), ((), ())), preferred_element_type=F32)


def _mm_tn(a, b):
    return lax.dot_general(a, b, (((0,), (0,)), ((), ())), preferred_element_type=F32)


def _rms(x):
    return x * lax.rsqrt(jnp.mean(x * x, axis=-1, keepdims=True) + EPS)


def _silu(x):
    return x * (1.0 / (1.0 + jnp.exp(-x)))


def _modulate(x, mod_ref, j):
    shift = mod_ref[0, 3 * j:3 * j + 1, :]
    scale = mod_ref[0, 3 * j + 1:3 * j + 2, :]
    return _rms(x) * (1.0 + scale) + shift


def _gate(mod_ref, j):
    return mod_ref[0, 3 * j + 2:3 * j + 3, :]


def _const_spec(a):
    nd = a.ndim
    return pl.BlockSpec(a.shape, lambda *_: (0,) * nd)


def _mod_spec(tm, rows_per_mod):
    return pl.BlockSpec((1, 6, D), lambda i, *_: ((i * tm) // rows_per_mod, 0, 0))


def _adaln_kernel(c_ref, w_ref, b_ref, o_ref):
    s = _silu(c_ref[...]).astype(BF)
    o_ref[0] = _mm(s, w_ref[0].astype(BF)) + b_ref[0]


def adaln_all(cvec8, w_mod, b_mod):
    nl, d, n = w_mod.shape
    tn = 1536
    return pl.pallas_call(
        _adaln_kernel,
        grid=(nl, n // tn),
        in_specs=[pl.BlockSpec((8, d), lambda l, j: (0, 0)),
                  pl.BlockSpec((1, d, tn), lambda l, j: (l, 0, j)),
                  pl.BlockSpec((1, 1, tn), lambda l, j: (l, 0, j))],
        out_specs=pl.BlockSpec((1, 8, tn), lambda l, j: (l, 0, j)),
        out_shape=jax.ShapeDtypeStruct((nl, 8, n), F32),
        compiler_params=_cp("arbitrary", "arbitrary"),
        name="adaln",
    )(cvec8, w_mod, b_mod.reshape(nl, 1, n))


def _gla_pre_kernel(x_ref, mod_ref, wp_ref, wlr_ref, wgk2_ref, bgk_ref,
                    q_ref, k_ref, v_ref, g_ref, gk_ref):
    h = _modulate(x_ref[...], mod_ref, 0).astype(BF)
    nk = GLA_H * GLA_DK
    nv = GLA_H * GLA_DV
    q_ref[...] = _mm(h, wp_ref[:, 0:nk]) * GLA_SCALE
    k_ref[...] = _mm(h, wp_ref[:, nk:2 * nk])
    v_ref[...] = _mm(h, wp_ref[:, 2 * nk:2 * nk + nv]).astype(BF)
    g_ref[...] = _mm(h, wp_ref[:, 2 * nk + nv:2 * nk + 2 * nv])
    lr = _mm(h, wlr_ref[...]).astype(BF)
    for r in range(2):
        z = _mm(lr, wgk2_ref[r]) + bgk_ref[r:r + 1, :]
        log_sig = jnp.minimum(z, 0.0) - jnp.log1p(jnp.exp(-jnp.abs(z)))
        gk_ref[r] = log_sig * GLA_INV_NORMALIZER


def _gla_scan_kernel(*refs, L, has_h0, emit_state):
    q_ref, k_ref, v_ref, g_ref, gk_ref, gn_ref = refs[:6]
    pos = 6
    h0_ref = None
    if has_h0:
        h0_ref = refs[pos]
        pos += 1
    y_ref = refs[pos]
    pos += 1
    st_ref = None
    if emit_state:
        st_ref = refs[pos]
        pos += 1
    st_s, o_s = refs[pos], refs[pos + 1]

    C = GLA_CHUNK
    n_chunks = L // C
    ii = lax.broadcasted_iota(jnp.int32, (C, C), 0)
    jj = lax.broadcasted_iota(jnp.int32, (C, C), 1)
    for r in range(2):
        keep = (jj <= ii) if r == 0 else (jj >= ii)
        tri = jnp.where(keep, 1.0, 0.0).astype(BF)
        if has_h0:
            st_s[...] = h0_ref[0, r, 0].T
        else:
            st_s[...] = jnp.zeros_like(st_s)

        def body(i, carry, r=r, keep=keep, tri=tri):
            n = i if r == 0 else n_chunks - 1 - i
            rows = pl.ds(pl.multiple_of(n * C, C), C)
            qc = q_ref[rows, :]
            kc = k_ref[rows, :]
            vc = v_ref[rows, :]
            gc = gk_ref[r, rows, :]
            hi = gc.astype(BF)
            rem = gc - hi.astype(F32)
            mid = rem.astype(BF)
            lo = (rem - mid.astype(F32)).astype(BF)
            b = _mm(tri, hi) + _mm(tri, mid) + _mm(tri, lo)
            b_last = b[C - 1:C, :] if r == 0 else b[0:1, :]
            qt = (qc * jnp.exp(b)).astype(BF)
            kt = (kc * jnp.exp(-b)).astype(BF)
            att = jnp.where(keep, _mm_nt(qt, kt), 0.0).astype(BF)
            st = st_s[...]
            o = _mm_nt(qt, st.astype(BF)) + _mm(att, vc)
            kdec = (kc * jnp.exp(b_last - b)).astype(BF)
            st_s[...] = st * jnp.exp(b_last) + _mm_tn(vc, kdec)
            if r == 0:
                o_s[rows, :] = o
            else:
                tot = o_s[rows, :] + o
                y = _rms(tot) * gn_ref[...]
                y_ref[rows, :] = (y * _silu(g_ref[rows, :])).astype(BF)
            return carry

        lax.fori_loop(0, n_chunks, body, 0)
        if emit_state:
            st_ref[0, r, 0] = st_s[...].T


def _residual_out_kernel(y_ref, x_ref, mod_ref, w_ref, o_ref, *, gate_j):
    o_ref[...] = x_ref[...] + _gate(mod_ref, gate_j) * _mm(y_ref[...], w_ref[...])


def gla_mixer(x, mod, w, L, h0):
    M = x.shape[0]
    B = M // L
    tm = 512
    nk, nv = GLA_H * GLA_DK, GLA_H * GLA_DV
    row = lambda n: pl.BlockSpec((tm, n), lambda i: (i, 0))
    q, k, v, g, gk = pl.pallas_call(
        _gla_pre_kernel,
        grid=(M // tm,),
        in_specs=[row(D), _mod_spec(tm, L if mod.shape[0] > 1 else M),
                  _const_spec(w["wp"]), _const_spec(w["wlr"]), _const_spec(w["wgk2"]),
                  _const_spec(w["bgk"])],
        out_specs=[row(nk), row(nk), row(nv), row(nv),
                   pl.BlockSpec((2, tm, nk), lambda i: (0, i, 0))],
        out_shape=[jax.ShapeDtypeStruct((M, nk), F32), jax.ShapeDtypeStruct((M, nk), F32),
                   jax.ShapeDtypeStruct((M, nv), BF), jax.ShapeDtypeStruct((M, nv), F32),
                   jax.ShapeDtypeStruct((2, M, nk), F32)],
        compiler_params=_cp("arbitrary"),
        name="gla_pre",
    )(x, mod, w["wp"], w["wlr"], w["wgk2"], w["bgk"])

    has_h0 = h0 is not None
    emit_state = not has_h0
    in_specs = [pl.BlockSpec((L, GLA_DK), lambda b, h: (b, h)),
                pl.BlockSpec((L, GLA_DK), lambda b, h: (b, h)),
                pl.BlockSpec((L, GLA_DV), lambda b, h: (b, h)),
                pl.BlockSpec((L, GLA_DV), lambda b, h: (b, h)),
                pl.BlockSpec((2, L, GLA_DK), lambda b, h: (0, b, h)),
                pl.BlockSpec((1, GLA_DV), lambda b, h: (0, 0))]
    args = [q, k, v, g, gk, w["gnorm"]]
    st_spec = pl.BlockSpec((1, 2, 1, GLA_DK, GLA_DV), lambda b, h: (b, 0, h, 0, 0))
    if has_h0:
        in_specs.append(st_spec)
        args.append(h0)
    out_specs = [pl.BlockSpec((L, GLA_DV), lambda b, h: (b, h))]
    out_shape = [jax.ShapeDtypeStruct((M, nv), BF)]
    if emit_state:
        out_specs.append(st_spec)
        out_shape.append(jax.ShapeDtypeStruct((B, 2, GLA_H, GLA_DK, GLA_DV), F32))
    res = pl.pallas_call(
        functools.partial(_gla_scan_kernel, L=L, has_h0=has_h0, emit_state=emit_state),
        grid=(B, GLA_H),
        in_specs=in_specs,
        out_specs=out_specs,
        out_shape=out_shape,
        scratch_shapes=[pltpu.VMEM((GLA_DV, GLA_DK), F32), pltpu.VMEM((L, GLA_DV), F32)],
        compiler_params=_cp("arbitrary", "arbitrary"),
        name="gla_scan",
    )(*args)
    y = res[0]
    state = res[1] if emit_state else None

    x_new = pl.pallas_call(
        functools.partial(_residual_out_kernel, gate_j=0),
        grid=(M // tm,),
        in_specs=[row(nv), row(D), _mod_spec(tm, L if mod.shape[0] > 1 else M), _const_spec(w["wo"])],
        out_specs=row(D),
        out_shape=jax.ShapeDtypeStruct((M, D), F32),
        compiler_params=_cp("arbitrary"),
        name="gla_post",
    )(y, x, mod, w["wo"])
    return x_new, state


def _softmax_sink_heads(q_of, k_of, v_of, sink_ref, o_s):
    for hq in range(SWA_H):
        kh = hq // (SWA_H // SWA_KVH)
        s = _mm_nt(q_of(hq), k_of(kh))
        sink = sink_ref[hq]
        m = jnp.maximum(jnp.max(s, axis=-1, keepdims=True), sink)
        e = jnp.exp(s - m)
        p = e / (jnp.sum(e, axis=-1, keepdims=True) + jnp.exp(sink - m))
        o_s[:, hq * SWA_HD:(hq + 1) * SWA_HD] = _mm(p.astype(BF), v_of(kh)).astype(BF)


def _swa_ctx_kernel(sink_ref, x_ref, mod_ref, wqkv_ref, wo_ref, o_ref, k_out, v_out, o_s):
    x = x_ref[...]
    h = _modulate(x, mod_ref, 0).astype(BF)
    qkv = _mm(h, wqkv_ref[...])
    k_out[...] = qkv[:, SWA_NQ:SWA_NQ + SWA_NK]
    v_out[...] = qkv[:, SWA_NQ + SWA_NK:]
    q_of = lambda hq: (qkv[:, hq * SWA_HD:(hq + 1) * SWA_HD] * SWA_SCALE).astype(BF)
    k_of = lambda kh: qkv[:, SWA_NQ + kh * SWA_HD:SWA_NQ + (kh + 1) * SWA_HD].astype(BF)
    v_of = lambda kh: qkv[:, SWA_NQ + SWA_NK + kh * SWA_HD:SWA_NQ + SWA_NK + (kh + 1) * SWA_HD].astype(BF)
    _softmax_sink_heads(q_of, k_of, v_of, sink_ref, o_s)
    o_ref[...] = x + _gate(mod_ref, 0) * _mm(o_s[...], wo_ref[...])


def swa_ctx_mixer(x, mod, w, L):
    M = x.shape[0]
    row = lambda n: pl.BlockSpec((L, n), lambda i: (i, 0))
    return pl.pallas_call(
        _swa_ctx_kernel,
        grid=(M // L,),
        in_specs=[pl.BlockSpec(memory_space=pltpu.SMEM), row(D), _mod_spec(L, M),
                  _const_spec(w["wqkv"]), _const_spec(w["wo"])],
        out_specs=[row(D), row(SWA_NK), row(SWA_NK)],
        out_shape=[jax.ShapeDtypeStruct((M, D), F32), jax.ShapeDtypeStruct((M, SWA_NK), F32),
                   jax.ShapeDtypeStruct((M, SWA_NK), F32)],
        scratch_shapes=[pltpu.VMEM((L, SWA_NQ), BF)],
        compiler_params=_cp("arbitrary"),
        name="swa_ctx",
    )(w["sink"], x, mod, w["wqkv"], w["wo"])


def _swa_lat_kernel(sink_ref, x_ref, mod_ref, wqkv_ref, wsw_ref, cos_ref, sin_ref, kc_ref, vc_ref,
                    wo_ref, o_ref, q_s, k_s, v_s, o_s, *, L):
    n = pl.program_id(1)
    RC = 256

    @pl.when(n == 0)
    def _():
        for c in range(L // RC):
            rows = slice(c * RC, (c + 1) * RC)
            h = _modulate(x_ref[rows, :], mod_ref, 0).astype(BF)
            qkv = _mm(h, wqkv_ref[...])
            sw = _mm(h, wsw_ref[...])
            cos = cos_ref[rows, :]
            sin = sin_ref[rows, :]
            cq = jnp.concatenate([cos] * (SWA_NQ // 128), axis=1)
            sq = jnp.concatenate([sin] * (SWA_NQ // 128), axis=1)
            ck = jnp.concatenate([cos] * (SWA_NK // 128), axis=1)
            sk = jnp.concatenate([sin] * (SWA_NK // 128), axis=1)
            q = (qkv[:, :SWA_NQ] * SWA_SCALE) * cq + (sw[:, :SWA_NQ] * SWA_SCALE) * sq
            q_s[rows, :] = q.astype(BF)
            k = qkv[:, SWA_NQ:SWA_NQ + SWA_NK] * ck + sw[:, SWA_NQ:SWA_NQ + SWA_NK] * sk
            k_s[rows, :] = k.astype(BF)
            v_s[rows, :] = qkv[:, SWA_NQ + SWA_NK:].astype(BF)

    QB = SWA_BLOCK
    KW = 3 * SWA_BLOCK
    r0 = pl.multiple_of(n * QB, QB)
    ws = pl.multiple_of(jnp.clip((n - 1) * QB, 0, L - KW), QB)
    qpos = r0 + lax.broadcasted_iota(jnp.int32, (QB, KW), 0)
    kpos = ws + lax.broadcasted_iota(jnp.int32, (QB, KW), 1)
    valid = jnp.abs(kpos - qpos) <= SWA_WINDOW
    for hq in range(SWA_H):
        kh = hq // (SWA_H // SWA_KVH)
        hs = slice(kh * SWA_HD, (kh + 1) * SWA_HD)
        q = q_s[pl.ds(r0, QB), hq * SWA_HD:(hq + 1) * SWA_HD]
        s1 = jnp.where(valid, _mm_nt(q, k_s[pl.ds(ws, KW), hs]), NEG)
        s2 = _mm_nt(q, kc_ref[0, :, hs].astype(BF))
        sink = sink_ref[hq]
        m = jnp.maximum(jnp.maximum(jnp.max(s1, axis=-1, keepdims=True),
                                    jnp.max(s2, axis=-1, keepdims=True)), sink)
        e1 = jnp.exp(s1 - m)
        e2 = jnp.exp(s2 - m)
        den = (jnp.sum(e1, axis=-1, keepdims=True) + jnp.sum(e2, axis=-1, keepdims=True)
               + jnp.exp(sink - m))
        o = (_mm((e1 / den).astype(BF), v_s[pl.ds(ws, KW), hs])
             + _mm((e2 / den).astype(BF), vc_ref[0, :, hs].astype(BF)))
        o_s[:, hq * SWA_HD:(hq + 1) * SWA_HD] = o.astype(BF)
    o_ref[...] = x_ref[pl.ds(r0, QB), :] + _gate(mod_ref, 0) * _mm(o_s[...], wo_ref[...])


def swa_lat_mixer(x, mod, w, L, kc, vc, cos128, sin128):
    M = x.shape[0]
    B = M // L
    P = kc.shape[1]
    nb = L // SWA_BLOCK
    return pl.pallas_call(
        functools.partial(_swa_lat_kernel, L=L),
        grid=(B, nb),
        in_specs=[pl.BlockSpec(memory_space=pltpu.SMEM),
                  pl.BlockSpec((L, D), lambda b, n: (b, 0)),
                  pl.BlockSpec((1, 6, D), lambda b, n: (b, 0, 0)),
                  _const_spec(w["wqkv"]), _const_spec(w["wsw"]),
                  _const_spec(cos128), _const_spec(sin128),
                  pl.BlockSpec((1, P, SWA_NK), lambda b, n: (b, 0, 0)),
                  pl.BlockSpec((1, P, SWA_NK), lambda b, n: (b, 0, 0)),
                  _const_spec(w["wo"])],
        out_specs=pl.BlockSpec((SWA_BLOCK, D), lambda b, n: (b * nb + n, 0)),
        out_shape=jax.ShapeDtypeStruct((M, D), F32),
        scratch_shapes=[pltpu.VMEM((L, SWA_NQ), BF), pltpu.VMEM((L, SWA_NK), BF),
                        pltpu.VMEM((L, SWA_NK), BF), pltpu.VMEM((SWA_BLOCK, SWA_NQ), BF)],
        compiler_params=_cp("arbitrary", "arbitrary"),
        name="swa_lat",
    )(w["sink"], x, mod, w["wqkv"], w["wsw"], cos128, sin128, kc, vc, w["wo"])


def _conv_kernel(x_ref, mod_ref, win_ref, cw_ref, wout_ref, o_ref, y_s, *, L):
    x = x_ref[...]
    h = _modulate(x, mod_ref, 0).astype(BF)
    CC = 256
    row = lax.broadcasted_iota(jnp.int32, (L, CC), 0)
    for c in range(D // CC):
        cols = slice(c * CC, (c + 1) * CC)
        bg = _mm(h, win_ref[:, c * CC:(c + 1) * CC])
        cg = _mm(h, win_ref[:, D + c * CC:D + (c + 1) * CC])
        u = _mm(h, win_ref[:, 2 * D + c * CC:2 * D + (c + 1) * CC])
        cu = cg * u
        prev = jnp.where(row == 0, 0.0, pltpu.roll(cu, 1, 0))
        nxt = jnp.where(row == L - 1, 0.0, pltpu.roll(cu, L - 1, 0))
        conv = prev * cw_ref[0:1, cols] + cu * cw_ref[1:2, cols] + nxt * cw_ref[2:3, cols]
        y_s[:, cols] = (bg * conv).astype(BF)
    o_ref[...] = x + _gate(mod_ref, 0) * _mm(y_s[...], wout_ref[...])


def conv_mixer(x, mod, w, L):
    M = x.shape[0]
    row = pl.BlockSpec((L, D), lambda i: (i, 0))
    return pl.pallas_call(
        functools.partial(_conv_kernel, L=L),
        grid=(M // L,),
        in_specs=[row, _mod_spec(L, L if mod.shape[0] > 1 else M), _const_spec(w["win"]),
                  _const_spec(w["cw"]), _const_spec(w["wout"])],
        out_specs=row,
        out_shape=jax.ShapeDtypeStruct((M, D), F32),
        scratch_shapes=[pltpu.VMEM((L, D), BF)],
        compiler_params=_cp("arbitrary"),
        name="conv_mix",
    )(x, mod, w["win"], w["cw"], w["wout"])


def _mla_ctx_kernel(x_ref, mod_ref, wdq_ref, qn_ref, wuq_ref, wdkv_ref, kvn_ref, wukv_ref, wo_ref,
                    o_ref, ckv_out, kr_out, o_s):
    x = x_ref[...]
    h = _modulate(x, mod_ref, 0).astype(BF)
    cq = (_rms(_mm(h, wdq_ref[...])) * qn_ref[...]).astype(BF)
    q = _mm(cq, wuq_ref[...]).astype(BF)
    kvc = _mm(h, wdkv_ref[...])
    ckv = _rms(kvc[:, :KV_LORA]) * kvn_ref[...]
    kr = kvc[:, KV_LORA:]
    ckv_out[...] = ckv
    kr_out[...] = kr
    kv = _mm(ckv.astype(BF), wukv_ref[...]).astype(BF)
    krb = kr.astype(BF)
    nn = MLA_H * QK_NOPE
    for hh in range(MLA_H):
        qh = q[:, hh * MLA_QW:(hh + 1) * MLA_QW]
        kh = jnp.concatenate([kv[:, hh * QK_NOPE:(hh + 1) * QK_NOPE], krb], axis=1)
        s = _mm_nt(qh, kh) * MLA_SCALE
        m = jnp.max(s, axis=-1, keepdims=True)
        e = jnp.exp(s - m)
        p = e / jnp.sum(e, axis=-1, keepdims=True)
        vh = kv[:, nn + hh * V_HEAD:nn + (hh + 1) * V_HEAD]
        o_s[:, hh * V_HEAD:(hh + 1) * V_HEAD] = _mm(p.astype(BF), vh).astype(BF)
    o_ref[...] = x + _gate(mod_ref, 0) * _mm(o_s[...], wo_ref[...])


def mla_ctx_mixer(x, mod, w, L):
    M = x.shape[0]
    row = lambda n: pl.BlockSpec((L, n), lambda i: (i, 0))
    ws = [w["wdq"], w["qnorm"], w["wuq"], w["wdkv"], w["kvnorm"], w["wukv"], w["wo"]]
    return pl.pallas_call(
        _mla_ctx_kernel,
        grid=(M // L,),
        in_specs=[row(D), _mod_spec(L, M)] + [_const_spec(a) for a in ws],
        out_specs=[row(D), row(KV_LORA), row(128)],
        out_shape=[jax.ShapeDtypeStruct((M, D), F32), jax.ShapeDtypeStruct((M, KV_LORA), F32),
                   jax.ShapeDtypeStruct((M, 128), F32)],
        scratch_shapes=[pltpu.VMEM((L, MLA_H * V_HEAD), BF)],
        compiler_params=_cp("arbitrary"),
        name="mla_ctx",
    )(x, mod, *ws)


def _mla_lat_kernel(x_ref, mod_ref, wdq_ref, qn_ref, wuq_ref, wuqsw_ref, wdkv_ref, wdkvsw_ref,
                    kvn_ref, wukv_ref, cq_ref, sq_ref, ck_ref, sk_ref, ckvc_ref, krc_ref, wo_ref,
                    o_ref, q_s, kn_s, v_s, kr_s, knc_s, vc_s, o_s, *, L, QB):
    n = pl.program_id(1)
    RC = 256
    nn = MLA_H * QK_NOPE

    @pl.when(n == 0)
    def _():
        for c in range(L // RC):
            rows = slice(c * RC, (c + 1) * RC)
            h = _modulate(x_ref[rows, :], mod_ref, 0).astype(BF)
            cq = (_rms(_mm(h, wdq_ref[...])) * qn_ref[...]).astype(BF)
            cosq = jnp.concatenate([cq_ref[rows, :]] * MLA_H, axis=1)
            sinq = jnp.concatenate([sq_ref[rows, :]] * MLA_H, axis=1)
            q = _mm(cq, wuq_ref[...]) * cosq + _mm(cq, wuqsw_ref[...]) * sinq
            q_s[rows, :] = q.astype(BF)
            kvc = _mm(h, wdkv_ref[...])
            ksw = _mm(h, wdkvsw_ref[...])
            ckv = _rms(kvc[:, :KV_LORA]) * kvn_ref[...]
            kr_s[rows, :] = (kvc[:, KV_LORA:] * ck_ref[rows, :] + ksw * sk_ref[rows, :]).astype(BF)
            kv = _mm(ckv.astype(BF), wukv_ref[...])
            kn_s[rows, :] = kv[:, :nn].astype(BF)
            v_s[rows, :] = kv[:, nn:].astype(BF)
        kvp = _mm(ckvc_ref[0].astype(BF), wukv_ref[...])
        knc_s[...] = kvp[:, :nn].astype(BF)
        vc_s[...] = kvp[:, nn:].astype(BF)

    r0 = pl.multiple_of(n * QB, QB)
    krc = krc_ref[0].astype(BF)
    for hh in range(MLA_H):
        ns = slice(hh * QK_NOPE, (hh + 1) * QK_NOPE)
        qh = q_s[pl.ds(r0, QB), hh * MLA_QW:(hh + 1) * MLA_QW]
        k1 = jnp.concatenate([kn_s[:, ns], kr_s[...]], axis=1)
        k2 = jnp.concatenate([knc_s[:, ns], krc], axis=1)
        s1 = _mm_nt(qh, k1) * MLA_SCALE
        s2 = _mm_nt(qh, k2) * MLA_SCALE
        m = jnp.maximum(jnp.max(s1, axis=-1, keepdims=True), jnp.max(s2, axis=-1, keepdims=True))
        e1 = jnp.exp(s1 - m)
        e2 = jnp.exp(s2 - m)
        den = jnp.sum(e1, axis=-1, keepdims=True) + jnp.sum(e2, axis=-1, keepdims=True)
        vs = slice(hh * V_HEAD, (hh + 1) * V_HEAD)
        o = _mm((e1 / den).astype(BF), v_s[:, vs]) + _mm((e2 / den).astype(BF), vc_s[:, vs])
        o_s[:, vs] = o.astype(BF)
    o_ref[...] = x_ref[pl.ds(r0, QB), :] + _gate(mod_ref, 0) * _mm(o_s[...], wo_ref[...])


def mla_lat_mixer(x, mod, w, L, ckv_c, kr_c128, tabs):
    M = x.shape[0]
    B = M // L
    P = ckv_c.shape[1]
    QB = 256
    nb = L // QB
    ws1 = [w["wdq"], w["qnorm"], w["wuq"], w["wuqsw"], w["wdkv"], w["wdkvsw"], w["kvnorm"], w["wukv"]]
    nv = MLA_H * V_HEAD
    return pl.pallas_call(
        functools.partial(_mla_lat_kernel, L=L, QB=QB),
        grid=(B, nb),
        in_specs=[pl.BlockSpec((L, D), lambda b, n: (b, 0)),
                  pl.BlockSpec((1, 6, D), lambda b, n: (b, 0, 0))]
                 + [_const_spec(a) for a in ws1] + [_const_spec(a) for a in tabs]
                 + [pl.BlockSpec((1, P, KV_LORA), lambda b, n: (b, 0, 0)),
                    pl.BlockSpec((1, P, 128), lambda b, n: (b, 0, 0)),
                    _const_spec(w["wo"])],
        out_specs=pl.BlockSpec((QB, D), lambda b, n: (b * nb + n, 0)),
        out_shape=jax.ShapeDtypeStruct((M, D), F32),
        scratch_shapes=[pltpu.VMEM((L, MLA_H * MLA_QW), BF), pltpu.VMEM((L, nv), BF),
                        pltpu.VMEM((L, nv), BF), pltpu.VMEM((L, 128), BF),
                        pltpu.VMEM((P, nv), BF), pltpu.VMEM((P, nv), BF), pltpu.VMEM((QB, nv), BF)],
        compiler_params=_cp("arbitrary", "arbitrary"),
        name="mla_lat",
    )(x, mod, *ws1, *tabs, ckv_c, kr_c128, w["wo"])


def _swiglu_partial(xb, wg_ref, wu_ref, wd_ref):
    hg = _mm(xb, wg_ref[...].astype(BF))
    hu = _mm(xb, wu_ref[...].astype(BF))
    a = (_silu(hg) * hu).astype(BF)
    return _mm(a, wd_ref[...].astype(BF))


def _dense_ffn_kernel(x_ref, mod_ref, wg_ref, wu_ref, wd_ref, o_ref, xm_s):
    f = pl.program_id(1)

    @pl.when(f == 0)
    def _():
        xm_s[...] = _modulate(x_ref[...], mod_ref, 1).astype(BF)

    part = _swiglu_partial(xm_s[...], wg_ref, wu_ref, wd_ref)

    @pl.when(f == 0)
    def _():
        o_ref[...] = part

    @pl.when(f > 0)
    def _():
        o_ref[...] += part

    @pl.when(f == pl.num_programs(1) - 1)
    def _():
        o_ref[...] = x_ref[...] + _gate(mod_ref, 1) * o_ref[...]


def dense_ffn(x, mod, wg, wu, wd, j, L):
    M = x.shape[0]
    F = wg.shape[-1]
    tm, tf = DENSE_TM, DENSE_TF
    return pl.pallas_call(
        _dense_ffn_kernel,
        grid=(M // tm, F // tf),
        in_specs=[pl.BlockSpec((tm, D), lambda i, f: (i, 0)),
                  pl.BlockSpec((1, 6, D), lambda i, f: ((i * tm) // (L if mod.shape[0] > 1 else M), 0, 0)),
                  pl.BlockSpec((None, D, tf), lambda i, f: (j, 0, f)),
                  pl.BlockSpec((None, D, tf), lambda i, f: (j, 0, f)),
                  pl.BlockSpec((None, tf, D), lambda i, f: (j, f, 0))],
        out_specs=pl.BlockSpec((tm, D), lambda i, f: (i, 0)),
        out_shape=jax.ShapeDtypeStruct((M, D), F32),
        scratch_shapes=[pltpu.VMEM((tm, D), BF)],
        compiler_params=_cp("arbitrary", "arbitrary"),
        name="dense_ffn",
    )(x, mod, wg, wu, wd)


def _router_kernel(x_ref, mod_ref, wr_ref, xm_ref, idx_ref, w_ref):
    xm = _modulate(x_ref[...], mod_ref, 1).astype(BF)
    xm_ref[...] = xm
    lane = lax.broadcasted_iota(jnp.int32, idx_ref.shape, 1)
    logits = jnp.where(lane < N_EXPERTS, _mm(xm, wr_ref[...]), -jnp.inf)
    m1 = jnp.max(logits, axis=-1, keepdims=True)
    i1 = jnp.min(jnp.where(logits == m1, lane, 128), axis=-1, keepdims=True)
    rest = jnp.where(lane == i1, -jnp.inf, logits)
    m2 = jnp.max(rest, axis=-1, keepdims=True)
    i2 = jnp.min(jnp.where(rest == m2, lane, 128), axis=-1, keepdims=True)
    e = jnp.exp(m2 - m1)
    idx_ref[...] = jnp.where(lane == 0, i1, i2)
    w_ref[...] = jnp.where(lane == 0, 1.0 / (1.0 + e), e / (1.0 + e))


def route(x, mod, wr, L):
    M = x.shape[0]
    tm = 512
    row = lambda n: pl.BlockSpec((tm, n), lambda i: (i, 0))
    return pl.pallas_call(
        _router_kernel,
        grid=(M // tm,),
        in_specs=[row(D), _mod_spec(tm, L if mod.shape[0] > 1 else M), _const_spec(wr)],
        out_specs=[row(D), row(128), row(128)],
        out_shape=[jax.ShapeDtypeStruct((M, D), BF), jax.ShapeDtypeStruct((M, 128), jnp.int32),
                   jax.ShapeDtypeStruct((M, 128), F32)],
        compiler_params=_cp("arbitrary"),
        name="moe_route",
    )(x, mod, wr)


def _moe_ffn_kernel(te_ref, nv_ref, x_ref, rw_ref, wg_ref, wu_ref, wd_ref, o_ref):
    i = pl.program_id(0)
    f = pl.program_id(1)
    valid = i < nv_ref[0]

    @pl.when(valid)
    def _():
        part = _swiglu_partial(x_ref[...], wg_ref, wu_ref, wd_ref)

        @pl.when(f == 0)
        def _():
            o_ref[...] = part

        @pl.when(f > 0)
        def _():
            o_ref[...] += part

        @pl.when(f == pl.num_programs(1) - 1)
        def _():
            o_ref[...] = o_ref[...] * rw_ref[...]

    @pl.when(jnp.logical_and(jnp.logical_not(valid), f == 0))
    def _():
        o_ref[...] = jnp.zeros_like(o_ref)


def moe_ffn(xs, row_w, tile_expert, n_valid, wg, wu, wd, j):
    R = xs.shape[0]
    F = wg.shape[-1]
    tm, tf = MOE_TM, MOE_TF
    nf = F // tf

    def fidx(i, f, nv):
        return jnp.where(i < nv[0], f, nf - 1)

    grid_spec = pltpu.PrefetchScalarGridSpec(
        num_scalar_prefetch=2,
        grid=(R // tm, nf),
        in_specs=[pl.BlockSpec((tm, D), lambda i, f, te, nv: (i, 0)),
                  pl.BlockSpec((tm, 1), lambda i, f, te, nv: (i, 0)),
                  pl.BlockSpec((None, None, D, tf), lambda i, f, te, nv: (j, te[i], 0, fidx(i, f, nv))),
                  pl.BlockSpec((None, None, D, tf), lambda i, f, te, nv: (j, te[i], 0, fidx(i, f, nv))),
                  pl.BlockSpec((None, None, tf, D), lambda i, f, te, nv: (j, te[i], fidx(i, f, nv), 0))],
        out_specs=pl.BlockSpec((tm, D), lambda i, f, te, nv: (i, 0)),
    )
    return pl.pallas_call(
        _moe_ffn_kernel,
        grid_spec=grid_spec,
        out_shape=jax.ShapeDtypeStruct((R, D), F32),
        compiler_params=_cp("arbitrary", "arbitrary"),
        name="moe_ffn",
    )(tile_expert, n_valid, xs, row_w, wg, wu, wd)


def _combine_kernel(x_ref, mod_ref, ya_ref, yb_ref, fn_ref, o_ref, *, final):
    o = x_ref[...] + _gate(mod_ref, 1) * (ya_ref[...] + yb_ref[...])
    if final:
        o = _rms(o) * fn_ref[...]
    o_ref[...] = o


def moe_combine(x, mod, ya, yb, fn, L, final):
    M = x.shape[0]
    tm = 512
    row = pl.BlockSpec((tm, D), lambda i: (i, 0))
    return pl.pallas_call(
        functools.partial(_combine_kernel, final=final),
        grid=(M // tm,),
        in_specs=[row, _mod_spec(tm, L if mod.shape[0] > 1 else M), row, row, _const_spec(fn)],
        out_specs=row,
        out_shape=jax.ShapeDtypeStruct((M, D), F32),
        compiler_params=_cp("arbitrary"),
        name="moe_combine",
    )(x, mod, ya, yb, fn)


def moe_layer(xp, xs, modp, mods, wr, wg, wu, wd, j, fn, Lp, Ls, final):
    xm_p, idx_p, w_p = route(xp, modp, wr, Lp)
    xm_s, idx_s, w_s = route(xs, mods, wr, Ls)
    Mp = xp.shape[0]
    xm = jnp.concatenate([xm_p, xm_s], axis=0)
    idx = jnp.concatenate([idx_p[:, :2], idx_s[:, :2]], axis=0)
    wts = jnp.concatenate([w_p[:, :2], w_s[:, :2]], axis=0)
    T = xm.shape[0]
    tm = MOE_TM
    R = 2 * T + N_EXPERTS * tm
    e_flat = idx.reshape(-1)
    onehot = (e_flat[:, None] == jnp.arange(N_EXPERTS, dtype=jnp.int32)[None, :]).astype(jnp.int32)
    csum = jnp.cumsum(onehot, axis=0)
    rank = jnp.take_along_axis(csum, e_flat[:, None], axis=1)[:, 0] - 1
    counts = csum[-1]
    tiles_per = (counts + tm - 1) // tm
    tile_end = jnp.cumsum(tiles_per)
    offs = (tile_end - tiles_per) * tm
    dest = offs[e_flat] + rank
    tok = jnp.arange(2 * T, dtype=jnp.int32) // 2
    src = jnp.zeros((R,), jnp.int32).at[dest].set(tok)
    row_w = jnp.zeros((R,), F32).at[dest].set(wts.reshape(-1)).reshape(R, 1)
    n_tiles = R // tm
    n_valid = tile_end[-1:].astype(jnp.int32)
    t_ids = jnp.arange(n_tiles, dtype=jnp.int32)
    te = jnp.sum((t_ids[:, None] >= tile_end[None, :]).astype(jnp.int32), axis=1)
    te_last = jnp.sum((n_valid - 1 >= tile_end).astype(jnp.int32))
    te = jnp.where(t_ids < n_valid[0], te, te_last).astype(jnp.int32)
    x_sorted = jnp.take(xm, src, axis=0)
    y_sorted = moe_ffn(x_sorted, row_w, te, n_valid, wg, wu, wd, j)
    d2 = dest.reshape(T, 2)
    ya = jnp.take(y_sorted, d2[:, 0], axis=0)
    yb = jnp.take(y_sorted, d2[:, 1], axis=0)
    xp_new = moe_combine(xp, modp, ya[:Mp], yb[:Mp], fn, Lp, final)
    xs_new = moe_combine(xs, mods, ya[Mp:], yb[Mp:], fn, Ls, final)
    return xp_new, xs_new


def _rope_partner_perm(n):
    c = np.arange(n)
    return np.where((c % 32) < 16, c + 16, c - 16)


def _rope_tables(L):
    nf = 16
    inv = ROPE_THETA ** (-np.arange(nf, dtype=np.float32) / nf)
    pos = np.arange(L)
    ang = np.stack([(pos // GRID_W).astype(np.float32), (pos % GRID_W).astype(np.float32)],
                   axis=-1)[:, :, None] * inv
    cos = jnp.cos(jnp.asarray(ang, F32))
    sin = jnp.sin(jnp.asarray(ang, F32))
    c64 = jnp.stack([cos, cos], axis=2).reshape(L, 64)
    s64 = jnp.stack([-sin, sin], axis=2).reshape(L, 64)
    return c64, s64


def kernel(x_prompt, x_sample, state_l0_gla, cache_l1_k, cache_l1_v, cache_l3_ckv, cache_l3_krope, c, c_ctx, w_mod, b_mod, gla_wq, gla_wk, gla_wv, gla_wg, gla_wgk1, gla_wgk2, gla_bgk, gla_norm, gla_wo, swa_wqkv, swa_sink, swa_wo, conv_win, conv_w, conv_wout, mla_wdq, mla_qnorm, mla_wuq, mla_wdkv, mla_kvnorm, mla_wukv, mla_wo, dense_w_gate, dense_w_up, dense_w_down, moe_router, moe_w_gate, moe_w_up, moe_w_down, final_norm):
    Bp, Lp, _ = x_prompt.shape
    Bs, Ls, _ = x_sample.shape
    P = cache_l1_k.shape[1]
    xp = x_prompt.reshape(Bp * Lp, D)
    xs = x_sample.reshape(Bs * Ls, D)

    cvec = jnp.zeros((8, D), F32).at[0].set(c_ctx).at[1:1 + Bs].set(c)
    mod_all = adaln_all(cvec, w_mod, b_mod).reshape(w_mod.shape[0], 8, 6, D)

    wlr = jnp.zeros((D, 128), F32).at[:, :GLA_RANK].set(gla_wgk1[0]).at[:, GLA_RANK:2 * GLA_RANK].set(gla_wgk1[1])
    wgk2 = jnp.zeros((2, 128, GLA_H * GLA_DK), F32)
    wgk2 = wgk2.at[0, :GLA_RANK].set(gla_wgk2[0]).at[1, GLA_RANK:2 * GLA_RANK].set(gla_wgk2[1])
    gla_w = dict(wp=jnp.concatenate([gla_wq, gla_wk, gla_wv, gla_wg], axis=1).astype(BF),
                 wlr=wlr.astype(BF), wgk2=wgk2.astype(BF), bgk=gla_bgk,
                 gnorm=gla_norm.reshape(1, GLA_DV), wo=gla_wo.astype(BF))
    modp, mods = mod_all[0, 0:1], mod_all[0, 1:1 + Bs]
    xp, new_state = gla_mixer(xp, modp, gla_w, Lp, None)
    xs, _ = gla_mixer(xs, mods, gla_w, Ls, state_l0_gla)
    xp = dense_ffn(xp, modp, dense_w_gate, dense_w_up, dense_w_down, 0, Lp)
    xs = dense_ffn(xs, mods, dense_w_gate, dense_w_up, dense_w_down, 0, Ls)

    c64, s64 = _rope_tables(Ls)
    cos128 = jnp.concatenate([c64, c64], axis=1)
    sin128 = jnp.concatenate([s64, s64], axis=1)
    perm = _rope_partner_perm(SWA_NQ + SWA_NK)
    swa_w = dict(wqkv=swa_wqkv.astype(BF), wsw=swa_wqkv[:, perm].astype(BF), wo=swa_wo.astype(BF),
                 sink=swa_sink)
    modp, mods = mod_all[1, 0:1], mod_all[1, 1:1 + Bs]
    xp, k1, v1 = swa_ctx_mixer(xp, modp, swa_w, Lp)
    xs = swa_lat_mixer(xs, mods, swa_w, Ls, cache_l1_k.reshape(Bs, P, SWA_NK),
                       cache_l1_v.reshape(Bs, P, SWA_NK), cos128, sin128)
    wr = jnp.zeros((moe_router.shape[0], D, 128), F32).at[:, :, :N_EXPERTS].set(moe_router).astype(BF)
    xp, xs = moe_layer(xp, xs, modp, mods, wr[0], moe_w_gate, moe_w_up, moe_w_down, 0,
                       final_norm.reshape(1, D), Lp, Ls, False)

    conv_wts = dict(win=conv_win.astype(BF), cw=conv_w, wout=conv_wout.astype(BF))
    modp, mods = mod_all[2, 0:1], mod_all[2, 1:1 + Bs]
    xp = conv_mixer(xp, modp, conv_wts, Lp)
    xs = conv_mixer(xs, mods, conv_wts, Ls)
    xp = dense_ffn(xp, modp, dense_w_gate, dense_w_up, dense_w_down, 1, Lp)
    xs = dense_ffn(xs, mods, dense_w_gate, dense_w_up, dense_w_down, 1, Ls)

    wuq3 = mla_wuq.reshape(Q_LORA, MLA_H, QK_NOPE + QK_ROPE)
    zpad = jnp.zeros((Q_LORA, MLA_H, MLA_QW - QK_NOPE - QK_ROPE), F32)
    wuq = jnp.concatenate([wuq3, zpad], axis=2).reshape(Q_LORA, MLA_H * MLA_QW)
    rperm = _rope_partner_perm(QK_ROPE)
    wuq_sw = jnp.concatenate([jnp.zeros((Q_LORA, MLA_H, QK_NOPE), F32), wuq3[:, :, QK_NOPE:][:, :, rperm], zpad],
                             axis=2).reshape(Q_LORA, MLA_H * MLA_QW)
    kpad = jnp.zeros((D, 128 - QK_ROPE), F32)
    wdkv = jnp.concatenate([mla_wdkv, kpad], axis=1)
    wdkv_sw = jnp.concatenate([mla_wdkv[:, KV_LORA:][:, rperm], kpad], axis=1)
    wukv3 = mla_wukv.reshape(KV_LORA, MLA_H, QK_NOPE + V_HEAD)
    wukv = jnp.concatenate([wukv3[:, :, :QK_NOPE].reshape(KV_LORA, MLA_H * QK_NOPE),
                            wukv3[:, :, QK_NOPE:].reshape(KV_LORA, MLA_H * V_HEAD)], axis=1)
    mla_w = dict(wdq=mla_wdq.astype(BF), qnorm=mla_qnorm.reshape(1, Q_LORA), wuq=wuq.astype(BF),
                 wuqsw=wuq_sw.astype(BF), wdkv=wdkv.astype(BF), wdkvsw=wdkv_sw.astype(BF),
                 kvnorm=mla_kvnorm.reshape(1, KV_LORA), wukv=wukv.astype(BF), wo=mla_wo.astype(BF))
    ones = jnp.ones((Ls, QK_NOPE), F32)
    z128 = jnp.zeros((Ls, QK_NOPE), F32)
    z64 = jnp.zeros((Ls, 64), F32)
    tabs = [jnp.concatenate([ones, c64, z64], axis=1), jnp.concatenate([z128, s64, z64], axis=1),
            jnp.concatenate([c64, z64], axis=1), jnp.concatenate([s64, z64], axis=1)]
    modp, mods = mod_all[3, 0:1], mod_all[3, 1:1 + Bs]
    xp, ckv3, kr3 = mla_ctx_mixer(xp, modp, mla_w, Lp)
    kr_c128 = jnp.concatenate([cache_l3_krope, jnp.zeros((Bs, P, 128 - QK_ROPE), F32)], axis=2)
    xs = mla_lat_mixer(xs, mods, mla_w, Ls, cache_l3_ckv, kr_c128, tabs)
    xp, xs = moe_layer(xp, xs, modp, mods, wr[1], moe_w_gate, moe_w_up, moe_w_down, 1,
                       final_norm.reshape(1, D), Lp, Ls, True)

    return (xp.reshape(Bp, Lp, D), xs.reshape(Bs, Ls, D), new_state,
            k1.reshape(Bp, Lp, SWA_KVH, SWA_HD), v1.reshape(Bp, Lp, SWA_KVH, SWA_HD),
            ckv3.reshape(Bp, Lp, KV_LORA), kr3[:, :QK_ROPE].reshape(Bp, Lp, QK_ROPE))
```

```python
import functools

import numpy as np
import jax
import jax.numpy as jnp
from jax import lax
from jax.experimental import pallas as pl
from jax.experimental.pallas import tpu as pltpu

BF = jnp.bfloat16
F32 = jnp.float32

D = 1024
EPS = 1e-6
NEG = -1e30
ROPE_THETA = 10000.0
GRID_W = 64

GLA_H, GLA_DK, GLA_DV, GLA_RANK, GLA_CHUNK = 4, 128, 256, 16, 64
GLA_SCALE = GLA_DK ** -0.5
GLA_INV_NORMALIZER = 1.0 / 16.0

SWA_H, SWA_KVH, SWA_HD, SWA_WINDOW, SWA_BLOCK = 16, 4, 64, 128, 128
SWA_SCALE = SWA_HD ** -0.5
SWA_NQ = SWA_H * SWA_HD
SWA_NK = SWA_KVH * SWA_HD

MLA_H, Q_LORA, KV_LORA, QK_NOPE, QK_ROPE, V_HEAD = 8, 384, 256, 128, 64, 128
MLA_SCALE = (QK_NOPE + QK_ROPE) ** -0.5
MLA_QW = 256

N_EXPERTS = 8
MOE_TM = 512
MOE_TF = 896
DENSE_TM = 1024
DENSE_TF = 256

VMEM_LIMIT = 56 * 1024 * 1024


def _cp(*sem):
    return pltpu.CompilerParams(dimension_semantics=sem, vmem_limit_bytes=VMEM_LIMIT)


def _mm(a, b):
    return jnp.dot(a, b, preferred_element_type=F32)


def _mm_nt(a, b):
    return lax.dot_general(a, b, (((1,), (1,)), ((), ())), preferred_element_type=F32)


def _mm_tn(a, b):
    return lax.dot_general(a, b, (((0,), (0,)), ((), ())), preferred_element_type=F32)


def _rms(x):
    return x * lax.rsqrt(jnp.mean(x * x, axis=-1, keepdims=True) + EPS)


def _silu(x):
    return x * (1.0 / (1.0 + jnp.exp(-x)))


def _modulate(x, mod_ref, j):
    shift = mod_ref[0, 3 * j:3 * j + 1, :]
    scale = mod_ref[0, 3 * j + 1:3 * j + 2, :]
    return _rms(x) * (1.0 + scale) + shift


def _gate(mod_ref, j):
    return mod_ref[0, 3 * j + 2:3 * j + 3, :]


def _const_spec(a):
    nd = a.ndim
    return pl.BlockSpec(a.shape, lambda *_: (0,) * nd)


def _mod_spec(tm, rows_per_mod):
    return pl.BlockSpec((1, 6, D), lambda i, *_: ((i * tm) // rows_per_mod, 0, 0))


def _adaln_kernel(c_ref, w_ref, b_ref, o_ref):
    s = _silu(c_ref[...]).astype(BF)
    o_ref[0] = _mm(s, w_ref[0].astype(BF)) + b_ref[0]


def adaln_all(cvec8, w_mod, b_mod):
    nl, d, n = w_mod.shape
    tn = 1536
    return pl.pallas_call(
        _adaln_kernel,
        grid=(nl, n // tn),
        in_specs=[pl.BlockSpec((8, d), lambda l, j: (0, 0)),
                  pl.BlockSpec((1, d, tn), lambda l, j: (l, 0, j)),
                  pl.BlockSpec((1, 1, tn), lambda l, j: (l, 0, j))],
        out_specs=pl.BlockSpec((1, 8, tn), lambda l, j: (l, 0, j)),
        out_shape=jax.ShapeDtypeStruct((nl, 8, n), F32),
        compiler_params=_cp("arbitrary", "arbitrary"),
        name="adaln",
    )(cvec8, w_mod, b_mod.reshape(nl, 1, n))


def _gla_pre_kernel(x_ref, mod_ref, wp_ref, wlr_ref, wgk2_ref, bgk_ref,
                    q_ref, k_ref, v_ref, g_ref, gk_ref):
    h = _modulate(x_ref[...], mod_ref, 0).astype(BF)
    nk = GLA_H * GLA_DK
    nv = GLA_H * GLA_DV
    q_ref[...] = _mm(h, wp_ref[:, 0:nk]) * GLA_SCALE
    k_ref[...] = _mm(h, wp_ref[:, nk:2 * nk])
    v_ref[...] = _mm(h, wp_ref[:, 2 * nk:2 * nk + nv]).astype(BF)
    g_ref[...] = _mm(h, wp_ref[:, 2 * nk + nv:2 * nk + 2 * nv])
    lr = _mm(h, wlr_ref[...]).astype(BF)
    for r in range(2):
        z = _mm(lr, wgk2_ref[r]) + bgk_ref[r:r + 1, :]
        log_sig = jnp.minimum(z, 0.0) - jnp.log1p(jnp.exp(-jnp.abs(z)))
        gk_ref[r] = log_sig * GLA_INV_NORMALIZER


def _gla_scan_kernel(*refs, L, has_h0, emit_state):
    q_ref, k_ref, v_ref, g_ref, gk_ref, gn_ref = refs[:6]
    pos = 6
    h0_ref = None
    if has_h0:
        h0_ref = refs[pos]
        pos += 1
    y_ref = refs[pos]
    pos += 1
    st_ref = None
    if emit_state:
        st_ref = refs[pos]
        pos += 1
    st_s, o_s = refs[pos], refs[pos + 1]

    C = GLA_CHUNK
    n_chunks = L // C
    ii = lax.broadcasted_iota(jnp.int32, (C, C), 0)
    jj = lax.broadcasted_iota(jnp.int32, (C, C), 1)
    for r in range(2):
        keep = (jj <= ii) if r == 0 else (jj >= ii)
        tri = jnp.where(keep, 1.0, 0.0).astype(BF)
        if has_h0:
            st_s[...] = h0_ref[0, r, 0].T
        else:
            st_s[...] = jnp.zeros_like(st_s)

        def body(i, carry, r=r, keep=keep, tri=tri):
            n = i if r == 0 else n_chunks - 1 - i
            rows = pl.ds(pl.multiple_of(n * C, C), C)
            qc = q_ref[rows, :]
            kc = k_ref[rows, :]
            vc = v_ref[rows, :]
            gc = gk_ref[r, rows, :]
            hi = gc.astype(BF)
            rem = gc - hi.astype(F32)
            mid = rem.astype(BF)
            lo = (rem - mid.astype(F32)).astype(BF)
            b = _mm(tri, hi) + _mm(tri, mid) + _mm(tri, lo)
            b_last = b[C - 1:C, :] if r == 0 else b[0:1, :]
            qt = (qc * jnp.exp(b)).astype(BF)
            kt = (kc * jnp.exp(-b)).astype(BF)
            att = jnp.where(keep, _mm_nt(qt, kt), 0.0).astype(BF)
            st = st_s[...]
            o = _mm_nt(qt, st.astype(BF)) + _mm(att, vc)
            kdec = (kc * jnp.exp(b_last - b)).astype(BF)
            st_s[...] = st * jnp.exp(b_last) + _mm_tn(vc, kdec)
            if r == 0:
                o_s[rows, :] = o
            else:
                tot = o_s[rows, :] + o
                y = _rms(tot) * gn_ref[...]
                y_ref[rows, :] = (y * _silu(g_ref[rows, :])).astype(BF)
            return carry

        lax.fori_loop(0, n_chunks, body, 0)
        if emit_state:
            st_ref[0, r, 0] = st_s[...].T


def _residual_out_kernel(y_ref, x_ref, mod_ref, w_ref, o_ref, *, gate_j):
    o_ref[...] = x_ref[...] + _gate(mod_ref, gate_j) * _mm(y_ref[...], w_ref[...])


def gla_mixer(x, mod, w, L, h0):
    M = x.shape[0]
    B = M // L
    tm = 512
    nk, nv = GLA_H * GLA_DK, GLA_H * GLA_DV
    row = lambda n: pl.BlockSpec((tm, n), lambda i: (i, 0))
    q, k, v, g, gk = pl.pallas_call(
        _gla_pre_kernel,
        grid=(M // tm,),
        in_specs=[row(D), _mod_spec(tm, L if mod.shape[0] > 1 else M),
                  _const_spec(w["wp"]), _const_spec(w["wlr"]), _const_spec(w["wgk2"]),
                  _const_spec(w["bgk"])],
        out_specs=[row(nk), row(nk), row(nv), row(nv),
                   pl.BlockSpec((2, tm, nk), lambda i: (0, i, 0))],
        out_shape=[jax.ShapeDtypeStruct((M, nk), F32), jax.ShapeDtypeStruct((M, nk), F32),
                   jax.ShapeDtypeStruct((M, nv), BF), jax.ShapeDtypeStruct((M, nv), F32),
                   jax.ShapeDtypeStruct((2, M, nk), F32)],
        compiler_params=_cp("arbitrary"),
        name="gla_pre",
    )(x, mod, w["wp"], w["wlr"], w["wgk2"], w["bgk"])

    has_h0 = h0 is not None
    emit_state = not has_h0
    in_specs = [pl.BlockSpec((L, GLA_DK), lambda b, h: (b, h)),
                pl.BlockSpec((L, GLA_DK), lambda b, h: (b, h)),
                pl.BlockSpec((L, GLA_DV), lambda b, h: (b, h)),
                pl.BlockSpec((L, GLA_DV), lambda b, h: (b, h)),
                pl.BlockSpec((2, L, GLA_DK), lambda b, h: (0, b, h)),
                pl.BlockSpec((1, GLA_DV), lambda b, h: (0, 0))]
    args = [q, k, v, g, gk, w["gnorm"]]
    st_spec = pl.BlockSpec((1, 2, 1, GLA_DK, GLA_DV), lambda b, h: (b, 0, h, 0, 0))
    if has_h0:
        in_specs.append(st_spec)
        args.append(h0)
    out_specs = [pl.BlockSpec((L, GLA_DV), lambda b, h: (b, h))]
    out_shape = [jax.ShapeDtypeStruct((M, nv), BF)]
    if emit_state:
        out_specs.append(st_spec)
        out_shape.append(jax.ShapeDtypeStruct((B, 2, GLA_H, GLA_DK, GLA_DV), F32))
    res = pl.pallas_call(
        functools.partial(_gla_scan_kernel, L=L, has_h0=has_h0, emit_state=emit_state),
        grid=(B, GLA_H),
        in_specs=in_specs,
        out_specs=out_specs,
        out_shape=out_shape,
        scratch_shapes=[pltpu.VMEM((GLA_DV, GLA_DK), F32), pltpu.VMEM((L, GLA_DV), F32)],
        compiler_params=_cp("arbitrary", "arbitrary"),
        name="gla_scan",
    )(*args)
    y = res[0]
    state = res[1] if emit_state else None

    x_new = pl.pallas_call(
        functools.partial(_residual_out_kernel, gate_j=0),
        grid=(M // tm,),
        in_specs=[row(nv), row(D), _mod_spec(tm, L if mod.shape[0] > 1 else M), _const_spec(w["wo"])],
        out_specs=row(D),
        out_shape=jax.ShapeDtypeStruct((M, D), F32),
        compiler_params=_cp("arbitrary"),
        name="gla_post",
    )(y, x, mod, w["wo"])
    return x_new, state


def _softmax_sink_heads(q_of, k_of, v_of, sink_ref, o_s):
    for hq in range(SWA_H):
        kh = hq // (SWA_H // SWA_KVH)
        s = _mm_nt(q_of(hq), k_of(kh))
        sink = sink_ref[hq]
        m = jnp.maximum(jnp.max(s, axis=-1, keepdims=True), sink)
        e = jnp.exp(s - m)
        p = e / (jnp.sum(e, axis=-1, keepdims=True) + jnp.exp(sink - m))
        o_s[:, hq * SWA_HD:(hq + 1) * SWA_HD] = _mm(p.astype(BF), v_of(kh)).astype(BF)


def _swa_ctx_kernel(sink_ref, x_ref, mod_ref, wqkv_ref, wo_ref, o_ref, k_out, v_out, o_s):
    x = x_ref[...]
    h = _modulate(x, mod_ref, 0).astype(BF)
    qkv = _mm(h, wqkv_ref[...])
    k_out[...] = qkv[:, SWA_NQ:SWA_NQ + SWA_NK]
    v_out[...] = qkv[:, SWA_NQ + SWA_NK:]
    q_of = lambda hq: (qkv[:, hq * SWA_HD:(hq + 1) * SWA_HD] * SWA_SCALE).astype(BF)
    k_of = lambda kh: qkv[:, SWA_NQ + kh * SWA_HD:SWA_NQ + (kh + 1) * SWA_HD].astype(BF)
    v_of = lambda kh: qkv[:, SWA_NQ + SWA_NK + kh * SWA_HD:SWA_NQ + SWA_NK + (kh + 1) * SWA_HD].astype(BF)
    _softmax_sink_heads(q_of, k_of, v_of, sink_ref, o_s)
    o_ref[...] = x + _gate(mod_ref, 0) * _mm(o_s[...], wo_ref[...])


def swa_ctx_mixer(x, mod, w, L):
    M = x.shape[0]
    row = lambda n: pl.BlockSpec((L, n), lambda i: (i, 0))
    return pl.pallas_call(
        _swa_ctx_kernel,
        grid=(M // L,),
        in_specs=[pl.BlockSpec(memory_space=pltpu.SMEM), row(D), _mod_spec(L, M),
                  _const_spec(w["wqkv"]), _const_spec(w["wo"])],
        out_specs=[row(D), row(SWA_NK), row(SWA_NK)],
        out_shape=[jax.ShapeDtypeStruct((M, D), F32), jax.ShapeDtypeStruct((M, SWA_NK), F32),
                   jax.ShapeDtypeStruct((M, SWA_NK), F32)],
        scratch_shapes=[pltpu.VMEM((L, SWA_NQ), BF)],
        compiler_params=_cp("arbitrary"),
        name="swa_ctx",
    )(w["sink"], x, mod, w["wqkv"], w["wo"])


def _swa_lat_kernel(sink_ref, x_ref, mod_ref, wqkv_ref, wsw_ref, cos_ref, sin_ref, kc_ref, vc_ref,
                    wo_ref, o_ref, q_s, k_s, v_s, o_s, *, L):
    n = pl.program_id(1)
    RC = 256

    @pl.when(n == 0)
    def _():
        for c in range(L // RC):
            rows = slice(c * RC, (c + 1) * RC)
            h = _modulate(x_ref[rows, :], mod_ref, 0).astype(BF)
            qkv = _mm(h, wqkv_ref[...])
            sw = _mm(h, wsw_ref[...])
            cos = cos_ref[rows, :]
            sin = sin_ref[rows, :]
            cq = jnp.concatenate([cos] * (SWA_NQ // 128), axis=1)
            sq = jnp.concatenate([sin] * (SWA_NQ // 128), axis=1)
            ck = jnp.concatenate([cos] * (SWA_NK // 128), axis=1)
            sk = jnp.concatenate([sin] * (SWA_NK // 128), axis=1)
            q = (qkv[:, :SWA_NQ] * SWA_SCALE) * cq + (sw[:, :SWA_NQ] * SWA_SCALE) * sq
            q_s[rows, :] = q.astype(BF)
            k = qkv[:, SWA_NQ:SWA_NQ + SWA_NK] * ck + sw[:, SWA_NQ:SWA_NQ + SWA_NK] * sk
            k_s[rows, :] = k.astype(BF)
            v_s[rows, :] = qkv[:, SWA_NQ + SWA_NK:].astype(BF)

    QB = SWA_BLOCK
    KW = 3 * SWA_BLOCK
    r0 = pl.multiple_of(n * QB, QB)
    ws = pl.multiple_of(jnp.clip((n - 1) * QB, 0, L - KW), QB)
    qpos = r0 + lax.broadcasted_iota(jnp.int32, (QB, KW), 0)
    kpos = ws + lax.broadcasted_iota(jnp.int32, (QB, KW), 1)
    valid = jnp.abs(kpos - qpos) <= SWA_WINDOW
    for hq in range(SWA_H):
        kh = hq // (SWA_H // SWA_KVH)
        hs = slice(kh * SWA_HD, (kh + 1) * SWA_HD)
        q = q_s[pl.ds(r0, QB), hq * SWA_HD:(hq + 1) * SWA_HD]
        s1 = jnp.where(valid, _mm_nt(q, k_s[pl.ds(ws, KW), hs]), NEG)
        s2 = _mm_nt(q, kc_ref[0, :, hs].astype(BF))
        sink = sink_ref[hq]
        m = jnp.maximum(jnp.maximum(jnp.max(s1, axis=-1, keepdims=True),
                                    jnp.max(s2, axis=-1, keepdims=True)), sink)
        e1 = jnp.exp(s1 - m)
        e2 = jnp.exp(s2 - m)
        den = (jnp.sum(e1, axis=-1, keepdims=True) + jnp.sum(e2, axis=-1, keepdims=True)
               + jnp.exp(sink - m))
        o = (_mm((e1 / den).astype(BF), v_s[pl.ds(ws, KW), hs])
             + _mm((e2 / den).astype(BF), vc_ref[0, :, hs].astype(BF)))
        o_s[:, hq * SWA_HD:(hq + 1) * SWA_HD] = o.astype(BF)
    o_ref[...] = x_ref[pl.ds(r0, QB), :] + _gate(mod_ref, 0) * _mm(o_s[...], wo_ref[...])


def swa_lat_mixer(x, mod, w, L, kc, vc, cos128, sin128):
    M = x.shape[0]
    B = M // L
    P = kc.shape[1]
    nb = L // SWA_BLOCK
    return pl.pallas_call(
        functools.partial(_swa_lat_kernel, L=L),
        grid=(B, nb),
        in_specs=[pl.BlockSpec(memory_space=pltpu.SMEM),
                  pl.BlockSpec((L, D), lambda b, n: (b, 0)),
                  pl.BlockSpec((1, 6, D), lambda b, n: (b, 0, 0)),
                  _const_spec(w["wqkv"]), _const_spec(w["wsw"]),
                  _const_spec(cos128), _const_spec(sin128),
                  pl.BlockSpec((1, P, SWA_NK), lambda b, n: (b, 0, 0)),
                  pl.BlockSpec((1, P, SWA_NK), lambda b, n: (b, 0, 0)),
                  _const_spec(w["wo"])],
        out_specs=pl.BlockSpec((SWA_BLOCK, D), lambda b, n: (b * nb + n, 0)),
        out_shape=jax.ShapeDtypeStruct((M, D), F32),
        scratch_shapes=[pltpu.VMEM((L, SWA_NQ), BF), pltpu.VMEM((L, SWA_NK), BF),
                        pltpu.VMEM((L, SWA_NK), BF), pltpu.VMEM((SWA_BLOCK, SWA_NQ), BF)],
        compiler_params=_cp("arbitrary", "arbitrary"),
        name="swa_lat",
    )(w["sink"], x, mod, w["wqkv"], w["wsw"], cos128, sin128, kc, vc, w["wo"])


def _conv_kernel(x_ref, mod_ref, win_ref, cw_ref, wout_ref, o_ref, y_s, *, L):
    x = x_ref[...]
    h = _modulate(x, mod_ref, 0).astype(BF)
    CC = 256
    row = lax.broadcasted_iota(jnp.int32, (L, CC), 0)
    for c in range(D // CC):
        cols = slice(c * CC, (c + 1) * CC)
        bg = _mm(h, win_ref[:, c * CC:(c + 1) * CC])
        cg = _mm(h, win_ref[:, D + c * CC:D + (c + 1) * CC])
        u = _mm(h, win_ref[:, 2 * D + c * CC:2 * D + (c + 1) * CC])
        cu = cg * u
        prev = jnp.where(row == 0, 0.0, pltpu.roll(cu, 1, 0))
        nxt = jnp.where(row == L - 1, 0.0, pltpu.roll(cu, L - 1, 0))
        conv = prev * cw_ref[0:1, cols] + cu * cw_ref[1:2, cols] + nxt * cw_ref[2:3, cols]
        y_s[:, cols] = (bg * conv).astype(BF)
    o_ref[...] = x + _gate(mod_ref, 0) * _mm(y_s[...], wout_ref[...])


def conv_mixer(x, mod, w, L):
    M = x.shape[0]
    row = pl.BlockSpec((L, D), lambda i: (i, 0))
    return pl.pallas_call(
        functools.partial(_conv_kernel, L=L),
        grid=(M // L,),
        in_specs=[row, _mod_spec(L, L if mod.shape[0] > 1 else M), _const_spec(w["win"]),
                  _const_spec(w["cw"]), _const_spec(w["wout"])],
        out_specs=row,
        out_shape=jax.ShapeDtypeStruct((M, D), F32),
        scratch_shapes=[pltpu.VMEM((L, D), BF)],
        compiler_params=_cp("arbitrary"),
        name="conv_mix",
    )(x, mod, w["win"], w["cw"], w["wout"])


def _mla_ctx_kernel(x_ref, mod_ref, wdq_ref, qn_ref, wuq_ref, wdkv_ref, kvn_ref, wukv_ref, wo_ref,
                    o_ref, ckv_out, kr_out, o_s):
    x = x_ref[...]
    h = _modulate(x, mod_ref, 0).astype(BF)
    cq = (_rms(_mm(h, wdq_ref[...])) * qn_ref[...]).astype(BF)
    q = _mm(cq, wuq_ref[...]).astype(BF)
    kvc = _mm(h, wdkv_ref[...])
    ckv = _rms(kvc[:, :KV_LORA]) * kvn_ref[...]
    kr = kvc[:, KV_LORA:]
    ckv_out[...] = ckv
    kr_out[...] = kr
    kv = _mm(ckv.astype(BF), wukv_ref[...]).astype(BF)
    krb = kr.astype(BF)
    nn = MLA_H * QK_NOPE
    for hh in range(MLA_H):
        qh = q[:, hh * MLA_QW:(hh + 1) * MLA_QW]
        kh = jnp.concatenate([kv[:, hh * QK_NOPE:(hh + 1) * QK_NOPE], krb], axis=1)
        s = _mm_nt(qh, kh) * MLA_SCALE
        m = jnp.max(s, axis=-1, keepdims=True)
        e = jnp.exp(s - m)
        p = e / jnp.sum(e, axis=-1, keepdims=True)
        vh = kv[:, nn + hh * V_HEAD:nn + (hh + 1) * V_HEAD]
        o_s[:, hh * V_HEAD:(hh + 1) * V_HEAD] = _mm(p.astype(BF), vh).astype(BF)
    o_ref[...] = x + _gate(mod_ref, 0) * _mm(o_s[...], wo_ref[...])


def mla_ctx_mixer(x, mod, w, L):
    M = x.shape[0]
    row = lambda n: pl.BlockSpec((L, n), lambda i: (i, 0))
    ws = [w["wdq"], w["qnorm"], w["wuq"], w["wdkv"], w["kvnorm"], w["wukv"], w["wo"]]
    return pl.pallas_call(
        _mla_ctx_kernel,
        grid=(M // L,),
        in_specs=[row(D), _mod_spec(L, M)] + [_const_spec(a) for a in ws],
        out_specs=[row(D), row(KV_LORA), row(128)],
        out_shape=[jax.ShapeDtypeStruct((M, D), F32), jax.ShapeDtypeStruct((M, KV_LORA), F32),
                   jax.ShapeDtypeStruct((M, 128), F32)],
        scratch_shapes=[pltpu.VMEM((L, MLA_H * V_HEAD), BF)],
        compiler_params=_cp("arbitrary"),
        name="mla_ctx",
    )(x, mod, *ws)


def _mla_lat_kernel(x_ref, mod_ref, wdq_ref, qn_ref, wuq_ref, wuqsw_ref, wdkv_ref, wdkvsw_ref,
                    kvn_ref, wukv_ref, cq_ref, sq_ref, ck_ref, sk_ref, ckvc_ref, krc_ref, wo_ref,
                    o_ref, q_s, kn_s, v_s, kr_s, knc_s, vc_s, o_s, *, L, QB):
    n = pl.program_id(1)
    RC = 256
    nn = MLA_H * QK_NOPE

    @pl.when(n == 0)
    def _():
        for c in range(L // RC):
            rows = slice(c * RC, (c + 1) * RC)
            h = _modulate(x_ref[rows, :], mod_ref, 0).astype(BF)
            cq = (_rms(_mm(h, wdq_ref[...])) * qn_ref[...]).astype(BF)
            cosq = jnp.concatenate([cq_ref[rows, :]] * MLA_H, axis=1)
            sinq = jnp.concatenate([sq_ref[rows, :]] * MLA_H, axis=1)
            q = _mm(cq, wuq_ref[...]) * cosq + _mm(cq, wuqsw_ref[...]) * sinq
            q_s[rows, :] = q.astype(BF)
            kvc = _mm(h, wdkv_ref[...])
            ksw = _mm(h, wdkvsw_ref[...])
            ckv = _rms(kvc[:, :KV_LORA]) * kvn_ref[...]
            kr_s[rows, :] = (kvc[:, KV_LORA:] * ck_ref[rows, :] + ksw * sk_ref[rows, :]).astype(BF)
            kv = _mm(ckv.astype(BF), wukv_ref[...])
            kn_s[rows, :] = kv[:, :nn].astype(BF)
            v_s[rows, :] = kv[:, nn:].astype(BF)
        kvp = _mm(ckvc_ref[0].astype(BF), wukv_ref[...])
        knc_s[...] = kvp[:, :nn].astype(BF)
        vc_s[...] = kvp[:, nn:].astype(BF)

    r0 = pl.multiple_of(n * QB, QB)
    krc = krc_ref[0].astype(BF)
    for hh in range(MLA_H):
        ns = slice(hh * QK_NOPE, (hh + 1) * QK_NOPE)
        qh = q_s[pl.ds(r0, QB), hh * MLA_QW:(hh + 1) * MLA_QW]
        k1 = jnp.concatenate([kn_s[:, ns], kr_s[...]], axis=1)
        k2 = jnp.concatenate([knc_s[:, ns], krc], axis=1)
        s1 = _mm_nt(qh, k1) * MLA_SCALE
        s2 = _mm_nt(qh, k2) * MLA_SCALE
        m = jnp.maximum(jnp.max(s1, axis=-1, keepdims=True), jnp.max(s2, axis=-1, keepdims=True))
        e1 = jnp.exp(s1 - m)
        e2 = jnp.exp(s2 - m)
        den = jnp.sum(e1, axis=-1, keepdims=True) + jnp.sum(e2, axis=-1, keepdims=True)
        vs = slice(hh * V_HEAD, (hh + 1) * V_HEAD)
        o = _mm((e1 / den).astype(BF), v_s[:, vs]) + _mm((e2 / den).astype(BF), vc_s[:, vs])
        o_s[:, vs] = o.astype(BF)
    o_ref[...] = x_ref[pl.ds(r0, QB), :] + _gate(mod_ref, 0) * _mm(o_s[...], wo_ref[...])


def mla_lat_mixer(x, mod, w, L, ckv_c, kr_c128, tabs):
    M = x.shape[0]
    B = M // L
    P = ckv_c.shape[1]
    QB = 256
    nb = L // QB
    ws1 = [w["wdq"], w["qnorm"], w["wuq"], w["wuqsw"], w["wdkv"], w["wdkvsw"], w["kvnorm"], w["wukv"]]
    nv = MLA_H * V_HEAD
    return pl.pallas_call(
        functools.partial(_mla_lat_kernel, L=L, QB=QB),
        grid=(B, nb),
        in_specs=[pl.BlockSpec((L, D), lambda b, n: (b, 0)),
                  pl.BlockSpec((1, 6, D), lambda b, n: (b, 0, 0))]
                 + [_const_spec(a) for a in ws1] + [_const_spec(a) for a in tabs]
                 + [pl.BlockSpec((1, P, KV_LORA), lambda b, n: (b, 0, 0)),
                    pl.BlockSpec((1, P, 128), lambda b, n: (b, 0, 0)),
                    _const_spec(w["wo"])],
        out_specs=pl.BlockSpec((QB, D), lambda b, n: (b * nb + n, 0)),
        out_shape=jax.ShapeDtypeStruct((M, D), F32),
        scratch_shapes=[pltpu.VMEM((L, MLA_H * MLA_QW), BF), pltpu.VMEM((L, nv), BF),
                        pltpu.VMEM((L, nv), BF), pltpu.VMEM((L, 128), BF),
                        pltpu.VMEM((P, nv), BF), pltpu.VMEM((P, nv), BF), pltpu.VMEM((QB, nv), BF)],
        compiler_params=_cp("arbitrary", "arbitrary"),
        name="mla_lat",
    )(x, mod, *ws1, *tabs, ckv_c, kr_c128, w["wo"])


def _swiglu_partial(xb, wg_ref, wu_ref, wd_ref):
    hg = _mm(xb, wg_ref[...].astype(BF))
    hu = _mm(xb, wu_ref[...].astype(BF))
    a = (_silu(hg) * hu).astype(BF)
    return _mm(a, wd_ref[...].astype(BF))


def _dense_ffn_kernel(x_ref, mod_ref, wg_ref, wu_ref, wd_ref, o_ref, xm_s):
    f = pl.program_id(1)

    @pl.when(f == 0)
    def _():
        xm_s[...] = _modulate(x_ref[...], mod_ref, 1).astype(BF)

    part = _swiglu_partial(xm_s[...], wg_ref, wu_ref, wd_ref)

    @pl.when(f == 0)
    def _():
        o_ref[...] = part

    @pl.when(f > 0)
    def _():
        o_ref[...] += part

    @pl.when(f == pl.num_programs(1) - 1)
    def _():
        o_ref[...] = x_ref[...] + _gate(mod_ref, 1) * o_ref[...]


def dense_ffn(x, mod, wg, wu, wd, j, L):
    M = x.shape[0]
    F = wg.shape[-1]
    tm, tf = DENSE_TM, DENSE_TF
    return pl.pallas_call(
        _dense_ffn_kernel,
        grid=(M // tm, F // tf),
        in_specs=[pl.BlockSpec((tm, D), lambda i, f: (i, 0)),
                  pl.BlockSpec((1, 6, D), lambda i, f: ((i * tm) // (L if mod.shape[0] > 1 else M), 0, 0)),
                  pl.BlockSpec((None, D, tf), lambda i, f: (j, 0, f)),
                  pl.BlockSpec((None, D, tf), lambda i, f: (j, 0, f)),
                  pl.BlockSpec((None, tf, D), lambda i, f: (j, f, 0))],
        out_specs=pl.BlockSpec((tm, D), lambda i, f: (i, 0)),
        out_shape=jax.ShapeDtypeStruct((M, D), F32),
        scratch_shapes=[pltpu.VMEM((tm, D), BF)],
        compiler_params=_cp("arbitrary", "arbitrary"),
        name="dense_ffn",
    )(x, mod, wg, wu, wd)


def _router_kernel(xp_ref, xs_ref, mod_ref, wr_ref, xm_ref, idx_ref, w_ref, cnt_ref, cnt_s, *, np_tiles):
    i = pl.program_id(0)
    tm = xm_ref.shape[0]

    @pl.when(i == 0)
    def _():
        cnt_s[...] = jnp.zeros_like(cnt_s)

    x = jnp.where(i < np_tiles, xp_ref[...], xs_ref[...])
    xm = _modulate(x, mod_ref, 1).astype(BF)
    xm_ref[...] = xm
    lane = lax.broadcasted_iota(jnp.int32, idx_ref.shape, 1)
    logits = jnp.where(lane < N_EXPERTS, _mm(xm, wr_ref[...]), -jnp.inf)
    m1 = jnp.max(logits, axis=-1, keepdims=True)
    i1 = jnp.min(jnp.where(logits == m1, lane, 128), axis=-1, keepdims=True)
    rest = jnp.where(lane == i1, -jnp.inf, logits)
    m2 = jnp.max(rest, axis=-1, keepdims=True)
    i2 = jnp.min(jnp.where(rest == m2, lane, 128), axis=-1, keepdims=True)
    e = jnp.exp(m2 - m1)
    w_ref[...] = jnp.where(lane == 0, 1.0 / (1.0 + e), e / (1.0 + e))
    oh1 = jnp.where(lane == i1, 1.0, 0.0)
    oh2 = jnp.where(lane == i2, 1.0, 0.0)
    rr = lax.broadcasted_iota(jnp.int32, (tm, tm), 0)
    cc = lax.broadcasted_iota(jnp.int32, (tm, tm), 1)
    below = jnp.where(cc < rr, 1.0, 0.0).astype(BF)
    run = cnt_s[0:1, :]
    tot1 = jnp.sum(oh1, axis=0, keepdims=True)
    tot2 = jnp.sum(oh2, axis=0, keepdims=True)
    r1 = jnp.sum(oh1 * (run + _mm(below, oh1.astype(BF))), axis=-1, keepdims=True)
    r2 = jnp.sum(oh2 * (run + tot1 + _mm(below, oh2.astype(BF))), axis=-1, keepdims=True)
    idx_ref[...] = jnp.where(lane == 0, i1, jnp.where(lane == 1, i2, jnp.where(
        lane == 2, r1.astype(jnp.int32), r2.astype(jnp.int32))))
    total = run + tot1 + tot2
    cnt_s[...] = jnp.broadcast_to(total, cnt_s.shape)
    cnt_ref[...] = jnp.broadcast_to(total, cnt_ref.shape)


def route(xp, xs, mod3, wr, Ls):
    Mp, Ms = xp.shape[0], xs.shape[0]
    T = Mp + Ms
    tm = 512
    npt = Mp // tm
    row = lambda n: pl.BlockSpec((tm, n), lambda i: (i, 0))
    return pl.pallas_call(
        functools.partial(_router_kernel, np_tiles=npt),
        grid=(T // tm,),
        in_specs=[pl.BlockSpec((tm, D), lambda i: (jnp.minimum(i, npt - 1), 0)),
                  pl.BlockSpec((tm, D), lambda i: (jnp.maximum(i - npt, 0), 0)),
                  pl.BlockSpec((1, 6, D), lambda i: (jnp.where(i < npt, 0, 1 + ((i - npt) * tm) // Ls), 0, 0)),
                  _const_spec(wr)],
        out_specs=[row(D), row(128), row(128), pl.BlockSpec((8, 128), lambda i: (0, 0))],
        out_shape=[jax.ShapeDtypeStruct((T, D), BF), jax.ShapeDtypeStruct((T, 128), jnp.int32),
                   jax.ShapeDtypeStruct((T, 128), F32), jax.ShapeDtypeStruct((8, 128), F32)],
        scratch_shapes=[pltpu.VMEM((8, 128), F32)],
        compiler_params=_cp("arbitrary"),
        name="moe_route",
    )(xp, xs, mod3, wr)


def _moe_ffn_kernel(te_ref, nv_ref, x_ref, wg_ref, wu_ref, wd_ref, o_ref):
    i = pl.program_id(0)
    f = pl.program_id(1)
    valid = i < nv_ref[0]

    @pl.when(valid)
    def _():
        part = _swiglu_partial(x_ref[...], wg_ref, wu_ref, wd_ref)

        @pl.when(f == 0)
        def _():
            o_ref[...] = part

        @pl.when(f > 0)
        def _():
            o_ref[...] += part

    @pl.when(jnp.logical_and(jnp.logical_not(valid), f == 0))
    def _():
        o_ref[...] = jnp.zeros_like(o_ref)


def moe_ffn(xs, tile_expert, n_valid, wg, wu, wd, j):
    R = xs.shape[0]
    F = wg.shape[-1]
    tm, tf = MOE_TM, MOE_TF
    nf = F // tf

    def fidx(i, f, nv):
        return jnp.where(i < nv[0], f, nf - 1)

    grid_spec = pltpu.PrefetchScalarGridSpec(
        num_scalar_prefetch=2,
        grid=(R // tm, nf),
        in_specs=[pl.BlockSpec((tm, D), lambda i, f, te, nv: (i, 0)),
                  pl.BlockSpec((None, None, D, tf), lambda i, f, te, nv: (j, te[i], 0, fidx(i, f, nv))),
                  pl.BlockSpec((None, None, D, tf), lambda i, f, te, nv: (j, te[i], 0, fidx(i, f, nv))),
                  pl.BlockSpec((None, None, tf, D), lambda i, f, te, nv: (j, te[i], fidx(i, f, nv), 0))],
        out_specs=pl.BlockSpec((tm, D), lambda i, f, te, nv: (i, 0)),
    )
    return pl.pallas_call(
        _moe_ffn_kernel,
        grid_spec=grid_spec,
        out_shape=jax.ShapeDtypeStruct((R, D), F32),
        compiler_params=_cp("arbitrary", "arbitrary"),
        name="moe_ffn",
    )(tile_expert, n_valid, xs, wg, wu, wd)


def _combine_kernel(x_ref, mod_ref, ya_ref, yb_ref, w_ref, fn_ref, o_ref, *, final):
    y = w_ref[:, 0:1] * ya_ref[...] + w_ref[:, 1:2] * yb_ref[...]
    o = x_ref[...] + _gate(mod_ref, 1) * y
    if final:
        o = _rms(o) * fn_ref[...]
    o_ref[...] = o


def moe_combine(x, mod, ya, yb, w, fn, L, final, row_off):
    M = x.shape[0]
    tm = 512
    off = row_off // tm
    row = pl.BlockSpec((tm, D), lambda i: (i, 0))
    row_o = lambda n: pl.BlockSpec((tm, n), lambda i: (i + off, 0))
    return pl.pallas_call(
        functools.partial(_combine_kernel, final=final),
        grid=(M // tm,),
        in_specs=[row, _mod_spec(tm, L if mod.shape[0] > 1 else M), row_o(D), row_o(D), row_o(128),
                  _const_spec(fn)],
        out_specs=row,
        out_shape=jax.ShapeDtypeStruct((M, D), F32),
        compiler_params=_cp("arbitrary"),
        name="moe_combine",
    )(x, mod, ya, yb, w, fn)


def moe_layer(xp, xs, mod3, wr, wg, wu, wd, j, fn, Lp, Ls, final):
    Mp = xp.shape[0]
    xm, idx, wts, cnt = route(xp, xs, mod3, wr, Ls)
    T = xm.shape[0]
    tm = MOE_TM
    R = 2 * T + N_EXPERTS * tm
    counts = cnt[0, :N_EXPERTS].astype(jnp.int32)
    tiles_per = (counts + tm - 1) // tm
    tile_end = jnp.cumsum(tiles_per)
    offs = (tile_end - tiles_per) * tm
    experts = jnp.arange(N_EXPERTS, dtype=jnp.int32)
    dest = jnp.sum(jnp.where(idx[:, 0:2, None] == experts, offs, 0), axis=-1) + idx[:, 2:4]
    tok = jnp.arange(2 * T, dtype=jnp.int32) // 2
    src = jnp.zeros((R,), jnp.int32).at[dest.reshape(-1)].set(
        tok, unique_indices=True, mode="promise_in_bounds")
    n_tiles = R // tm
    n_valid = tile_end[-1:].astype(jnp.int32)
    t_ids = jnp.arange(n_tiles, dtype=jnp.int32)
    te = jnp.sum((t_ids[:, None] >= tile_end[None, :]).astype(jnp.int32), axis=1)
    te_last = jnp.sum((n_valid - 1 >= tile_end).astype(jnp.int32))
    te = jnp.where(t_ids < n_valid[0], te, te_last).astype(jnp.int32)
    x_sorted = xm.at[src].get(mode="promise_in_bounds")
    y_sorted = moe_ffn(x_sorted, te, n_valid, wg, wu, wd, j)
    ya = y_sorted.at[dest[:, 0]].get(mode="promise_in_bounds")
    yb = y_sorted.at[dest[:, 1]].get(mode="promise_in_bounds")
    xp_new = moe_combine(xp, mod3[0:1], ya, yb, wts, fn, Lp, final, 0)
    xs_new = moe_combine(xs, mod3[1:], ya, yb, wts, fn, Ls, final, Mp)
    return xp_new, xs_new


def _rope_partner_perm(n):
    c = np.arange(n)
    return np.where((c % 32) < 16, c + 16, c - 16)


def _rope_tables(L):
    nf = 16
    inv = ROPE_THETA ** (-np.arange(nf, dtype=np.float32) / nf)
    pos = np.arange(L)
    ang = np.stack([(pos // GRID_W).astype(np.float32), (pos % GRID_W).astype(np.float32)],
                   axis=-1)[:, :, None] * inv
    cos = jnp.cos(jnp.asarray(ang, F32))
    sin = jnp.sin(jnp.asarray(ang, F32))
    c64 = jnp.stack([cos, cos], axis=2).reshape(L, 64)
    s64 = jnp.stack([-sin, sin], axis=2).reshape(L, 64)
    return c64, s64


def kernel(x_prompt, x_sample, state_l0_gla, cache_l1_k, cache_l1_v, cache_l3_ckv, cache_l3_krope, c, c_ctx, w_mod, b_mod, gla_wq, gla_wk, gla_wv, gla_wg, gla_wgk1, gla_wgk2, gla_bgk, gla_norm, gla_wo, swa_wqkv, swa_sink, swa_wo, conv_win, conv_w, conv_wout, mla_wdq, mla_qnorm, mla_wuq, mla_wdkv, mla_kvnorm, mla_wukv, mla_wo, dense_w_gate, dense_w_up, dense_w_down, moe_router, moe_w_gate, moe_w_up, moe_w_down, final_norm):
    Bp, Lp, _ = x_prompt.shape
    Bs, Ls, _ = x_sample.shape
    P = cache_l1_k.shape[1]
    xp = x_prompt.reshape(Bp * Lp, D)
    xs = x_sample.reshape(Bs * Ls, D)

    cvec = jnp.zeros((8, D), F32).at[0].set(c_ctx).at[1:1 + Bs].set(c)
    mod_all = adaln_all(cvec, w_mod, b_mod).reshape(w_mod.shape[0], 8, 6, D)

    wlr = jnp.zeros((D, 128), F32).at[:, :GLA_RANK].set(gla_wgk1[0]).at[:, GLA_RANK:2 * GLA_RANK].set(gla_wgk1[1])
    wgk2 = jnp.zeros((2, 128, GLA_H * GLA_DK), F32)
    wgk2 = wgk2.at[0, :GLA_RANK].set(gla_wgk2[0]).at[1, GLA_RANK:2 * GLA_RANK].set(gla_wgk2[1])
    gla_w = dict(wp=jnp.concatenate([gla_wq, gla_wk, gla_wv, gla_wg], axis=1).astype(BF),
                 wlr=wlr.astype(BF), wgk2=wgk2.astype(BF), bgk=gla_bgk,
                 gnorm=gla_norm.reshape(1, GLA_DV), wo=gla_wo.astype(BF))
    modp, mods = mod_all[0, 0:1], mod_all[0, 1:1 + Bs]
    xp, new_state = gla_mixer(xp, modp, gla_w, Lp, None)
    xs, _ = gla_mixer(xs, mods, gla_w, Ls, state_l0_gla)
    xp = dense_ffn(xp, modp, dense_w_gate, dense_w_up, dense_w_down, 0, Lp)
    xs = dense_ffn(xs, mods, dense_w_gate, dense_w_up, dense_w_down, 0, Ls)

    c64, s64 = _rope_tables(Ls)
    cos128 = jnp.concatenate([c64, c64], axis=1)
    sin128 = jnp.concatenate([s64, s64], axis=1)
    perm = _rope_partner_perm(SWA_NQ + SWA_NK)
    swa_w = dict(wqkv=swa_wqkv.astype(BF), wsw=swa_wqkv[:, perm].astype(BF), wo=swa_wo.astype(BF),
                 sink=swa_sink)
    modp, mods = mod_all[1, 0:1], mod_all[1, 1:1 + Bs]
    xp, k1, v1 = swa_ctx_mixer(xp, modp, swa_w, Lp)
    xs = swa_lat_mixer(xs, mods, swa_w, Ls, cache_l1_k.reshape(Bs, P, SWA_NK),
                       cache_l1_v.reshape(Bs, P, SWA_NK), cos128, sin128)
    wr = jnp.zeros((moe_router.shape[0], D, 128), F32).at[:, :, :N_EXPERTS].set(moe_router).astype(BF)
    xp, xs = moe_layer(xp, xs, mod_all[1, 0:1 + Bs], wr[0], moe_w_gate, moe_w_up, moe_w_down, 0,
                       final_norm.reshape(1, D), Lp, Ls, False)

    conv_wts = dict(win=conv_win.astype(BF), cw=conv_w, wout=conv_wout.astype(BF))
    modp, mods = mod_all[2, 0:1], mod_all[2, 1:1 + Bs]
    xp = conv_mixer(xp, modp, conv_wts, Lp)
    xs = conv_mixer(xs, mods, conv_wts, Ls)
    xp = dense_ffn(xp, modp, dense_w_gate, dense_w_up, dense_w_down, 1, Lp)
    xs = dense_ffn(xs, mods, dense_w_gate, dense_w_up, dense_w_down, 1, Ls)

    wuq3 = mla_wuq.reshape(Q_LORA, MLA_H, QK_NOPE + QK_ROPE)
    zpad = jnp.zeros((Q_LORA, MLA_H, MLA_QW - QK_NOPE - QK_ROPE), F32)
    wuq = jnp.concatenate([wuq3, zpad], axis=2).reshape(Q_LORA, MLA_H * MLA_QW)
    rperm = _rope_partner_perm(QK_ROPE)
    wuq_sw = jnp.concatenate([jnp.zeros((Q_LORA, MLA_H, QK_NOPE), F32), wuq3[:, :, QK_NOPE:][:, :, rperm], zpad],
                             axis=2).reshape(Q_LORA, MLA_H * MLA_QW)
    kpad = jnp.zeros((D, 128 - QK_ROPE), F32)
    wdkv = jnp.concatenate([mla_wdkv, kpad], axis=1)
    wdkv_sw = jnp.concatenate([mla_wdkv[:, KV_LORA:][:, rperm], kpad], axis=1)
    wukv3 = mla_wukv.reshape(KV_LORA, MLA_H, QK_NOPE + V_HEAD)
    wukv = jnp.concatenate([wukv3[:, :, :QK_NOPE].reshape(KV_LORA, MLA_H * QK_NOPE),
                            wukv3[:, :, QK_NOPE:].reshape(KV_LORA, MLA_H * V_HEAD)], axis=1)
    mla_w = dict(wdq=mla_wdq.astype(BF), qnorm=mla_qnorm.reshape(1, Q_LORA), wuq=wuq.astype(BF),
                 wuqsw=wuq_sw.astype(BF), wdkv=wdkv.astype(BF), wdkvsw=wdkv_sw.astype(BF),
                 kvnorm=mla_kvnorm.reshape(1, KV_LORA), wukv=wukv.astype(BF), wo=mla_wo.astype(BF))
    ones = jnp.ones((Ls, QK_NOPE), F32)
    z128 = jnp.zeros((Ls, QK_NOPE), F32)
    z64 = jnp.zeros((Ls, 64), F32)
    tabs = [jnp.concatenate([ones, c64, z64], axis=1), jnp.concatenate([z128, s64, z64], axis=1),
            jnp.concatenate([c64, z64], axis=1), jnp.concatenate([s64, z64], axis=1)]
    modp, mods = mod_all[3, 0:1], mod_all[3, 1:1 + Bs]
    xp, ckv3, kr3 = mla_ctx_mixer(xp, modp, mla_w, Lp)
    kr_c128 = jnp.concatenate([cache_l3_krope, jnp.zeros((Bs, P, 128 - QK_ROPE), F32)], axis=2)
    xs = mla_lat_mixer(xs, mods, mla_w, Ls, cache_l3_ckv, kr_c128, tabs)
    xp, xs = moe_layer(xp, xs, mod_all[3, 0:1 + Bs], wr[1], moe_w_gate, moe_w_up, moe_w_down, 1,
                       final_norm.reshape(1, D), Lp, Ls, True)

    return (xp.reshape(Bp, Lp, D), xs.reshape(Bs, Ls, D), new_state,
            k1.reshape(Bp, Lp, SWA_KVH, SWA_HD), v1.reshape(Bp, Lp, SWA_KVH, SWA_HD),
            ckv3.reshape(Bp, Lp, KV_LORA), kr3[:, :QK_ROPE].reshape(Bp, Lp, QK_ROPE))
```

```python
import functools

import numpy as np
import jax
import jax.numpy as jnp
from jax import lax
from jax.experimental import pallas as pl
from jax.experimental.pallas import tpu as pltpu

BF = jnp.bfloat16
F32 = jnp.float32

D = 1024
EPS = 1e-6
NEG = -1e30
ROPE_THETA = 10000.0
GRID_W = 64

GLA_H, GLA_DK, GLA_DV, GLA_RANK, GLA_CHUNK = 4, 128, 256, 16, 64
GLA_SCALE = GLA_DK ** -0.5
GLA_INV_NORMALIZER = 1.0 / 16.0

SWA_H, SWA_KVH, SWA_HD, SWA_WINDOW, SWA_BLOCK = 16, 4, 64, 128, 128
SWA_SCALE = SWA_HD ** -0.5
SWA_NQ = SWA_H * SWA_HD
SWA_NK = SWA_KVH * SWA_HD

MLA_H, Q_LORA, KV_LORA, QK_NOPE, QK_ROPE, V_HEAD = 8, 384, 256, 128, 64, 128
MLA_SCALE = (QK_NOPE + QK_ROPE) ** -0.5
MLA_QW = 256

N_EXPERTS = 8
MOE_TM = 512
MOE_TF = 896
DENSE_TM = 1024
DENSE_TF = 256

VMEM_LIMIT = 56 * 1024 * 1024


def _cp(*sem):
    return pltpu.CompilerParams(dimension_semantics=sem, vmem_limit_bytes=VMEM_LIMIT)


def _mm(a, b):
    return jnp.dot(a, b, preferred_element_type=F32)


def _mm_nt(a, b):
    return lax.dot_general(a, b, (((1,), (1,)), ((), ())), preferred_element_type=F32)


def _mm_tn(a, b):
    return lax.dot_general(a, b, (((0,), (0,)), ((), ())), preferred_element_type=F32)


def _rms(x):
    return x * lax.rsqrt(jnp.mean(x * x, axis=-1, keepdims=True) + EPS)


def _silu(x):
    return x * (1.0 / (1.0 + jnp.exp(-x)))


def _modulate(x, mod_ref, j):
    shift = mod_ref[0, 3 * j:3 * j + 1, :]
    scale = mod_ref[0, 3 * j + 1:3 * j + 2, :]
    return _rms(x) * (1.0 + scale) + shift


def _gate(mod_ref, j):
    return mod_ref[0, 3 * j + 2:3 * j + 3, :]


def _const_spec(a):
    nd = a.ndim
    return pl.BlockSpec(a.shape, lambda *_: (0,) * nd)


def _mod_spec(tm, rows_per_mod):
    return pl.BlockSpec((1, 6, D), lambda i, *_: ((i * tm) // rows_per_mod, 0, 0))


def _adaln_kernel(c_ref, w_ref, b_ref, o_ref):
    s = _silu(c_ref[...]).astype(BF)
    o_ref[0] = _mm(s, w_ref[0].astype(BF)) + b_ref[0]


def adaln_all(cvec8, w_mod, b_mod):
    nl, d, n = w_mod.shape
    tn = 1536
    return pl.pallas_call(
        _adaln_kernel,
        grid=(nl, n // tn),
        in_specs=[pl.BlockSpec((8, d), lambda l, j: (0, 0)),
                  pl.BlockSpec((1, d, tn), lambda l, j: (l, 0, j)),
                  pl.BlockSpec((1, 1, tn), lambda l, j: (l, 0, j))],
        out_specs=pl.BlockSpec((1, 8, tn), lambda l, j: (l, 0, j)),
        out_shape=jax.ShapeDtypeStruct((nl, 8, n), F32),
        compiler_params=_cp("arbitrary", "arbitrary"),
        name="adaln",
    )(cvec8, w_mod, b_mod.reshape(nl, 1, n))


def _split3(x):
    hi = x.astype(BF)
    rem = x - hi.astype(F32)
    mid = rem.astype(BF)
    lo = (rem - mid.astype(F32)).astype(BF)
    return hi, mid, lo


def _mm3(a, pieces):
    return _mm(a, pieces[0]) + _mm(a, pieces[1]) + _mm(a, pieces[2])


def _gla_pre_kernel(x_ref, mod_ref, wp_ref, wlr_ref, wgk2_ref, bgk_ref,
                    qt_ref, kt_ref, kd_ref, v_ref, g_ref, dec_ref):
    tm = x_ref.shape[0]
    C = GLA_CHUNK
    G = 256
    h = _modulate(x_ref[...], mod_ref, 0).astype(BF)
    nk = GLA_H * GLA_DK
    nv = GLA_H * GLA_DV
    q = _mm(h, wp_ref[:, 0:nk]) * GLA_SCALE
    k = _mm(h, wp_ref[:, nk:2 * nk])
    v_ref[...] = _mm(h, wp_ref[:, 2 * nk:2 * nk + nv]).astype(BF)
    g_ref[...] = _mm(h, wp_ref[:, 2 * nk + nv:2 * nk + 2 * nv])
    lr = _mm(h, wlr_ref[...]).astype(BF)
    ii = lax.broadcasted_iota(jnp.int32, (G, G), 0)
    jj = lax.broadcasted_iota(jnp.int32, (G, G), 1)
    same = (ii // C) == (jj // C)
    ones_blk = jnp.where(same, 1.0, 0.0).astype(BF)
    ci = lax.broadcasted_iota(jnp.int32, (tm // C, tm), 0)
    cj = lax.broadcasted_iota(jnp.int32, (tm // C, tm), 1)
    sel = jnp.where((cj // C) == ci, 1.0, 0.0).astype(BF)
    for r in range(2):
        z = _mm(lr, wgk2_ref[r]) + bgk_ref[r:r + 1, :]
        gk = (jnp.minimum(z, 0.0) - jnp.log1p(jnp.exp(-jnp.abs(z)))) * GLA_INV_NORMALIZER
        pieces = _split3(gk)
        tri = jnp.where(jnp.logical_and(same, (jj <= ii) if r == 0 else (jj >= ii)), 1.0, 0.0).astype(BF)
        dec_ref[r] = jnp.exp(_mm3(sel, pieces))
        for gi in range(tm // G):
            rows = slice(gi * G, (gi + 1) * G)
            pg = tuple(p[rows] for p in pieces)
            b = _mm3(tri, pg)
            tot = _mm3(ones_blk, pg)
            qt_ref[r, rows, :] = (q[rows] * jnp.exp(b)).astype(BF)
            kt_ref[r, rows, :] = (k[rows] * jnp.exp(-b)).astype(BF)
            kd_ref[r, rows, :] = (k[rows] * jnp.exp(tot - b)).astype(BF)


def _gla_scan_kernel(*refs, L, has_h0, emit_state):
    qt_ref, kt_ref, kd_ref, v_ref, g_ref, dec_ref, gn_ref = refs[:7]
    pos = 7
    h0_ref = None
    if has_h0:
        h0_ref = refs[pos]
        pos += 1
    y_ref = refs[pos]
    pos += 1
    st_ref = None
    if emit_state:
        st_ref = refs[pos]
        pos += 1
    st_s, of_s, ob_s = refs[pos], refs[pos + 1], refs[pos + 2]

    C = GLA_CHUNK
    n_chunks = L // C
    ii = lax.broadcasted_iota(jnp.int32, (C, C), 0)
    jj = lax.broadcasted_iota(jnp.int32, (C, C), 1)
    keeps = ((jj <= ii), (jj >= ii))
    for r in range(2):
        for hd in range(GLA_H):
            if has_h0:
                st_s[r * GLA_H + hd] = h0_ref[0, r, hd].T
            else:
                st_s[r * GLA_H + hd] = jnp.zeros((GLA_DV, GLA_DK), F32)

    def step(i):
        for r in range(2):
            n = i if r == 0 else n_chunks - 1 - i
            start = n * C
            if not isinstance(start, int):
                start = pl.multiple_of(start, C)
            rows = pl.ds(start, C)
            o_s = of_s if r == 0 else ob_s
            for hd in range(GLA_H):
                kc = slice(hd * GLA_DK, (hd + 1) * GLA_DK)
                vc = slice(hd * GLA_DV, (hd + 1) * GLA_DV)
                qt = qt_ref[r, rows, kc]
                v = v_ref[rows, vc]
                att = jnp.where(keeps[r], _mm_nt(qt, kt_ref[r, rows, kc]), 0.0).astype(BF)
                st = st_s[r * GLA_H + hd]
                o_s[rows, vc] = _mm_nt(qt, st.astype(BF)) + _mm(att, v)
                st_s[r * GLA_H + hd] = st * dec_ref[r, n, :, kc] + _mm_tn(v, kd_ref[r, rows, kc])

    if n_chunks <= 4:
        for i in range(n_chunks):
            step(i)
    else:
        def body(i, carry):
            step(i)
            return carry
        lax.fori_loop(0, n_chunks, body, 0)

    RC = 256
    for c in range(L // RC):
        rows = slice(c * RC, (c + 1) * RC)
        for hd in range(GLA_H):
            vc = slice(hd * GLA_DV, (hd + 1) * GLA_DV)
            y = _rms(of_s[rows, vc] + ob_s[rows, vc]) * gn_ref[...]
            y_ref[rows, vc] = (y * _silu(g_ref[rows, vc])).astype(BF)
    if emit_state:
        for r in range(2):
            for hd in range(GLA_H):
                st_ref[0, r, hd] = st_s[r * GLA_H + hd].T


def _residual_out_kernel(y_ref, x_ref, mod_ref, w_ref, o_ref, *, gate_j):
    o_ref[...] = x_ref[...] + _gate(mod_ref, gate_j) * _mm(y_ref[...], w_ref[...])


def gla_mixer(x, mod, w, L, h0):
    M = x.shape[0]
    B = M // L
    tm = 512
    nk, nv = GLA_H * GLA_DK, GLA_H * GLA_DV
    row = lambda n: pl.BlockSpec((tm, n), lambda i: (i, 0))
    dirs = lambda: pl.BlockSpec((2, tm, nk), lambda i: (0, i, 0))
    n_chunks = L // GLA_CHUNK
    qt, kt, kd, v, g, dec = pl.pallas_call(
        _gla_pre_kernel,
        grid=(M // tm,),
        in_specs=[row(D), _mod_spec(tm, L if mod.shape[0] > 1 else M),
                  _const_spec(w["wp"]), _const_spec(w["wlr"]), _const_spec(w["wgk2"]),
                  _const_spec(w["bgk"])],
        out_specs=[dirs(), dirs(), dirs(), row(nv), row(nv),
                   pl.BlockSpec((2, tm // GLA_CHUNK, nk), lambda i: (0, i, 0))],
        out_shape=[jax.ShapeDtypeStruct((2, M, nk), BF), jax.ShapeDtypeStruct((2, M, nk), BF),
                   jax.ShapeDtypeStruct((2, M, nk), BF),
                   jax.ShapeDtypeStruct((M, nv), BF), jax.ShapeDtypeStruct((M, nv), F32),
                   jax.ShapeDtypeStruct((2, M // GLA_CHUNK, nk), F32)],
        compiler_params=_cp("arbitrary"),
        name="gla_pre",
    )(x, mod, w["wp"], w["wlr"], w["wgk2"], w["bgk"])
    dec = dec.reshape(2, B, n_chunks, 1, nk)

    has_h0 = h0 is not None
    emit_state = not has_h0
    dir_spec = lambda: pl.BlockSpec((2, L, nk), lambda b: (0, b, 0))
    in_specs = [dir_spec(), dir_spec(), dir_spec(),
                pl.BlockSpec((L, nv), lambda b: (b, 0)),
                pl.BlockSpec((L, nv), lambda b: (b, 0)),
                pl.BlockSpec((2, None, n_chunks, 1, nk), lambda b: (0, b, 0, 0, 0)),
                pl.BlockSpec((1, GLA_DV), lambda b: (0, 0))]
    args = [qt, kt, kd, v, g, dec, w["gnorm"]]
    st_spec = pl.BlockSpec((1, 2, GLA_H, GLA_DK, GLA_DV), lambda b: (b, 0, 0, 0, 0))
    if has_h0:
        in_specs.append(st_spec)
        args.append(h0)
    out_specs = [pl.BlockSpec((L, nv), lambda b: (b, 0))]
    out_shape = [jax.ShapeDtypeStruct((M, nv), BF)]
    if emit_state:
        out_specs.append(st_spec)
        out_shape.append(jax.ShapeDtypeStruct((B, 2, GLA_H, GLA_DK, GLA_DV), F32))
    res = pl.pallas_call(
        functools.partial(_gla_scan_kernel, L=L, has_h0=has_h0, emit_state=emit_state),
        grid=(B,),
        in_specs=in_specs,
        out_specs=out_specs,
        out_shape=out_shape,
        scratch_shapes=[pltpu.VMEM((2 * GLA_H, GLA_DV, GLA_DK), F32), pltpu.VMEM((L, nv), F32),
                        pltpu.VMEM((L, nv), F32)],
        compiler_params=_cp("arbitrary"),
        name="gla_scan",
    )(*args)
    y = res[0]
    state = res[1] if emit_state else None

    x_new = pl.pallas_call(
        functools.partial(_residual_out_kernel, gate_j=0),
        grid=(M // tm,),
        in_specs=[row(nv), row(D), _mod_spec(tm, L if mod.shape[0] > 1 else M), _const_spec(w["wo"])],
        out_specs=row(D),
        out_shape=jax.ShapeDtypeStruct((M, D), F32),
        compiler_params=_cp("arbitrary"),
        name="gla_post",
    )(y, x, mod, w["wo"])
    return x_new, state


def _softmax_sink_heads(q_of, k_of, v_of, sink_ref, o_s):
    for hq in range(SWA_H):
        kh = hq // (SWA_H // SWA_KVH)
        s = _mm_nt(q_of(hq), k_of(kh))
        sink = sink_ref[hq]
        m = jnp.maximum(jnp.max(s, axis=-1, keepdims=True), sink)
        e = jnp.exp(s - m)
        p = e / (jnp.sum(e, axis=-1, keepdims=True) + jnp.exp(sink - m))
        o_s[:, hq * SWA_HD:(hq + 1) * SWA_HD] = _mm(p.astype(BF), v_of(kh)).astype(BF)


def _swa_ctx_kernel(sink_ref, x_ref, mod_ref, wqkv_ref, wo_ref, o_ref, k_out, v_out, o_s):
    x = x_ref[...]
    h = _modulate(x, mod_ref, 0).astype(BF)
    qkv = _mm(h, wqkv_ref[...])
    k_out[...] = qkv[:, SWA_NQ:SWA_NQ + SWA_NK]
    v_out[...] = qkv[:, SWA_NQ + SWA_NK:]
    q_of = lambda hq: (qkv[:, hq * SWA_HD:(hq + 1) * SWA_HD] * SWA_SCALE).astype(BF)
    k_of = lambda kh: qkv[:, SWA_NQ + kh * SWA_HD:SWA_NQ + (kh + 1) * SWA_HD].astype(BF)
    v_of = lambda kh: qkv[:, SWA_NQ + SWA_NK + kh * SWA_HD:SWA_NQ + SWA_NK + (kh + 1) * SWA_HD].astype(BF)
    _softmax_sink_heads(q_of, k_of, v_of, sink_ref, o_s)
    o_ref[...] = x + _gate(mod_ref, 0) * _mm(o_s[...], wo_ref[...])


def swa_ctx_mixer(x, mod, w, L):
    M = x.shape[0]
    row = lambda n: pl.BlockSpec((L, n), lambda i: (i, 0))
    return pl.pallas_call(
        _swa_ctx_kernel,
        grid=(M // L,),
        in_specs=[pl.BlockSpec(memory_space=pltpu.SMEM), row(D), _mod_spec(L, M),
                  _const_spec(w["wqkv"]), _const_spec(w["wo"])],
        out_specs=[row(D), row(SWA_NK), row(SWA_NK)],
        out_shape=[jax.ShapeDtypeStruct((M, D), F32), jax.ShapeDtypeStruct((M, SWA_NK), F32),
                   jax.ShapeDtypeStruct((M, SWA_NK), F32)],
        scratch_shapes=[pltpu.VMEM((L, SWA_NQ), BF)],
        compiler_params=_cp("arbitrary"),
        name="swa_ctx",
    )(w["sink"], x, mod, w["wqkv"], w["wo"])


def _swa_lat_kernel(sink_ref, x_ref, mod_ref, wqkv_ref, wsw_ref, cos_ref, sin_ref, kc_ref, vc_ref,
                    wo_ref, o_ref, q_s, k_s, v_s, o_s, *, L):
    n = pl.program_id(1)
    RC = 256

    @pl.when(n == 0)
    def _():
        for c in range(L // RC):
            rows = slice(c * RC, (c + 1) * RC)
            h = _modulate(x_ref[rows, :], mod_ref, 0).astype(BF)
            qkv = _mm(h, wqkv_ref[...])
            sw = _mm(h, wsw_ref[...])
            cos = cos_ref[rows, :]
            sin = sin_ref[rows, :]
            cq = jnp.concatenate([cos] * (SWA_NQ // 128), axis=1)
            sq = jnp.concatenate([sin] * (SWA_NQ // 128), axis=1)
            ck = jnp.concatenate([cos] * (SWA_NK // 128), axis=1)
            sk = jnp.concatenate([sin] * (SWA_NK // 128), axis=1)
            q = (qkv[:, :SWA_NQ] * SWA_SCALE) * cq + (sw[:, :SWA_NQ] * SWA_SCALE) * sq
            q_s[rows, :] = q.astype(BF)
            k = qkv[:, SWA_NQ:SWA_NQ + SWA_NK] * ck + sw[:, SWA_NQ:SWA_NQ + SWA_NK] * sk
            k_s[rows, :] = k.astype(BF)
            v_s[rows, :] = qkv[:, SWA_NQ + SWA_NK:].astype(BF)

    QB = SWA_BLOCK
    KW = 3 * SWA_BLOCK
    r0 = pl.multiple_of(n * QB, QB)
    ws = pl.multiple_of(jnp.clip((n - 1) * QB, 0, L - KW), QB)
    qpos = r0 + lax.broadcasted_iota(jnp.int32, (QB, KW), 0)
    kpos = ws + lax.broadcasted_iota(jnp.int32, (QB, KW), 1)
    valid = jnp.abs(kpos - qpos) <= SWA_WINDOW
    for hq in range(SWA_H):
        kh = hq // (SWA_H // SWA_KVH)
        hs = slice(kh * SWA_HD, (kh + 1) * SWA_HD)
        q = q_s[pl.ds(r0, QB), hq * SWA_HD:(hq + 1) * SWA_HD]
        s1 = jnp.where(valid, _mm_nt(q, k_s[pl.ds(ws, KW), hs]), NEG)
        s2 = _mm_nt(q, kc_ref[0, :, hs].astype(BF))
        sink = sink_ref[hq]
        m = jnp.maximum(jnp.maximum(jnp.max(s1, axis=-1, keepdims=True),
                                    jnp.max(s2, axis=-1, keepdims=True)), sink)
        e1 = jnp.exp(s1 - m)
        e2 = jnp.exp(s2 - m)
        den = (jnp.sum(e1, axis=-1, keepdims=True) + jnp.sum(e2, axis=-1, keepdims=True)
               + jnp.exp(sink - m))
        o = (_mm((e1 / den).astype(BF), v_s[pl.ds(ws, KW), hs])
             + _mm((e2 / den).astype(BF), vc_ref[0, :, hs].astype(BF)))
        o_s[:, hq * SWA_HD:(hq + 1) * SWA_HD] = o.astype(BF)
    o_ref[...] = x_ref[pl.ds(r0, QB), :] + _gate(mod_ref, 0) * _mm(o_s[...], wo_ref[...])


def swa_lat_mixer(x, mod, w, L, kc, vc, cos128, sin128):
    M = x.shape[0]
    B = M // L
    P = kc.shape[1]
    nb = L // SWA_BLOCK
    return pl.pallas_call(
        functools.partial(_swa_lat_kernel, L=L),
        grid=(B, nb),
        in_specs=[pl.BlockSpec(memory_space=pltpu.SMEM),
                  pl.BlockSpec((L, D), lambda b, n: (b, 0)),
                  pl.BlockSpec((1, 6, D), lambda b, n: (b, 0, 0)),
                  _const_spec(w["wqkv"]), _const_spec(w["wsw"]),
                  _const_spec(cos128), _const_spec(sin128),
                  pl.BlockSpec((1, P, SWA_NK), lambda b, n: (b, 0, 0)),
                  pl.BlockSpec((1, P, SWA_NK), lambda b, n: (b, 0, 0)),
                  _const_spec(w["wo"])],
        out_specs=pl.BlockSpec((SWA_BLOCK, D), lambda b, n: (b * nb + n, 0)),
        out_shape=jax.ShapeDtypeStruct((M, D), F32),
        scratch_shapes=[pltpu.VMEM((L, SWA_NQ), BF), pltpu.VMEM((L, SWA_NK), BF),
                        pltpu.VMEM((L, SWA_NK), BF), pltpu.VMEM((SWA_BLOCK, SWA_NQ), BF)],
        compiler_params=_cp("arbitrary", "arbitrary"),
        name="swa_lat",
    )(w["sink"], x, mod, w["wqkv"], w["wsw"], cos128, sin128, kc, vc, w["wo"])


def _conv_kernel(x_ref, mod_ref, win_ref, cw_ref, wout_ref, o_ref, y_s, *, L):
    x = x_ref[...]
    h = _modulate(x, mod_ref, 0).astype(BF)
    CC = 256
    row = lax.broadcasted_iota(jnp.int32, (L, CC), 0)
    for c in range(D // CC):
        cols = slice(c * CC, (c + 1) * CC)
        bg = _mm(h, win_ref[:, c * CC:(c + 1) * CC])
        cg = _mm(h, win_ref[:, D + c * CC:D + (c + 1) * CC])
        u = _mm(h, win_ref[:, 2 * D + c * CC:2 * D + (c + 1) * CC])
        cu = cg * u
        prev = jnp.where(row == 0, 0.0, pltpu.roll(cu, 1, 0))
        nxt = jnp.where(row == L - 1, 0.0, pltpu.roll(cu, L - 1, 0))
        conv = prev * cw_ref[0:1, cols] + cu * cw_ref[1:2, cols] + nxt * cw_ref[2:3, cols]
        y_s[:, cols] = (bg * conv).astype(BF)
    o_ref[...] = x + _gate(mod_ref, 0) * _mm(y_s[...], wout_ref[...])


def conv_mixer(x, mod, w, L):
    M = x.shape[0]
    row = pl.BlockSpec((L, D), lambda i: (i, 0))
    return pl.pallas_call(
        functools.partial(_conv_kernel, L=L),
        grid=(M // L,),
        in_specs=[row, _mod_spec(L, L if mod.shape[0] > 1 else M), _const_spec(w["win"]),
                  _const_spec(w["cw"]), _const_spec(w["wout"])],
        out_specs=row,
        out_shape=jax.ShapeDtypeStruct((M, D), F32),
        scratch_shapes=[pltpu.VMEM((L, D), BF)],
        compiler_params=_cp("arbitrary"),
        name="conv_mix",
    )(x, mod, w["win"], w["cw"], w["wout"])


def _mla_ctx_kernel(x_ref, mod_ref, wdq_ref, qn_ref, wuq_ref, wdkv_ref, kvn_ref, wukv_ref, wo_ref,
                    o_ref, ckv_out, kr_out, o_s):
    x = x_ref[...]
    h = _modulate(x, mod_ref, 0).astype(BF)
    cq = (_rms(_mm(h, wdq_ref[...])) * qn_ref[...]).astype(BF)
    q = _mm(cq, wuq_ref[...]).astype(BF)
    kvc = _mm(h, wdkv_ref[...])
    ckv = _rms(kvc[:, :KV_LORA]) * kvn_ref[...]
    kr = kvc[:, KV_LORA:]
    ckv_out[...] = ckv
    kr_out[...] = kr
    kv = _mm(ckv.astype(BF), wukv_ref[...]).astype(BF)
    krb = kr.astype(BF)
    nn = MLA_H * QK_NOPE
    for hh in range(MLA_H):
        qh = q[:, hh * MLA_QW:(hh + 1) * MLA_QW]
        kh = jnp.concatenate([kv[:, hh * QK_NOPE:(hh + 1) * QK_NOPE], krb], axis=1)
        s = _mm_nt(qh, kh) * MLA_SCALE
        m = jnp.max(s, axis=-1, keepdims=True)
        e = jnp.exp(s - m)
        p = e / jnp.sum(e, axis=-1, keepdims=True)
        vh = kv[:, nn + hh * V_HEAD:nn + (hh + 1) * V_HEAD]
        o_s[:, hh * V_HEAD:(hh + 1) * V_HEAD] = _mm(p.astype(BF), vh).astype(BF)
    o_ref[...] = x + _gate(mod_ref, 0) * _mm(o_s[...], wo_ref[...])


def mla_ctx_mixer(x, mod, w, L):
    M = x.shape[0]
    row = lambda n: pl.BlockSpec((L, n), lambda i: (i, 0))
    ws = [w["wdq"], w["qnorm"], w["wuq"], w["wdkv"], w["kvnorm"], w["wukv"], w["wo"]]
    return pl.pallas_call(
        _mla_ctx_kernel,
        grid=(M // L,),
        in_specs=[row(D), _mod_spec(L, M)] + [_const_spec(a) for a in ws],
        out_specs=[row(D), row(KV_LORA), row(128)],
        out_shape=[jax.ShapeDtypeStruct((M, D), F32), jax.ShapeDtypeStruct((M, KV_LORA), F32),
                   jax.ShapeDtypeStruct((M, 128), F32)],
        scratch_shapes=[pltpu.VMEM((L, MLA_H * V_HEAD), BF)],
        compiler_params=_cp("arbitrary"),
        name="mla_ctx",
    )(x, mod, *ws)


def _mla_lat_kernel(x_ref, mod_ref, wdq_ref, qn_ref, wuq_ref, wuqsw_ref, wdkv_ref, wdkvsw_ref,
                    kvn_ref, wukv_ref, cq_ref, sq_ref, ck_ref, sk_ref, ckvc_ref, krc_ref, wo_ref,
                    o_ref, q_s, kn_s, v_s, kr_s, knc_s, vc_s, o_s, *, L, QB):
    n = pl.program_id(1)
    RC = 256
    nn = MLA_H * QK_NOPE

    @pl.when(n == 0)
    def _():
        for c in range(L // RC):
            rows = slice(c * RC, (c + 1) * RC)
            h = _modulate(x_ref[rows, :], mod_ref, 0).astype(BF)
            cq = (_rms(_mm(h, wdq_ref[...])) * qn_ref[...]).astype(BF)
            cosq = jnp.concatenate([cq_ref[rows, :]] * MLA_H, axis=1)
            sinq = jnp.concatenate([sq_ref[rows, :]] * MLA_H, axis=1)
            q = _mm(cq, wuq_ref[...]) * cosq + _mm(cq, wuqsw_ref[...]) * sinq
            q_s[rows, :] = q.astype(BF)
            kvc = _mm(h, wdkv_ref[...])
            ksw = _mm(h, wdkvsw_ref[...])
            ckv = _rms(kvc[:, :KV_LORA]) * kvn_ref[...]
            kr_s[rows, :] = (kvc[:, KV_LORA:] * ck_ref[rows, :] + ksw * sk_ref[rows, :]).astype(BF)
            kv = _mm(ckv.astype(BF), wukv_ref[...])
            kn_s[rows, :] = kv[:, :nn].astype(BF)
            v_s[rows, :] = kv[:, nn:].astype(BF)
        kvp = _mm(ckvc_ref[0].astype(BF), wukv_ref[...])
        knc_s[...] = kvp[:, :nn].astype(BF)
        vc_s[...] = kvp[:, nn:].astype(BF)

    r0 = pl.multiple_of(n * QB, QB)
    krc = krc_ref[0].astype(BF)
    for hh in range(MLA_H):
        ns = slice(hh * QK_NOPE, (hh + 1) * QK_NOPE)
        qh = q_s[pl.ds(r0, QB), hh * MLA_QW:(hh + 1) * MLA_QW]
        k1 = jnp.concatenate([kn_s[:, ns], kr_s[...]], axis=1)
        k2 = jnp.concatenate([knc_s[:, ns], krc], axis=1)
        s1 = _mm_nt(qh, k1) * MLA_SCALE
        s2 = _mm_nt(qh, k2) * MLA_SCALE
        m = jnp.maximum(jnp.max(s1, axis=-1, keepdims=True), jnp.max(s2, axis=-1, keepdims=True))
        e1 = jnp.exp(s1 - m)
        e2 = jnp.exp(s2 - m)
        den = jnp.sum(e1, axis=-1, keepdims=True) + jnp.sum(e2, axis=-1, keepdims=True)
        vs = slice(hh * V_HEAD, (hh + 1) * V_HEAD)
        o = _mm((e1 / den).astype(BF), v_s[:, vs]) + _mm((e2 / den).astype(BF), vc_s[:, vs])
        o_s[:, vs] = o.astype(BF)
    o_ref[...] = x_ref[pl.ds(r0, QB), :] + _gate(mod_ref, 0) * _mm(o_s[...], wo_ref[...])


def mla_lat_mixer(x, mod, w, L, ckv_c, kr_c128, tabs):
    M = x.shape[0]
    B = M // L
    P = ckv_c.shape[1]
    QB = 256
    nb = L // QB
    ws1 = [w["wdq"], w["qnorm"], w["wuq"], w["wuqsw"], w["wdkv"], w["wdkvsw"], w["kvnorm"], w["wukv"]]
    nv = MLA_H * V_HEAD
    return pl.pallas_call(
        functools.partial(_mla_lat_kernel, L=L, QB=QB),
        grid=(B, nb),
        in_specs=[pl.BlockSpec((L, D), lambda b, n: (b, 0)),
                  pl.BlockSpec((1, 6, D), lambda b, n: (b, 0, 0))]
                 + [_const_spec(a) for a in ws1] + [_const_spec(a) for a in tabs]
                 + [pl.BlockSpec((1, P, KV_LORA), lambda b, n: (b, 0, 0)),
                    pl.BlockSpec((1, P, 128), lambda b, n: (b, 0, 0)),
                    _const_spec(w["wo"])],
        out_specs=pl.BlockSpec((QB, D), lambda b, n: (b * nb + n, 0)),
        out_shape=jax.ShapeDtypeStruct((M, D), F32),
        scratch_shapes=[pltpu.VMEM((L, MLA_H * MLA_QW), BF), pltpu.VMEM((L, nv), BF),
                        pltpu.VMEM((L, nv), BF), pltpu.VMEM((L, 128), BF),
                        pltpu.VMEM((P, nv), BF), pltpu.VMEM((P, nv), BF), pltpu.VMEM((QB, nv), BF)],
        compiler_params=_cp("arbitrary", "arbitrary"),
        name="mla_lat",
    )(x, mod, *ws1, *tabs, ckv_c, kr_c128, w["wo"])


def _swiglu_partial(xb, wg_ref, wu_ref, wd_ref):
    hg = _mm(xb, wg_ref[...].astype(BF))
    hu = _mm(xb, wu_ref[...].astype(BF))
    a = (_silu(hg) * hu).astype(BF)
    return _mm(a, wd_ref[...].astype(BF))


def _dense_ffn_kernel(x_ref, mod_ref, wg_ref, wu_ref, wd_ref, o_ref, xm_s):
    f = pl.program_id(1)

    @pl.when(f == 0)
    def _():
        xm_s[...] = _modulate(x_ref[...], mod_ref, 1).astype(BF)

    part = _swiglu_partial(xm_s[...], wg_ref, wu_ref, wd_ref)

    @pl.when(f == 0)
    def _():
        o_ref[...] = part

    @pl.when(f > 0)
    def _():
        o_ref[...] += part

    @pl.when(f == pl.num_programs(1) - 1)
    def _():
        o_ref[...] = x_ref[...] + _gate(mod_ref, 1) * o_ref[...]


def dense_ffn(x, mod, wg, wu, wd, j, L):
    M = x.shape[0]
    F = wg.shape[-1]
    tm, tf = DENSE_TM, DENSE_TF
    return pl.pallas_call(
        _dense_ffn_kernel,
        grid=(M // tm, F // tf),
        in_specs=[pl.BlockSpec((tm, D), lambda i, f: (i, 0)),
                  pl.BlockSpec((1, 6, D), lambda i, f: ((i * tm) // (L if mod.shape[0] > 1 else M), 0, 0)),
                  pl.BlockSpec((None, D, tf), lambda i, f: (j, 0, f)),
                  pl.BlockSpec((None, D, tf), lambda i, f: (j, 0, f)),
                  pl.BlockSpec((None, tf, D), lambda i, f: (j, f, 0))],
        out_specs=pl.BlockSpec((tm, D), lambda i, f: (i, 0)),
        out_shape=jax.ShapeDtypeStruct((M, D), F32),
        scratch_shapes=[pltpu.VMEM((tm, D), BF)],
        compiler_params=_cp("arbitrary", "arbitrary"),
        name="dense_ffn",
    )(x, mod, wg, wu, wd)


def _router_kernel(xp_ref, xs_ref, mod_ref, wr_ref, xm_ref, idx_ref, w_ref, cnt_ref, cnt_s, *, np_tiles):
    i = pl.program_id(0)
    tm = xm_ref.shape[0]

    @pl.when(i == 0)
    def _():
        cnt_s[...] = jnp.zeros_like(cnt_s)

    x = jnp.where(i < np_tiles, xp_ref[...], xs_ref[...])
    xmod = _modulate(x, mod_ref, 1)
    xm_ref[...] = xmod
    xm = xmod.astype(BF)
    lane = lax.broadcasted_iota(jnp.int32, idx_ref.shape, 1)
    logits = jnp.where(lane < N_EXPERTS, _mm(xm, wr_ref[...]), -jnp.inf)
    m1 = jnp.max(logits, axis=-1, keepdims=True)
    i1 = jnp.min(jnp.where(logits == m1, lane, 128), axis=-1, keepdims=True)
    rest = jnp.where(lane == i1, -jnp.inf, logits)
    m2 = jnp.max(rest, axis=-1, keepdims=True)
    i2 = jnp.min(jnp.where(rest == m2, lane, 128), axis=-1, keepdims=True)
    e = jnp.exp(m2 - m1)
    w_ref[...] = jnp.where(lane == 0, 1.0 / (1.0 + e), e / (1.0 + e))
    oh1 = jnp.where(lane == i1, 1.0, 0.0)
    oh2 = jnp.where(lane == i2, 1.0, 0.0)
    rr = lax.broadcasted_iota(jnp.int32, (tm, tm), 0)
    cc = lax.broadcasted_iota(jnp.int32, (tm, tm), 1)
    below = jnp.where(cc < rr, 1.0, 0.0).astype(BF)
    run = cnt_s[0:1, :]
    tot1 = jnp.sum(oh1, axis=0, keepdims=True)
    tot2 = jnp.sum(oh2, axis=0, keepdims=True)
    r1 = jnp.sum(oh1 * (run + _mm(below, oh1.astype(BF))), axis=-1, keepdims=True)
    r2 = jnp.sum(oh2 * (run + tot1 + _mm(below, oh2.astype(BF))), axis=-1, keepdims=True)
    idx_ref[...] = jnp.where(lane == 0, i1, jnp.where(lane == 1, i2, jnp.where(
        lane == 2, r1.astype(jnp.int32), r2.astype(jnp.int32))))
    total = run + tot1 + tot2
    cnt_s[...] = jnp.broadcast_to(total, cnt_s.shape)
    cnt_ref[...] = jnp.broadcast_to(total, cnt_ref.shape)


def route(xp, xs, mod3, wr, Ls):
    Mp, Ms = xp.shape[0], xs.shape[0]
    T = Mp + Ms
    tm = 512
    npt = Mp // tm
    row = lambda n: pl.BlockSpec((tm, n), lambda i: (i, 0))
    return pl.pallas_call(
        functools.partial(_router_kernel, np_tiles=npt),
        grid=(T // tm,),
        in_specs=[pl.BlockSpec((tm, D), lambda i: (jnp.minimum(i, npt - 1), 0)),
                  pl.BlockSpec((tm, D), lambda i: (jnp.maximum(i - npt, 0), 0)),
                  pl.BlockSpec((1, 6, D), lambda i: (jnp.where(i < npt, 0, 1 + ((i - npt) * tm) // Ls), 0, 0)),
                  _const_spec(wr)],
        out_specs=[row(D), row(128), row(128), pl.BlockSpec((8, 128), lambda i: (0, 0))],
        out_shape=[jax.ShapeDtypeStruct((T, D), F32), jax.ShapeDtypeStruct((T, 128), jnp.int32),
                   jax.ShapeDtypeStruct((T, 128), F32), jax.ShapeDtypeStruct((8, 128), F32)],
        scratch_shapes=[pltpu.VMEM((8, 128), F32)],
        compiler_params=_cp("arbitrary"),
        name="moe_route",
    )(xp, xs, mod3, wr)


def _moe_ffn_kernel(te_ref, nv_ref, xa_ref, xb_ref, wg_ref, wu_ref, wd_ref, o_ref, x_s, *, half_tiles):
    i = pl.program_id(0)
    f = pl.program_id(1)
    valid = i < nv_ref[0]

    @pl.when(jnp.logical_and(valid, f == 0))
    def _():
        x_s[...] = jnp.where(i < half_tiles, xa_ref[...], xb_ref[...]).astype(BF)

    @pl.when(valid)
    def _():
        part = _swiglu_partial(x_s[...], wg_ref, wu_ref, wd_ref)

        @pl.when(f == 0)
        def _():
            o_ref[...] = part

        @pl.when(f > 0)
        def _():
            o_ref[...] += part

    @pl.when(jnp.logical_and(jnp.logical_not(valid), f == 0))
    def _():
        o_ref[...] = jnp.zeros_like(o_ref)


def moe_ffn(xa, xb, tile_expert, n_valid, wg, wu, wd, j):
    R = 2 * xa.shape[0]
    F = wg.shape[-1]
    tm, tf = MOE_TM, MOE_TF
    nf = F // tf
    nh = xa.shape[0] // tm

    def fidx(i, f, nv):
        return jnp.where(i < nv[0], f, nf - 1)

    grid_spec = pltpu.PrefetchScalarGridSpec(
        num_scalar_prefetch=2,
        grid=(R // tm, nf),
        in_specs=[pl.BlockSpec((tm, D), lambda i, f, te, nv: (jnp.minimum(i, nh - 1), 0)),
                  pl.BlockSpec((tm, D), lambda i, f, te, nv: (jnp.maximum(i - nh, 0), 0)),
                  pl.BlockSpec((None, None, D, tf), lambda i, f, te, nv: (j, te[i], 0, fidx(i, f, nv))),
                  pl.BlockSpec((None, None, D, tf), lambda i, f, te, nv: (j, te[i], 0, fidx(i, f, nv))),
                  pl.BlockSpec((None, None, tf, D), lambda i, f, te, nv: (j, te[i], fidx(i, f, nv), 0))],
        out_specs=pl.BlockSpec((tm, D), lambda i, f, te, nv: (i, 0)),
        scratch_shapes=[pltpu.VMEM((tm, D), BF)],
    )
    return pl.pallas_call(
        functools.partial(_moe_ffn_kernel, half_tiles=nh),
        grid_spec=grid_spec,
        out_shape=jax.ShapeDtypeStruct((R, D), F32),
        compiler_params=_cp("arbitrary", "arbitrary"),
        name="moe_ffn",
    )(tile_expert, n_valid, xa, xb, wg, wu, wd)


def _combine_kernel(x_ref, mod_ref, ya_ref, yb_ref, w_ref, fn_ref, o_ref, *, final):
    y = w_ref[:, 0:1] * ya_ref[...] + w_ref[:, 1:2] * yb_ref[...]
    o = x_ref[...] + _gate(mod_ref, 1) * y
    if final:
        o = _rms(o) * fn_ref[...]
    o_ref[...] = o


def moe_combine(x, mod, ya, yb, w, fn, L, final, row_off):
    M = x.shape[0]
    tm = 512
    off = row_off // tm
    row = pl.BlockSpec((tm, D), lambda i: (i, 0))
    row_o = lambda n: pl.BlockSpec((tm, n), lambda i: (i + off, 0))
    return pl.pallas_call(
        functools.partial(_combine_kernel, final=final),
        grid=(M // tm,),
        in_specs=[row, _mod_spec(tm, L if mod.shape[0] > 1 else M), row_o(D), row_o(D), row_o(128),
                  _const_spec(fn)],
        out_specs=row,
        out_shape=jax.ShapeDtypeStruct((M, D), F32),
        compiler_params=_cp("arbitrary"),
        name="moe_combine",
    )(x, mod, ya, yb, w, fn)


def moe_layer(xp, xs, mod3, wr, wg, wu, wd, j, fn, Lp, Ls, final):
    Mp = xp.shape[0]
    xm, idx, wts, cnt = route(xp, xs, mod3, wr, Ls)
    T = xm.shape[0]
    tm = MOE_TM
    R = 2 * T + N_EXPERTS * tm
    counts = cnt[0, :N_EXPERTS].astype(jnp.int32)
    tiles_per = (counts + tm - 1) // tm
    tile_end = jnp.cumsum(tiles_per)
    offs = (tile_end - tiles_per) * tm
    experts = jnp.arange(N_EXPERTS, dtype=jnp.int32)
    dest = jnp.sum(jnp.where(idx[:, 0:2, None] == experts, offs, 0), axis=-1) + idx[:, 2:4]
    tok = jnp.arange(2 * T, dtype=jnp.int32) // 2
    src = jnp.zeros((R,), jnp.int32).at[dest.reshape(-1)].set(
        tok, unique_indices=True, mode="promise_in_bounds")
    n_tiles = R // tm
    n_valid = tile_end[-1:].astype(jnp.int32)
    t_ids = jnp.arange(n_tiles, dtype=jnp.int32)
    te = jnp.sum((t_ids[:, None] >= tile_end[None, :]).astype(jnp.int32), axis=1)
    te_last = jnp.sum((n_valid - 1 >= tile_end).astype(jnp.int32))
    te = jnp.where(t_ids < n_valid[0], te, te_last).astype(jnp.int32)
    xa = xm.at[src[:R // 2]].get(mode="promise_in_bounds")
    xb = xm.at[src[R // 2:]].get(mode="promise_in_bounds")
    y_sorted = moe_ffn(xa, xb, te, n_valid, wg, wu, wd, j)
    ya = y_sorted.at[dest[:, 0]].get(mode="promise_in_bounds")
    yb = y_sorted.at[dest[:, 1]].get(mode="promise_in_bounds")
    xp_new = moe_combine(xp, mod3[0:1], ya, yb, wts, fn, Lp, final, 0)
    xs_new = moe_combine(xs, mod3[1:], ya, yb, wts, fn, Ls, final, Mp)
    return xp_new, xs_new


def _rope_partner_perm(n):
    c = np.arange(n)
    return np.where((c % 32) < 16, c + 16, c - 16)


def _rope_tables(L):
    nf = 16
    inv = ROPE_THETA ** (-np.arange(nf, dtype=np.float32) / nf)
    pos = np.arange(L)
    ang = np.stack([(pos // GRID_W).astype(np.float32), (pos % GRID_W).astype(np.float32)],
                   axis=-1)[:, :, None] * inv
    cos = jnp.cos(jnp.asarray(ang, F32))
    sin = jnp.sin(jnp.asarray(ang, F32))
    c64 = jnp.stack([cos, cos], axis=2).reshape(L, 64)
    s64 = jnp.stack([-sin, sin], axis=2).reshape(L, 64)
    return c64, s64


def kernel(x_prompt, x_sample, state_l0_gla, cache_l1_k, cache_l1_v, cache_l3_ckv, cache_l3_krope, c, c_ctx, w_mod, b_mod, gla_wq, gla_wk, gla_wv, gla_wg, gla_wgk1, gla_wgk2, gla_bgk, gla_norm, gla_wo, swa_wqkv, swa_sink, swa_wo, conv_win, conv_w, conv_wout, mla_wdq, mla_qnorm, mla_wuq, mla_wdkv, mla_kvnorm, mla_wukv, mla_wo, dense_w_gate, dense_w_up, dense_w_down, moe_router, moe_w_gate, moe_w_up, moe_w_down, final_norm):
    Bp, Lp, _ = x_prompt.shape
    Bs, Ls, _ = x_sample.shape
    P = cache_l1_k.shape[1]
    xp = x_prompt.reshape(Bp * Lp, D)
    xs = x_sample.reshape(Bs * Ls, D)

    cvec = jnp.zeros((8, D), F32).at[0].set(c_ctx).at[1:1 + Bs].set(c)
    mod_all = adaln_all(cvec, w_mod, b_mod).reshape(w_mod.shape[0], 8, 6, D)

    wlr = jnp.zeros((D, 128), F32).at[:, :GLA_RANK].set(gla_wgk1[0]).at[:, GLA_RANK:2 * GLA_RANK].set(gla_wgk1[1])
    wgk2 = jnp.zeros((2, 128, GLA_H * GLA_DK), F32)
    wgk2 = wgk2.at[0, :GLA_RANK].set(gla_wgk2[0]).at[1, GLA_RANK:2 * GLA_RANK].set(gla_wgk2[1])
    gla_w = dict(wp=jnp.concatenate([gla_wq, gla_wk, gla_wv, gla_wg], axis=1).astype(BF),
                 wlr=wlr.astype(BF), wgk2=wgk2.astype(BF), bgk=gla_bgk,
                 gnorm=gla_norm.reshape(1, GLA_DV), wo=gla_wo.astype(BF))
    modp, mods = mod_all[0, 0:1], mod_all[0, 1:1 + Bs]
    xp, new_state = gla_mixer(xp, modp, gla_w, Lp, None)
    xs, _ = gla_mixer(xs, mods, gla_w, Ls, state_l0_gla)
    xp = dense_ffn(xp, modp, dense_w_gate, dense_w_up, dense_w_down, 0, Lp)
    xs = dense_ffn(xs, mods, dense_w_gate, dense_w_up, dense_w_down, 0, Ls)

    c64, s64 = _rope_tables(Ls)
    cos128 = jnp.concatenate([c64, c64], axis=1)
    sin128 = jnp.concatenate([s64, s64], axis=1)
    perm = _rope_partner_perm(SWA_NQ + SWA_NK)
    swa_w = dict(wqkv=swa_wqkv.astype(BF), wsw=swa_wqkv[:, perm].astype(BF), wo=swa_wo.astype(BF),
                 sink=swa_sink)
    modp, mods = mod_all[1, 0:1], mod_all[1, 1:1 + Bs]
    xp, k1, v1 = swa_ctx_mixer(xp, modp, swa_w, Lp)
    xs = swa_lat_mixer(xs, mods, swa_w, Ls, cache_l1_k.reshape(Bs, P, SWA_NK),
                       cache_l1_v.reshape(Bs, P, SWA_NK), cos128, sin128)
    wr = jnp.zeros((moe_router.shape[0], D, 128), F32).at[:, :, :N_EXPERTS].set(moe_router).astype(BF)
    xp, xs = moe_layer(xp, xs, mod_all[1, 0:1 + Bs], wr[0], moe_w_gate, moe_w_up, moe_w_down, 0,
                       final_norm.reshape(1, D), Lp, Ls, False)

    conv_wts = dict(win=conv_win.astype(BF), cw=conv_w, wout=conv_wout.astype(BF))
    modp, mods = mod_all[2, 0:1], mod_all[2, 1:1 + Bs]
    xp = conv_mixer(xp, modp, conv_wts, Lp)
    xs = conv_mixer(xs, mods, conv_wts, Ls)
    xp = dense_ffn(xp, modp, dense_w_gate, dense_w_up, dense_w_down, 1, Lp)
    xs = dense_ffn(xs, mods, dense_w_gate, dense_w_up, dense_w_down, 1, Ls)

    wuq3 = mla_wuq.reshape(Q_LORA, MLA_H, QK_NOPE + QK_ROPE)
    zpad = jnp.zeros((Q_LORA, MLA_H, MLA_QW - QK_NOPE - QK_ROPE), F32)
    wuq = jnp.concatenate([wuq3, zpad], axis=2).reshape(Q_LORA, MLA_H * MLA_QW)
    rperm = _rope_partner_perm(QK_ROPE)
    wuq_sw = jnp.concatenate([jnp.zeros((Q_LORA, MLA_H, QK_NOPE), F32), wuq3[:, :, QK_NOPE:][:, :, rperm], zpad],
                             axis=2).reshape(Q_LORA, MLA_H * MLA_QW)
    kpad = jnp.zeros((D, 128 - QK_ROPE), F32)
    wdkv = jnp.concatenate([mla_wdkv, kpad], axis=1)
    wdkv_sw = jnp.concatenate([mla_wdkv[:, KV_LORA:][:, rperm], kpad], axis=1)
    wukv3 = mla_wukv.reshape(KV_LORA, MLA_H, QK_NOPE + V_HEAD)
    wukv = jnp.concatenate([wukv3[:, :, :QK_NOPE].reshape(KV_LORA, MLA_H * QK_NOPE),
                            wukv3[:, :, QK_NOPE:].reshape(KV_LORA, MLA_H * V_HEAD)], axis=1)
    mla_w = dict(wdq=mla_wdq.astype(BF), qnorm=mla_qnorm.reshape(1, Q_LORA), wuq=wuq.astype(BF),
                 wuqsw=wuq_sw.astype(BF), wdkv=wdkv.astype(BF), wdkvsw=wdkv_sw.astype(BF),
                 kvnorm=mla_kvnorm.reshape(1, KV_LORA), wukv=wukv.astype(BF), wo=mla_wo.astype(BF))
    ones = jnp.ones((Ls, QK_NOPE), F32)
    z128 = jnp.zeros((Ls, QK_NOPE), F32)
    z64 = jnp.zeros((Ls, 64), F32)
    tabs = [jnp.concatenate([ones, c64, z64], axis=1), jnp.concatenate([z128, s64, z64], axis=1),
            jnp.concatenate([c64, z64], axis=1), jnp.concatenate([s64, z64], axis=1)]
    modp, mods = mod_all[3, 0:1], mod_all[3, 1:1 + Bs]
    xp, ckv3, kr3 = mla_ctx_mixer(xp, modp, mla_w, Lp)
    kr_c128 = jnp.concatenate([cache_l3_krope, jnp.zeros((Bs, P, 128 - QK_ROPE), F32)], axis=2)
    xs = mla_lat_mixer(xs, mods, mla_w, Ls, cache_l3_ckv, kr_c128, tabs)
    xp, xs = moe_layer(xp, xs, mod_all[3, 0:1 + Bs], wr[1], moe_w_gate, moe_w_up, moe_w_down, 1,
                       final_norm.reshape(1, D), Lp, Ls, True)

    return (xp.reshape(Bp, Lp, D), xs.reshape(Bs, Ls, D), new_state,
            k1.reshape(Bp, Lp, SWA_KVH, SWA_HD), v1.reshape(Bp, Lp, SWA_KVH, SWA_HD),
            ckv3.reshape(Bp, Lp, KV_LORA), kr3[:, :QK_ROPE].reshape(Bp, Lp, QK_ROPE))
```

```python
import functools

import numpy as np
import jax
import jax.numpy as jnp
from jax import lax
from jax.experimental import pallas as pl
from jax.experimental.pallas import tpu as pltpu

BF = jnp.bfloat16
F32 = jnp.float32

D = 1024
EPS = 1e-6
NEG = -1e30
ROPE_THETA = 10000.0
GRID_W = 64

GLA_H, GLA_DK, GLA_DV, GLA_RANK, GLA_CHUNK = 4, 128, 256, 16, 64
GLA_SCALE = GLA_DK ** -0.5
GLA_INV_NORMALIZER = 1.0 / 16.0

SWA_H, SWA_KVH, SWA_HD, SWA_WINDOW, SWA_BLOCK = 16, 4, 64, 128, 128
SWA_SCALE = SWA_HD ** -0.5
SWA_NQ = SWA_H * SWA_HD
SWA_NK = SWA_KVH * SWA_HD

MLA_H, Q_LORA, KV_LORA, QK_NOPE, QK_ROPE, V_HEAD = 8, 384, 256, 128, 64, 128
MLA_SCALE = (QK_NOPE + QK_ROPE) ** -0.5
MLA_QW = 256

N_EXPERTS = 8
MOE_TM = 512
MOE_TF = 896
DENSE_TM = 1024
DENSE_TF = 256

VMEM_LIMIT = 56 * 1024 * 1024


def _cp(*sem):
    return pltpu.CompilerParams(dimension_semantics=sem, vmem_limit_bytes=VMEM_LIMIT)


def _mm(a, b):
    return jnp.dot(a, b, preferred_element_type=F32)


def _mm_nt(a, b):
    return lax.dot_general(a, b, (((1,), (1,)), ((), ())), preferred_element_type=F32)


def _mm_tn(a, b):
    return lax.dot_general(a, b, (((0,), (0,)), ((), ())), preferred_element_type=F32)


def _rms(x):
    return x * lax.rsqrt(jnp.mean(x * x, axis=-1, keepdims=True) + EPS)


def _silu(x):
    return x * (1.0 / (1.0 + jnp.exp(-x)))


def _modulate(x, mod_ref, j):
    shift = mod_ref[0, 3 * j:3 * j + 1, :]
    scale = mod_ref[0, 3 * j + 1:3 * j + 2, :]
    return _rms(x) * (1.0 + scale) + shift


def _gate(mod_ref, j):
    return mod_ref[0, 3 * j + 2:3 * j + 3, :]


def _const_spec(a):
    nd = a.ndim
    return pl.BlockSpec(a.shape, lambda *_: (0,) * nd)


def _mod_spec(tm, rows_per_mod):
    return pl.BlockSpec((1, 6, D), lambda i, *_: ((i * tm) // rows_per_mod, 0, 0))


def _adaln_kernel(c_ref, w_ref, b_ref, o_ref):
    s = _silu(c_ref[...]).astype(BF)
    o_ref[0] = _mm(s, w_ref[0].astype(BF)) + b_ref[0]


def adaln_all(cvec8, w_mod, b_mod):
    nl, d, n = w_mod.shape
    tn = 1536
    return pl.pallas_call(
        _adaln_kernel,
        grid=(nl, n // tn),
        in_specs=[pl.BlockSpec((8, d), lambda l, j: (0, 0)),
                  pl.BlockSpec((1, d, tn), lambda l, j: (l, 0, j)),
                  pl.BlockSpec((1, 1, tn), lambda l, j: (l, 0, j))],
        out_specs=pl.BlockSpec((1, 8, tn), lambda l, j: (l, 0, j)),
        out_shape=jax.ShapeDtypeStruct((nl, 8, n), F32),
        compiler_params=_cp("arbitrary", "arbitrary"),
        name="adaln",
    )(cvec8, w_mod, b_mod.reshape(nl, 1, n))


def _split3(x):
    hi = x.astype(BF)
    rem = x - hi.astype(F32)
    mid = rem.astype(BF)
    lo = (rem - mid.astype(F32)).astype(BF)
    return hi, mid, lo


def _mm3(a, pieces):
    return _mm(a, pieces[0]) + _mm(a, pieces[1]) + _mm(a, pieces[2])


def _gla_pre_kernel(x_ref, mod_ref, wp_ref, wlr_ref, wgk2_ref, bgk_ref,
                    qt_ref, kt_ref, kd_ref, v_ref, g_ref, dec_ref):
    tm = x_ref.shape[0]
    C = GLA_CHUNK
    G = 256
    h = _modulate(x_ref[...], mod_ref, 0).astype(BF)
    nk = GLA_H * GLA_DK
    nv = GLA_H * GLA_DV
    q = _mm(h, wp_ref[:, 0:nk]) * GLA_SCALE
    k = _mm(h, wp_ref[:, nk:2 * nk])
    v_ref[...] = _mm(h, wp_ref[:, 2 * nk:2 * nk + nv]).astype(BF)
    g_ref[...] = _mm(h, wp_ref[:, 2 * nk + nv:2 * nk + 2 * nv])
    lr = _mm(h, wlr_ref[...]).astype(BF)
    ii = lax.broadcasted_iota(jnp.int32, (G, G), 0)
    jj = lax.broadcasted_iota(jnp.int32, (G, G), 1)
    same = (ii // C) == (jj // C)
    ones_blk = jnp.where(same, 1.0, 0.0).astype(BF)
    ci = lax.broadcasted_iota(jnp.int32, (tm // C, tm), 0)
    cj = lax.broadcasted_iota(jnp.int32, (tm // C, tm), 1)
    sel = jnp.where((cj // C) == ci, 1.0, 0.0).astype(BF)
    for r in range(2):
        z = _mm(lr, wgk2_ref[r]) + bgk_ref[r:r + 1, :]
        gk = (jnp.minimum(z, 0.0) - jnp.log1p(jnp.exp(-jnp.abs(z)))) * GLA_INV_NORMALIZER
        pieces = _split3(gk)
        tri = jnp.where(jnp.logical_and(same, (jj <= ii) if r == 0 else (jj >= ii)), 1.0, 0.0).astype(BF)
        dec_ref[r] = jnp.exp(_mm3(sel, pieces))
        for gi in range(tm // G):
            rows = slice(gi * G, (gi + 1) * G)
            pg = tuple(p[rows] for p in pieces)
            b = _mm3(tri, pg)
            tot = _mm3(ones_blk, pg)
            qt_ref[r, rows, :] = (q[rows] * jnp.exp(b)).astype(BF)
            kt_ref[r, rows, :] = (k[rows] * jnp.exp(-b)).astype(BF)
            kd_ref[r, rows, :] = (k[rows] * jnp.exp(tot - b)).astype(BF)


def _gla_scan_kernel(*refs, L, has_h0, emit_state):
    qt_ref, kt_ref, kd_ref, v_ref, g_ref, dec_ref, gn_ref = refs[:7]
    pos = 7
    h0_ref = None
    if has_h0:
        h0_ref = refs[pos]
        pos += 1
    y_ref = refs[pos]
    pos += 1
    st_ref = None
    if emit_state:
        st_ref = refs[pos]
        pos += 1
    st_s, of_s, ob_s = refs[pos], refs[pos + 1], refs[pos + 2]

    C = GLA_CHUNK
    n_chunks = L // C
    ii = lax.broadcasted_iota(jnp.int32, (C, C), 0)
    jj = lax.broadcasted_iota(jnp.int32, (C, C), 1)
    keeps = ((jj <= ii), (jj >= ii))
    for r in range(2):
        for hd in range(GLA_H):
            if has_h0:
                st_s[r * GLA_H + hd] = h0_ref[0, r, hd].T
            else:
                st_s[r * GLA_H + hd] = jnp.zeros((GLA_DV, GLA_DK), F32)

    def step(i):
        for r in range(2):
            n = i if r == 0 else n_chunks - 1 - i
            start = n * C
            if not isinstance(start, int):
                start = pl.multiple_of(start, C)
            rows = pl.ds(start, C)
            o_s = of_s if r == 0 else ob_s
            for hd in range(GLA_H):
                kc = slice(hd * GLA_DK, (hd + 1) * GLA_DK)
                vc = slice(hd * GLA_DV, (hd + 1) * GLA_DV)
                qt = qt_ref[r, rows, kc]
                v = v_ref[rows, vc]
                att = jnp.where(keeps[r], _mm_nt(qt, kt_ref[r, rows, kc]), 0.0).astype(BF)
                st = st_s[r * GLA_H + hd]
                o_s[rows, vc] = _mm_nt(qt, st.astype(BF)) + _mm(att, v)
                st_s[r * GLA_H + hd] = st * dec_ref[r, n, :, kc] + _mm_tn(v, kd_ref[r, rows, kc])

    if n_chunks <= 4:
        for i in range(n_chunks):
            step(i)
    else:
        def body(i, carry):
            step(i)
            return carry
        lax.fori_loop(0, n_chunks, body, 0)

    RC = 256
    for c in range(L // RC):
        rows = slice(c * RC, (c + 1) * RC)
        for hd in range(GLA_H):
            vc = slice(hd * GLA_DV, (hd + 1) * GLA_DV)
            y = _rms(of_s[rows, vc] + ob_s[rows, vc]) * gn_ref[...]
            y_ref[rows, vc] = (y * _silu(g_ref[rows, vc])).astype(BF)
    if emit_state:
        for r in range(2):
            for hd in range(GLA_H):
                st_ref[0, r, hd] = st_s[r * GLA_H + hd].T


def _residual_out_kernel(y_ref, x_ref, mod_ref, w_ref, o_ref, *, gate_j):
    o_ref[...] = x_ref[...] + _gate(mod_ref, gate_j) * _mm(y_ref[...], w_ref[...])


def gla_mixer(x, mod, w, L, h0):
    M = x.shape[0]
    B = M // L
    tm = 512
    nk, nv = GLA_H * GLA_DK, GLA_H * GLA_DV
    row = lambda n: pl.BlockSpec((tm, n), lambda i: (i, 0))
    dirs = lambda: pl.BlockSpec((2, tm, nk), lambda i: (0, i, 0))
    n_chunks = L // GLA_CHUNK
    qt, kt, kd, v, g, dec = pl.pallas_call(
        _gla_pre_kernel,
        grid=(M // tm,),
        in_specs=[row(D), _mod_spec(tm, L if mod.shape[0] > 1 else M),
                  _const_spec(w["wp"]), _const_spec(w["wlr"]), _const_spec(w["wgk2"]),
                  _const_spec(w["bgk"])],
        out_specs=[dirs(), dirs(), dirs(), row(nv), row(nv),
                   pl.BlockSpec((2, tm // GLA_CHUNK, nk), lambda i: (0, i, 0))],
        out_shape=[jax.ShapeDtypeStruct((2, M, nk), BF), jax.ShapeDtypeStruct((2, M, nk), BF),
                   jax.ShapeDtypeStruct((2, M, nk), BF),
                   jax.ShapeDtypeStruct((M, nv), BF), jax.ShapeDtypeStruct((M, nv), F32),
                   jax.ShapeDtypeStruct((2, M // GLA_CHUNK, nk), F32)],
        compiler_params=_cp("arbitrary"),
        name="gla_pre",
    )(x, mod, w["wp"], w["wlr"], w["wgk2"], w["bgk"])
    dec = dec.reshape(2, B, n_chunks, 1, nk)

    has_h0 = h0 is not None
    emit_state = not has_h0
    dir_spec = lambda: pl.BlockSpec((2, L, nk), lambda b: (0, b, 0))
    in_specs = [dir_spec(), dir_spec(), dir_spec(),
                pl.BlockSpec((L, nv), lambda b: (b, 0)),
                pl.BlockSpec((L, nv), lambda b: (b, 0)),
                pl.BlockSpec((2, None, n_chunks, 1, nk), lambda b: (0, b, 0, 0, 0)),
                pl.BlockSpec((1, GLA_DV), lambda b: (0, 0))]
    args = [qt, kt, kd, v, g, dec, w["gnorm"]]
    st_spec = pl.BlockSpec((1, 2, GLA_H, GLA_DK, GLA_DV), lambda b: (b, 0, 0, 0, 0))
    if has_h0:
        in_specs.append(st_spec)
        args.append(h0)
    out_specs = [pl.BlockSpec((L, nv), lambda b: (b, 0))]
    out_shape = [jax.ShapeDtypeStruct((M, nv), BF)]
    if emit_state:
        out_specs.append(st_spec)
        out_shape.append(jax.ShapeDtypeStruct((B, 2, GLA_H, GLA_DK, GLA_DV), F32))
    res = pl.pallas_call(
        functools.partial(_gla_scan_kernel, L=L, has_h0=has_h0, emit_state=emit_state),
        grid=(B,),
        in_specs=in_specs,
        out_specs=out_specs,
        out_shape=out_shape,
        scratch_shapes=[pltpu.VMEM((2 * GLA_H, GLA_DV, GLA_DK), F32), pltpu.VMEM((L, nv), F32),
                        pltpu.VMEM((L, nv), F32)],
        compiler_params=_cp("arbitrary"),
        name="gla_scan",
    )(*args)
    y = res[0]
    state = res[1] if emit_state else None

    x_new = pl.pallas_call(
        functools.partial(_residual_out_kernel, gate_j=0),
        grid=(M // tm,),
        in_specs=[row(nv), row(D), _mod_spec(tm, L if mod.shape[0] > 1 else M), _const_spec(w["wo"])],
        out_specs=row(D),
        out_shape=jax.ShapeDtypeStruct((M, D), F32),
        compiler_params=_cp("arbitrary"),
        name="gla_post",
    )(y, x, mod, w["wo"])
    return x_new, state


def _softmax_sink_heads(q_of, k_of, v_of, sink_ref, o_s):
    for hq in range(SWA_H):
        kh = hq // (SWA_H // SWA_KVH)
        s = _mm_nt(q_of(hq), k_of(kh))
        sink = sink_ref[hq]
        m = jnp.maximum(jnp.max(s, axis=-1, keepdims=True), sink)
        e = jnp.exp(s - m)
        p = e / (jnp.sum(e, axis=-1, keepdims=True) + jnp.exp(sink - m))
        o_s[:, hq * SWA_HD:(hq + 1) * SWA_HD] = _mm(p.astype(BF), v_of(kh)).astype(BF)


def _swa_ctx_kernel(sink_ref, x_ref, mod_ref, wqkv_ref, wo_ref, o_ref, k_out, v_out, o_s):
    x = x_ref[...]
    h = _modulate(x, mod_ref, 0).astype(BF)
    qkv = _mm(h, wqkv_ref[...])
    k_out[...] = qkv[:, SWA_NQ:SWA_NQ + SWA_NK]
    v_out[...] = qkv[:, SWA_NQ + SWA_NK:]
    q_of = lambda hq: (qkv[:, hq * SWA_HD:(hq + 1) * SWA_HD] * SWA_SCALE).astype(BF)
    k_of = lambda kh: qkv[:, SWA_NQ + kh * SWA_HD:SWA_NQ + (kh + 1) * SWA_HD].astype(BF)
    v_of = lambda kh: qkv[:, SWA_NQ + SWA_NK + kh * SWA_HD:SWA_NQ + SWA_NK + (kh + 1) * SWA_HD].astype(BF)
    _softmax_sink_heads(q_of, k_of, v_of, sink_ref, o_s)
    o_ref[...] = x + _gate(mod_ref, 0) * _mm(o_s[...], wo_ref[...])


def swa_ctx_mixer(x, mod, w, L):
    M = x.shape[0]
    row = lambda n: pl.BlockSpec((L, n), lambda i: (i, 0))
    return pl.pallas_call(
        _swa_ctx_kernel,
        grid=(M // L,),
        in_specs=[pl.BlockSpec(memory_space=pltpu.SMEM), row(D), _mod_spec(L, M),
                  _const_spec(w["wqkv"]), _const_spec(w["wo"])],
        out_specs=[row(D), row(SWA_NK), row(SWA_NK)],
        out_shape=[jax.ShapeDtypeStruct((M, D), F32), jax.ShapeDtypeStruct((M, SWA_NK), F32),
                   jax.ShapeDtypeStruct((M, SWA_NK), F32)],
        scratch_shapes=[pltpu.VMEM((L, SWA_NQ), BF)],
        compiler_params=_cp("arbitrary"),
        name="swa_ctx",
    )(w["sink"], x, mod, w["wqkv"], w["wo"])


def _swa_lat_kernel(sink_ref, x_ref, mod_ref, wqkv_ref, wsw_ref, cos_ref, sin_ref, kc_ref, vc_ref,
                    wo_ref, o_ref, q_s, k_s, v_s, o_s, *, L):
    n = pl.program_id(1)
    RC = 256

    @pl.when(n == 0)
    def _():
        for c in range(L // RC):
            rows = slice(c * RC, (c + 1) * RC)
            h = _modulate(x_ref[rows, :], mod_ref, 0).astype(BF)
            qkv = _mm(h, wqkv_ref[...])
            sw = _mm(h, wsw_ref[...])
            cos = cos_ref[rows, :]
            sin = sin_ref[rows, :]
            cq = jnp.concatenate([cos] * (SWA_NQ // 128), axis=1)
            sq = jnp.concatenate([sin] * (SWA_NQ // 128), axis=1)
            ck = jnp.concatenate([cos] * (SWA_NK // 128), axis=1)
            sk = jnp.concatenate([sin] * (SWA_NK // 128), axis=1)
            q = (qkv[:, :SWA_NQ] * SWA_SCALE) * cq + (sw[:, :SWA_NQ] * SWA_SCALE) * sq
            q_s[rows, :] = q.astype(BF)
            k = qkv[:, SWA_NQ:SWA_NQ + SWA_NK] * ck + sw[:, SWA_NQ:SWA_NQ + SWA_NK] * sk
            k_s[rows, :] = k.astype(BF)
            v_s[rows, :] = qkv[:, SWA_NQ + SWA_NK:].astype(BF)

    QB = SWA_BLOCK
    KW = 3 * SWA_BLOCK
    r0 = pl.multiple_of(n * QB, QB)
    ws = pl.multiple_of(jnp.clip((n - 1) * QB, 0, L - KW), QB)
    qpos = r0 + lax.broadcasted_iota(jnp.int32, (QB, KW), 0)
    kpos = ws + lax.broadcasted_iota(jnp.int32, (QB, KW), 1)
    valid = jnp.abs(kpos - qpos) <= SWA_WINDOW
    for hq in range(SWA_H):
        kh = hq // (SWA_H // SWA_KVH)
        hs = slice(kh * SWA_HD, (kh + 1) * SWA_HD)
        q = q_s[pl.ds(r0, QB), hq * SWA_HD:(hq + 1) * SWA_HD]
        s1 = jnp.where(valid, _mm_nt(q, k_s[pl.ds(ws, KW), hs]), NEG)
        s2 = _mm_nt(q, kc_ref[0, :, hs].astype(BF))
        sink = sink_ref[hq]
        m = jnp.maximum(jnp.maximum(jnp.max(s1, axis=-1, keepdims=True),
                                    jnp.max(s2, axis=-1, keepdims=True)), sink)
        e1 = jnp.exp(s1 - m)
        e2 = jnp.exp(s2 - m)
        den = (jnp.sum(e1, axis=-1, keepdims=True) + jnp.sum(e2, axis=-1, keepdims=True)
               + jnp.exp(sink - m))
        o = (_mm((e1 / den).astype(BF), v_s[pl.ds(ws, KW), hs])
             + _mm((e2 / den).astype(BF), vc_ref[0, :, hs].astype(BF)))
        o_s[:, hq * SWA_HD:(hq + 1) * SWA_HD] = o.astype(BF)
    o_ref[...] = x_ref[pl.ds(r0, QB), :] + _gate(mod_ref, 0) * _mm(o_s[...], wo_ref[...])


def swa_lat_mixer(x, mod, w, L, kc, vc, cos128, sin128):
    M = x.shape[0]
    B = M // L
    P = kc.shape[1]
    nb = L // SWA_BLOCK
    return pl.pallas_call(
        functools.partial(_swa_lat_kernel, L=L),
        grid=(B, nb),
        in_specs=[pl.BlockSpec(memory_space=pltpu.SMEM),
                  pl.BlockSpec((L, D), lambda b, n: (b, 0)),
                  pl.BlockSpec((1, 6, D), lambda b, n: (b, 0, 0)),
                  _const_spec(w["wqkv"]), _const_spec(w["wsw"]),
                  _const_spec(cos128), _const_spec(sin128),
                  pl.BlockSpec((1, P, SWA_NK), lambda b, n: (b, 0, 0)),
                  pl.BlockSpec((1, P, SWA_NK), lambda b, n: (b, 0, 0)),
                  _const_spec(w["wo"])],
        out_specs=pl.BlockSpec((SWA_BLOCK, D), lambda b, n: (b * nb + n, 0)),
        out_shape=jax.ShapeDtypeStruct((M, D), F32),
        scratch_shapes=[pltpu.VMEM((L, SWA_NQ), BF), pltpu.VMEM((L, SWA_NK), BF),
                        pltpu.VMEM((L, SWA_NK), BF), pltpu.VMEM((SWA_BLOCK, SWA_NQ), BF)],
        compiler_params=_cp("arbitrary", "arbitrary"),
        name="swa_lat",
    )(w["sink"], x, mod, w["wqkv"], w["wsw"], cos128, sin128, kc, vc, w["wo"])


def _conv_kernel(x_ref, mod_ref, win_ref, cw_ref, wout_ref, o_ref, y_s, *, L):
    x = x_ref[...]
    h = _modulate(x, mod_ref, 0).astype(BF)
    CC = 256
    row = lax.broadcasted_iota(jnp.int32, (L, CC), 0)
    for c in range(D // CC):
        cols = slice(c * CC, (c + 1) * CC)
        bg = _mm(h, win_ref[:, c * CC:(c + 1) * CC])
        cg = _mm(h, win_ref[:, D + c * CC:D + (c + 1) * CC])
        u = _mm(h, win_ref[:, 2 * D + c * CC:2 * D + (c + 1) * CC])
        cu = cg * u
        prev = jnp.where(row == 0, 0.0, pltpu.roll(cu, 1, 0))
        nxt = jnp.where(row == L - 1, 0.0, pltpu.roll(cu, L - 1, 0))
        conv = prev * cw_ref[0:1, cols] + cu * cw_ref[1:2, cols] + nxt * cw_ref[2:3, cols]
        y_s[:, cols] = (bg * conv).astype(BF)
    o_ref[...] = x + _gate(mod_ref, 0) * _mm(y_s[...], wout_ref[...])


def conv_mixer(x, mod, w, L):
    M = x.shape[0]
    row = pl.BlockSpec((L, D), lambda i: (i, 0))
    return pl.pallas_call(
        functools.partial(_conv_kernel, L=L),
        grid=(M // L,),
        in_specs=[row, _mod_spec(L, L if mod.shape[0] > 1 else M), _const_spec(w["win"]),
                  _const_spec(w["cw"]), _const_spec(w["wout"])],
        out_specs=row,
        out_shape=jax.ShapeDtypeStruct((M, D), F32),
        scratch_shapes=[pltpu.VMEM((L, D), BF)],
        compiler_params=_cp("arbitrary"),
        name="conv_mix",
    )(x, mod, w["win"], w["cw"], w["wout"])


def _mla_ctx_kernel(x_ref, mod_ref, wdq_ref, qn_ref, wuq_ref, wdkv_ref, kvn_ref, wukv_ref, wo_ref,
                    o_ref, ckv_out, kr_out, o_s):
    x = x_ref[...]
    h = _modulate(x, mod_ref, 0).astype(BF)
    cq = (_rms(_mm(h, wdq_ref[...])) * qn_ref[...]).astype(BF)
    q = _mm(cq, wuq_ref[...]).astype(BF)
    kvc = _mm(h, wdkv_ref[...])
    ckv = _rms(kvc[:, :KV_LORA]) * kvn_ref[...]
    kr = kvc[:, KV_LORA:]
    ckv_out[...] = ckv
    kr_out[...] = kr
    kv = _mm(ckv.astype(BF), wukv_ref[...]).astype(BF)
    krb = kr.astype(BF)
    nn = MLA_H * QK_NOPE
    for hh in range(MLA_H):
        qh = q[:, hh * MLA_QW:(hh + 1) * MLA_QW]
        kh = jnp.concatenate([kv[:, hh * QK_NOPE:(hh + 1) * QK_NOPE], krb], axis=1)
        s = _mm_nt(qh, kh) * MLA_SCALE
        m = jnp.max(s, axis=-1, keepdims=True)
        e = jnp.exp(s - m)
        p = e / jnp.sum(e, axis=-1, keepdims=True)
        vh = kv[:, nn + hh * V_HEAD:nn + (hh + 1) * V_HEAD]
        o_s[:, hh * V_HEAD:(hh + 1) * V_HEAD] = _mm(p.astype(BF), vh).astype(BF)
    o_ref[...] = x + _gate(mod_ref, 0) * _mm(o_s[...], wo_ref[...])


def mla_ctx_mixer(x, mod, w, L):
    M = x.shape[0]
    row = lambda n: pl.BlockSpec((L, n), lambda i: (i, 0))
    ws = [w["wdq"], w["qnorm"], w["wuq"], w["wdkv"], w["kvnorm"], w["wukv"], w["wo"]]
    return pl.pallas_call(
        _mla_ctx_kernel,
        grid=(M // L,),
        in_specs=[row(D), _mod_spec(L, M)] + [_const_spec(a) for a in ws],
        out_specs=[row(D), row(KV_LORA), row(128)],
        out_shape=[jax.ShapeDtypeStruct((M, D), F32), jax.ShapeDtypeStruct((M, KV_LORA), F32),
                   jax.ShapeDtypeStruct((M, 128), F32)],
        scratch_shapes=[pltpu.VMEM((L, MLA_H * V_HEAD), BF)],
        compiler_params=_cp("arbitrary"),
        name="mla_ctx",
    )(x, mod, *ws)


def _mla_lat_kernel(x_ref, mod_ref, wdq_ref, qn_ref, wuq_ref, wuqsw_ref, wdkv_ref, wdkvsw_ref,
                    kvn_ref, wukv_ref, cq_ref, sq_ref, ck_ref, sk_ref, ckvc_ref, krc_ref, wo_ref,
                    o_ref, q_s, kn_s, v_s, kr_s, knc_s, vc_s, o_s, *, L, QB):
    n = pl.program_id(1)
    RC = 256
    nn = MLA_H * QK_NOPE

    @pl.when(n == 0)
    def _():
        for c in range(L // RC):
            rows = slice(c * RC, (c + 1) * RC)
            h = _modulate(x_ref[rows, :], mod_ref, 0).astype(BF)
            cq = (_rms(_mm(h, wdq_ref[...])) * qn_ref[...]).astype(BF)
            cosq = jnp.concatenate([cq_ref[rows, :]] * MLA_H, axis=1)
            sinq = jnp.concatenate([sq_ref[rows, :]] * MLA_H, axis=1)
            q = _mm(cq, wuq_ref[...]) * cosq + _mm(cq, wuqsw_ref[...]) * sinq
            q_s[rows, :] = q.astype(BF)
            kvc = _mm(h, wdkv_ref[...])
            ksw = _mm(h, wdkvsw_ref[...])
            ckv = _rms(kvc[:, :KV_LORA]) * kvn_ref[...]
            kr_s[rows, :] = (kvc[:, KV_LORA:] * ck_ref[rows, :] + ksw * sk_ref[rows, :]).astype(BF)
            kv = _mm(ckv.astype(BF), wukv_ref[...])
            kn_s[rows, :] = kv[:, :nn].astype(BF)
            v_s[rows, :] = kv[:, nn:].astype(BF)
        kvp = _mm(ckvc_ref[0].astype(BF), wukv_ref[...])
        knc_s[...] = kvp[:, :nn].astype(BF)
        vc_s[...] = kvp[:, nn:].astype(BF)

    r0 = pl.multiple_of(n * QB, QB)
    krc = krc_ref[0].astype(BF)
    for hh in range(MLA_H):
        ns = slice(hh * QK_NOPE, (hh + 1) * QK_NOPE)
        qh = q_s[pl.ds(r0, QB), hh * MLA_QW:(hh + 1) * MLA_QW]
        k1 = jnp.concatenate([kn_s[:, ns], kr_s[...]], axis=1)
        k2 = jnp.concatenate([knc_s[:, ns], krc], axis=1)
        s1 = _mm_nt(qh, k1) * MLA_SCALE
        s2 = _mm_nt(qh, k2) * MLA_SCALE
        m = jnp.maximum(jnp.max(s1, axis=-1, keepdims=True), jnp.max(s2, axis=-1, keepdims=True))
        e1 = jnp.exp(s1 - m)
        e2 = jnp.exp(s2 - m)
        den = jnp.sum(e1, axis=-1, keepdims=True) + jnp.sum(e2, axis=-1, keepdims=True)
        vs = slice(hh * V_HEAD, (hh + 1) * V_HEAD)
        o = _mm((e1 / den).astype(BF), v_s[:, vs]) + _mm((e2 / den).astype(BF), vc_s[:, vs])
        o_s[:, vs] = o.astype(BF)
    o_ref[...] = x_ref[pl.ds(r0, QB), :] + _gate(mod_ref, 0) * _mm(o_s[...], wo_ref[...])


def mla_lat_mixer(x, mod, w, L, ckv_c, kr_c128, tabs):
    M = x.shape[0]
    B = M // L
    P = ckv_c.shape[1]
    QB = 256
    nb = L // QB
    ws1 = [w["wdq"], w["qnorm"], w["wuq"], w["wuqsw"], w["wdkv"], w["wdkvsw"], w["kvnorm"], w["wukv"]]
    nv = MLA_H * V_HEAD
    return pl.pallas_call(
        functools.partial(_mla_lat_kernel, L=L, QB=QB),
        grid=(B, nb),
        in_specs=[pl.BlockSpec((L, D), lambda b, n: (b, 0)),
                  pl.BlockSpec((1, 6, D), lambda b, n: (b, 0, 0))]
                 + [_const_spec(a) for a in ws1] + [_const_spec(a) for a in tabs]
                 + [pl.BlockSpec((1, P, KV_LORA), lambda b, n: (b, 0, 0)),
                    pl.BlockSpec((1, P, 128), lambda b, n: (b, 0, 0)),
                    _const_spec(w["wo"])],
        out_specs=pl.BlockSpec((QB, D), lambda b, n: (b * nb + n, 0)),
        out_shape=jax.ShapeDtypeStruct((M, D), F32),
        scratch_shapes=[pltpu.VMEM((L, MLA_H * MLA_QW), BF), pltpu.VMEM((L, nv), BF),
                        pltpu.VMEM((L, nv), BF), pltpu.VMEM((L, 128), BF),
                        pltpu.VMEM((P, nv), BF), pltpu.VMEM((P, nv), BF), pltpu.VMEM((QB, nv), BF)],
        compiler_params=_cp("arbitrary", "arbitrary"),
        name="mla_lat",
    )(x, mod, *ws1, *tabs, ckv_c, kr_c128, w["wo"])


def _swiglu_partial(xb, wg_ref, wu_ref, wd_ref):
    hg = _mm(xb, wg_ref[...].astype(BF))
    hu = _mm(xb, wu_ref[...].astype(BF))
    a = (_silu(hg) * hu).astype(BF)
    return _mm(a, wd_ref[...].astype(BF))


def _dense_ffn_kernel(x_ref, mod_ref, wg_ref, wu_ref, wd_ref, o_ref, xm_s):
    f = pl.program_id(1)

    @pl.when(f == 0)
    def _():
        xm_s[...] = _modulate(x_ref[...], mod_ref, 1).astype(BF)
        o_ref[...] = jnp.zeros_like(o_ref)

    o_ref[...] += _swiglu_partial(xm_s[...], wg_ref, wu_ref, wd_ref)

    @pl.when(f == pl.num_programs(1) - 1)
    def _():
        o_ref[...] = x_ref[...] + _gate(mod_ref, 1) * o_ref[...]


def dense_ffn(x, mod, wg, wu, wd, j, L):
    M = x.shape[0]
    F = wg.shape[-1]
    tm, tf = DENSE_TM, DENSE_TF
    return pl.pallas_call(
        _dense_ffn_kernel,
        grid=(M // tm, F // tf),
        in_specs=[pl.BlockSpec((tm, D), lambda i, f: (i, 0)),
                  pl.BlockSpec((1, 6, D), lambda i, f: ((i * tm) // (L if mod.shape[0] > 1 else M), 0, 0)),
                  pl.BlockSpec((None, D, tf), lambda i, f: (j, 0, f)),
                  pl.BlockSpec((None, D, tf), lambda i, f: (j, 0, f)),
                  pl.BlockSpec((None, tf, D), lambda i, f: (j, f, 0))],
        out_specs=pl.BlockSpec((tm, D), lambda i, f: (i, 0)),
        out_shape=jax.ShapeDtypeStruct((M, D), F32),
        scratch_shapes=[pltpu.VMEM((tm, D), BF)],
        compiler_params=_cp("arbitrary", "arbitrary"),
        name="dense_ffn",
    )(x, mod, wg, wu, wd)


def _router_kernel(xp_ref, xs_ref, mod_ref, wr_ref, xm_ref, idx_ref, w_ref, cnt_ref, cnt_s, *, np_tiles):
    i = pl.program_id(0)
    tm = xm_ref.shape[0]

    @pl.when(i == 0)
    def _():
        cnt_s[...] = jnp.zeros_like(cnt_s)

    x = jnp.where(i < np_tiles, xp_ref[...], xs_ref[...])
    xm = _modulate(x, mod_ref, 1).astype(BF)
    xm_ref[...] = xm
    lane = lax.broadcasted_iota(jnp.int32, idx_ref.shape, 1)
    logits = jnp.where(lane < N_EXPERTS, _mm(xm, wr_ref[...]), -jnp.inf)
    m1 = jnp.max(logits, axis=-1, keepdims=True)
    i1 = jnp.min(jnp.where(logits == m1, lane, 128), axis=-1, keepdims=True)
    rest = jnp.where(lane == i1, -jnp.inf, logits)
    m2 = jnp.max(rest, axis=-1, keepdims=True)
    i2 = jnp.min(jnp.where(rest == m2, lane, 128), axis=-1, keepdims=True)
    e = jnp.exp(m2 - m1)
    w_ref[...] = jnp.where(lane == 0, 1.0 / (1.0 + e), e / (1.0 + e))
    oh1 = jnp.where(lane == i1, 1.0, 0.0)
    oh2 = jnp.where(lane == i2, 1.0, 0.0)
    rr = lax.broadcasted_iota(jnp.int32, (tm, tm), 0)
    cc = lax.broadcasted_iota(jnp.int32, (tm, tm), 1)
    below = jnp.where(cc < rr, 1.0, 0.0).astype(BF)
    run = cnt_s[0:1, :]
    tot1 = jnp.sum(oh1, axis=0, keepdims=True)
    tot2 = jnp.sum(oh2, axis=0, keepdims=True)
    r1 = jnp.sum(oh1 * (run + _mm(below, oh1.astype(BF))), axis=-1, keepdims=True)
    r2 = jnp.sum(oh2 * (run + tot1 + _mm(below, oh2.astype(BF))), axis=-1, keepdims=True)
    idx_ref[...] = jnp.where(lane == 0, i1, jnp.where(lane == 1, i2, jnp.where(
        lane == 2, r1.astype(jnp.int32), r2.astype(jnp.int32))))
    total = run + tot1 + tot2
    cnt_s[...] = jnp.broadcast_to(total, cnt_s.shape)
    cnt_ref[...] = jnp.broadcast_to(total, cnt_ref.shape)


def route(xp, xs, mod3, wr, Ls):
    Mp, Ms = xp.shape[0], xs.shape[0]
    T = Mp + Ms
    tm = 512
    npt = Mp // tm
    row = lambda n: pl.BlockSpec((tm, n), lambda i: (i, 0))
    return pl.pallas_call(
        functools.partial(_router_kernel, np_tiles=npt),
        grid=(T // tm,),
        in_specs=[pl.BlockSpec((tm, D), lambda i: (jnp.minimum(i, npt - 1), 0)),
                  pl.BlockSpec((tm, D), lambda i: (jnp.maximum(i - npt, 0), 0)),
                  pl.BlockSpec((1, 6, D), lambda i: (jnp.where(i < npt, 0, 1 + ((i - npt) * tm) // Ls), 0, 0)),
                  _const_spec(wr)],
        out_specs=[row(D), row(128), row(128), pl.BlockSpec((8, 128), lambda i: (0, 0))],
        out_shape=[jax.ShapeDtypeStruct((T, D), BF), jax.ShapeDtypeStruct((T, 128), jnp.int32),
                   jax.ShapeDtypeStruct((T, 128), F32), jax.ShapeDtypeStruct((8, 128), F32)],
        scratch_shapes=[pltpu.VMEM((8, 128), F32)],
        compiler_params=_cp("arbitrary"),
        name="moe_route",
    )(xp, xs, mod3, wr)


def _moe_ffn_kernel(te_ref, nv_ref, x_ref, wg_ref, wu_ref, wd_ref, o_ref):
    i = pl.program_id(0)
    f = pl.program_id(1)

    @pl.when(f == 0)
    def _():
        o_ref[...] = jnp.zeros_like(o_ref)

    @pl.when(i < nv_ref[0])
    def _():
        o_ref[...] += _swiglu_partial(x_ref[...], wg_ref, wu_ref, wd_ref)


def moe_ffn(xs, tile_expert, n_valid, wg, wu, wd, j):
    R = xs.shape[0]
    F = wg.shape[-1]
    tm, tf = MOE_TM, MOE_TF
    nf = F // tf

    def fidx(i, f, nv):
        return jnp.where(i < nv[0], f, nf - 1)

    grid_spec = pltpu.PrefetchScalarGridSpec(
        num_scalar_prefetch=2,
        grid=(R // tm, nf),
        in_specs=[pl.BlockSpec((tm, D), lambda i, f, te, nv: (i, 0)),
                  pl.BlockSpec((None, None, D, tf), lambda i, f, te, nv: (j, te[i], 0, fidx(i, f, nv))),
                  pl.BlockSpec((None, None, D, tf), lambda i, f, te, nv: (j, te[i], 0, fidx(i, f, nv))),
                  pl.BlockSpec((None, None, tf, D), lambda i, f, te, nv: (j, te[i], fidx(i, f, nv), 0))],
        out_specs=pl.BlockSpec((tm, D), lambda i, f, te, nv: (i, 0)),
    )
    return pl.pallas_call(
        _moe_ffn_kernel,
        grid_spec=grid_spec,
        out_shape=jax.ShapeDtypeStruct((R, D), F32),
        compiler_params=_cp("arbitrary", "arbitrary"),
        name="moe_ffn",
    )(tile_expert, n_valid, xs, wg, wu, wd)


def _combine_kernel(x_ref, mod_ref, ya_ref, yb_ref, w_ref, fn_ref, o_ref, *, final):
    y = w_ref[:, 0:1] * ya_ref[...] + w_ref[:, 1:2] * yb_ref[...]
    o = x_ref[...] + _gate(mod_ref, 1) * y
    if final:
        o = _rms(o) * fn_ref[...]
    o_ref[...] = o


def moe_combine(x, mod, ya, yb, w, fn, L, final, row_off):
    M = x.shape[0]
    tm = 512
    off = row_off // tm
    row = pl.BlockSpec((tm, D), lambda i: (i, 0))
    row_o = lambda n: pl.BlockSpec((tm, n), lambda i: (i + off, 0))
    return pl.pallas_call(
        functools.partial(_combine_kernel, final=final),
        grid=(M // tm,),
        in_specs=[row, _mod_spec(tm, L if mod.shape[0] > 1 else M), row_o(D), row_o(D), row_o(128),
                  _const_spec(fn)],
        out_specs=row,
        out_shape=jax.ShapeDtypeStruct((M, D), F32),
        compiler_params=_cp("arbitrary"),
        name="moe_combine",
    )(x, mod, ya, yb, w, fn)


def moe_layer(xp, xs, mod3, wr, wg, wu, wd, j, fn, Lp, Ls, final):
    Mp = xp.shape[0]
    xm, idx, wts, cnt = route(xp, xs, mod3, wr, Ls)
    T = xm.shape[0]
    tm = MOE_TM
    R = 2 * T + N_EXPERTS * tm
    counts = cnt[0, :N_EXPERTS].astype(jnp.int32)
    tiles_per = (counts + tm - 1) // tm
    tile_end = jnp.cumsum(tiles_per)
    offs = (tile_end - tiles_per) * tm
    experts = jnp.arange(N_EXPERTS, dtype=jnp.int32)
    dest = jnp.sum(jnp.where(idx[:, 0:2, None] == experts, offs, 0), axis=-1) + idx[:, 2:4]
    tok = jnp.arange(2 * T, dtype=jnp.int32) // 2
    src = jnp.zeros((R,), jnp.int32).at[dest.reshape(-1)].set(
        tok, unique_indices=True, mode="promise_in_bounds")
    n_tiles = R // tm
    n_valid = tile_end[-1:].astype(jnp.int32)
    t_ids = jnp.arange(n_tiles, dtype=jnp.int32)
    te = jnp.sum((t_ids[:, None] >= tile_end[None, :]).astype(jnp.int32), axis=1)
    te_last = jnp.sum((n_valid - 1 >= tile_end).astype(jnp.int32))
    te = jnp.where(t_ids < n_valid[0], te, te_last).astype(jnp.int32)
    x_sorted = xm.at[src].get(mode="promise_in_bounds")
    y_sorted = moe_ffn(x_sorted, te, n_valid, wg, wu, wd, j)
    ya = y_sorted.at[dest[:, 0]].get(mode="promise_in_bounds")
    yb = y_sorted.at[dest[:, 1]].get(mode="promise_in_bounds")
    xp_new = moe_combine(xp, mod3[0:1], ya, yb, wts, fn, Lp, final, 0)
    xs_new = moe_combine(xs, mod3[1:], ya, yb, wts, fn, Ls, final, Mp)
    return xp_new, xs_new


def _rope_partner_perm(n):
    c = np.arange(n)
    return np.where((c % 32) < 16, c + 16, c - 16)


def _rope_tables(L):
    nf = 16
    inv = ROPE_THETA ** (-np.arange(nf, dtype=np.float32) / nf)
    pos = np.arange(L)
    ang = np.stack([(pos // GRID_W).astype(np.float32), (pos % GRID_W).astype(np.float32)],
                   axis=-1)[:, :, None] * inv
    cos = jnp.cos(jnp.asarray(ang, F32))
    sin = jnp.sin(jnp.asarray(ang, F32))
    c64 = jnp.stack([cos, cos], axis=2).reshape(L, 64)
    s64 = jnp.stack([-sin, sin], axis=2).reshape(L, 64)
    return c64, s64


def kernel(x_prompt, x_sample, state_l0_gla, cache_l1_k, cache_l1_v, cache_l3_ckv, cache_l3_krope, c, c_ctx, w_mod, b_mod, gla_wq, gla_wk, gla_wv, gla_wg, gla_wgk1, gla_wgk2, gla_bgk, gla_norm, gla_wo, swa_wqkv, swa_sink, swa_wo, conv_win, conv_w, conv_wout, mla_wdq, mla_qnorm, mla_wuq, mla_wdkv, mla_kvnorm, mla_wukv, mla_wo, dense_w_gate, dense_w_up, dense_w_down, moe_router, moe_w_gate, moe_w_up, moe_w_down, final_norm):
    Bp, Lp, _ = x_prompt.shape
    Bs, Ls, _ = x_sample.shape
    P = cache_l1_k.shape[1]
    xp = x_prompt.reshape(Bp * Lp, D)
    xs = x_sample.reshape(Bs * Ls, D)

    cvec = jnp.zeros((8, D), F32).at[0].set(c_ctx).at[1:1 + Bs].set(c)
    mod_all = adaln_all(cvec, w_mod, b_mod).reshape(w_mod.shape[0], 8, 6, D)

    wlr = jnp.zeros((D, 128), F32).at[:, :GLA_RANK].set(gla_wgk1[0]).at[:, GLA_RANK:2 * GLA_RANK].set(gla_wgk1[1])
    wgk2 = jnp.zeros((2, 128, GLA_H * GLA_DK), F32)
    wgk2 = wgk2.at[0, :GLA_RANK].set(gla_wgk2[0]).at[1, GLA_RANK:2 * GLA_RANK].set(gla_wgk2[1])
    gla_w = dict(wp=jnp.concatenate([gla_wq, gla_wk, gla_wv, gla_wg], axis=1).astype(BF),
                 wlr=wlr.astype(BF), wgk2=wgk2.astype(BF), bgk=gla_bgk,
                 gnorm=gla_norm.reshape(1, GLA_DV), wo=gla_wo.astype(BF))
    modp, mods = mod_all[0, 0:1], mod_all[0, 1:1 + Bs]
    xp, new_state = gla_mixer(xp, modp, gla_w, Lp, None)
    xs, _ = gla_mixer(xs, mods, gla_w, Ls, state_l0_gla)
    xp = dense_ffn(xp, modp, dense_w_gate, dense_w_up, dense_w_down, 0, Lp)
    xs = dense_ffn(xs, mods, dense_w_gate, dense_w_up, dense_w_down, 0, Ls)

    c64, s64 = _rope_tables(Ls)
    cos128 = jnp.concatenate([c64, c64], axis=1)
    sin128 = jnp.concatenate([s64, s64], axis=1)
    perm = _rope_partner_perm(SWA_NQ + SWA_NK)
    swa_w = dict(wqkv=swa_wqkv.astype(BF), wsw=swa_wqkv[:, perm].astype(BF), wo=swa_wo.astype(BF),
                 sink=swa_sink)
    modp, mods = mod_all[1, 0:1], mod_all[1, 1:1 + Bs]
    xp, k1, v1 = swa_ctx_mixer(xp, modp, swa_w, Lp)
    xs = swa_lat_mixer(xs, mods, swa_w, Ls, cache_l1_k.reshape(Bs, P, SWA_NK),
                       cache_l1_v.reshape(Bs, P, SWA_NK), cos128, sin128)
    wr = jnp.zeros((moe_router.shape[0], D, 128), F32).at[:, :, :N_EXPERTS].set(moe_router).astype(BF)
    xp, xs = moe_layer(xp, xs, mod_all[1, 0:1 + Bs], wr[0], moe_w_gate, moe_w_up, moe_w_down, 0,
                       final_norm.reshape(1, D), Lp, Ls, False)

    conv_wts = dict(win=conv_win.astype(BF), cw=conv_w, wout=conv_wout.astype(BF))
    modp, mods = mod_all[2, 0:1], mod_all[2, 1:1 + Bs]
    xp = conv_mixer(xp, modp, conv_wts, Lp)
    xs = conv_mixer(xs, mods, conv_wts, Ls)
    xp = dense_ffn(xp, modp, dense_w_gate, dense_w_up, dense_w_down, 1, Lp)
    xs = dense_ffn(xs, mods, dense_w_gate, dense_w_up, dense_w_down, 1, Ls)

    wuq3 = mla_wuq.reshape(Q_LORA, MLA_H, QK_NOPE + QK_ROPE)
    zpad = jnp.zeros((Q_LORA, MLA_H, MLA_QW - QK_NOPE - QK_ROPE), F32)
    wuq = jnp.concatenate([wuq3, zpad], axis=2).reshape(Q_LORA, MLA_H * MLA_QW)
    rperm = _rope_partner_perm(QK_ROPE)
    wuq_sw = jnp.concatenate([jnp.zeros((Q_LORA, MLA_H, QK_NOPE), F32), wuq3[:, :, QK_NOPE:][:, :, rperm], zpad],
                             axis=2).reshape(Q_LORA, MLA_H * MLA_QW)
    kpad = jnp.zeros((D, 128 - QK_ROPE), F32)
    wdkv = jnp.concatenate([mla_wdkv, kpad], axis=1)
    wdkv_sw = jnp.concatenate([mla_wdkv[:, KV_LORA:][:, rperm], kpad], axis=1)
    wukv3 = mla_wukv.reshape(KV_LORA, MLA_H, QK_NOPE + V_HEAD)
    wukv = jnp.concatenate([wukv3[:, :, :QK_NOPE].reshape(KV_LORA, MLA_H * QK_NOPE),
                            wukv3[:, :, QK_NOPE:].reshape(KV_LORA, MLA_H * V_HEAD)], axis=1)
    mla_w = dict(wdq=mla_wdq.astype(BF), qnorm=mla_qnorm.reshape(1, Q_LORA), wuq=wuq.astype(BF),
                 wuqsw=wuq_sw.astype(BF), wdkv=wdkv.astype(BF), wdkvsw=wdkv_sw.astype(BF),
                 kvnorm=mla_kvnorm.reshape(1, KV_LORA), wukv=wukv.astype(BF), wo=mla_wo.astype(BF))
    ones = jnp.ones((Ls, QK_NOPE), F32)
    z128 = jnp.zeros((Ls, QK_NOPE), F32)
    z64 = jnp.zeros((Ls, 64), F32)
    tabs = [jnp.concatenate([ones, c64, z64], axis=1), jnp.concatenate([z128, s64, z64], axis=1),
            jnp.concatenate([c64, z64], axis=1), jnp.concatenate([s64, z64], axis=1)]
    modp, mods = mod_all[3, 0:1], mod_all[3, 1:1 + Bs]
    xp, ckv3, kr3 = mla_ctx_mixer(xp, modp, mla_w, Lp)
    kr_c128 = jnp.concatenate([cache_l3_krope, jnp.zeros((Bs, P, 128 - QK_ROPE), F32)], axis=2)
    xs = mla_lat_mixer(xs, mods, mla_w, Ls, cache_l3_ckv, kr_c128, tabs)
    xp, xs = moe_layer(xp, xs, mod_all[3, 0:1 + Bs], wr[1], moe_w_gate, moe_w_up, moe_w_down, 1,
                       final_norm.reshape(1, D), Lp, Ls, True)

    return (xp.reshape(Bp, Lp, D), xs.reshape(Bs, Ls, D), new_state,
            k1.reshape(Bp, Lp, SWA_KVH, SWA_HD), v1.reshape(Bp, Lp, SWA_KVH, SWA_HD),
            ckv3.reshape(Bp, Lp, KV_LORA), kr3[:, :QK_ROPE].reshape(Bp, Lp, QK_ROPE))
```

```python
import functools

import numpy as np
import jax
import jax.numpy as jnp
from jax import lax
from jax.experimental import pallas as pl
from jax.experimental.pallas import tpu as pltpu

BF = jnp.bfloat16
F32 = jnp.float32

D = 1024
EPS = 1e-6
NEG = -1e30
ROPE_THETA = 10000.0
GRID_W = 64

GLA_H, GLA_DK, GLA_DV, GLA_RANK, GLA_CHUNK = 4, 128, 256, 16, 64
GLA_SCALE = GLA_DK ** -0.5
GLA_INV_NORMALIZER = 1.0 / 16.0

SWA_H, SWA_KVH, SWA_HD, SWA_WINDOW, SWA_BLOCK = 16, 4, 64, 128, 128
SWA_SCALE = SWA_HD ** -0.5
SWA_NQ = SWA_H * SWA_HD
SWA_NK = SWA_KVH * SWA_HD

MLA_H, Q_LORA, KV_LORA, QK_NOPE, QK_ROPE, V_HEAD = 8, 384, 256, 128, 64, 128
MLA_SCALE = (QK_NOPE + QK_ROPE) ** -0.5
MLA_QW = 256

N_EXPERTS = 8
MOE_TM = 512
MOE_TF = 512
DENSE_TM = 1024
DENSE_TF = 256

VMEM_LIMIT = 56 * 1024 * 1024


def _cp(*sem):
    return pltpu.CompilerParams(dimension_semantics=sem, vmem_limit_bytes=VMEM_LIMIT)


def _mm(a, b):
    return jnp.dot(a, b, preferred_element_type=F32)


def _mm_nt(a, b):
    return lax.dot_general(a, b, (((1,), (1,)), ((), ())), preferred_element_type=F32)


def _mm_tn(a, b):
    return lax.dot_general(a, b, (((0,), (0,)), ((), ())), preferred_element_type=F32)


def _rms(x):
    return x * lax.rsqrt(jnp.mean(x * x, axis=-1, keepdims=True) + EPS)


def _silu(x):
    return x * (1.0 / (1.0 + jnp.exp(-x)))


def _modulate(x, mod_ref, j):
    shift = mod_ref[0, 3 * j:3 * j + 1, :]
    scale = mod_ref[0, 3 * j + 1:3 * j + 2, :]
    return _rms(x) * (1.0 + scale) + shift


def _gate(mod_ref, j):
    return mod_ref[0, 3 * j + 2:3 * j + 3, :]


def _const_spec(a):
    nd = a.ndim
    return pl.BlockSpec(a.shape, lambda *_: (0,) * nd)


def _mod_spec(tm, rows_per_mod):
    return pl.BlockSpec((1, 6, D), lambda i, *_: ((i * tm) // rows_per_mod, 0, 0))


def _adaln_kernel(c_ref, w_ref, b_ref, o_ref):
    s = _silu(c_ref[...]).astype(BF)
    o_ref[0] = _mm(s, w_ref[0].astype(BF)) + b_ref[0]


def adaln_all(cvec8, w_mod, b_mod):
    nl, d, n = w_mod.shape
    tn = 1536
    return pl.pallas_call(
        _adaln_kernel,
        grid=(nl, n // tn),
        in_specs=[pl.BlockSpec((8, d), lambda l, j: (0, 0)),
                  pl.BlockSpec((1, d, tn), lambda l, j: (l, 0, j)),
                  pl.BlockSpec((1, 1, tn), lambda l, j: (l, 0, j))],
        out_specs=pl.BlockSpec((1, 8, tn), lambda l, j: (l, 0, j)),
        out_shape=jax.ShapeDtypeStruct((nl, 8, n), F32),
        compiler_params=_cp("arbitrary", "arbitrary"),
        name="adaln",
    )(cvec8, w_mod, b_mod.reshape(nl, 1, n))


def _split3(x):
    hi = x.astype(BF)
    rem = x - hi.astype(F32)
    mid = rem.astype(BF)
    lo = (rem - mid.astype(F32)).astype(BF)
    return hi, mid, lo


def _mm3(a, pieces):
    return _mm(a, pieces[0]) + _mm(a, pieces[1]) + _mm(a, pieces[2])


def _gla_pre_kernel(x_ref, mod_ref, wp_ref, wlr_ref, wgk2_ref, bgk_ref,
                    qt_ref, kt_ref, kd_ref, v_ref, g_ref, dec_ref):
    tm = x_ref.shape[0]
    C = GLA_CHUNK
    G = 256
    h = _modulate(x_ref[...], mod_ref, 0).astype(BF)
    nk = GLA_H * GLA_DK
    nv = GLA_H * GLA_DV
    q = _mm(h, wp_ref[:, 0:nk]) * GLA_SCALE
    k = _mm(h, wp_ref[:, nk:2 * nk])
    v_ref[...] = _mm(h, wp_ref[:, 2 * nk:2 * nk + nv]).astype(BF)
    g_ref[...] = _mm(h, wp_ref[:, 2 * nk + nv:2 * nk + 2 * nv])
    lr = _mm(h, wlr_ref[...]).astype(BF)
    ii = lax.broadcasted_iota(jnp.int32, (G, G), 0)
    jj = lax.broadcasted_iota(jnp.int32, (G, G), 1)
    same = (ii // C) == (jj // C)
    ones_blk = jnp.where(same, 1.0, 0.0).astype(BF)
    ci = lax.broadcasted_iota(jnp.int32, (tm // C, tm), 0)
    cj = lax.broadcasted_iota(jnp.int32, (tm // C, tm), 1)
    sel = jnp.where((cj // C) == ci, 1.0, 0.0).astype(BF)
    for r in range(2):
        z = _mm(lr, wgk2_ref[r]) + bgk_ref[r:r + 1, :]
        gk = (jnp.minimum(z, 0.0) - jnp.log1p(jnp.exp(-jnp.abs(z)))) * GLA_INV_NORMALIZER
        pieces = _split3(gk)
        tri = jnp.where(jnp.logical_and(same, (jj <= ii) if r == 0 else (jj >= ii)), 1.0, 0.0).astype(BF)
        dec_ref[r] = jnp.exp(_mm3(sel, pieces))
        for gi in range(tm // G):
            rows = slice(gi * G, (gi + 1) * G)
            pg = tuple(p[rows] for p in pieces)
            b = _mm3(tri, pg)
            tot = _mm3(ones_blk, pg)
            qt_ref[r, rows, :] = (q[rows] * jnp.exp(b)).astype(BF)
            kt_ref[r, rows, :] = (k[rows] * jnp.exp(-b)).astype(BF)
            kd_ref[r, rows, :] = (k[rows] * jnp.exp(tot - b)).astype(BF)


def _gla_scan_kernel(*refs, L, has_h0, emit_state):
    qt_ref, kt_ref, kd_ref, v_ref, g_ref, dec_ref, gn_ref = refs[:7]
    pos = 7
    h0_ref = None
    if has_h0:
        h0_ref = refs[pos]
        pos += 1
    y_ref = refs[pos]
    pos += 1
    st_ref = None
    if emit_state:
        st_ref = refs[pos]
        pos += 1
    st_s, of_s, ob_s = refs[pos], refs[pos + 1], refs[pos + 2]

    C = GLA_CHUNK
    n_chunks = L // C
    ii = lax.broadcasted_iota(jnp.int32, (C, C), 0)
    jj = lax.broadcasted_iota(jnp.int32, (C, C), 1)
    keeps = ((jj <= ii), (jj >= ii))
    for r in range(2):
        for hd in range(GLA_H):
            if has_h0:
                st_s[r * GLA_H + hd] = h0_ref[0, r, hd].T
            else:
                st_s[r * GLA_H + hd] = jnp.zeros((GLA_DV, GLA_DK), F32)

    def step(i):
        for r in range(2):
            n = i if r == 0 else n_chunks - 1 - i
            start = n * C
            if not isinstance(start, int):
                start = pl.multiple_of(start, C)
            rows = pl.ds(start, C)
            o_s = of_s if r == 0 else ob_s
            for hd in range(GLA_H):
                kc = slice(hd * GLA_DK, (hd + 1) * GLA_DK)
                vc = slice(hd * GLA_DV, (hd + 1) * GLA_DV)
                qt = qt_ref[r, rows, kc]
                v = v_ref[rows, vc]
                att = jnp.where(keeps[r], _mm_nt(qt, kt_ref[r, rows, kc]), 0.0).astype(BF)
                st = st_s[r * GLA_H + hd]
                o_s[rows, vc] = _mm_nt(qt, st.astype(BF)) + _mm(att, v)
                st_s[r * GLA_H + hd] = st * dec_ref[r, n, :, kc] + _mm_tn(v, kd_ref[r, rows, kc])

    if n_chunks <= 4:
        for i in range(n_chunks):
            step(i)
    else:
        def body(i, carry):
            step(i)
            return carry
        lax.fori_loop(0, n_chunks, body, 0)

    RC = 256
    for c in range(L // RC):
        rows = slice(c * RC, (c + 1) * RC)
        for hd in range(GLA_H):
            vc = slice(hd * GLA_DV, (hd + 1) * GLA_DV)
            y = _rms(of_s[rows, vc] + ob_s[rows, vc]) * gn_ref[...]
            y_ref[rows, vc] = (y * _silu(g_ref[rows, vc])).astype(BF)
    if emit_state:
        for r in range(2):
            for hd in range(GLA_H):
                st_ref[0, r, hd] = st_s[r * GLA_H + hd].T


def _residual_out_kernel(y_ref, x_ref, mod_ref, w_ref, o_ref, *, gate_j):
    o_ref[...] = x_ref[...] + _gate(mod_ref, gate_j) * _mm(y_ref[...], w_ref[...])


def gla_mixer(x, mod, w, L, h0):
    M = x.shape[0]
    B = M // L
    tm = 512
    nk, nv = GLA_H * GLA_DK, GLA_H * GLA_DV
    row = lambda n: pl.BlockSpec((tm, n), lambda i: (i, 0))
    dirs = lambda: pl.BlockSpec((2, tm, nk), lambda i: (0, i, 0))
    n_chunks = L // GLA_CHUNK
    qt, kt, kd, v, g, dec = pl.pallas_call(
        _gla_pre_kernel,
        grid=(M // tm,),
        in_specs=[row(D), _mod_spec(tm, L if mod.shape[0] > 1 else M),
                  _const_spec(w["wp"]), _const_spec(w["wlr"]), _const_spec(w["wgk2"]),
                  _const_spec(w["bgk"])],
        out_specs=[dirs(), dirs(), dirs(), row(nv), row(nv),
                   pl.BlockSpec((2, tm // GLA_CHUNK, nk), lambda i: (0, i, 0))],
        out_shape=[jax.ShapeDtypeStruct((2, M, nk), BF), jax.ShapeDtypeStruct((2, M, nk), BF),
                   jax.ShapeDtypeStruct((2, M, nk), BF),
                   jax.ShapeDtypeStruct((M, nv), BF), jax.ShapeDtypeStruct((M, nv), F32),
                   jax.ShapeDtypeStruct((2, M // GLA_CHUNK, nk), F32)],
        compiler_params=_cp("arbitrary"),
        name="gla_pre",
    )(x, mod, w["wp"], w["wlr"], w["wgk2"], w["bgk"])
    dec = dec.reshape(2, B, n_chunks, 1, nk)

    has_h0 = h0 is not None
    emit_state = not has_h0
    dir_spec = lambda: pl.BlockSpec((2, L, nk), lambda b: (0, b, 0))
    in_specs = [dir_spec(), dir_spec(), dir_spec(),
                pl.BlockSpec((L, nv), lambda b: (b, 0)),
                pl.BlockSpec((L, nv), lambda b: (b, 0)),
                pl.BlockSpec((2, None, n_chunks, 1, nk), lambda b: (0, b, 0, 0, 0)),
                pl.BlockSpec((1, GLA_DV), lambda b: (0, 0))]
    args = [qt, kt, kd, v, g, dec, w["gnorm"]]
    st_spec = pl.BlockSpec((1, 2, GLA_H, GLA_DK, GLA_DV), lambda b: (b, 0, 0, 0, 0))
    if has_h0:
        in_specs.append(st_spec)
        args.append(h0)
    out_specs = [pl.BlockSpec((L, nv), lambda b: (b, 0))]
    out_shape = [jax.ShapeDtypeStruct((M, nv), BF)]
    if emit_state:
        out_specs.append(st_spec)
        out_shape.append(jax.ShapeDtypeStruct((B, 2, GLA_H, GLA_DK, GLA_DV), F32))
    res = pl.pallas_call(
        functools.partial(_gla_scan_kernel, L=L, has_h0=has_h0, emit_state=emit_state),
        grid=(B,),
        in_specs=in_specs,
        out_specs=out_specs,
        out_shape=out_shape,
        scratch_shapes=[pltpu.VMEM((2 * GLA_H, GLA_DV, GLA_DK), F32), pltpu.VMEM((L, nv), F32),
                        pltpu.VMEM((L, nv), F32)],
        compiler_params=_cp("arbitrary"),
        name="gla_scan",
    )(*args)
    y = res[0]
    state = res[1] if emit_state else None

    x_new = pl.pallas_call(
        functools.partial(_residual_out_kernel, gate_j=0),
        grid=(M // tm,),
        in_specs=[row(nv), row(D), _mod_spec(tm, L if mod.shape[0] > 1 else M), _const_spec(w["wo"])],
        out_specs=row(D),
        out_shape=jax.ShapeDtypeStruct((M, D), F32),
        compiler_params=_cp("arbitrary"),
        name="gla_post",
    )(y, x, mod, w["wo"])
    return x_new, state


def _softmax_sink_heads(q_of, k_of, v_of, sink_ref, o_s):
    for hq in range(SWA_H):
        kh = hq // (SWA_H // SWA_KVH)
        s = _mm_nt(q_of(hq), k_of(kh))
        sink = sink_ref[hq]
        m = jnp.maximum(jnp.max(s, axis=-1, keepdims=True), sink)
        e = jnp.exp(s - m)
        p = e / (jnp.sum(e, axis=-1, keepdims=True) + jnp.exp(sink - m))
        o_s[:, hq * SWA_HD:(hq + 1) * SWA_HD] = _mm(p.astype(BF), v_of(kh)).astype(BF)


def _swa_ctx_kernel(sink_ref, x_ref, mod_ref, wqkv_ref, wo_ref, o_ref, k_out, v_out, o_s):
    x = x_ref[...]
    h = _modulate(x, mod_ref, 0).astype(BF)
    qkv = _mm(h, wqkv_ref[...])
    k_out[...] = qkv[:, SWA_NQ:SWA_NQ + SWA_NK]
    v_out[...] = qkv[:, SWA_NQ + SWA_NK:]
    q_of = lambda hq: (qkv[:, hq * SWA_HD:(hq + 1) * SWA_HD] * SWA_SCALE).astype(BF)
    k_of = lambda kh: qkv[:, SWA_NQ + kh * SWA_HD:SWA_NQ + (kh + 1) * SWA_HD].astype(BF)
    v_of = lambda kh: qkv[:, SWA_NQ + SWA_NK + kh * SWA_HD:SWA_NQ + SWA_NK + (kh + 1) * SWA_HD].astype(BF)
    _softmax_sink_heads(q_of, k_of, v_of, sink_ref, o_s)
    o_ref[...] = x + _gate(mod_ref, 0) * _mm(o_s[...], wo_ref[...])


def swa_ctx_mixer(x, mod, w, L):
    M = x.shape[0]
    row = lambda n: pl.BlockSpec((L, n), lambda i: (i, 0))
    return pl.pallas_call(
        _swa_ctx_kernel,
        grid=(M // L,),
        in_specs=[pl.BlockSpec(memory_space=pltpu.SMEM), row(D), _mod_spec(L, M),
                  _const_spec(w["wqkv"]), _const_spec(w["wo"])],
        out_specs=[row(D), row(SWA_NK), row(SWA_NK)],
        out_shape=[jax.ShapeDtypeStruct((M, D), F32), jax.ShapeDtypeStruct((M, SWA_NK), F32),
                   jax.ShapeDtypeStruct((M, SWA_NK), F32)],
        scratch_shapes=[pltpu.VMEM((L, SWA_NQ), BF)],
        compiler_params=_cp("arbitrary"),
        name="swa_ctx",
    )(w["sink"], x, mod, w["wqkv"], w["wo"])


def _swa_lat_kernel(sink_ref, x_ref, mod_ref, wqkv_ref, wsw_ref, cos_ref, sin_ref, kc_ref, vc_ref,
                    wo_ref, o_ref, q_s, k_s, v_s, o_s, *, L):
    n = pl.program_id(1)
    RC = 256

    @pl.when(n == 0)
    def _():
        for c in range(L // RC):
            rows = slice(c * RC, (c + 1) * RC)
            h = _modulate(x_ref[rows, :], mod_ref, 0).astype(BF)
            qkv = _mm(h, wqkv_ref[...])
            sw = _mm(h, wsw_ref[...])
            cos = cos_ref[rows, :]
            sin = sin_ref[rows, :]
            cq = jnp.concatenate([cos] * (SWA_NQ // 128), axis=1)
            sq = jnp.concatenate([sin] * (SWA_NQ // 128), axis=1)
            ck = jnp.concatenate([cos] * (SWA_NK // 128), axis=1)
            sk = jnp.concatenate([sin] * (SWA_NK // 128), axis=1)
            q = (qkv[:, :SWA_NQ] * SWA_SCALE) * cq + (sw[:, :SWA_NQ] * SWA_SCALE) * sq
            q_s[rows, :] = q.astype(BF)
            k = qkv[:, SWA_NQ:SWA_NQ + SWA_NK] * ck + sw[:, SWA_NQ:SWA_NQ + SWA_NK] * sk
            k_s[rows, :] = k.astype(BF)
            v_s[rows, :] = qkv[:, SWA_NQ + SWA_NK:].astype(BF)

    QB = SWA_BLOCK
    KW = 3 * SWA_BLOCK
    r0 = pl.multiple_of(n * QB, QB)
    ws = pl.multiple_of(jnp.clip((n - 1) * QB, 0, L - KW), QB)
    qpos = r0 + lax.broadcasted_iota(jnp.int32, (QB, KW), 0)
    kpos = ws + lax.broadcasted_iota(jnp.int32, (QB, KW), 1)
    valid = jnp.abs(kpos - qpos) <= SWA_WINDOW
    for hq in range(SWA_H):
        kh = hq // (SWA_H // SWA_KVH)
        hs = slice(kh * SWA_HD, (kh + 1) * SWA_HD)
        q = q_s[pl.ds(r0, QB), hq * SWA_HD:(hq + 1) * SWA_HD]
        s1 = jnp.where(valid, _mm_nt(q, k_s[pl.ds(ws, KW), hs]), NEG)
        s2 = _mm_nt(q, kc_ref[0, :, hs].astype(BF))
        sink = sink_ref[hq]
        m = jnp.maximum(jnp.maximum(jnp.max(s1, axis=-1, keepdims=True),
                                    jnp.max(s2, axis=-1, keepdims=True)), sink)
        e1 = jnp.exp(s1 - m)
        e2 = jnp.exp(s2 - m)
        den = (jnp.sum(e1, axis=-1, keepdims=True) + jnp.sum(e2, axis=-1, keepdims=True)
               + jnp.exp(sink - m))
        o = (_mm((e1 / den).astype(BF), v_s[pl.ds(ws, KW), hs])
             + _mm((e2 / den).astype(BF), vc_ref[0, :, hs].astype(BF)))
        o_s[:, hq * SWA_HD:(hq + 1) * SWA_HD] = o.astype(BF)
    o_ref[...] = x_ref[pl.ds(r0, QB), :] + _gate(mod_ref, 0) * _mm(o_s[...], wo_ref[...])


def swa_lat_mixer(x, mod, w, L, kc, vc, cos128, sin128):
    M = x.shape[0]
    B = M // L
    P = kc.shape[1]
    nb = L // SWA_BLOCK
    return pl.pallas_call(
        functools.partial(_swa_lat_kernel, L=L),
        grid=(B, nb),
        in_specs=[pl.BlockSpec(memory_space=pltpu.SMEM),
                  pl.BlockSpec((L, D), lambda b, n: (b, 0)),
                  pl.BlockSpec((1, 6, D), lambda b, n: (b, 0, 0)),
                  _const_spec(w["wqkv"]), _const_spec(w["wsw"]),
                  _const_spec(cos128), _const_spec(sin128),
                  pl.BlockSpec((1, P, SWA_NK), lambda b, n: (b, 0, 0)),
                  pl.BlockSpec((1, P, SWA_NK), lambda b, n: (b, 0, 0)),
                  _const_spec(w["wo"])],
        out_specs=pl.BlockSpec((SWA_BLOCK, D), lambda b, n: (b * nb + n, 0)),
        out_shape=jax.ShapeDtypeStruct((M, D), F32),
        scratch_shapes=[pltpu.VMEM((L, SWA_NQ), BF), pltpu.VMEM((L, SWA_NK), BF),
                        pltpu.VMEM((L, SWA_NK), BF), pltpu.VMEM((SWA_BLOCK, SWA_NQ), BF)],
        compiler_params=_cp("arbitrary", "arbitrary"),
        name="swa_lat",
    )(w["sink"], x, mod, w["wqkv"], w["wsw"], cos128, sin128, kc, vc, w["wo"])


def _conv_kernel(x_ref, mod_ref, win_ref, cw_ref, wout_ref, o_ref, y_s, *, L):
    x = x_ref[...]
    h = _modulate(x, mod_ref, 0).astype(BF)
    CC = 256
    row = lax.broadcasted_iota(jnp.int32, (L, CC), 0)
    for c in range(D // CC):
        cols = slice(c * CC, (c + 1) * CC)
        bg = _mm(h, win_ref[:, c * CC:(c + 1) * CC])
        cg = _mm(h, win_ref[:, D + c * CC:D + (c + 1) * CC])
        u = _mm(h, win_ref[:, 2 * D + c * CC:2 * D + (c + 1) * CC])
        cu = cg * u
        prev = jnp.where(row == 0, 0.0, pltpu.roll(cu, 1, 0))
        nxt = jnp.where(row == L - 1, 0.0, pltpu.roll(cu, L - 1, 0))
        conv = prev * cw_ref[0:1, cols] + cu * cw_ref[1:2, cols] + nxt * cw_ref[2:3, cols]
        y_s[:, cols] = (bg * conv).astype(BF)
    o_ref[...] = x + _gate(mod_ref, 0) * _mm(y_s[...], wout_ref[...])


def conv_mixer(x, mod, w, L):
    M = x.shape[0]
    row = pl.BlockSpec((L, D), lambda i: (i, 0))
    return pl.pallas_call(
        functools.partial(_conv_kernel, L=L),
        grid=(M // L,),
        in_specs=[row, _mod_spec(L, L if mod.shape[0] > 1 else M), _const_spec(w["win"]),
                  _const_spec(w["cw"]), _const_spec(w["wout"])],
        out_specs=row,
        out_shape=jax.ShapeDtypeStruct((M, D), F32),
        scratch_shapes=[pltpu.VMEM((L, D), BF)],
        compiler_params=_cp("arbitrary"),
        name="conv_mix",
    )(x, mod, w["win"], w["cw"], w["wout"])


def _mla_ctx_kernel(x_ref, mod_ref, wdq_ref, qn_ref, wuq_ref, wdkv_ref, kvn_ref, wukv_ref, wo_ref,
                    o_ref, ckv_out, kr_out, o_s):
    x = x_ref[...]
    h = _modulate(x, mod_ref, 0).astype(BF)
    cq = (_rms(_mm(h, wdq_ref[...])) * qn_ref[...]).astype(BF)
    q = _mm(cq, wuq_ref[...]).astype(BF)
    kvc = _mm(h, wdkv_ref[...])
    ckv = _rms(kvc[:, :KV_LORA]) * kvn_ref[...]
    kr = kvc[:, KV_LORA:]
    ckv_out[...] = ckv
    kr_out[...] = kr
    kv = _mm(ckv.astype(BF), wukv_ref[...]).astype(BF)
    krb = kr.astype(BF)
    nn = MLA_H * QK_NOPE
    for hh in range(MLA_H):
        qh = q[:, hh * MLA_QW:(hh + 1) * MLA_QW]
        kh = jnp.concatenate([kv[:, hh * QK_NOPE:(hh + 1) * QK_NOPE], krb], axis=1)
        s = _mm_nt(qh, kh) * MLA_SCALE
        m = jnp.max(s, axis=-1, keepdims=True)
        e = jnp.exp(s - m)
        p = e / jnp.sum(e, axis=-1, keepdims=True)
        vh = kv[:, nn + hh * V_HEAD:nn + (hh + 1) * V_HEAD]
        o_s[:, hh * V_HEAD:(hh + 1) * V_HEAD] = _mm(p.astype(BF), vh).astype(BF)
    o_ref[...] = x + _gate(mod_ref, 0) * _mm(o_s[...], wo_ref[...])


def mla_ctx_mixer(x, mod, w, L):
    M = x.shape[0]
    row = lambda n: pl.BlockSpec((L, n), lambda i: (i, 0))
    ws = [w["wdq"], w["qnorm"], w["wuq"], w["wdkv"], w["kvnorm"], w["wukv"], w["wo"]]
    return pl.pallas_call(
        _mla_ctx_kernel,
        grid=(M // L,),
        in_specs=[row(D), _mod_spec(L, M)] + [_const_spec(a) for a in ws],
        out_specs=[row(D), row(KV_LORA), row(128)],
        out_shape=[jax.ShapeDtypeStruct((M, D), F32), jax.ShapeDtypeStruct((M, KV_LORA), F32),
                   jax.ShapeDtypeStruct((M, 128), F32)],
        scratch_shapes=[pltpu.VMEM((L, MLA_H * V_HEAD), BF)],
        compiler_params=_cp("arbitrary"),
        name="mla_ctx",
    )(x, mod, *ws)


def _mla_lat_kernel(x_ref, mod_ref, wdq_ref, qn_ref, wuq_ref, wuqsw_ref, wdkv_ref, wdkvsw_ref,
                    kvn_ref, wukv_ref, cq_ref, sq_ref, ck_ref, sk_ref, ckvc_ref, krc_ref, wo_ref,
                    o_ref, q_s, kn_s, v_s, kr_s, knc_s, vc_s, o_s, *, L, QB):
    n = pl.program_id(1)
    RC = 256
    nn = MLA_H * QK_NOPE

    @pl.when(n == 0)
    def _():
        for c in range(L // RC):
            rows = slice(c * RC, (c + 1) * RC)
            h = _modulate(x_ref[rows, :], mod_ref, 0).astype(BF)
            cq = (_rms(_mm(h, wdq_ref[...])) * qn_ref[...]).astype(BF)
            cosq = jnp.concatenate([cq_ref[rows, :]] * MLA_H, axis=1)
            sinq = jnp.concatenate([sq_ref[rows, :]] * MLA_H, axis=1)
            q = _mm(cq, wuq_ref[...]) * cosq + _mm(cq, wuqsw_ref[...]) * sinq
            q_s[rows, :] = q.astype(BF)
            kvc = _mm(h, wdkv_ref[...])
            ksw = _mm(h, wdkvsw_ref[...])
            ckv = _rms(kvc[:, :KV_LORA]) * kvn_ref[...]
            kr_s[rows, :] = (kvc[:, KV_LORA:] * ck_ref[rows, :] + ksw * sk_ref[rows, :]).astype(BF)
            kv = _mm(ckv.astype(BF), wukv_ref[...])
            kn_s[rows, :] = kv[:, :nn].astype(BF)
            v_s[rows, :] = kv[:, nn:].astype(BF)
        kvp = _mm(ckvc_ref[0].astype(BF), wukv_ref[...])
        knc_s[...] = kvp[:, :nn].astype(BF)
        vc_s[...] = kvp[:, nn:].astype(BF)

    r0 = pl.multiple_of(n * QB, QB)
    krc = krc_ref[0].astype(BF)
    for hh in range(MLA_H):
        ns = slice(hh * QK_NOPE, (hh + 1) * QK_NOPE)
        qh = q_s[pl.ds(r0, QB), hh * MLA_QW:(hh + 1) * MLA_QW]
        k1 = jnp.concatenate([kn_s[:, ns], kr_s[...]], axis=1)
        k2 = jnp.concatenate([knc_s[:, ns], krc], axis=1)
        s1 = _mm_nt(qh, k1) * MLA_SCALE
        s2 = _mm_nt(qh, k2) * MLA_SCALE
        m = jnp.maximum(jnp.max(s1, axis=-1, keepdims=True), jnp.max(s2, axis=-1, keepdims=True))
        e1 = jnp.exp(s1 - m)
        e2 = jnp.exp(s2 - m)
        den = jnp.sum(e1, axis=-1, keepdims=True) + jnp.sum(e2, axis=-1, keepdims=True)
        vs = slice(hh * V_HEAD, (hh + 1) * V_HEAD)
        o = _mm((e1 / den).astype(BF), v_s[:, vs]) + _mm((e2 / den).astype(BF), vc_s[:, vs])
        o_s[:, vs] = o.astype(BF)
    o_ref[...] = x_ref[pl.ds(r0, QB), :] + _gate(mod_ref, 0) * _mm(o_s[...], wo_ref[...])


def mla_lat_mixer(x, mod, w, L, ckv_c, kr_c128, tabs):
    M = x.shape[0]
    B = M // L
    P = ckv_c.shape[1]
    QB = 256
    nb = L // QB
    ws1 = [w["wdq"], w["qnorm"], w["wuq"], w["wuqsw"], w["wdkv"], w["wdkvsw"], w["kvnorm"], w["wukv"]]
    nv = MLA_H * V_HEAD
    return pl.pallas_call(
        functools.partial(_mla_lat_kernel, L=L, QB=QB),
        grid=(B, nb),
        in_specs=[pl.BlockSpec((L, D), lambda b, n: (b, 0)),
                  pl.BlockSpec((1, 6, D), lambda b, n: (b, 0, 0))]
                 + [_const_spec(a) for a in ws1] + [_const_spec(a) for a in tabs]
                 + [pl.BlockSpec((1, P, KV_LORA), lambda b, n: (b, 0, 0)),
                    pl.BlockSpec((1, P, 128), lambda b, n: (b, 0, 0)),
                    _const_spec(w["wo"])],
        out_specs=pl.BlockSpec((QB, D), lambda b, n: (b * nb + n, 0)),
        out_shape=jax.ShapeDtypeStruct((M, D), F32),
        scratch_shapes=[pltpu.VMEM((L, MLA_H * MLA_QW), BF), pltpu.VMEM((L, nv), BF),
                        pltpu.VMEM((L, nv), BF), pltpu.VMEM((L, 128), BF),
                        pltpu.VMEM((P, nv), BF), pltpu.VMEM((P, nv), BF), pltpu.VMEM((QB, nv), BF)],
        compiler_params=_cp("arbitrary", "arbitrary"),
        name="mla_lat",
    )(x, mod, *ws1, *tabs, ckv_c, kr_c128, w["wo"])


def _swiglu_partial(xb, wg_ref, wu_ref, wd_ref):
    hg = _mm(xb, wg_ref[...].astype(BF))
    hu = _mm(xb, wu_ref[...].astype(BF))
    a = (_silu(hg) * hu).astype(BF)
    return _mm(a, wd_ref[...].astype(BF))


def _dense_ffn_kernel(x_ref, mod_ref, wg_ref, wu_ref, wd_ref, o_ref, xm_s):
    f = pl.program_id(1)

    @pl.when(f == 0)
    def _():
        xm_s[...] = _modulate(x_ref[...], mod_ref, 1).astype(BF)
        o_ref[...] = jnp.zeros_like(o_ref)

    o_ref[...] += _swiglu_partial(xm_s[...], wg_ref, wu_ref, wd_ref)

    @pl.when(f == pl.num_programs(1) - 1)
    def _():
        o_ref[...] = x_ref[...] + _gate(mod_ref, 1) * o_ref[...]


def dense_ffn(x, mod, wg, wu, wd, j, L):
    M = x.shape[0]
    F = wg.shape[-1]
    tm, tf = DENSE_TM, DENSE_TF
    return pl.pallas_call(
        _dense_ffn_kernel,
        grid=(M // tm, F // tf),
        in_specs=[pl.BlockSpec((tm, D), lambda i, f: (i, 0)),
                  pl.BlockSpec((1, 6, D), lambda i, f: ((i * tm) // (L if mod.shape[0] > 1 else M), 0, 0)),
                  pl.BlockSpec((None, D, tf), lambda i, f: (j, 0, f)),
                  pl.BlockSpec((None, D, tf), lambda i, f: (j, 0, f)),
                  pl.BlockSpec((None, tf, D), lambda i, f: (j, f, 0))],
        out_specs=pl.BlockSpec((tm, D), lambda i, f: (i, 0)),
        out_shape=jax.ShapeDtypeStruct((M, D), F32),
        scratch_shapes=[pltpu.VMEM((tm, D), BF)],
        compiler_params=_cp("arbitrary", "arbitrary"),
        name="dense_ffn",
    )(x, mod, wg, wu, wd)


def _router_kernel(xp_ref, xs_ref, mod_ref, wr_ref, xm_ref, idx_ref, w_ref, cnt_ref, cnt_s, *, np_tiles):
    i = pl.program_id(0)
    tm = xm_ref.shape[0]

    @pl.when(i == 0)
    def _():
        cnt_s[...] = jnp.zeros_like(cnt_s)

    x = jnp.where(i < np_tiles, xp_ref[...], xs_ref[...])
    xm = _modulate(x, mod_ref, 1).astype(BF)
    xm_ref[...] = xm
    lane = lax.broadcasted_iota(jnp.int32, idx_ref.shape, 1)
    logits = jnp.where(lane < N_EXPERTS, _mm(xm, wr_ref[...]), -jnp.inf)
    m1 = jnp.max(logits, axis=-1, keepdims=True)
    i1 = jnp.min(jnp.where(logits == m1, lane, 128), axis=-1, keepdims=True)
    rest = jnp.where(lane == i1, -jnp.inf, logits)
    m2 = jnp.max(rest, axis=-1, keepdims=True)
    i2 = jnp.min(jnp.where(rest == m2, lane, 128), axis=-1, keepdims=True)
    e = jnp.exp(m2 - m1)
    w_ref[...] = jnp.where(lane == 0, 1.0 / (1.0 + e), e / (1.0 + e))
    oh1 = jnp.where(lane == i1, 1.0, 0.0)
    oh2 = jnp.where(lane == i2, 1.0, 0.0)
    rr = lax.broadcasted_iota(jnp.int32, (tm, tm), 0)
    cc = lax.broadcasted_iota(jnp.int32, (tm, tm), 1)
    below = jnp.where(cc < rr, 1.0, 0.0).astype(BF)
    run = cnt_s[0:1, :]
    tot1 = jnp.sum(oh1, axis=0, keepdims=True)
    tot2 = jnp.sum(oh2, axis=0, keepdims=True)
    r1 = jnp.sum(oh1 * (run + _mm(below, oh1.astype(BF))), axis=-1, keepdims=True)
    r2 = jnp.sum(oh2 * (run + tot1 + _mm(below, oh2.astype(BF))), axis=-1, keepdims=True)
    idx_ref[...] = jnp.where(lane == 0, i1, jnp.where(lane == 1, i2, jnp.where(
        lane == 2, r1.astype(jnp.int32), r2.astype(jnp.int32))))
    total = run + tot1 + tot2
    cnt_s[...] = jnp.broadcast_to(total, cnt_s.shape)
    cnt_ref[...] = jnp.broadcast_to(total, cnt_ref.shape)


def route(xp, xs, mod3, wr, Ls):
    Mp, Ms = xp.shape[0], xs.shape[0]
    T = Mp + Ms
    tm = 512
    npt = Mp // tm
    row = lambda n: pl.BlockSpec((tm, n), lambda i: (i, 0))
    return pl.pallas_call(
        functools.partial(_router_kernel, np_tiles=npt),
        grid=(T // tm,),
        in_specs=[pl.BlockSpec((tm, D), lambda i: (jnp.minimum(i, npt - 1), 0)),
                  pl.BlockSpec((tm, D), lambda i: (jnp.maximum(i - npt, 0), 0)),
                  pl.BlockSpec((1, 6, D), lambda i: (jnp.where(i < npt, 0, 1 + ((i - npt) * tm) // Ls), 0, 0)),
                  _const_spec(wr)],
        out_specs=[row(D), row(128), row(128), pl.BlockSpec((8, 128), lambda i: (0, 0))],
        out_shape=[jax.ShapeDtypeStruct((T, D), BF), jax.ShapeDtypeStruct((T, 128), jnp.int32),
                   jax.ShapeDtypeStruct((T, 128), F32), jax.ShapeDtypeStruct((8, 128), F32)],
        scratch_shapes=[pltpu.VMEM((8, 128), F32)],
        compiler_params=_cp("arbitrary"),
        name="moe_route",
    )(xp, xs, mod3, wr)


def _moe_ffn_kernel(vb_ref, ve_ref, vm_ref, nv_ref, x_ref, wg_ref, wu_ref, wd_ref, o_ref):
    v = pl.program_id(0)
    f = pl.program_id(1)
    valid = v < nv_ref[0]
    mode = vm_ref[v]
    H = MOE_TM

    def run(rows):
        @pl.when(f == 0)
        def _():
            o_ref[rows, :] = jnp.zeros((rows.stop - rows.start, D), F32)

        o_ref[rows, :] += _swiglu_partial(x_ref[rows, :], wg_ref, wu_ref, wd_ref)

    for m, rows in ((0, slice(0, 2 * H)), (1, slice(0, H)), (2, slice(H, 2 * H))):
        @pl.when(jnp.logical_and(valid, mode == m))
        def _(rows=rows):
            run(rows)


def moe_ffn(xs, tile_expert, n_valid, wg, wu, wd, j):
    R = xs.shape[0]
    F = wg.shape[-1]
    bm, tf = 2 * MOE_TM, MOE_TF
    nf = F // tf
    nb = R // bm
    n_vis = nb + N_EXPERTS - 1
    blocks = jnp.arange(nb, dtype=jnp.int32)
    t0, t1 = tile_expert[0::2], tile_expert[1::2]
    valid0 = 2 * blocks < n_valid[0]
    valid1 = 2 * blocks + 1 < n_valid[0]
    two = jnp.logical_and(valid1, t0 != t1)
    cnt = jnp.where(valid0, 1 + two.astype(jnp.int32), 0)
    cum = jnp.cumsum(cnt)
    nv = cum[-1:].astype(jnp.int32)
    v_eff = jnp.minimum(jnp.arange(n_vis, dtype=jnp.int32), nv[0] - 1)
    vb = jnp.sum((v_eff[:, None] >= cum[None, :]).astype(jnp.int32), axis=1)
    pick = lambda a: jnp.sum(jnp.where(vb[:, None] == blocks[None, :], a[None, :].astype(jnp.int32), 0), axis=1)
    k = v_eff - pick(cum - cnt)
    two_v, t0_v, t1_v, valid1_v = pick(two), pick(t0), pick(t1), pick(valid1)
    vm = jnp.where(two_v == 1, jnp.where(k == 0, 1, 2), jnp.where(valid1_v == 1, 0, 1)).astype(jnp.int32)
    ve = jnp.where(jnp.logical_and(two_v == 1, k == 1), t1_v, t0_v).astype(jnp.int32)

    def fidx(v, f, nv):
        return jnp.where(v < nv[0], f, nf - 1)

    grid_spec = pltpu.PrefetchScalarGridSpec(
        num_scalar_prefetch=4,
        grid=(n_vis, nf),
        in_specs=[pl.BlockSpec((bm, D), lambda v, f, vb, ve, vm, nv: (vb[v], 0)),
                  pl.BlockSpec((None, None, D, tf), lambda v, f, vb, ve, vm, nv: (j, ve[v], 0, fidx(v, f, nv))),
                  pl.BlockSpec((None, None, D, tf), lambda v, f, vb, ve, vm, nv: (j, ve[v], 0, fidx(v, f, nv))),
                  pl.BlockSpec((None, None, tf, D), lambda v, f, vb, ve, vm, nv: (j, ve[v], fidx(v, f, nv), 0))],
        out_specs=pl.BlockSpec((bm, D), lambda v, f, vb, ve, vm, nv: (vb[v], 0)),
    )
    return pl.pallas_call(
        _moe_ffn_kernel,
        grid_spec=grid_spec,
        out_shape=jax.ShapeDtypeStruct((R, D), F32),
        compiler_params=_cp("arbitrary", "arbitrary"),
        name="moe_ffn",
    )(vb.astype(jnp.int32), ve, vm, nv, xs, wg, wu, wd)


def _combine_kernel(x_ref, mod_ref, ya_ref, yb_ref, w_ref, fn_ref, o_ref, *, final):
    y = w_ref[:, 0:1] * ya_ref[...] + w_ref[:, 1:2] * yb_ref[...]
    o = x_ref[...] + _gate(mod_ref, 1) * y
    if final:
        o = _rms(o) * fn_ref[...]
    o_ref[...] = o


def moe_combine(x, mod, ya, yb, w, fn, L, final, row_off):
    M = x.shape[0]
    tm = 512
    off = row_off // tm
    row = pl.BlockSpec((tm, D), lambda i: (i, 0))
    row_o = lambda n: pl.BlockSpec((tm, n), lambda i: (i + off, 0))
    return pl.pallas_call(
        functools.partial(_combine_kernel, final=final),
        grid=(M // tm,),
        in_specs=[row, _mod_spec(tm, L if mod.shape[0] > 1 else M), row_o(D), row_o(D), row_o(128),
                  _const_spec(fn)],
        out_specs=row,
        out_shape=jax.ShapeDtypeStruct((M, D), F32),
        compiler_params=_cp("arbitrary"),
        name="moe_combine",
    )(x, mod, ya, yb, w, fn)


def moe_layer(xp, xs, mod3, wr, wg, wu, wd, j, fn, Lp, Ls, final):
    Mp = xp.shape[0]
    xm, idx, wts, cnt = route(xp, xs, mod3, wr, Ls)
    T = xm.shape[0]
    tm = MOE_TM
    R = 2 * T + N_EXPERTS * tm
    counts = cnt[0, :N_EXPERTS].astype(jnp.int32)
    tiles_per = (counts + tm - 1) // tm
    tile_end = jnp.cumsum(tiles_per)
    offs = (tile_end - tiles_per) * tm
    experts = jnp.arange(N_EXPERTS, dtype=jnp.int32)
    dest = jnp.sum(jnp.where(idx[:, 0:2, None] == experts, offs, 0), axis=-1) + idx[:, 2:4]
    tok = jnp.arange(2 * T, dtype=jnp.int32) // 2
    src = jnp.zeros((R,), jnp.int32).at[dest.reshape(-1)].set(
        tok, unique_indices=True, mode="promise_in_bounds")
    n_tiles = R // tm
    n_valid = tile_end[-1:].astype(jnp.int32)
    t_ids = jnp.arange(n_tiles, dtype=jnp.int32)
    te = jnp.sum((t_ids[:, None] >= tile_end[None, :]).astype(jnp.int32), axis=1)
    te_last = jnp.sum((n_valid - 1 >= tile_end).astype(jnp.int32))
    te = jnp.where(t_ids < n_valid[0], te, te_last).astype(jnp.int32)
    x_sorted = xm.at[src].get(mode="promise_in_bounds")
    y_sorted = moe_ffn(x_sorted, te, n_valid, wg, wu, wd, j)
    ya = y_sorted.at[dest[:, 0]].get(mode="promise_in_bounds")
    yb = y_sorted.at[dest[:, 1]].get(mode="promise_in_bounds")
    xp_new = moe_combine(xp, mod3[0:1], ya, yb, wts, fn, Lp, final, 0)
    xs_new = moe_combine(xs, mod3[1:], ya, yb, wts, fn, Ls, final, Mp)
    return xp_new, xs_new


def _rope_partner_perm(n):
    c = np.arange(n)
    return np.where((c % 32) < 16, c + 16, c - 16)


def _rope_tables(L):
    nf = 16
    inv = ROPE_THETA ** (-np.arange(nf, dtype=np.float32) / nf)
    pos = np.arange(L)
    ang = np.stack([(pos // GRID_W).astype(np.float32), (pos % GRID_W).astype(np.float32)],
                   axis=-1)[:, :, None] * inv
    cos = jnp.cos(jnp.asarray(ang, F32))
    sin = jnp.sin(jnp.asarray(ang, F32))
    c64 = jnp.stack([cos, cos], axis=2).reshape(L, 64)
    s64 = jnp.stack([-sin, sin], axis=2).reshape(L, 64)
    return c64, s64


def kernel(x_prompt, x_sample, state_l0_gla, cache_l1_k, cache_l1_v, cache_l3_ckv, cache_l3_krope, c, c_ctx, w_mod, b_mod, gla_wq, gla_wk, gla_wv, gla_wg, gla_wgk1, gla_wgk2, gla_bgk, gla_norm, gla_wo, swa_wqkv, swa_sink, swa_wo, conv_win, conv_w, conv_wout, mla_wdq, mla_qnorm, mla_wuq, mla_wdkv, mla_kvnorm, mla_wukv, mla_wo, dense_w_gate, dense_w_up, dense_w_down, moe_router, moe_w_gate, moe_w_up, moe_w_down, final_norm):
    Bp, Lp, _ = x_prompt.shape
    Bs, Ls, _ = x_sample.shape
    P = cache_l1_k.shape[1]
    xp = x_prompt.reshape(Bp * Lp, D)
    xs = x_sample.reshape(Bs * Ls, D)

    cvec = jnp.zeros((8, D), F32).at[0].set(c_ctx).at[1:1 + Bs].set(c)
    mod_all = adaln_all(cvec, w_mod, b_mod).reshape(w_mod.shape[0], 8, 6, D)

    wlr = jnp.zeros((D, 128), F32).at[:, :GLA_RANK].set(gla_wgk1[0]).at[:, GLA_RANK:2 * GLA_RANK].set(gla_wgk1[1])
    wgk2 = jnp.zeros((2, 128, GLA_H * GLA_DK), F32)
    wgk2 = wgk2.at[0, :GLA_RANK].set(gla_wgk2[0]).at[1, GLA_RANK:2 * GLA_RANK].set(gla_wgk2[1])
    gla_w = dict(wp=jnp.concatenate([gla_wq, gla_wk, gla_wv, gla_wg], axis=1).astype(BF),
                 wlr=wlr.astype(BF), wgk2=wgk2.astype(BF), bgk=gla_bgk,
                 gnorm=gla_norm.reshape(1, GLA_DV), wo=gla_wo.astype(BF))
    modp, mods = mod_all[0, 0:1], mod_all[0, 1:1 + Bs]
    xp, new_state = gla_mixer(xp, modp, gla_w, Lp, None)
    xs, _ = gla_mixer(xs, mods, gla_w, Ls, state_l0_gla)
    xp = dense_ffn(xp, modp, dense_w_gate, dense_w_up, dense_w_down, 0, Lp)
    xs = dense_ffn(xs, mods, dense_w_gate, dense_w_up, dense_w_down, 0, Ls)

    c64, s64 = _rope_tables(Ls)
    cos128 = jnp.concatenate([c64, c64], axis=1)
    sin128 = jnp.concatenate([s64, s64], axis=1)
    perm = _rope_partner_perm(SWA_NQ + SWA_NK)
    swa_w = dict(wqkv=swa_wqkv.astype(BF), wsw=swa_wqkv[:, perm].astype(BF), wo=swa_wo.astype(BF),
                 sink=swa_sink)
    modp, mods = mod_all[1, 0:1], mod_all[1, 1:1 + Bs]
    xp, k1, v1 = swa_ctx_mixer(xp, modp, swa_w, Lp)
    xs = swa_lat_mixer(xs, mods, swa_w, Ls, cache_l1_k.reshape(Bs, P, SWA_NK),
                       cache_l1_v.reshape(Bs, P, SWA_NK), cos128, sin128)
    wr = jnp.zeros((moe_router.shape[0], D, 128), F32).at[:, :, :N_EXPERTS].set(moe_router).astype(BF)
    xp, xs = moe_layer(xp, xs, mod_all[1, 0:1 + Bs], wr[0], moe_w_gate, moe_w_up, moe_w_down, 0,
                       final_norm.reshape(1, D), Lp, Ls, False)

    conv_wts = dict(win=conv_win.astype(BF), cw=conv_w, wout=conv_wout.astype(BF))
    modp, mods = mod_all[2, 0:1], mod_all[2, 1:1 + Bs]
    xp = conv_mixer(xp, modp, conv_wts, Lp)
    xs = conv_mixer(xs, mods, conv_wts, Ls)
    xp = dense_ffn(xp, modp, dense_w_gate, dense_w_up, dense_w_down, 1, Lp)
    xs = dense_ffn(xs, mods, dense_w_gate, dense_w_up, dense_w_down, 1, Ls)

    wuq3 = mla_wuq.reshape(Q_LORA, MLA_H, QK_NOPE + QK_ROPE)
    zpad = jnp.zeros((Q_LORA, MLA_H, MLA_QW - QK_NOPE - QK_ROPE), F32)
    wuq = jnp.concatenate([wuq3, zpad], axis=2).reshape(Q_LORA, MLA_H * MLA_QW)
    rperm = _rope_partner_perm(QK_ROPE)
    wuq_sw = jnp.concatenate([jnp.zeros((Q_LORA, MLA_H, QK_NOPE), F32), wuq3[:, :, QK_NOPE:][:, :, rperm], zpad],
                             axis=2).reshape(Q_LORA, MLA_H * MLA_QW)
    kpad = jnp.zeros((D, 128 - QK_ROPE), F32)
    wdkv = jnp.concatenate([mla_wdkv, kpad], axis=1)
    wdkv_sw = jnp.concatenate([mla_wdkv[:, KV_LORA:][:, rperm], kpad], axis=1)
    wukv3 = mla_wukv.reshape(KV_LORA, MLA_H, QK_NOPE + V_HEAD)
    wukv = jnp.concatenate([wukv3[:, :, :QK_NOPE].reshape(KV_LORA, MLA_H * QK_NOPE),
                            wukv3[:, :, QK_NOPE:].reshape(KV_LORA, MLA_H * V_HEAD)], axis=1)
    mla_w = dict(wdq=mla_wdq.astype(BF), qnorm=mla_qnorm.reshape(1, Q_LORA), wuq=wuq.astype(BF),
                 wuqsw=wuq_sw.astype(BF), wdkv=wdkv.astype(BF), wdkvsw=wdkv_sw.astype(BF),
                 kvnorm=mla_kvnorm.reshape(1, KV_LORA), wukv=wukv.astype(BF), wo=mla_wo.astype(BF))
    ones = jnp.ones((Ls, QK_NOPE), F32)
    z128 = jnp.zeros((Ls, QK_NOPE), F32)
    z64 = jnp.zeros((Ls, 64), F32)
    tabs = [jnp.concatenate([ones, c64, z64], axis=1), jnp.concatenate([z128, s64, z64], axis=1),
            jnp.concatenate([c64, z64], axis=1), jnp.concatenate([s64, z64], axis=1)]
    modp, mods = mod_all[3, 0:1], mod_all[3, 1:1 + Bs]
    xp, ckv3, kr3 = mla_ctx_mixer(xp, modp, mla_w, Lp)
    kr_c128 = jnp.concatenate([cache_l3_krope, jnp.zeros((Bs, P, 128 - QK_ROPE), F32)], axis=2)
    xs = mla_lat_mixer(xs, mods, mla_w, Ls, cache_l3_ckv, kr_c128, tabs)
    xp, xs = moe_layer(xp, xs, mod_all[3, 0:1 + Bs], wr[1], moe_w_gate, moe_w_up, moe_w_down, 1,
                       final_norm.reshape(1, D), Lp, Ls, True)

    return (xp.reshape(Bp, Lp, D), xs.reshape(Bs, Ls, D), new_state,
            k1.reshape(Bp, Lp, SWA_KVH, SWA_HD), v1.reshape(Bp, Lp, SWA_KVH, SWA_HD),
            ckv3.reshape(Bp, Lp, KV_LORA), kr3[:, :QK_ROPE].reshape(Bp, Lp, QK_ROPE))
```

```python
import functools

import numpy as np
import jax
import jax.numpy as jnp
from jax import lax
from jax.experimental import pallas as pl
from jax.experimental.pallas import tpu as pltpu

BF = jnp.bfloat16
F32 = jnp.float32

D = 1024
EPS = 1e-6
NEG = -1e30
ROPE_THETA = 10000.0
GRID_W = 64

GLA_H, GLA_DK, GLA_DV, GLA_RANK, GLA_CHUNK = 4, 128, 256, 16, 64
GLA_SCALE = GLA_DK ** -0.5
GLA_INV_NORMALIZER = 1.0 / 16.0

SWA_H, SWA_KVH, SWA_HD, SWA_WINDOW, SWA_BLOCK = 16, 4, 64, 128, 128
SWA_SCALE = SWA_HD ** -0.5
SWA_NQ = SWA_H * SWA_HD
SWA_NK = SWA_KVH * SWA_HD

MLA_H, Q_LORA, KV_LORA, QK_NOPE, QK_ROPE, V_HEAD = 8, 384, 256, 128, 64, 128
MLA_SCALE = (QK_NOPE + QK_ROPE) ** -0.5
MLA_QW = 256

N_EXPERTS = 8
MOE_TM = 512
MOE_TF = 512
DENSE_TM = 1024
DENSE_TF = 256

VMEM_LIMIT = 56 * 1024 * 1024


def _cp(*sem):
    return pltpu.CompilerParams(dimension_semantics=sem, vmem_limit_bytes=VMEM_LIMIT)


def _mm(a, b):
    return jnp.dot(a, b, preferred_element_type=F32)


def _mm_nt(a, b):
    return lax.dot_general(a, b, (((1,), (1,)), ((), ())), preferred_element_type=F32)


def _mm_tn(a, b):
    return lax.dot_general(a, b, (((0,), (0,)), ((), ())), preferred_element_type=F32)


def _rms(x):
    return x * lax.rsqrt(jnp.mean(x * x, axis=-1, keepdims=True) + EPS)


def _silu(x):
    return x * (1.0 / (1.0 + jnp.exp(-x)))


def _modulate(x, mod_ref, j):
    shift = mod_ref[0, 3 * j:3 * j + 1, :]
    scale = mod_ref[0, 3 * j + 1:3 * j + 2, :]
    return _rms(x) * (1.0 + scale) + shift


def _gate(mod_ref, j):
    return mod_ref[0, 3 * j + 2:3 * j + 3, :]


def _const_spec(a):
    nd = a.ndim
    return pl.BlockSpec(a.shape, lambda *_: (0,) * nd)


def _mod_spec(tm, rows_per_mod):
    return pl.BlockSpec((1, 6, D), lambda i, *_: ((i * tm) // rows_per_mod, 0, 0))


def _adaln_kernel(c_ref, w_ref, b_ref, o_ref):
    s = _silu(c_ref[...]).astype(BF)
    o_ref[0] = _mm(s, w_ref[0].astype(BF)) + b_ref[0]


def adaln_all(cvec8, w_mod, b_mod):
    nl, d, n = w_mod.shape
    tn = 1536
    return pl.pallas_call(
        _adaln_kernel,
        grid=(nl, n // tn),
        in_specs=[pl.BlockSpec((8, d), lambda l, j: (0, 0)),
                  pl.BlockSpec((1, d, tn), lambda l, j: (l, 0, j)),
                  pl.BlockSpec((1, 1, tn), lambda l, j: (l, 0, j))],
        out_specs=pl.BlockSpec((1, 8, tn), lambda l, j: (l, 0, j)),
        out_shape=jax.ShapeDtypeStruct((nl, 8, n), F32),
        compiler_params=_cp("arbitrary", "arbitrary"),
        name="adaln",
    )(cvec8, w_mod, b_mod.reshape(nl, 1, n))


def _split3(x):
    hi = x.astype(BF)
    rem = x - hi.astype(F32)
    mid = rem.astype(BF)
    lo = (rem - mid.astype(F32)).astype(BF)
    return hi, mid, lo


def _mm3(a, pieces):
    return _mm(a, pieces[0]) + _mm(a, pieces[1]) + _mm(a, pieces[2])


def _gla_pre_kernel(x_ref, mod_ref, wp_ref, wlr_ref, wgk2_ref, bgk_ref,
                    qt_ref, kt_ref, kd_ref, v_ref, g_ref, dec_ref):
    tm = x_ref.shape[0]
    C = GLA_CHUNK
    G = 256
    h = _modulate(x_ref[...], mod_ref, 0).astype(BF)
    nk = GLA_H * GLA_DK
    nv = GLA_H * GLA_DV
    q = _mm(h, wp_ref[:, 0:nk]) * GLA_SCALE
    k = _mm(h, wp_ref[:, nk:2 * nk])
    v_ref[...] = _mm(h, wp_ref[:, 2 * nk:2 * nk + nv]).astype(BF)
    g_ref[...] = _mm(h, wp_ref[:, 2 * nk + nv:2 * nk + 2 * nv])
    lr = _mm(h, wlr_ref[...]).astype(BF)
    ii = lax.broadcasted_iota(jnp.int32, (G, G), 0)
    jj = lax.broadcasted_iota(jnp.int32, (G, G), 1)
    same = (ii // C) == (jj // C)
    ones_blk = jnp.where(same, 1.0, 0.0).astype(BF)
    ci = lax.broadcasted_iota(jnp.int32, (tm // C, tm), 0)
    cj = lax.broadcasted_iota(jnp.int32, (tm // C, tm), 1)
    sel = jnp.where((cj // C) == ci, 1.0, 0.0).astype(BF)
    for r in range(2):
        z = _mm(lr, wgk2_ref[r]) + bgk_ref[r:r + 1, :]
        gk = (jnp.minimum(z, 0.0) - jnp.log1p(jnp.exp(-jnp.abs(z)))) * GLA_INV_NORMALIZER
        pieces = _split3(gk)
        tri = jnp.where(jnp.logical_and(same, (jj <= ii) if r == 0 else (jj >= ii)), 1.0, 0.0).astype(BF)
        dec_ref[r] = jnp.exp(_mm3(sel, pieces))
        for gi in range(tm // G):
            rows = slice(gi * G, (gi + 1) * G)
            pg = tuple(p[rows] for p in pieces)
            b = _mm3(tri, pg)
            tot = _mm3(ones_blk, pg)
            qt_ref[r, rows, :] = (q[rows] * jnp.exp(b)).astype(BF)
            kt_ref[r, rows, :] = (k[rows] * jnp.exp(-b)).astype(BF)
            kd_ref[r, rows, :] = (k[rows] * jnp.exp(tot - b)).astype(BF)


def _gla_scan_kernel(*refs, L, S, has_h0, emit_state):
    qt_ref, kt_ref, kd_ref, v_ref, g_ref, dec_ref, gn_ref = refs[:7]
    pos = 7
    h0_ref = None
    if has_h0:
        h0_ref = refs[pos]
        pos += 1
    y_ref = refs[pos]
    pos += 1
    st_ref = None
    if emit_state:
        st_ref = refs[pos]
        pos += 1
    st_s, of_s, ob_s = refs[pos], refs[pos + 1], refs[pos + 2]

    C = GLA_CHUNK
    n_chunks = L // C
    ii = lax.broadcasted_iota(jnp.int32, (C, C), 0)
    jj = lax.broadcasted_iota(jnp.int32, (C, C), 1)
    keeps = ((jj <= ii), (jj >= ii))
    chains = [(sq, r, hd) for sq in range(S) for r in range(2) for hd in range(GLA_H)]
    slot = lambda sq, r, hd: (sq * 2 + r) * GLA_H + hd
    for sq, r, hd in chains:
        if has_h0:
            st_s[slot(sq, r, hd)] = h0_ref[sq, r, hd].T
        else:
            st_s[slot(sq, r, hd)] = jnp.zeros((GLA_DV, GLA_DK), F32)

    def step(i):
        for sq, r, hd in chains:
            n = i if r == 0 else n_chunks - 1 - i
            start = sq * L + n * C
            if not isinstance(start, int):
                start = pl.multiple_of(start, C)
            rows = pl.ds(start, C)
            o_s = of_s if r == 0 else ob_s
            kc = slice(hd * GLA_DK, (hd + 1) * GLA_DK)
            vc = slice(hd * GLA_DV, (hd + 1) * GLA_DV)
            qt = qt_ref[r, rows, kc]
            v = v_ref[rows, vc]
            att = jnp.where(keeps[r], _mm_nt(qt, kt_ref[r, rows, kc]), 0.0).astype(BF)
            st = st_s[slot(sq, r, hd)]
            o_s[rows, vc] = _mm_nt(qt, st.astype(BF)) + _mm(att, v)
            st_s[slot(sq, r, hd)] = st * dec_ref[r, sq, n, :, kc] + _mm_tn(v, kd_ref[r, rows, kc])

    if n_chunks <= 4:
        for i in range(n_chunks):
            step(i)
    else:
        def body(i, carry):
            step(i)
            return carry
        lax.fori_loop(0, n_chunks, body, 0)

    RC = 256
    for c in range(S * L // RC):
        rows = slice(c * RC, (c + 1) * RC)
        for hd in range(GLA_H):
            vc = slice(hd * GLA_DV, (hd + 1) * GLA_DV)
            y = _rms(of_s[rows, vc] + ob_s[rows, vc]) * gn_ref[...]
            y_ref[rows, vc] = (y * _silu(g_ref[rows, vc])).astype(BF)
    if emit_state:
        for sq, r, hd in chains:
            st_ref[sq, r, hd] = st_s[slot(sq, r, hd)].T


def _residual_out_kernel(y_ref, x_ref, mod_ref, w_ref, o_ref, *, gate_j):
    o_ref[...] = x_ref[...] + _gate(mod_ref, gate_j) * _mm(y_ref[...], w_ref[...])


def gla_mixer(x, mod, w, L, h0):
    M = x.shape[0]
    B = M // L
    tm = 512
    nk, nv = GLA_H * GLA_DK, GLA_H * GLA_DV
    row = lambda n: pl.BlockSpec((tm, n), lambda i: (i, 0))
    dirs = lambda: pl.BlockSpec((2, tm, nk), lambda i: (0, i, 0))
    n_chunks = L // GLA_CHUNK
    qt, kt, kd, v, g, dec = pl.pallas_call(
        _gla_pre_kernel,
        grid=(M // tm,),
        in_specs=[row(D), _mod_spec(tm, L if mod.shape[0] > 1 else M),
                  _const_spec(w["wp"]), _const_spec(w["wlr"]), _const_spec(w["wgk2"]),
                  _const_spec(w["bgk"])],
        out_specs=[dirs(), dirs(), dirs(), row(nv), row(nv),
                   pl.BlockSpec((2, tm // GLA_CHUNK, nk), lambda i: (0, i, 0))],
        out_shape=[jax.ShapeDtypeStruct((2, M, nk), BF), jax.ShapeDtypeStruct((2, M, nk), BF),
                   jax.ShapeDtypeStruct((2, M, nk), BF),
                   jax.ShapeDtypeStruct((M, nv), BF), jax.ShapeDtypeStruct((M, nv), F32),
                   jax.ShapeDtypeStruct((2, M // GLA_CHUNK, nk), F32)],
        compiler_params=_cp("arbitrary"),
        name="gla_pre",
    )(x, mod, w["wp"], w["wlr"], w["wgk2"], w["bgk"])
    dec = dec.reshape(2, B, n_chunks, 1, nk)

    has_h0 = h0 is not None
    emit_state = not has_h0
    S = 2 if (L <= 256 and B % 2 == 0) else 1
    SL = S * L
    dir_spec = lambda: pl.BlockSpec((2, SL, nk), lambda b: (0, b, 0))
    in_specs = [dir_spec(), dir_spec(), dir_spec(),
                pl.BlockSpec((SL, nv), lambda b: (b, 0)),
                pl.BlockSpec((SL, nv), lambda b: (b, 0)),
                pl.BlockSpec((2, S, n_chunks, 1, nk), lambda b: (0, b, 0, 0, 0)),
                pl.BlockSpec((1, GLA_DV), lambda b: (0, 0))]
    args = [qt, kt, kd, v, g, dec, w["gnorm"]]
    st_spec = pl.BlockSpec((S, 2, GLA_H, GLA_DK, GLA_DV), lambda b: (b, 0, 0, 0, 0))
    if has_h0:
        in_specs.append(st_spec)
        args.append(h0)
    out_specs = [pl.BlockSpec((SL, nv), lambda b: (b, 0))]
    out_shape = [jax.ShapeDtypeStruct((M, nv), BF)]
    if emit_state:
        out_specs.append(st_spec)
        out_shape.append(jax.ShapeDtypeStruct((B, 2, GLA_H, GLA_DK, GLA_DV), F32))
    res = pl.pallas_call(
        functools.partial(_gla_scan_kernel, L=L, S=S, has_h0=has_h0, emit_state=emit_state),
        grid=(B // S,),
        in_specs=in_specs,
        out_specs=out_specs,
        out_shape=out_shape,
        scratch_shapes=[pltpu.VMEM((S * 2 * GLA_H, GLA_DV, GLA_DK), F32), pltpu.VMEM((SL, nv), F32),
                        pltpu.VMEM((SL, nv), F32)],
        compiler_params=_cp("arbitrary"),
        name="gla_scan",
    )(*args)
    y = res[0]
    state = res[1] if emit_state else None

    x_new = pl.pallas_call(
        functools.partial(_residual_out_kernel, gate_j=0),
        grid=(M // tm,),
        in_specs=[row(nv), row(D), _mod_spec(tm, L if mod.shape[0] > 1 else M), _const_spec(w["wo"])],
        out_specs=row(D),
        out_shape=jax.ShapeDtypeStruct((M, D), F32),
        compiler_params=_cp("arbitrary"),
        name="gla_post",
    )(y, x, mod, w["wo"])
    return x_new, state


def _softmax_sink_heads(q_of, k_of, v_of, sink_ref, o_s):
    for hq in range(SWA_H):
        kh = hq // (SWA_H // SWA_KVH)
        s = _mm_nt(q_of(hq), k_of(kh))
        sink = sink_ref[hq]
        m = jnp.maximum(jnp.max(s, axis=-1, keepdims=True), sink)
        e = jnp.exp(s - m)
        inv = 1.0 / (jnp.sum(e, axis=-1, keepdims=True) + jnp.exp(sink - m))
        o_s[:, hq * SWA_HD:(hq + 1) * SWA_HD] = (_mm(e.astype(BF), v_of(kh)) * inv).astype(BF)


def _swa_ctx_kernel(sink_ref, x_ref, mod_ref, wqkv_ref, wo_ref, o_ref, k_out, v_out, o_s):
    x = x_ref[...]
    h = _modulate(x, mod_ref, 0).astype(BF)
    qkv = _mm(h, wqkv_ref[...])
    k_out[...] = qkv[:, SWA_NQ:SWA_NQ + SWA_NK]
    v_out[...] = qkv[:, SWA_NQ + SWA_NK:]
    q_of = lambda hq: (qkv[:, hq * SWA_HD:(hq + 1) * SWA_HD] * SWA_SCALE).astype(BF)
    k_of = lambda kh: qkv[:, SWA_NQ + kh * SWA_HD:SWA_NQ + (kh + 1) * SWA_HD].astype(BF)
    v_of = lambda kh: qkv[:, SWA_NQ + SWA_NK + kh * SWA_HD:SWA_NQ + SWA_NK + (kh + 1) * SWA_HD].astype(BF)
    _softmax_sink_heads(q_of, k_of, v_of, sink_ref, o_s)
    o_ref[...] = x + _gate(mod_ref, 0) * _mm(o_s[...], wo_ref[...])


def swa_ctx_mixer(x, mod, w, L):
    M = x.shape[0]
    row = lambda n: pl.BlockSpec((L, n), lambda i: (i, 0))
    return pl.pallas_call(
        _swa_ctx_kernel,
        grid=(M // L,),
        in_specs=[pl.BlockSpec(memory_space=pltpu.SMEM), row(D), _mod_spec(L, M),
                  _const_spec(w["wqkv"]), _const_spec(w["wo"])],
        out_specs=[row(D), row(SWA_NK), row(SWA_NK)],
        out_shape=[jax.ShapeDtypeStruct((M, D), F32), jax.ShapeDtypeStruct((M, SWA_NK), F32),
                   jax.ShapeDtypeStruct((M, SWA_NK), F32)],
        scratch_shapes=[pltpu.VMEM((L, SWA_NQ), BF)],
        compiler_params=_cp("arbitrary"),
        name="swa_ctx",
    )(w["sink"], x, mod, w["wqkv"], w["wo"])


def _swa_lat_kernel(sink_ref, x_ref, mod_ref, wqkv_ref, wsw_ref, cos_ref, sin_ref, kc_ref, vc_ref,
                    wo_ref, o_ref, q_s, k_s, v_s, o_s, *, L):
    n = pl.program_id(1)
    RC = 256

    @pl.when(n == 0)
    def _():
        for c in range(L // RC):
            rows = slice(c * RC, (c + 1) * RC)
            h = _modulate(x_ref[rows, :], mod_ref, 0).astype(BF)
            qkv = _mm(h, wqkv_ref[...])
            sw = _mm(h, wsw_ref[...])
            cos = cos_ref[rows, :]
            sin = sin_ref[rows, :]
            cq = jnp.concatenate([cos] * (SWA_NQ // 128), axis=1)
            sq = jnp.concatenate([sin] * (SWA_NQ // 128), axis=1)
            ck = jnp.concatenate([cos] * (SWA_NK // 128), axis=1)
            sk = jnp.concatenate([sin] * (SWA_NK // 128), axis=1)
            q = (qkv[:, :SWA_NQ] * SWA_SCALE) * cq + (sw[:, :SWA_NQ] * SWA_SCALE) * sq
            q_s[rows, :] = q.astype(BF)
            k = qkv[:, SWA_NQ:SWA_NQ + SWA_NK] * ck + sw[:, SWA_NQ:SWA_NQ + SWA_NK] * sk
            k_s[rows, :] = k.astype(BF)
            v_s[rows, :] = qkv[:, SWA_NQ + SWA_NK:].astype(BF)

    QB = SWA_BLOCK
    KW = 3 * SWA_BLOCK
    r0 = pl.multiple_of(n * QB, QB)
    ws = pl.multiple_of(jnp.clip((n - 1) * QB, 0, L - KW), QB)
    qpos = r0 + lax.broadcasted_iota(jnp.int32, (QB, KW), 0)
    kpos = ws + lax.broadcasted_iota(jnp.int32, (QB, KW), 1)
    valid = jnp.abs(kpos - qpos) <= SWA_WINDOW
    for hq in range(SWA_H):
        kh = hq // (SWA_H // SWA_KVH)
        hs = slice(kh * SWA_HD, (kh + 1) * SWA_HD)
        q = q_s[pl.ds(r0, QB), hq * SWA_HD:(hq + 1) * SWA_HD]
        s1 = jnp.where(valid, _mm_nt(q, k_s[pl.ds(ws, KW), hs]), NEG)
        s2 = _mm_nt(q, kc_ref[0, :, hs].astype(BF))
        sink = sink_ref[hq]
        m = jnp.maximum(jnp.maximum(jnp.max(s1, axis=-1, keepdims=True),
                                    jnp.max(s2, axis=-1, keepdims=True)), sink)
        e1 = jnp.exp(s1 - m)
        e2 = jnp.exp(s2 - m)
        den = (jnp.sum(e1, axis=-1, keepdims=True) + jnp.sum(e2, axis=-1, keepdims=True)
               + jnp.exp(sink - m))
        o = (_mm(e1.astype(BF), v_s[pl.ds(ws, KW), hs])
             + _mm(e2.astype(BF), vc_ref[0, :, hs].astype(BF)))
        o_s[:, hq * SWA_HD:(hq + 1) * SWA_HD] = (o * (1.0 / den)).astype(BF)
    o_ref[...] = x_ref[pl.ds(r0, QB), :] + _gate(mod_ref, 0) * _mm(o_s[...], wo_ref[...])


def swa_lat_mixer(x, mod, w, L, kc, vc, cos128, sin128):
    M = x.shape[0]
    B = M // L
    P = kc.shape[1]
    nb = L // SWA_BLOCK
    return pl.pallas_call(
        functools.partial(_swa_lat_kernel, L=L),
        grid=(B, nb),
        in_specs=[pl.BlockSpec(memory_space=pltpu.SMEM),
                  pl.BlockSpec((L, D), lambda b, n: (b, 0)),
                  pl.BlockSpec((1, 6, D), lambda b, n: (b, 0, 0)),
                  _const_spec(w["wqkv"]), _const_spec(w["wsw"]),
                  _const_spec(cos128), _const_spec(sin128),
                  pl.BlockSpec((1, P, SWA_NK), lambda b, n: (b, 0, 0)),
                  pl.BlockSpec((1, P, SWA_NK), lambda b, n: (b, 0, 0)),
                  _const_spec(w["wo"])],
        out_specs=pl.BlockSpec((SWA_BLOCK, D), lambda b, n: (b * nb + n, 0)),
        out_shape=jax.ShapeDtypeStruct((M, D), F32),
        scratch_shapes=[pltpu.VMEM((L, SWA_NQ), BF), pltpu.VMEM((L, SWA_NK), BF),
                        pltpu.VMEM((L, SWA_NK), BF), pltpu.VMEM((SWA_BLOCK, SWA_NQ), BF)],
        compiler_params=_cp("arbitrary", "arbitrary"),
        name="swa_lat",
    )(w["sink"], x, mod, w["wqkv"], w["wsw"], cos128, sin128, kc, vc, w["wo"])


def _conv_kernel(x_ref, mod_ref, win_ref, cw_ref, wout_ref, o_ref, y_s, *, L):
    x = x_ref[...]
    h = _modulate(x, mod_ref, 0).astype(BF)
    CC = 256
    row = lax.broadcasted_iota(jnp.int32, (L, CC), 0)
    for c in range(D // CC):
        cols = slice(c * CC, (c + 1) * CC)
        bg = _mm(h, win_ref[:, c * CC:(c + 1) * CC])
        cg = _mm(h, win_ref[:, D + c * CC:D + (c + 1) * CC])
        u = _mm(h, win_ref[:, 2 * D + c * CC:2 * D + (c + 1) * CC])
        cu = cg * u
        prev = jnp.where(row == 0, 0.0, pltpu.roll(cu, 1, 0))
        nxt = jnp.where(row == L - 1, 0.0, pltpu.roll(cu, L - 1, 0))
        conv = prev * cw_ref[0:1, cols] + cu * cw_ref[1:2, cols] + nxt * cw_ref[2:3, cols]
        y_s[:, cols] = (bg * conv).astype(BF)
    o_ref[...] = x + _gate(mod_ref, 0) * _mm(y_s[...], wout_ref[...])


def conv_mixer(x, mod, w, L):
    M = x.shape[0]
    row = pl.BlockSpec((L, D), lambda i: (i, 0))
    return pl.pallas_call(
        functools.partial(_conv_kernel, L=L),
        grid=(M // L,),
        in_specs=[row, _mod_spec(L, L if mod.shape[0] > 1 else M), _const_spec(w["win"]),
                  _const_spec(w["cw"]), _const_spec(w["wout"])],
        out_specs=row,
        out_shape=jax.ShapeDtypeStruct((M, D), F32),
        scratch_shapes=[pltpu.VMEM((L, D), BF)],
        compiler_params=_cp("arbitrary"),
        name="conv_mix",
    )(x, mod, w["win"], w["cw"], w["wout"])


def _mla_ctx_kernel(x_ref, mod_ref, wdq_ref, qn_ref, wuq_ref, wdkv_ref, kvn_ref, wukv_ref, wo_ref,
                    o_ref, ckv_out, kr_out, o_s):
    x = x_ref[...]
    h = _modulate(x, mod_ref, 0).astype(BF)
    cq = (_rms(_mm(h, wdq_ref[...])) * qn_ref[...]).astype(BF)
    q = _mm(cq, wuq_ref[...]).astype(BF)
    kvc = _mm(h, wdkv_ref[...])
    ckv = _rms(kvc[:, :KV_LORA]) * kvn_ref[...]
    kr = kvc[:, KV_LORA:]
    ckv_out[...] = ckv
    kr_out[...] = kr
    kv = _mm(ckv.astype(BF), wukv_ref[...]).astype(BF)
    krb = kr.astype(BF)
    nn = MLA_H * QK_NOPE
    for hh in range(MLA_H):
        qh = q[:, hh * MLA_QW:(hh + 1) * MLA_QW]
        kh = jnp.concatenate([kv[:, hh * QK_NOPE:(hh + 1) * QK_NOPE], krb], axis=1)
        s = _mm_nt(qh, kh) * MLA_SCALE
        m = jnp.max(s, axis=-1, keepdims=True)
        e = jnp.exp(s - m)
        inv = 1.0 / jnp.sum(e, axis=-1, keepdims=True)
        vh = kv[:, nn + hh * V_HEAD:nn + (hh + 1) * V_HEAD]
        o_s[:, hh * V_HEAD:(hh + 1) * V_HEAD] = (_mm(e.astype(BF), vh) * inv).astype(BF)
    o_ref[...] = x + _gate(mod_ref, 0) * _mm(o_s[...], wo_ref[...])


def mla_ctx_mixer(x, mod, w, L):
    M = x.shape[0]
    row = lambda n: pl.BlockSpec((L, n), lambda i: (i, 0))
    ws = [w["wdq"], w["qnorm"], w["wuq"], w["wdkv"], w["kvnorm"], w["wukv"], w["wo"]]
    return pl.pallas_call(
        _mla_ctx_kernel,
        grid=(M // L,),
        in_specs=[row(D), _mod_spec(L, M)] + [_const_spec(a) for a in ws],
        out_specs=[row(D), row(KV_LORA), row(128)],
        out_shape=[jax.ShapeDtypeStruct((M, D), F32), jax.ShapeDtypeStruct((M, KV_LORA), F32),
                   jax.ShapeDtypeStruct((M, 128), F32)],
        scratch_shapes=[pltpu.VMEM((L, MLA_H * V_HEAD), BF)],
        compiler_params=_cp("arbitrary"),
        name="mla_ctx",
    )(x, mod, *ws)


def _mla_lat_kernel(x_ref, mod_ref, wdq_ref, qn_ref, wuq_ref, wuqsw_ref, wdkv_ref, wdkvsw_ref,
                    kvn_ref, wukv_ref, cq_ref, sq_ref, ck_ref, sk_ref, ckvc_ref, krc_ref, wo_ref,
                    o_ref, q_s, kn_s, v_s, kr_s, knc_s, vc_s, o_s, *, L, QB):
    n = pl.program_id(1)
    RC = 256
    nn = MLA_H * QK_NOPE

    @pl.when(n == 0)
    def _():
        for c in range(L // RC):
            rows = slice(c * RC, (c + 1) * RC)
            h = _modulate(x_ref[rows, :], mod_ref, 0).astype(BF)
            cq = (_rms(_mm(h, wdq_ref[...])) * qn_ref[...]).astype(BF)
            cosq = jnp.concatenate([cq_ref[rows, :]] * MLA_H, axis=1)
            sinq = jnp.concatenate([sq_ref[rows, :]] * MLA_H, axis=1)
            q = _mm(cq, wuq_ref[...]) * cosq + _mm(cq, wuqsw_ref[...]) * sinq
            q_s[rows, :] = q.astype(BF)
            kvc = _mm(h, wdkv_ref[...])
            ksw = _mm(h, wdkvsw_ref[...])
            ckv = _rms(kvc[:, :KV_LORA]) * kvn_ref[...]
            kr_s[rows, :] = (kvc[:, KV_LORA:] * ck_ref[rows, :] + ksw * sk_ref[rows, :]).astype(BF)
            kv = _mm(ckv.astype(BF), wukv_ref[...])
            kn_s[rows, :] = kv[:, :nn].astype(BF)
            v_s[rows, :] = kv[:, nn:].astype(BF)
        kvp = _mm(ckvc_ref[0].astype(BF), wukv_ref[...])
        knc_s[...] = kvp[:, :nn].astype(BF)
        vc_s[...] = kvp[:, nn:].astype(BF)

    r0 = pl.multiple_of(n * QB, QB)
    krc = krc_ref[0].astype(BF)
    for hh in range(MLA_H):
        ns = slice(hh * QK_NOPE, (hh + 1) * QK_NOPE)
        qh = q_s[pl.ds(r0, QB), hh * MLA_QW:(hh + 1) * MLA_QW]
        k1 = jnp.concatenate([kn_s[:, ns], kr_s[...]], axis=1)
        k2 = jnp.concatenate([knc_s[:, ns], krc], axis=1)
        s1 = _mm_nt(qh, k1) * MLA_SCALE
        s2 = _mm_nt(qh, k2) * MLA_SCALE
        m = jnp.maximum(jnp.max(s1, axis=-1, keepdims=True), jnp.max(s2, axis=-1, keepdims=True))
        e1 = jnp.exp(s1 - m)
        e2 = jnp.exp(s2 - m)
        den = jnp.sum(e1, axis=-1, keepdims=True) + jnp.sum(e2, axis=-1, keepdims=True)
        vs = slice(hh * V_HEAD, (hh + 1) * V_HEAD)
        o = _mm(e1.astype(BF), v_s[:, vs]) + _mm(e2.astype(BF), vc_s[:, vs])
        o_s[:, vs] = (o * (1.0 / den)).astype(BF)
    o_ref[...] = x_ref[pl.ds(r0, QB), :] + _gate(mod_ref, 0) * _mm(o_s[...], wo_ref[...])


def mla_lat_mixer(x, mod, w, L, ckv_c, kr_c128, tabs):
    M = x.shape[0]
    B = M // L
    P = ckv_c.shape[1]
    QB = 256
    nb = L // QB
    ws1 = [w["wdq"], w["qnorm"], w["wuq"], w["wuqsw"], w["wdkv"], w["wdkvsw"], w["kvnorm"], w["wukv"]]
    nv = MLA_H * V_HEAD
    return pl.pallas_call(
        functools.partial(_mla_lat_kernel, L=L, QB=QB),
        grid=(B, nb),
        in_specs=[pl.BlockSpec((L, D), lambda b, n: (b, 0)),
                  pl.BlockSpec((1, 6, D), lambda b, n: (b, 0, 0))]
                 + [_const_spec(a) for a in ws1] + [_const_spec(a) for a in tabs]
                 + [pl.BlockSpec((1, P, KV_LORA), lambda b, n: (b, 0, 0)),
                    pl.BlockSpec((1, P, 128), lambda b, n: (b, 0, 0)),
                    _const_spec(w["wo"])],
        out_specs=pl.BlockSpec((QB, D), lambda b, n: (b * nb + n, 0)),
        out_shape=jax.ShapeDtypeStruct((M, D), F32),
        scratch_shapes=[pltpu.VMEM((L, MLA_H * MLA_QW), BF), pltpu.VMEM((L, nv), BF),
                        pltpu.VMEM((L, nv), BF), pltpu.VMEM((L, 128), BF),
                        pltpu.VMEM((P, nv), BF), pltpu.VMEM((P, nv), BF), pltpu.VMEM((QB, nv), BF)],
        compiler_params=_cp("arbitrary", "arbitrary"),
        name="mla_lat",
    )(x, mod, *ws1, *tabs, ckv_c, kr_c128, w["wo"])


def _swiglu_partial(xb, wg_ref, wu_ref, wd_ref):
    hg = _mm(xb, wg_ref[...].astype(BF))
    hu = _mm(xb, wu_ref[...].astype(BF))
    a = (_silu(hg) * hu).astype(BF)
    return _mm(a, wd_ref[...].astype(BF))


def _dense_ffn_kernel(x_ref, mod_ref, wga_ref, wgb_ref, wua_ref, wub_ref, wda_ref, wdb_ref, o_ref, xm_s,
                      *, odd_blocks):
    f = pl.program_id(1)
    last = pl.num_programs(1) - 1

    @pl.when(f == 0)
    def _():
        xm_s[...] = _modulate(x_ref[...], mod_ref, 1).astype(BF)
        o_ref[...] = jnp.zeros_like(o_ref)

    def pair():
        xb = xm_s[...]
        acts = []
        for wg_ref, wu_ref in ((wga_ref, wua_ref), (wgb_ref, wub_ref)):
            hg = _mm(xb, wg_ref[...].astype(BF))
            hu = _mm(xb, wu_ref[...].astype(BF))
            acts.append((_silu(hg) * hu).astype(BF))
        wd = jnp.concatenate([wda_ref[...].astype(BF), wdb_ref[...].astype(BF)], axis=0)
        return _mm(jnp.concatenate(acts, axis=1), wd)

    if odd_blocks:
        @pl.when(f < last)
        def _():
            o_ref[...] += pair()

        @pl.when(f == last)
        def _():
            o_ref[...] += _swiglu_partial(xm_s[...], wga_ref, wua_ref, wda_ref)
    else:
        o_ref[...] += pair()

    @pl.when(f == last)
    def _():
        o_ref[...] = x_ref[...] + _gate(mod_ref, 1) * o_ref[...]


def dense_ffn(x, mod, wg, wu, wd, j, L):
    M = x.shape[0]
    F = wg.shape[-1]
    tm, tf = DENSE_TM, DENSE_TF
    nfb = F // tf
    blk_a = lambda f: 2 * f
    blk_b = lambda f: jnp.minimum(2 * f + 1, nfb - 1)
    col = lambda blk: pl.BlockSpec((None, D, tf), lambda i, f: (j, 0, blk(f)))
    row = lambda blk: pl.BlockSpec((None, tf, D), lambda i, f: (j, blk(f), 0))
    return pl.pallas_call(
        functools.partial(_dense_ffn_kernel, odd_blocks=nfb % 2 == 1),
        grid=(M // tm, (nfb + 1) // 2),
        in_specs=[pl.BlockSpec((tm, D), lambda i, f: (i, 0)),
                  pl.BlockSpec((1, 6, D), lambda i, f: ((i * tm) // (L if mod.shape[0] > 1 else M), 0, 0)),
                  col(blk_a), col(blk_b), col(blk_a), col(blk_b), row(blk_a), row(blk_b)],
        out_specs=pl.BlockSpec((tm, D), lambda i, f: (i, 0)),
        out_shape=jax.ShapeDtypeStruct((M, D), F32),
        scratch_shapes=[pltpu.VMEM((tm, D), BF)],
        compiler_params=_cp("arbitrary", "arbitrary"),
        name="dense_ffn",
    )(x, mod, wg, wg, wu, wu, wd, wd)


def _router_kernel(xp_ref, xs_ref, mod_ref, wr_ref, xm_ref, idx_ref, w_ref, cnt_ref, cnt_s, *, np_tiles):
    i = pl.program_id(0)
    tm = xm_ref.shape[0]

    @pl.when(i == 0)
    def _():
        cnt_s[...] = jnp.zeros_like(cnt_s)

    x = jnp.where(i < np_tiles, xp_ref[...], xs_ref[...])
    xm = _modulate(x, mod_ref, 1).astype(BF)
    xm_ref[...] = xm
    lane = lax.broadcasted_iota(jnp.int32, idx_ref.shape, 1)
    logits = jnp.where(lane < N_EXPERTS, _mm(xm, wr_ref[...]), -jnp.inf)
    m1 = jnp.max(logits, axis=-1, keepdims=True)
    i1 = jnp.min(jnp.where(logits == m1, lane, 128), axis=-1, keepdims=True)
    rest = jnp.where(lane == i1, -jnp.inf, logits)
    m2 = jnp.max(rest, axis=-1, keepdims=True)
    i2 = jnp.min(jnp.where(rest == m2, lane, 128), axis=-1, keepdims=True)
    e = jnp.exp(m2 - m1)
    w_ref[...] = jnp.where(lane == 0, 1.0 / (1.0 + e), e / (1.0 + e))
    oh1 = jnp.where(lane == i1, 1.0, 0.0)
    oh2 = jnp.where(lane == i2, 1.0, 0.0)
    rr = lax.broadcasted_iota(jnp.int32, (tm, tm), 0)
    cc = lax.broadcasted_iota(jnp.int32, (tm, tm), 1)
    below = jnp.where(cc < rr, 1.0, 0.0).astype(BF)
    run = cnt_s[0:1, :]
    tot1 = jnp.sum(oh1, axis=0, keepdims=True)
    tot2 = jnp.sum(oh2, axis=0, keepdims=True)
    r1 = jnp.sum(oh1 * (run + _mm(below, oh1.astype(BF))), axis=-1, keepdims=True)
    r2 = jnp.sum(oh2 * (run + tot1 + _mm(below, oh2.astype(BF))), axis=-1, keepdims=True)
    idx_ref[...] = jnp.where(lane == 0, i1, jnp.where(lane == 1, i2, jnp.where(
        lane == 2, r1.astype(jnp.int32), r2.astype(jnp.int32))))
    total = run + tot1 + tot2
    cnt_s[...] = jnp.broadcast_to(total, cnt_s.shape)
    cnt_ref[...] = jnp.broadcast_to(total, cnt_ref.shape)


def route(xp, xs, mod3, wr, Ls):
    Mp, Ms = xp.shape[0], xs.shape[0]
    T = Mp + Ms
    tm = 512
    npt = Mp // tm
    row = lambda n: pl.BlockSpec((tm, n), lambda i: (i, 0))
    return pl.pallas_call(
        functools.partial(_router_kernel, np_tiles=npt),
        grid=(T // tm,),
        in_specs=[pl.BlockSpec((tm, D), lambda i: (jnp.minimum(i, npt - 1), 0)),
                  pl.BlockSpec((tm, D), lambda i: (jnp.maximum(i - npt, 0), 0)),
                  pl.BlockSpec((1, 6, D), lambda i: (jnp.where(i < npt, 0, 1 + ((i - npt) * tm) // Ls), 0, 0)),
                  _const_spec(wr)],
        out_specs=[row(D), row(128), row(128), pl.BlockSpec((8, 128), lambda i: (0, 0))],
        out_shape=[jax.ShapeDtypeStruct((T, D), BF), jax.ShapeDtypeStruct((T, 128), jnp.int32),
                   jax.ShapeDtypeStruct((T, 128), F32), jax.ShapeDtypeStruct((8, 128), F32)],
        scratch_shapes=[pltpu.VMEM((8, 128), F32)],
        compiler_params=_cp("arbitrary"),
        name="moe_route",
    )(xp, xs, mod3, wr)


def _moe_ffn_kernel(vb_ref, ve_ref, vm_ref, nv_ref, x_ref, wg_ref, wu_ref, wd_ref, o_ref):
    v = pl.program_id(0)
    f = pl.program_id(1)
    valid = v < nv_ref[0]
    mode = vm_ref[v]
    H = MOE_TM

    def run(rows):
        @pl.when(f == 0)
        def _():
            o_ref[rows, :] = jnp.zeros((rows.stop - rows.start, D), F32)

        o_ref[rows, :] += _swiglu_partial(x_ref[rows, :], wg_ref, wu_ref, wd_ref)

    for m, rows in ((0, slice(0, 2 * H)), (1, slice(0, H)), (2, slice(H, 2 * H))):
        @pl.when(jnp.logical_and(valid, mode == m))
        def _(rows=rows):
            run(rows)


def moe_ffn(xs, tile_expert, n_valid, wg, wu, wd, j):
    R = xs.shape[0]
    F = wg.shape[-1]
    bm, tf = 2 * MOE_TM, MOE_TF
    nf = F // tf
    nb = R // bm
    n_vis = nb + N_EXPERTS - 1
    blocks = jnp.arange(nb, dtype=jnp.int32)
    t0, t1 = tile_expert[0::2], tile_expert[1::2]
    valid0 = 2 * blocks < n_valid[0]
    valid1 = 2 * blocks + 1 < n_valid[0]
    two = jnp.logical_and(valid1, t0 != t1)
    cnt = jnp.where(valid0, 1 + two.astype(jnp.int32), 0)
    cum = jnp.cumsum(cnt)
    nv = cum[-1:].astype(jnp.int32)
    v_eff = jnp.minimum(jnp.arange(n_vis, dtype=jnp.int32), nv[0] - 1)
    vb = jnp.sum((v_eff[:, None] >= cum[None, :]).astype(jnp.int32), axis=1)
    pick = lambda a: jnp.sum(jnp.where(vb[:, None] == blocks[None, :], a[None, :].astype(jnp.int32), 0), axis=1)
    k = v_eff - pick(cum - cnt)
    two_v, t0_v, t1_v, valid1_v = pick(two), pick(t0), pick(t1), pick(valid1)
    vm = jnp.where(two_v == 1, jnp.where(k == 0, 1, 2), jnp.where(valid1_v == 1, 0, 1)).astype(jnp.int32)
    ve = jnp.where(jnp.logical_and(two_v == 1, k == 1), t1_v, t0_v).astype(jnp.int32)

    def fidx(v, f, nv):
        return jnp.where(v < nv[0], f, nf - 1)

    grid_spec = pltpu.PrefetchScalarGridSpec(
        num_scalar_prefetch=4,
        grid=(n_vis, nf),
        in_specs=[pl.BlockSpec((bm, D), lambda v, f, vb, ve, vm, nv: (vb[v], 0)),
                  pl.BlockSpec((None, None, D, tf), lambda v, f, vb, ve, vm, nv: (j, ve[v], 0, fidx(v, f, nv))),
                  pl.BlockSpec((None, None, D, tf), lambda v, f, vb, ve, vm, nv: (j, ve[v], 0, fidx(v, f, nv))),
                  pl.BlockSpec((None, None, tf, D), lambda v, f, vb, ve, vm, nv: (j, ve[v], fidx(v, f, nv), 0))],
        out_specs=pl.BlockSpec((bm, D), lambda v, f, vb, ve, vm, nv: (vb[v], 0)),
    )
    return pl.pallas_call(
        _moe_ffn_kernel,
        grid_spec=grid_spec,
        out_shape=jax.ShapeDtypeStruct((R, D), F32),
        compiler_params=_cp("arbitrary", "arbitrary"),
        name="moe_ffn",
    )(vb.astype(jnp.int32), ve, vm, nv, xs, wg, wu, wd)


def _combine_kernel(x_ref, mod_ref, ya_ref, yb_ref, w_ref, fn_ref, o_ref, *, final):
    y = w_ref[:, 0:1] * ya_ref[...] + w_ref[:, 1:2] * yb_ref[...]
    o = x_ref[...] + _gate(mod_ref, 1) * y
    if final:
        o = _rms(o) * fn_ref[...]
    o_ref[...] = o


def moe_combine(x, mod, ya, yb, w, fn, L, final, row_off):
    M = x.shape[0]
    tm = 512
    off = row_off // tm
    row = pl.BlockSpec((tm, D), lambda i: (i, 0))
    row_o = lambda n: pl.BlockSpec((tm, n), lambda i: (i + off, 0))
    return pl.pallas_call(
        functools.partial(_combine_kernel, final=final),
        grid=(M // tm,),
        in_specs=[row, _mod_spec(tm, L if mod.shape[0] > 1 else M), row_o(D), row_o(D), row_o(128),
                  _const_spec(fn)],
        out_specs=row,
        out_shape=jax.ShapeDtypeStruct((M, D), F32),
        compiler_params=_cp("arbitrary"),
        name="moe_combine",
    )(x, mod, ya, yb, w, fn)


def moe_layer(xp, xs, mod3, wr, wg, wu, wd, j, fn, Lp, Ls, final):
    Mp = xp.shape[0]
    xm, idx, wts, cnt = route(xp, xs, mod3, wr, Ls)
    T = xm.shape[0]
    tm = MOE_TM
    R = 2 * T + N_EXPERTS * tm
    counts = cnt[0, :N_EXPERTS].astype(jnp.int32)
    tiles_per = (counts + tm - 1) // tm
    tile_end = jnp.cumsum(tiles_per)
    offs = (tile_end - tiles_per) * tm
    experts = jnp.arange(N_EXPERTS, dtype=jnp.int32)
    dest = jnp.sum(jnp.where(idx[:, 0:2, None] == experts, offs, 0), axis=-1) + idx[:, 2:4]
    tok = jnp.arange(2 * T, dtype=jnp.int32) // 2
    src = jnp.zeros((R,), jnp.int32).at[dest.reshape(-1)].set(
        tok, unique_indices=True, mode="promise_in_bounds")
    n_tiles = R // tm
    n_valid = tile_end[-1:].astype(jnp.int32)
    t_ids = jnp.arange(n_tiles, dtype=jnp.int32)
    te = jnp.sum((t_ids[:, None] >= tile_end[None, :]).astype(jnp.int32), axis=1)
    te_last = jnp.sum((n_valid - 1 >= tile_end).astype(jnp.int32))
    te = jnp.where(t_ids < n_valid[0], te, te_last).astype(jnp.int32)
    x_sorted = xm.at[src].get(mode="promise_in_bounds")
    y_sorted = moe_ffn(x_sorted, te, n_valid, wg, wu, wd, j)
    ya = y_sorted.at[dest[:, 0]].get(mode="promise_in_bounds")
    yb = y_sorted.at[dest[:, 1]].get(mode="promise_in_bounds")
    xp_new = moe_combine(xp, mod3[0:1], ya, yb, wts, fn, Lp, final, 0)
    xs_new = moe_combine(xs, mod3[1:], ya, yb, wts, fn, Ls, final, Mp)
    return xp_new, xs_new


def _rope_partner_perm(n):
    c = np.arange(n)
    return np.where((c % 32) < 16, c + 16, c - 16)


def _rope_tables(L):
    nf = 16
    inv = ROPE_THETA ** (-np.arange(nf, dtype=np.float32) / nf)
    pos = np.arange(L)
    ang = np.stack([(pos // GRID_W).astype(np.float32), (pos % GRID_W).astype(np.float32)],
                   axis=-1)[:, :, None] * inv
    cos = jnp.cos(jnp.asarray(ang, F32))
    sin = jnp.sin(jnp.asarray(ang, F32))
    c64 = jnp.stack([cos, cos], axis=2).reshape(L, 64)
    s64 = jnp.stack([-sin, sin], axis=2).reshape(L, 64)
    return c64, s64


def kernel(x_prompt, x_sample, state_l0_gla, cache_l1_k, cache_l1_v, cache_l3_ckv, cache_l3_krope, c, c_ctx, w_mod, b_mod, gla_wq, gla_wk, gla_wv, gla_wg, gla_wgk1, gla_wgk2, gla_bgk, gla_norm, gla_wo, swa_wqkv, swa_sink, swa_wo, conv_win, conv_w, conv_wout, mla_wdq, mla_qnorm, mla_wuq, mla_wdkv, mla_kvnorm, mla_wukv, mla_wo, dense_w_gate, dense_w_up, dense_w_down, moe_router, moe_w_gate, moe_w_up, moe_w_down, final_norm):
    Bp, Lp, _ = x_prompt.shape
    Bs, Ls, _ = x_sample.shape
    P = cache_l1_k.shape[1]
    xp = x_prompt.reshape(Bp * Lp, D)
    xs = x_sample.reshape(Bs * Ls, D)

    cvec = jnp.zeros((8, D), F32).at[0].set(c_ctx).at[1:1 + Bs].set(c)
    mod_all = adaln_all(cvec, w_mod, b_mod).reshape(w_mod.shape[0], 8, 6, D)

    wlr = jnp.zeros((D, 128), F32).at[:, :GLA_RANK].set(gla_wgk1[0]).at[:, GLA_RANK:2 * GLA_RANK].set(gla_wgk1[1])
    wgk2 = jnp.zeros((2, 128, GLA_H * GLA_DK), F32)
    wgk2 = wgk2.at[0, :GLA_RANK].set(gla_wgk2[0]).at[1, GLA_RANK:2 * GLA_RANK].set(gla_wgk2[1])
    gla_w = dict(wp=jnp.concatenate([gla_wq, gla_wk, gla_wv, gla_wg], axis=1).astype(BF),
                 wlr=wlr.astype(BF), wgk2=wgk2.astype(BF), bgk=gla_bgk,
                 gnorm=gla_norm.reshape(1, GLA_DV), wo=gla_wo.astype(BF))
    modp, mods = mod_all[0, 0:1], mod_all[0, 1:1 + Bs]
    xp, new_state = gla_mixer(xp, modp, gla_w, Lp, None)
    xs, _ = gla_mixer(xs, mods, gla_w, Ls, state_l0_gla)
    xp = dense_ffn(xp, modp, dense_w_gate, dense_w_up, dense_w_down, 0, Lp)
    xs = dense_ffn(xs, mods, dense_w_gate, dense_w_up, dense_w_down, 0, Ls)

    c64, s64 = _rope_tables(Ls)
    cos128 = jnp.concatenate([c64, c64], axis=1)
    sin128 = jnp.concatenate([s64, s64], axis=1)
    perm = _rope_partner_perm(SWA_NQ + SWA_NK)
    swa_w = dict(wqkv=swa_wqkv.astype(BF), wsw=swa_wqkv[:, perm].astype(BF), wo=swa_wo.astype(BF),
                 sink=swa_sink)
    modp, mods = mod_all[1, 0:1], mod_all[1, 1:1 + Bs]
    xp, k1, v1 = swa_ctx_mixer(xp, modp, swa_w, Lp)
    xs = swa_lat_mixer(xs, mods, swa_w, Ls, cache_l1_k.reshape(Bs, P, SWA_NK),
                       cache_l1_v.reshape(Bs, P, SWA_NK), cos128, sin128)
    wr = jnp.zeros((moe_router.shape[0], D, 128), F32).at[:, :, :N_EXPERTS].set(moe_router).astype(BF)
    xp, xs = moe_layer(xp, xs, mod_all[1, 0:1 + Bs], wr[0], moe_w_gate, moe_w_up, moe_w_down, 0,
                       final_norm.reshape(1, D), Lp, Ls, False)

    conv_wts = dict(win=conv_win.astype(BF), cw=conv_w, wout=conv_wout.astype(BF))
    modp, mods = mod_all[2, 0:1], mod_all[2, 1:1 + Bs]
    xp = conv_mixer(xp, modp, conv_wts, Lp)
    xs = conv_mixer(xs, mods, conv_wts, Ls)
    xp = dense_ffn(xp, modp, dense_w_gate, dense_w_up, dense_w_down, 1, Lp)
    xs = dense_ffn(xs, mods, dense_w_gate, dense_w_up, dense_w_down, 1, Ls)

    wuq3 = mla_wuq.reshape(Q_LORA, MLA_H, QK_NOPE + QK_ROPE)
    zpad = jnp.zeros((Q_LORA, MLA_H, MLA_QW - QK_NOPE - QK_ROPE), F32)
    wuq = jnp.concatenate([wuq3, zpad], axis=2).reshape(Q_LORA, MLA_H * MLA_QW)
    rperm = _rope_partner_perm(QK_ROPE)
    wuq_sw = jnp.concatenate([jnp.zeros((Q_LORA, MLA_H, QK_NOPE), F32), wuq3[:, :, QK_NOPE:][:, :, rperm], zpad],
                             axis=2).reshape(Q_LORA, MLA_H * MLA_QW)
    kpad = jnp.zeros((D, 128 - QK_ROPE), F32)
    wdkv = jnp.concatenate([mla_wdkv, kpad], axis=1)
    wdkv_sw = jnp.concatenate([mla_wdkv[:, KV_LORA:][:, rperm], kpad], axis=1)
    wukv3 = mla_wukv.reshape(KV_LORA, MLA_H, QK_NOPE + V_HEAD)
    wukv = jnp.concatenate([wukv3[:, :, :QK_NOPE].reshape(KV_LORA, MLA_H * QK_NOPE),
                            wukv3[:, :, QK_NOPE:].reshape(KV_LORA, MLA_H * V_HEAD)], axis=1)
    mla_w = dict(wdq=mla_wdq.astype(BF), qnorm=mla_qnorm.reshape(1, Q_LORA), wuq=wuq.astype(BF),
                 wuqsw=wuq_sw.astype(BF), wdkv=wdkv.astype(BF), wdkvsw=wdkv_sw.astype(BF),
                 kvnorm=mla_kvnorm.reshape(1, KV_LORA), wukv=wukv.astype(BF), wo=mla_wo.astype(BF))
    ones = jnp.ones((Ls, QK_NOPE), F32)
    z128 = jnp.zeros((Ls, QK_NOPE), F32)
    z64 = jnp.zeros((Ls, 64), F32)
    tabs = [jnp.concatenate([ones, c64, z64], axis=1), jnp.concatenate([z128, s64, z64], axis=1),
            jnp.concatenate([c64, z64], axis=1), jnp.concatenate([s64, z64], axis=1)]
    modp, mods = mod_all[3, 0:1], mod_all[3, 1:1 + Bs]
    xp, ckv3, kr3 = mla_ctx_mixer(xp, modp, mla_w, Lp)
    kr_c128 = jnp.concatenate([cache_l3_krope, jnp.zeros((Bs, P, 128 - QK_ROPE), F32)], axis=2)
    xs = mla_lat_mixer(xs, mods, mla_w, Ls, cache_l3_ckv, kr_c128, tabs)
    xp, xs = moe_layer(xp, xs, mod_all[3, 0:1 + Bs], wr[1], moe_w_gate, moe_w_up, moe_w_down, 1,
                       final_norm.reshape(1, D), Lp, Ls, True)

    return (xp.reshape(Bp, Lp, D), xs.reshape(Bs, Ls, D), new_state,
            k1.reshape(Bp, Lp, SWA_KVH, SWA_HD), v1.reshape(Bp, Lp, SWA_KVH, SWA_HD),
            ckv3.reshape(Bp, Lp, KV_LORA), kr3[:, :QK_ROPE].reshape(Bp, Lp, QK_ROPE))
```

```python
import functools

import numpy as np
import jax
import jax.numpy as jnp
from jax import lax
from jax.experimental import pallas as pl
from jax.experimental.pallas import tpu as pltpu

BF = jnp.bfloat16
F32 = jnp.float32

D = 1024
EPS = 1e-6
NEG = -1e30
ROPE_THETA = 10000.0
GRID_W = 64

GLA_H, GLA_DK, GLA_DV, GLA_RANK, GLA_CHUNK = 4, 128, 256, 16, 64
GLA_SCALE = GLA_DK ** -0.5
GLA_INV_NORMALIZER = 1.0 / 16.0

SWA_H, SWA_KVH, SWA_HD, SWA_WINDOW, SWA_BLOCK = 16, 4, 64, 128, 128
SWA_SCALE = SWA_HD ** -0.5
SWA_NQ = SWA_H * SWA_HD
SWA_NK = SWA_KVH * SWA_HD

MLA_H, Q_LORA, KV_LORA, QK_NOPE, QK_ROPE, V_HEAD = 8, 384, 256, 128, 64, 128
MLA_SCALE = (QK_NOPE + QK_ROPE) ** -0.5
MLA_QW = 256

N_EXPERTS = 8
MOE_TM = 512
MOE_TF = 512
DENSE_TM = 1024
DENSE_TF = 256

VMEM_LIMIT = 56 * 1024 * 1024


def _cp(*sem):
    return pltpu.CompilerParams(dimension_semantics=sem, vmem_limit_bytes=VMEM_LIMIT)


def _mm(a, b):
    return jnp.dot(a, b, preferred_element_type=F32)


def _mm_nt(a, b):
    return lax.dot_general(a, b, (((1,), (1,)), ((), ())), preferred_element_type=F32)


def _mm_tn(a, b):
    return lax.dot_general(a, b, (((0,), (0,)), ((), ())), preferred_element_type=F32)


def _rms(x):
    return x * lax.rsqrt(jnp.mean(x * x, axis=-1, keepdims=True) + EPS)


def _silu(x):
    return x * (1.0 / (1.0 + jnp.exp(-x)))


def _modulate(x, mod_ref, j):
    shift = mod_ref[0, 3 * j:3 * j + 1, :]
    scale = mod_ref[0, 3 * j + 1:3 * j + 2, :]
    return _rms(x) * (1.0 + scale) + shift


def _gate(mod_ref, j):
    return mod_ref[0, 3 * j + 2:3 * j + 3, :]


def _const_spec(a):
    nd = a.ndim
    return pl.BlockSpec(a.shape, lambda *_: (0,) * nd)


def _mod_spec(tm, rows_per_mod):
    return pl.BlockSpec((1, 6, D), lambda i, *_: ((i * tm) // rows_per_mod, 0, 0))


def _adaln_kernel(c_ref, w_ref, b_ref, o_ref):
    s = _silu(c_ref[...]).astype(BF)
    o_ref[0] = _mm(s, w_ref[0].astype(BF)) + b_ref[0]


def adaln_all(cvec8, w_mod, b_mod):
    nl, d, n = w_mod.shape
    tn = 1536
    return pl.pallas_call(
        _adaln_kernel,
        grid=(nl, n // tn),
        in_specs=[pl.BlockSpec((8, d), lambda l, j: (0, 0)),
                  pl.BlockSpec((1, d, tn), lambda l, j: (l, 0, j)),
                  pl.BlockSpec((1, 1, tn), lambda l, j: (l, 0, j))],
        out_specs=pl.BlockSpec((1, 8, tn), lambda l, j: (l, 0, j)),
        out_shape=jax.ShapeDtypeStruct((nl, 8, n), F32),
        compiler_params=_cp("arbitrary", "arbitrary"),
        name="adaln",
    )(cvec8, w_mod, b_mod.reshape(nl, 1, n))


def _split3(x):
    hi = x.astype(BF)
    rem = x - hi.astype(F32)
    mid = rem.astype(BF)
    lo = (rem - mid.astype(F32)).astype(BF)
    return hi, mid, lo


def _mm3(a, pieces):
    return _mm(a, pieces[0]) + _mm(a, pieces[1]) + _mm(a, pieces[2])


def _gla_pre_kernel(x_ref, mod_ref, wp_ref, wlr_ref, wgk2_ref, bgk_ref,
                    qt_ref, kt_ref, kd_ref, v_ref, g_ref, dec_ref):
    tm = x_ref.shape[0]
    C = GLA_CHUNK
    G = 256
    h = _modulate(x_ref[...], mod_ref, 0).astype(BF)
    nk = GLA_H * GLA_DK
    nv = GLA_H * GLA_DV
    q = _mm(h, wp_ref[:, 0:nk]) * GLA_SCALE
    k = _mm(h, wp_ref[:, nk:2 * nk])
    v_ref[...] = _mm(h, wp_ref[:, 2 * nk:2 * nk + nv]).astype(BF)
    g_ref[...] = _mm(h, wp_ref[:, 2 * nk + nv:2 * nk + 2 * nv])
    lr = _mm(h, wlr_ref[...]).astype(BF)
    ii = lax.broadcasted_iota(jnp.int32, (G, G), 0)
    jj = lax.broadcasted_iota(jnp.int32, (G, G), 1)
    same = (ii // C) == (jj // C)
    ones_blk = jnp.where(same, 1.0, 0.0).astype(BF)
    ci = lax.broadcasted_iota(jnp.int32, (tm // C, tm), 0)
    cj = lax.broadcasted_iota(jnp.int32, (tm // C, tm), 1)
    sel = jnp.where((cj // C) == ci, 1.0, 0.0).astype(BF)
    for r in range(2):
        z = _mm(lr, wgk2_ref[r]) + bgk_ref[r:r + 1, :]
        gk = (jnp.minimum(z, 0.0) - jnp.log1p(jnp.exp(-jnp.abs(z)))) * GLA_INV_NORMALIZER
        pieces = _split3(gk)
        tri = jnp.where(jnp.logical_and(same, (jj <= ii) if r == 0 else (jj >= ii)), 1.0, 0.0).astype(BF)
        dec_ref[r] = jnp.exp(_mm3(sel, pieces))
        for gi in range(tm // G):
            rows = slice(gi * G, (gi + 1) * G)
            pg = tuple(p[rows] for p in pieces)
            b = _mm3(tri, pg)
            tot = _mm3(ones_blk, pg)
            qt_ref[r, rows, :] = (q[rows] * jnp.exp(b)).astype(BF)
            kt_ref[r, rows, :] = (k[rows] * jnp.exp(-b)).astype(BF)
            kd_ref[r, rows, :] = (k[rows] * jnp.exp(tot - b)).astype(BF)


def _gla_scan_kernel(*refs, L, S, has_h0, emit_state):
    qt_ref, kt_ref, kd_ref, v_ref, g_ref, dec_ref, gn_ref = refs[:7]
    pos = 7
    h0_ref = None
    if has_h0:
        h0_ref = refs[pos]
        pos += 1
    y_ref = refs[pos]
    pos += 1
    st_ref = None
    if emit_state:
        st_ref = refs[pos]
        pos += 1
    st_s, of_s, ob_s = refs[pos], refs[pos + 1], refs[pos + 2]

    C = GLA_CHUNK
    n_chunks = L // C
    ii = lax.broadcasted_iota(jnp.int32, (C, C), 0)
    jj = lax.broadcasted_iota(jnp.int32, (C, C), 1)
    keeps = ((jj <= ii), (jj >= ii))
    chains = [(sq, r, hd) for sq in range(S) for r in range(2) for hd in range(GLA_H)]
    slot = lambda sq, r, hd: (sq * 2 + r) * GLA_H + hd
    for sq, r, hd in chains:
        if has_h0:
            st_s[slot(sq, r, hd)] = h0_ref[sq, r, hd].T
        else:
            st_s[slot(sq, r, hd)] = jnp.zeros((GLA_DV, GLA_DK), F32)

    def step(i):
        for sq, r, hd in chains:
            n = i if r == 0 else n_chunks - 1 - i
            start = sq * L + n * C
            if not isinstance(start, int):
                start = pl.multiple_of(start, C)
            rows = pl.ds(start, C)
            o_s = of_s if r == 0 else ob_s
            kc = slice(hd * GLA_DK, (hd + 1) * GLA_DK)
            vc = slice(hd * GLA_DV, (hd + 1) * GLA_DV)
            qt = qt_ref[r, rows, kc]
            v = v_ref[rows, vc]
            att = jnp.where(keeps[r], _mm_nt(qt, kt_ref[r, rows, kc]), 0.0).astype(BF)
            st = st_s[slot(sq, r, hd)]
            o_s[rows, vc] = _mm_nt(qt, st.astype(BF)) + _mm(att, v)
            st_s[slot(sq, r, hd)] = st * dec_ref[r, sq, n, :, kc] + _mm_tn(v, kd_ref[r, rows, kc])

    if n_chunks <= 4:
        for i in range(n_chunks):
            step(i)
    else:
        def body(i, carry):
            step(i)
            return carry
        lax.fori_loop(0, n_chunks, body, 0)

    RC = 256
    for c in range(S * L // RC):
        rows = slice(c * RC, (c + 1) * RC)
        for hd in range(GLA_H):
            vc = slice(hd * GLA_DV, (hd + 1) * GLA_DV)
            y = _rms(of_s[rows, vc] + ob_s[rows, vc]) * gn_ref[...]
            y_ref[rows, vc] = (y * _silu(g_ref[rows, vc])).astype(BF)
    if emit_state:
        for sq, r, hd in chains:
            st_ref[sq, r, hd] = st_s[slot(sq, r, hd)].T


def _residual_out_kernel(y_ref, x_ref, mod_ref, w_ref, o_ref, *, gate_j):
    o_ref[...] = x_ref[...] + _gate(mod_ref, gate_j) * _mm(y_ref[...], w_ref[...])


def gla_mixer(x, mod, w, L, h0):
    M = x.shape[0]
    B = M // L
    tm = 512
    nk, nv = GLA_H * GLA_DK, GLA_H * GLA_DV
    row = lambda n: pl.BlockSpec((tm, n), lambda i: (i, 0))
    dirs = lambda: pl.BlockSpec((2, tm, nk), lambda i: (0, i, 0))
    n_chunks = L // GLA_CHUNK
    qt, kt, kd, v, g, dec = pl.pallas_call(
        _gla_pre_kernel,
        grid=(M // tm,),
        in_specs=[row(D), _mod_spec(tm, L if mod.shape[0] > 1 else M),
                  _const_spec(w["wp"]), _const_spec(w["wlr"]), _const_spec(w["wgk2"]),
                  _const_spec(w["bgk"])],
        out_specs=[dirs(), dirs(), dirs(), row(nv), row(nv),
                   pl.BlockSpec((2, tm // GLA_CHUNK, nk), lambda i: (0, i, 0))],
        out_shape=[jax.ShapeDtypeStruct((2, M, nk), BF), jax.ShapeDtypeStruct((2, M, nk), BF),
                   jax.ShapeDtypeStruct((2, M, nk), BF),
                   jax.ShapeDtypeStruct((M, nv), BF), jax.ShapeDtypeStruct((M, nv), F32),
                   jax.ShapeDtypeStruct((2, M // GLA_CHUNK, nk), F32)],
        compiler_params=_cp("arbitrary"),
        name="gla_pre",
    )(x, mod, w["wp"], w["wlr"], w["wgk2"], w["bgk"])
    dec = dec.reshape(2, B, n_chunks, 1, nk)

    has_h0 = h0 is not None
    emit_state = not has_h0
    S = 2 if (L <= 256 and B % 2 == 0) else 1
    SL = S * L
    dir_spec = lambda: pl.BlockSpec((2, SL, nk), lambda b: (0, b, 0))
    in_specs = [dir_spec(), dir_spec(), dir_spec(),
                pl.BlockSpec((SL, nv), lambda b: (b, 0)),
                pl.BlockSpec((SL, nv), lambda b: (b, 0)),
                pl.BlockSpec((2, S, n_chunks, 1, nk), lambda b: (0, b, 0, 0, 0)),
                pl.BlockSpec((1, GLA_DV), lambda b: (0, 0))]
    args = [qt, kt, kd, v, g, dec, w["gnorm"]]
    st_spec = pl.BlockSpec((S, 2, GLA_H, GLA_DK, GLA_DV), lambda b: (b, 0, 0, 0, 0))
    if has_h0:
        in_specs.append(st_spec)
        args.append(h0)
    out_specs = [pl.BlockSpec((SL, nv), lambda b: (b, 0))]
    out_shape = [jax.ShapeDtypeStruct((M, nv), BF)]
    if emit_state:
        out_specs.append(st_spec)
        out_shape.append(jax.ShapeDtypeStruct((B, 2, GLA_H, GLA_DK, GLA_DV), F32))
    res = pl.pallas_call(
        functools.partial(_gla_scan_kernel, L=L, S=S, has_h0=has_h0, emit_state=emit_state),
        grid=(B // S,),
        in_specs=in_specs,
        out_specs=out_specs,
        out_shape=out_shape,
        scratch_shapes=[pltpu.VMEM((S * 2 * GLA_H, GLA_DV, GLA_DK), F32), pltpu.VMEM((SL, nv), F32),
                        pltpu.VMEM((SL, nv), F32)],
        compiler_params=_cp("arbitrary"),
        name="gla_scan",
    )(*args)
    y = res[0]
    state = res[1] if emit_state else None

    x_new = pl.pallas_call(
        functools.partial(_residual_out_kernel, gate_j=0),
        grid=(M // tm,),
        in_specs=[row(nv), row(D), _mod_spec(tm, L if mod.shape[0] > 1 else M), _const_spec(w["wo"])],
        out_specs=row(D),
        out_shape=jax.ShapeDtypeStruct((M, D), F32),
        compiler_params=_cp("arbitrary"),
        name="gla_post",
    )(y, x, mod, w["wo"])
    return x_new, state


def _softmax_sink_heads(q_of, k_of, v_of, sink_ref, o_s):
    for hq in range(SWA_H):
        kh = hq // (SWA_H // SWA_KVH)
        s = _mm_nt(q_of(hq), k_of(kh))
        sink = sink_ref[hq]
        m = jnp.maximum(jnp.max(s, axis=-1, keepdims=True), sink)
        e = jnp.exp(s - m)
        p = e / (jnp.sum(e, axis=-1, keepdims=True) + jnp.exp(sink - m))
        o_s[:, hq * SWA_HD:(hq + 1) * SWA_HD] = _mm(p.astype(BF), v_of(kh)).astype(BF)


def _swa_ctx_kernel(sink_ref, x_ref, mod_ref, wqkv_ref, wo_ref, o_ref, k_out, v_out, o_s):
    x = x_ref[...]
    h = _modulate(x, mod_ref, 0).astype(BF)
    qkv = _mm(h, wqkv_ref[...])
    k_out[...] = qkv[:, SWA_NQ:SWA_NQ + SWA_NK]
    v_out[...] = qkv[:, SWA_NQ + SWA_NK:]
    q_of = lambda hq: (qkv[:, hq * SWA_HD:(hq + 1) * SWA_HD] * SWA_SCALE).astype(BF)
    k_of = lambda kh: qkv[:, SWA_NQ + kh * SWA_HD:SWA_NQ + (kh + 1) * SWA_HD].astype(BF)
    v_of = lambda kh: qkv[:, SWA_NQ + SWA_NK + kh * SWA_HD:SWA_NQ + SWA_NK + (kh + 1) * SWA_HD].astype(BF)
    _softmax_sink_heads(q_of, k_of, v_of, sink_ref, o_s)
    o_ref[...] = x + _gate(mod_ref, 0) * _mm(o_s[...], wo_ref[...])


def swa_ctx_mixer(x, mod, w, L):
    M = x.shape[0]
    row = lambda n: pl.BlockSpec((L, n), lambda i: (i, 0))
    return pl.pallas_call(
        _swa_ctx_kernel,
        grid=(M // L,),
        in_specs=[pl.BlockSpec(memory_space=pltpu.SMEM), row(D), _mod_spec(L, M),
                  _const_spec(w["wqkv"]), _const_spec(w["wo"])],
        out_specs=[row(D), row(SWA_NK), row(SWA_NK)],
        out_shape=[jax.ShapeDtypeStruct((M, D), F32), jax.ShapeDtypeStruct((M, SWA_NK), F32),
                   jax.ShapeDtypeStruct((M, SWA_NK), F32)],
        scratch_shapes=[pltpu.VMEM((L, SWA_NQ), BF)],
        compiler_params=_cp("arbitrary"),
        name="swa_ctx",
    )(w["sink"], x, mod, w["wqkv"], w["wo"])


def _swa_lat_kernel(sink_ref, x_ref, mod_ref, wqkv_ref, wsw_ref, cos_ref, sin_ref, kc_ref, vc_ref,
                    wo_ref, o_ref, q_s, k_s, v_s, o_s, *, L):
    n = pl.program_id(1)
    RC = 256

    @pl.when(n == 0)
    def _():
        for c in range(L // RC):
            rows = slice(c * RC, (c + 1) * RC)
            h = _modulate(x_ref[rows, :], mod_ref, 0).astype(BF)
            qkv = _mm(h, wqkv_ref[...])
            sw = _mm(h, wsw_ref[...])
            cos = cos_ref[rows, :]
            sin = sin_ref[rows, :]
            cq = jnp.concatenate([cos] * (SWA_NQ // 128), axis=1)
            sq = jnp.concatenate([sin] * (SWA_NQ // 128), axis=1)
            ck = jnp.concatenate([cos] * (SWA_NK // 128), axis=1)
            sk = jnp.concatenate([sin] * (SWA_NK // 128), axis=1)
            q = (qkv[:, :SWA_NQ] * SWA_SCALE) * cq + (sw[:, :SWA_NQ] * SWA_SCALE) * sq
            q_s[rows, :] = q.astype(BF)
            k = qkv[:, SWA_NQ:SWA_NQ + SWA_NK] * ck + sw[:, SWA_NQ:SWA_NQ + SWA_NK] * sk
            k_s[rows, :] = k.astype(BF)
            v_s[rows, :] = qkv[:, SWA_NQ + SWA_NK:].astype(BF)

    QB = SWA_BLOCK
    KW = 3 * SWA_BLOCK
    r0 = pl.multiple_of(n * QB, QB)
    ws = pl.multiple_of(jnp.clip((n - 1) * QB, 0, L - KW), QB)
    qpos = r0 + lax.broadcasted_iota(jnp.int32, (QB, KW), 0)
    kpos = ws + lax.broadcasted_iota(jnp.int32, (QB, KW), 1)
    valid = jnp.abs(kpos - qpos) <= SWA_WINDOW
    for hq in range(SWA_H):
        kh = hq // (SWA_H // SWA_KVH)
        hs = slice(kh * SWA_HD, (kh + 1) * SWA_HD)
        q = q_s[pl.ds(r0, QB), hq * SWA_HD:(hq + 1) * SWA_HD]
        s1 = jnp.where(valid, _mm_nt(q, k_s[pl.ds(ws, KW), hs]), NEG)
        s2 = _mm_nt(q, kc_ref[0, :, hs].astype(BF))
        sink = sink_ref[hq]
        m = jnp.maximum(jnp.maximum(jnp.max(s1, axis=-1, keepdims=True),
                                    jnp.max(s2, axis=-1, keepdims=True)), sink)
        e1 = jnp.exp(s1 - m)
        e2 = jnp.exp(s2 - m)
        den = (jnp.sum(e1, axis=-1, keepdims=True) + jnp.sum(e2, axis=-1, keepdims=True)
               + jnp.exp(sink - m))
        o = (_mm(e1.astype(BF), v_s[pl.ds(ws, KW), hs])
             + _mm(e2.astype(BF), vc_ref[0, :, hs].astype(BF)))
        o_s[:, hq * SWA_HD:(hq + 1) * SWA_HD] = (o * (1.0 / den)).astype(BF)
    o_ref[...] = x_ref[pl.ds(r0, QB), :] + _gate(mod_ref, 0) * _mm(o_s[...], wo_ref[...])


def swa_lat_mixer(x, mod, w, L, kc, vc, cos128, sin128):
    M = x.shape[0]
    B = M // L
    P = kc.shape[1]
    nb = L // SWA_BLOCK
    return pl.pallas_call(
        functools.partial(_swa_lat_kernel, L=L),
        grid=(B, nb),
        in_specs=[pl.BlockSpec(memory_space=pltpu.SMEM),
                  pl.BlockSpec((L, D), lambda b, n: (b, 0)),
                  pl.BlockSpec((1, 6, D), lambda b, n: (b, 0, 0)),
                  _const_spec(w["wqkv"]), _const_spec(w["wsw"]),
                  _const_spec(cos128), _const_spec(sin128),
                  pl.BlockSpec((1, P, SWA_NK), lambda b, n: (b, 0, 0)),
                  pl.BlockSpec((1, P, SWA_NK), lambda b, n: (b, 0, 0)),
                  _const_spec(w["wo"])],
        out_specs=pl.BlockSpec((SWA_BLOCK, D), lambda b, n: (b * nb + n, 0)),
        out_shape=jax.ShapeDtypeStruct((M, D), F32),
        scratch_shapes=[pltpu.VMEM((L, SWA_NQ), BF), pltpu.VMEM((L, SWA_NK), BF),
                        pltpu.VMEM((L, SWA_NK), BF), pltpu.VMEM((SWA_BLOCK, SWA_NQ), BF)],
        compiler_params=_cp("arbitrary", "arbitrary"),
        name="swa_lat",
    )(w["sink"], x, mod, w["wqkv"], w["wsw"], cos128, sin128, kc, vc, w["wo"])


def _conv_kernel(x_ref, mod_ref, win_ref, cw_ref, wout_ref, o_ref, y_s, *, L):
    x = x_ref[...]
    h = _modulate(x, mod_ref, 0).astype(BF)
    CC = 256
    row = lax.broadcasted_iota(jnp.int32, (L, CC), 0)
    for c in range(D // CC):
        cols = slice(c * CC, (c + 1) * CC)
        bg = _mm(h, win_ref[:, c * CC:(c + 1) * CC])
        cg = _mm(h, win_ref[:, D + c * CC:D + (c + 1) * CC])
        u = _mm(h, win_ref[:, 2 * D + c * CC:2 * D + (c + 1) * CC])
        cu = cg * u
        prev = jnp.where(row == 0, 0.0, pltpu.roll(cu, 1, 0))
        nxt = jnp.where(row == L - 1, 0.0, pltpu.roll(cu, L - 1, 0))
        conv = prev * cw_ref[0:1, cols] + cu * cw_ref[1:2, cols] + nxt * cw_ref[2:3, cols]
        y_s[:, cols] = (bg * conv).astype(BF)
    o_ref[...] = x + _gate(mod_ref, 0) * _mm(y_s[...], wout_ref[...])


def conv_mixer(x, mod, w, L):
    M = x.shape[0]
    row = pl.BlockSpec((L, D), lambda i: (i, 0))
    return pl.pallas_call(
        functools.partial(_conv_kernel, L=L),
        grid=(M // L,),
        in_specs=[row, _mod_spec(L, L if mod.shape[0] > 1 else M), _const_spec(w["win"]),
                  _const_spec(w["cw"]), _const_spec(w["wout"])],
        out_specs=row,
        out_shape=jax.ShapeDtypeStruct((M, D), F32),
        scratch_shapes=[pltpu.VMEM((L, D), BF)],
        compiler_params=_cp("arbitrary"),
        name="conv_mix",
    )(x, mod, w["win"], w["cw"], w["wout"])


def _mla_ctx_kernel(x_ref, mod_ref, wdq_ref, qn_ref, wuq_ref, wdkv_ref, kvn_ref, wukv_ref, wo_ref,
                    o_ref, ckv_out, kr_out, o_s):
    x = x_ref[...]
    h = _modulate(x, mod_ref, 0).astype(BF)
    cq = (_rms(_mm(h, wdq_ref[...])) * qn_ref[...]).astype(BF)
    q = _mm(cq, wuq_ref[...]).astype(BF)
    kvc = _mm(h, wdkv_ref[...])
    ckv = _rms(kvc[:, :KV_LORA]) * kvn_ref[...]
    kr = kvc[:, KV_LORA:]
    ckv_out[...] = ckv
    kr_out[...] = kr
    kv = _mm(ckv.astype(BF), wukv_ref[...]).astype(BF)
    krb = kr.astype(BF)
    nn = MLA_H * QK_NOPE
    for hh in range(MLA_H):
        qh = q[:, hh * MLA_QW:(hh + 1) * MLA_QW]
        kh = jnp.concatenate([kv[:, hh * QK_NOPE:(hh + 1) * QK_NOPE], krb], axis=1)
        s = _mm_nt(qh, kh) * MLA_SCALE
        m = jnp.max(s, axis=-1, keepdims=True)
        e = jnp.exp(s - m)
        inv = 1.0 / jnp.sum(e, axis=-1, keepdims=True)
        vh = kv[:, nn + hh * V_HEAD:nn + (hh + 1) * V_HEAD]
        o_s[:, hh * V_HEAD:(hh + 1) * V_HEAD] = (_mm(e.astype(BF), vh) * inv).astype(BF)
    o_ref[...] = x + _gate(mod_ref, 0) * _mm(o_s[...], wo_ref[...])


def mla_ctx_mixer(x, mod, w, L):
    M = x.shape[0]
    row = lambda n: pl.BlockSpec((L, n), lambda i: (i, 0))
    ws = [w["wdq"], w["qnorm"], w["wuq"], w["wdkv"], w["kvnorm"], w["wukv"], w["wo"]]
    return pl.pallas_call(
        _mla_ctx_kernel,
        grid=(M // L,),
        in_specs=[row(D), _mod_spec(L, M)] + [_const_spec(a) for a in ws],
        out_specs=[row(D), row(KV_LORA), row(128)],
        out_shape=[jax.ShapeDtypeStruct((M, D), F32), jax.ShapeDtypeStruct((M, KV_LORA), F32),
                   jax.ShapeDtypeStruct((M, 128), F32)],
        scratch_shapes=[pltpu.VMEM((L, MLA_H * V_HEAD), BF)],
        compiler_params=_cp("arbitrary"),
        name="mla_ctx",
    )(x, mod, *ws)


def _mla_lat_kernel(x_ref, mod_ref, wdq_ref, qn_ref, wuq_ref, wuqsw_ref, wdkv_ref, wdkvsw_ref,
                    kvn_ref, wukv_ref, cq_ref, sq_ref, ck_ref, sk_ref, ckvc_ref, krc_ref, wo_ref,
                    o_ref, q_s, kn_s, v_s, kr_s, knc_s, vc_s, o_s, *, L, QB):
    n = pl.program_id(1)
    RC = 256
    nn = MLA_H * QK_NOPE

    @pl.when(n == 0)
    def _():
        for c in range(L // RC):
            rows = slice(c * RC, (c + 1) * RC)
            h = _modulate(x_ref[rows, :], mod_ref, 0).astype(BF)
            cq = (_rms(_mm(h, wdq_ref[...])) * qn_ref[...]).astype(BF)
            cosq = jnp.concatenate([cq_ref[rows, :]] * MLA_H, axis=1)
            sinq = jnp.concatenate([sq_ref[rows, :]] * MLA_H, axis=1)
            q = _mm(cq, wuq_ref[...]) * cosq + _mm(cq, wuqsw_ref[...]) * sinq
            q_s[rows, :] = q.astype(BF)
            kvc = _mm(h, wdkv_ref[...])
            ksw = _mm(h, wdkvsw_ref[...])
            ckv = _rms(kvc[:, :KV_LORA]) * kvn_ref[...]
            kr_s[rows, :] = (kvc[:, KV_LORA:] * ck_ref[rows, :] + ksw * sk_ref[rows, :]).astype(BF)
            kv = _mm(ckv.astype(BF), wukv_ref[...])
            kn_s[rows, :] = kv[:, :nn].astype(BF)
            v_s[rows, :] = kv[:, nn:].astype(BF)
        kvp = _mm(ckvc_ref[0].astype(BF), wukv_ref[...])
        knc_s[...] = kvp[:, :nn].astype(BF)
        vc_s[...] = kvp[:, nn:].astype(BF)

    r0 = pl.multiple_of(n * QB, QB)
    krc = krc_ref[0].astype(BF)
    for hh in range(MLA_H):
        ns = slice(hh * QK_NOPE, (hh + 1) * QK_NOPE)
        qh = q_s[pl.ds(r0, QB), hh * MLA_QW:(hh + 1) * MLA_QW]
        k1 = jnp.concatenate([kn_s[:, ns], kr_s[...]], axis=1)
        k2 = jnp.concatenate([knc_s[:, ns], krc], axis=1)
        s1 = _mm_nt(qh, k1) * MLA_SCALE
        s2 = _mm_nt(qh, k2) * MLA_SCALE
        m = jnp.maximum(jnp.max(s1, axis=-1, keepdims=True), jnp.max(s2, axis=-1, keepdims=True))
        e1 = jnp.exp(s1 - m)
        e2 = jnp.exp(s2 - m)
        den = jnp.sum(e1, axis=-1, keepdims=True) + jnp.sum(e2, axis=-1, keepdims=True)
        vs = slice(hh * V_HEAD, (hh + 1) * V_HEAD)
        o = _mm(e1.astype(BF), v_s[:, vs]) + _mm(e2.astype(BF), vc_s[:, vs])
        o_s[:, vs] = (o * (1.0 / den)).astype(BF)
    o_ref[...] = x_ref[pl.ds(r0, QB), :] + _gate(mod_ref, 0) * _mm(o_s[...], wo_ref[...])


def mla_lat_mixer(x, mod, w, L, ckv_c, kr_c128, tabs):
    M = x.shape[0]
    B = M // L
    P = ckv_c.shape[1]
    QB = 256
    nb = L // QB
    ws1 = [w["wdq"], w["qnorm"], w["wuq"], w["wuqsw"], w["wdkv"], w["wdkvsw"], w["kvnorm"], w["wukv"]]
    nv = MLA_H * V_HEAD
    return pl.pallas_call(
        functools.partial(_mla_lat_kernel, L=L, QB=QB),
        grid=(B, nb),
        in_specs=[pl.BlockSpec((L, D), lambda b, n: (b, 0)),
                  pl.BlockSpec((1, 6, D), lambda b, n: (b, 0, 0))]
                 + [_const_spec(a) for a in ws1] + [_const_spec(a) for a in tabs]
                 + [pl.BlockSpec((1, P, KV_LORA), lambda b, n: (b, 0, 0)),
                    pl.BlockSpec((1, P, 128), lambda b, n: (b, 0, 0)),
                    _const_spec(w["wo"])],
        out_specs=pl.BlockSpec((QB, D), lambda b, n: (b * nb + n, 0)),
        out_shape=jax.ShapeDtypeStruct((M, D), F32),
        scratch_shapes=[pltpu.VMEM((L, MLA_H * MLA_QW), BF), pltpu.VMEM((L, nv), BF),
                        pltpu.VMEM((L, nv), BF), pltpu.VMEM((L, 128), BF),
                        pltpu.VMEM((P, nv), BF), pltpu.VMEM((P, nv), BF), pltpu.VMEM((QB, nv), BF)],
        compiler_params=_cp("arbitrary", "arbitrary"),
        name="mla_lat",
    )(x, mod, *ws1, *tabs, ckv_c, kr_c128, w["wo"])


def _swiglu_partial(xb, wg_ref, wu_ref, wd_ref):
    hg = _mm(xb, wg_ref[...].astype(BF))
    hu = _mm(xb, wu_ref[...].astype(BF))
    a = (_silu(hg) * hu).astype(BF)
    return _mm(a, wd_ref[...].astype(BF))


def _dense_ffn_kernel(x_ref, mod_ref, wga_ref, wgb_ref, wua_ref, wub_ref, wda_ref, wdb_ref, o_ref, xm_s,
                      *, odd_blocks):
    f = pl.program_id(1)
    last = pl.num_programs(1) - 1

    @pl.when(f == 0)
    def _():
        xm_s[...] = _modulate(x_ref[...], mod_ref, 1).astype(BF)
        o_ref[...] = jnp.zeros_like(o_ref)

    def pair():
        xb = xm_s[...]
        acts = []
        for wg_ref, wu_ref in ((wga_ref, wua_ref), (wgb_ref, wub_ref)):
            hg = _mm(xb, wg_ref[...].astype(BF))
            hu = _mm(xb, wu_ref[...].astype(BF))
            acts.append((_silu(hg) * hu).astype(BF))
        wd = jnp.concatenate([wda_ref[...].astype(BF), wdb_ref[...].astype(BF)], axis=0)
        return _mm(jnp.concatenate(acts, axis=1), wd)

    if odd_blocks:
        @pl.when(f < last)
        def _():
            o_ref[...] += pair()

        @pl.when(f == last)
        def _():
            o_ref[...] += _swiglu_partial(xm_s[...], wga_ref, wua_ref, wda_ref)
    else:
        o_ref[...] += pair()

    @pl.when(f == last)
    def _():
        o_ref[...] = x_ref[...] + _gate(mod_ref, 1) * o_ref[...]


def dense_ffn(x, mod, wg, wu, wd, j, L):
    M = x.shape[0]
    F = wg.shape[-1]
    tm, tf = DENSE_TM, DENSE_TF
    nfb = F // tf
    blk_a = lambda f: 2 * f
    blk_b = lambda f: jnp.minimum(2 * f + 1, nfb - 1)
    col = lambda blk: pl.BlockSpec((None, D, tf), lambda i, f: (j, 0, blk(f)))
    row = lambda blk: pl.BlockSpec((None, tf, D), lambda i, f: (j, blk(f), 0))
    return pl.pallas_call(
        functools.partial(_dense_ffn_kernel, odd_blocks=nfb % 2 == 1),
        grid=(M // tm, (nfb + 1) // 2),
        in_specs=[pl.BlockSpec((tm, D), lambda i, f: (i, 0)),
                  pl.BlockSpec((1, 6, D), lambda i, f: ((i * tm) // (L if mod.shape[0] > 1 else M), 0, 0)),
                  col(blk_a), col(blk_b), col(blk_a), col(blk_b), row(blk_a), row(blk_b)],
        out_specs=pl.BlockSpec((tm, D), lambda i, f: (i, 0)),
        out_shape=jax.ShapeDtypeStruct((M, D), F32),
        scratch_shapes=[pltpu.VMEM((tm, D), BF)],
        compiler_params=_cp("arbitrary", "arbitrary"),
        name="dense_ffn",
    )(x, mod, wg, wg, wu, wu, wd, wd)


def _router_kernel(xp_ref, xs_ref, mod_ref, wr_ref, xm_ref, idx_ref, w_ref, cnt_ref, cnt_s, *, np_tiles):
    i = pl.program_id(0)
    tm = xm_ref.shape[0]

    @pl.when(i == 0)
    def _():
        cnt_s[...] = jnp.zeros_like(cnt_s)

    x = jnp.where(i < np_tiles, xp_ref[...], xs_ref[...])
    xmod = _modulate(x, mod_ref, 1)
    xm_ref[...] = xmod
    xm = xmod.astype(BF)
    lane = lax.broadcasted_iota(jnp.int32, idx_ref.shape, 1)
    logits = jnp.where(lane < N_EXPERTS, _mm(xm, wr_ref[...]), -jnp.inf)
    m1 = jnp.max(logits, axis=-1, keepdims=True)
    i1 = jnp.min(jnp.where(logits == m1, lane, 128), axis=-1, keepdims=True)
    rest = jnp.where(lane == i1, -jnp.inf, logits)
    m2 = jnp.max(rest, axis=-1, keepdims=True)
    i2 = jnp.min(jnp.where(rest == m2, lane, 128), axis=-1, keepdims=True)
    e = jnp.exp(m2 - m1)
    w_ref[...] = jnp.where(lane == 0, 1.0 / (1.0 + e), e / (1.0 + e))
    oh1 = jnp.where(lane == i1, 1.0, 0.0)
    oh2 = jnp.where(lane == i2, 1.0, 0.0)
    rr = lax.broadcasted_iota(jnp.int32, (tm, tm), 0)
    cc = lax.broadcasted_iota(jnp.int32, (tm, tm), 1)
    below = jnp.where(cc < rr, 1.0, 0.0).astype(BF)
    run = cnt_s[0:1, :]
    tot1 = jnp.sum(oh1, axis=0, keepdims=True)
    tot2 = jnp.sum(oh2, axis=0, keepdims=True)
    r1 = jnp.sum(oh1 * (run + _mm(below, oh1.astype(BF))), axis=-1, keepdims=True)
    r2 = jnp.sum(oh2 * (run + tot1 + _mm(below, oh2.astype(BF))), axis=-1, keepdims=True)
    idx_ref[...] = jnp.where(lane == 0, i1, jnp.where(lane == 1, i2, jnp.where(
        lane == 2, r1.astype(jnp.int32), r2.astype(jnp.int32))))
    total = run + tot1 + tot2
    cnt_s[...] = jnp.broadcast_to(total, cnt_s.shape)
    cnt_ref[...] = jnp.broadcast_to(total, cnt_ref.shape)


def route(xp, xs, mod3, wr, Ls):
    Mp, Ms = xp.shape[0], xs.shape[0]
    T = Mp + Ms
    tm = 512
    npt = Mp // tm
    row = lambda n: pl.BlockSpec((tm, n), lambda i: (i, 0))
    return pl.pallas_call(
        functools.partial(_router_kernel, np_tiles=npt),
        grid=(T // tm,),
        in_specs=[pl.BlockSpec((tm, D), lambda i: (jnp.minimum(i, npt - 1), 0)),
                  pl.BlockSpec((tm, D), lambda i: (jnp.maximum(i - npt, 0), 0)),
                  pl.BlockSpec((1, 6, D), lambda i: (jnp.where(i < npt, 0, 1 + ((i - npt) * tm) // Ls), 0, 0)),
                  _const_spec(wr)],
        out_specs=[row(D), row(128), row(128), pl.BlockSpec((8, 128), lambda i: (0, 0))],
        out_shape=[jax.ShapeDtypeStruct((T, D), F32), jax.ShapeDtypeStruct((T, 128), jnp.int32),
                   jax.ShapeDtypeStruct((T, 128), F32), jax.ShapeDtypeStruct((8, 128), F32)],
        scratch_shapes=[pltpu.VMEM((8, 128), F32)],
        compiler_params=_cp("arbitrary"),
        name="moe_route",
    )(xp, xs, mod3, wr)


def _moe_ffn_kernel(vb_ref, ve_ref, vm_ref, nv_ref, xa_ref, xb_ref, wg_ref, wu_ref, wd_ref, o_ref, x_s,
                    *, half_blocks):
    v = pl.program_id(0)
    f = pl.program_id(1)
    valid = v < nv_ref[0]
    mode = vm_ref[v]
    H = MOE_TM

    @pl.when(jnp.logical_and(valid, f == 0))
    def _():
        x_s[...] = jnp.where(vb_ref[v] < half_blocks, xa_ref[...], xb_ref[...]).astype(BF)

    def run(rows):
        @pl.when(f == 0)
        def _():
            o_ref[rows, :] = jnp.zeros((rows.stop - rows.start, D), F32)

        o_ref[rows, :] += _swiglu_partial(x_s[rows, :], wg_ref, wu_ref, wd_ref)

    for m, rows in ((0, slice(0, 2 * H)), (1, slice(0, H)), (2, slice(H, 2 * H))):
        @pl.when(jnp.logical_and(valid, mode == m))
        def _(rows=rows):
            run(rows)


def moe_ffn(xa, xb, tile_expert, n_valid, wg, wu, wd, j):
    R = 2 * xa.shape[0]
    F = wg.shape[-1]
    bm, tf = 2 * MOE_TM, MOE_TF
    nf = F // tf
    nb = R // bm
    nhb = nb // 2
    n_vis = nb + N_EXPERTS - 1
    blocks = jnp.arange(nb, dtype=jnp.int32)
    t0, t1 = tile_expert[0::2], tile_expert[1::2]
    valid0 = 2 * blocks < n_valid[0]
    valid1 = 2 * blocks + 1 < n_valid[0]
    two = jnp.logical_and(valid1, t0 != t1)
    cnt = jnp.where(valid0, 1 + two.astype(jnp.int32), 0)
    cum = jnp.cumsum(cnt)
    nv = cum[-1:].astype(jnp.int32)
    v_eff = jnp.minimum(jnp.arange(n_vis, dtype=jnp.int32), nv[0] - 1)
    vb = jnp.sum((v_eff[:, None] >= cum[None, :]).astype(jnp.int32), axis=1)
    pick = lambda a: jnp.sum(jnp.where(vb[:, None] == blocks[None, :], a[None, :].astype(jnp.int32), 0), axis=1)
    k = v_eff - pick(cum - cnt)
    two_v, t0_v, t1_v, valid1_v = pick(two), pick(t0), pick(t1), pick(valid1)
    vm = jnp.where(two_v == 1, jnp.where(k == 0, 1, 2), jnp.where(valid1_v == 1, 0, 1)).astype(jnp.int32)
    ve = jnp.where(jnp.logical_and(two_v == 1, k == 1), t1_v, t0_v).astype(jnp.int32)

    def fidx(v, f, nv):
        return jnp.where(v < nv[0], f, nf - 1)

    grid_spec = pltpu.PrefetchScalarGridSpec(
        num_scalar_prefetch=4,
        grid=(n_vis, nf),
        in_specs=[pl.BlockSpec((bm, D), lambda v, f, vb, ve, vm, nv: (jnp.minimum(vb[v], nhb - 1), 0)),
                  pl.BlockSpec((bm, D), lambda v, f, vb, ve, vm, nv: (jnp.maximum(vb[v] - nhb, 0), 0)),
                  pl.BlockSpec((None, None, D, tf), lambda v, f, vb, ve, vm, nv: (j, ve[v], 0, fidx(v, f, nv))),
                  pl.BlockSpec((None, None, D, tf), lambda v, f, vb, ve, vm, nv: (j, ve[v], 0, fidx(v, f, nv))),
                  pl.BlockSpec((None, None, tf, D), lambda v, f, vb, ve, vm, nv: (j, ve[v], fidx(v, f, nv), 0))],
        out_specs=pl.BlockSpec((bm, D), lambda v, f, vb, ve, vm, nv: (vb[v], 0)),
        scratch_shapes=[pltpu.VMEM((bm, D), BF)],
    )
    return pl.pallas_call(
        functools.partial(_moe_ffn_kernel, half_blocks=nhb),
        grid_spec=grid_spec,
        out_shape=jax.ShapeDtypeStruct((R, D), F32),
        compiler_params=_cp("arbitrary", "arbitrary"),
        name="moe_ffn",
    )(vb.astype(jnp.int32), ve, vm, nv, xa, xb, wg, wu, wd)


def _combine_kernel(x_ref, mod_ref, ya_ref, yb_ref, w_ref, fn_ref, o_ref, *, final):
    y = w_ref[:, 0:1] * ya_ref[...] + w_ref[:, 1:2] * yb_ref[...]
    o = x_ref[...] + _gate(mod_ref, 1) * y
    if final:
        o = _rms(o) * fn_ref[...]
    o_ref[...] = o


def moe_combine(x, mod, ya, yb, w, fn, L, final, row_off):
    M = x.shape[0]
    tm = 512
    off = row_off // tm
    row = pl.BlockSpec((tm, D), lambda i: (i, 0))
    row_o = lambda n: pl.BlockSpec((tm, n), lambda i: (i + off, 0))
    return pl.pallas_call(
        functools.partial(_combine_kernel, final=final),
        grid=(M // tm,),
        in_specs=[row, _mod_spec(tm, L if mod.shape[0] > 1 else M), row_o(D), row_o(D), row_o(128),
                  _const_spec(fn)],
        out_specs=row,
        out_shape=jax.ShapeDtypeStruct((M, D), F32),
        compiler_params=_cp("arbitrary"),
        name="moe_combine",
    )(x, mod, ya, yb, w, fn)


def moe_layer(xp, xs, mod3, wr, wg, wu, wd, j, fn, Lp, Ls, final):
    Mp = xp.shape[0]
    xm, idx, wts, cnt = route(xp, xs, mod3, wr, Ls)
    T = xm.shape[0]
    tm = MOE_TM
    R = 2 * T + N_EXPERTS * tm
    counts = cnt[0, :N_EXPERTS].astype(jnp.int32)
    tiles_per = (counts + tm - 1) // tm
    tile_end = jnp.cumsum(tiles_per)
    offs = (tile_end - tiles_per) * tm
    experts = jnp.arange(N_EXPERTS, dtype=jnp.int32)
    dest = jnp.sum(jnp.where(idx[:, 0:2, None] == experts, offs, 0), axis=-1) + idx[:, 2:4]
    tok = jnp.arange(2 * T, dtype=jnp.int32) // 2
    src = (jnp.arange(R, dtype=jnp.int32) % T).at[dest.reshape(-1)].set(
        tok, unique_indices=True, mode="promise_in_bounds")
    n_tiles = R // tm
    n_valid = tile_end[-1:].astype(jnp.int32)
    t_ids = jnp.arange(n_tiles, dtype=jnp.int32)
    te = jnp.sum((t_ids[:, None] >= tile_end[None, :]).astype(jnp.int32), axis=1)
    te_last = jnp.sum((n_valid - 1 >= tile_end).astype(jnp.int32))
    te = jnp.where(t_ids < n_valid[0], te, te_last).astype(jnp.int32)
    xa = xm.at[src[:R // 2]].get(mode="promise_in_bounds")
    xb = xm.at[src[R // 2:]].get(mode="promise_in_bounds")
    y_sorted = moe_ffn(xa, xb, te, n_valid, wg, wu, wd, j)
    ya = y_sorted.at[dest[:, 0]].get(mode="promise_in_bounds")
    yb = y_sorted.at[dest[:, 1]].get(mode="promise_in_bounds")
    xp_new = moe_combine(xp, mod3[0:1], ya, yb, wts, fn, Lp, final, 0)
    xs_new = moe_combine(xs, mod3[1:], ya, yb, wts, fn, Ls, final, Mp)
    return xp_new, xs_new


def _rope_partner_perm(n):
    c = np.arange(n)
    return np.where((c % 32) < 16, c + 16, c - 16)


def _rope_tables(L):
    nf = 16
    inv = ROPE_THETA ** (-np.arange(nf, dtype=np.float32) / nf)
    pos = np.arange(L)
    ang = np.stack([(pos // GRID_W).astype(np.float32), (pos % GRID_W).astype(np.float32)],
                   axis=-1)[:, :, None] * inv
    cos = jnp.cos(jnp.asarray(ang, F32))
    sin = jnp.sin(jnp.asarray(ang, F32))
    c64 = jnp.stack([cos, cos], axis=2).reshape(L, 64)
    s64 = jnp.stack([-sin, sin], axis=2).reshape(L, 64)
    return c64, s64


def kernel(x_prompt, x_sample, state_l0_gla, cache_l1_k, cache_l1_v, cache_l3_ckv, cache_l3_krope, c, c_ctx, w_mod, b_mod, gla_wq, gla_wk, gla_wv, gla_wg, gla_wgk1, gla_wgk2, gla_bgk, gla_norm, gla_wo, swa_wqkv, swa_sink, swa_wo, conv_win, conv_w, conv_wout, mla_wdq, mla_qnorm, mla_wuq, mla_wdkv, mla_kvnorm, mla_wukv, mla_wo, dense_w_gate, dense_w_up, dense_w_down, moe_router, moe_w_gate, moe_w_up, moe_w_down, final_norm):
    Bp, Lp, _ = x_prompt.shape
    Bs, Ls, _ = x_sample.shape
    P = cache_l1_k.shape[1]
    xp = x_prompt.reshape(Bp * Lp, D)
    xs = x_sample.reshape(Bs * Ls, D)

    cvec = jnp.zeros((8, D), F32).at[0].set(c_ctx).at[1:1 + Bs].set(c)
    mod_all = adaln_all(cvec, w_mod, b_mod).reshape(w_mod.shape[0], 8, 6, D)

    wlr = jnp.zeros((D, 128), F32).at[:, :GLA_RANK].set(gla_wgk1[0]).at[:, GLA_RANK:2 * GLA_RANK].set(gla_wgk1[1])
    wgk2 = jnp.zeros((2, 128, GLA_H * GLA_DK), F32)
    wgk2 = wgk2.at[0, :GLA_RANK].set(gla_wgk2[0]).at[1, GLA_RANK:2 * GLA_RANK].set(gla_wgk2[1])
    gla_w = dict(wp=jnp.concatenate([gla_wq, gla_wk, gla_wv, gla_wg], axis=1).astype(BF),
                 wlr=wlr.astype(BF), wgk2=wgk2.astype(BF), bgk=gla_bgk,
                 gnorm=gla_norm.reshape(1, GLA_DV), wo=gla_wo.astype(BF))
    modp, mods = mod_all[0, 0:1], mod_all[0, 1:1 + Bs]
    xp, new_state = gla_mixer(xp, modp, gla_w, Lp, None)
    xs, _ = gla_mixer(xs, mods, gla_w, Ls, state_l0_gla)
    xp = dense_ffn(xp, modp, dense_w_gate, dense_w_up, dense_w_down, 0, Lp)
    xs = dense_ffn(xs, mods, dense_w_gate, dense_w_up, dense_w_down, 0, Ls)

    c64, s64 = _rope_tables(Ls)
    cos128 = jnp.concatenate([c64, c64], axis=1)
    sin128 = jnp.concatenate([s64, s64], axis=1)
    perm = _rope_partner_perm(SWA_NQ + SWA_NK)
    swa_w = dict(wqkv=swa_wqkv.astype(BF), wsw=swa_wqkv[:, perm].astype(BF), wo=swa_wo.astype(BF),
                 sink=swa_sink)
    modp, mods = mod_all[1, 0:1], mod_all[1, 1:1 + Bs]
    xp, k1, v1 = swa_ctx_mixer(xp, modp, swa_w, Lp)
    xs = swa_lat_mixer(xs, mods, swa_w, Ls, cache_l1_k.reshape(Bs, P, SWA_NK),
                       cache_l1_v.reshape(Bs, P, SWA_NK), cos128, sin128)
    wr = jnp.zeros((moe_router.shape[0], D, 128), F32).at[:, :, :N_EXPERTS].set(moe_router).astype(BF)
    xp, xs = moe_layer(xp, xs, mod_all[1, 0:1 + Bs], wr[0], moe_w_gate, moe_w_up, moe_w_down, 0,
                       final_norm.reshape(1, D), Lp, Ls, False)

    conv_wts = dict(win=conv_win.astype(BF), cw=conv_w, wout=conv_wout.astype(BF))
    modp, mods = mod_all[2, 0:1], mod_all[2, 1:1 + Bs]
    xp = conv_mixer(xp, modp, conv_wts, Lp)
    xs = conv_mixer(xs, mods, conv_wts, Ls)
    xp = dense_ffn(xp, modp, dense_w_gate, dense_w_up, dense_w_down, 1, Lp)
    xs = dense_ffn(xs, mods, dense_w_gate, dense_w_up, dense_w_down, 1, Ls)

    wuq3 = mla_wuq.reshape(Q_LORA, MLA_H, QK_NOPE + QK_ROPE)
    zpad = jnp.zeros((Q_LORA, MLA_H, MLA_QW - QK_NOPE - QK_ROPE), F32)
    wuq = jnp.concatenate([wuq3, zpad], axis=2).reshape(Q_LORA, MLA_H * MLA_QW)
    rperm = _rope_partner_perm(QK_ROPE)
    wuq_sw = jnp.concatenate([jnp.zeros((Q_LORA, MLA_H, QK_NOPE), F32), wuq3[:, :, QK_NOPE:][:, :, rperm], zpad],
                             axis=2).reshape(Q_LORA, MLA_H * MLA_QW)
    kpad = jnp.zeros((D, 128 - QK_ROPE), F32)
    wdkv = jnp.concatenate([mla_wdkv, kpad], axis=1)
    wdkv_sw = jnp.concatenate([mla_wdkv[:, KV_LORA:][:, rperm], kpad], axis=1)
    wukv3 = mla_wukv.reshape(KV_LORA, MLA_H, QK_NOPE + V_HEAD)
    wukv = jnp.concatenate([wukv3[:, :, :QK_NOPE].reshape(KV_LORA, MLA_H * QK_NOPE),
                            wukv3[:, :, QK_NOPE:].reshape(KV_LORA, MLA_H * V_HEAD)], axis=1)
    mla_w = dict(wdq=mla_wdq.astype(BF), qnorm=mla_qnorm.reshape(1, Q_LORA), wuq=wuq.astype(BF),
                 wuqsw=wuq_sw.astype(BF), wdkv=wdkv.astype(BF), wdkvsw=wdkv_sw.astype(BF),
                 kvnorm=mla_kvnorm.reshape(1, KV_LORA), wukv=wukv.astype(BF), wo=mla_wo.astype(BF))
    ones = jnp.ones((Ls, QK_NOPE), F32)
    z128 = jnp.zeros((Ls, QK_NOPE), F32)
    z64 = jnp.zeros((Ls, 64), F32)
    tabs = [jnp.concatenate([ones, c64, z64], axis=1), jnp.concatenate([z128, s64, z64], axis=1),
            jnp.concatenate([c64, z64], axis=1), jnp.concatenate([s64, z64], axis=1)]
    modp, mods = mod_all[3, 0:1], mod_all[3, 1:1 + Bs]
    xp, ckv3, kr3 = mla_ctx_mixer(xp, modp, mla_w, Lp)
    kr_c128 = jnp.concatenate([cache_l3_krope, jnp.zeros((Bs, P, 128 - QK_ROPE), F32)], axis=2)
    xs = mla_lat_mixer(xs, mods, mla_w, Ls, cache_l3_ckv, kr_c128, tabs)
    xp, xs = moe_layer(xp, xs, mod_all[3, 0:1 + Bs], wr[1], moe_w_gate, moe_w_up, moe_w_down, 1,
                       final_norm.reshape(1, D), Lp, Ls, True)

    return (xp.reshape(Bp, Lp, D), xs.reshape(Bs, Ls, D), new_state,
            k1.reshape(Bp, Lp, SWA_KVH, SWA_HD), v1.reshape(Bp, Lp, SWA_KVH, SWA_HD),
            ckv3.reshape(Bp, Lp, KV_LORA), kr3[:, :QK_ROPE].reshape(Bp, Lp, QK_ROPE))
```

```python
import functools

import numpy as np
import jax
import jax.numpy as jnp
from jax import lax
from jax.experimental import pallas as pl
from jax.experimental.pallas import tpu as pltpu

BF = jnp.bfloat16
F32 = jnp.float32

D = 1024
EPS = 1e-6
NEG = -1e30
ROPE_THETA = 10000.0
GRID_W = 64

GLA_H, GLA_DK, GLA_DV, GLA_RANK, GLA_CHUNK = 4, 128, 256, 16, 64
GLA_SCALE = GLA_DK ** -0.5
GLA_INV_NORMALIZER = 1.0 / 16.0

SWA_H, SWA_KVH, SWA_HD, SWA_WINDOW, SWA_BLOCK = 16, 4, 64, 128, 128
SWA_SCALE = SWA_HD ** -0.5
SWA_NQ = SWA_H * SWA_HD
SWA_NK = SWA_KVH * SWA_HD

MLA_H, Q_LORA, KV_LORA, QK_NOPE, QK_ROPE, V_HEAD = 8, 384, 256, 128, 64, 128
MLA_SCALE = (QK_NOPE + QK_ROPE) ** -0.5
MLA_QW = 256

N_EXPERTS = 8
MOE_TM = 512
MOE_TF = 512
MOE_FULL, MOE_LO, MOE_HI, MOE_LO_ZERO_HI, MOE_ZERO, MOE_IDLE = range(6)
DENSE_TM = 1024
DENSE_TF = 256

VMEM_LIMIT = 56 * 1024 * 1024


def _cp(*sem):
    return pltpu.CompilerParams(dimension_semantics=sem, vmem_limit_bytes=VMEM_LIMIT)


def _mm(a, b):
    return jnp.dot(a, b, preferred_element_type=F32)


def _mm_nt(a, b):
    return lax.dot_general(a, b, (((1,), (1,)), ((), ())), preferred_element_type=F32)


def _mm_tn(a, b):
    return lax.dot_general(a, b, (((0,), (0,)), ((), ())), preferred_element_type=F32)


def _rms(x):
    return x * lax.rsqrt(jnp.mean(x * x, axis=-1, keepdims=True) + EPS)


def _silu(x):
    return x * (1.0 / (1.0 + jnp.exp(-x)))


def _modulate(x, mod_ref, j):
    shift = mod_ref[0, 3 * j:3 * j + 1, :]
    scale = mod_ref[0, 3 * j + 1:3 * j + 2, :]
    return _rms(x) * (1.0 + scale) + shift


def _gate(mod_ref, j):
    return mod_ref[0, 3 * j + 2:3 * j + 3, :]


def _const_spec(a):
    nd = a.ndim
    return pl.BlockSpec(a.shape, lambda *_: (0,) * nd)


def _mod_spec(tm, rows_per_mod):
    return pl.BlockSpec((1, 6, D), lambda i, *_: ((i * tm) // rows_per_mod, 0, 0))


def _adaln_kernel(c_ref, w_ref, b_ref, o_ref):
    s = _silu(c_ref[...]).astype(BF)
    o_ref[0] = _mm(s, w_ref[0].astype(BF)) + b_ref[0]


def adaln_all(cvec8, w_mod, b_mod):
    nl, d, n = w_mod.shape
    tn = 1536
    return pl.pallas_call(
        _adaln_kernel,
        grid=(nl, n // tn),
        in_specs=[pl.BlockSpec((8, d), lambda l, j: (0, 0)),
                  pl.BlockSpec((1, d, tn), lambda l, j: (l, 0, j)),
                  pl.BlockSpec((1, 1, tn), lambda l, j: (l, 0, j))],
        out_specs=pl.BlockSpec((1, 8, tn), lambda l, j: (l, 0, j)),
        out_shape=jax.ShapeDtypeStruct((nl, 8, n), F32),
        compiler_params=_cp("arbitrary", "arbitrary"),
        name="adaln",
    )(cvec8, w_mod, b_mod.reshape(nl, 1, n))


def _split3(x):
    hi = x.astype(BF)
    rem = x - hi.astype(F32)
    mid = rem.astype(BF)
    lo = (rem - mid.astype(F32)).astype(BF)
    return hi, mid, lo


def _mm3(a, pieces):
    return _mm(a, pieces[0]) + _mm(a, pieces[1]) + _mm(a, pieces[2])


def _gla_pre_kernel(x_ref, mod_ref, wp_ref, wlr_ref, wgk2_ref, bgk_ref,
                    qt_ref, kt_ref, kd_ref, v_ref, g_ref, dec_ref):
    tm = x_ref.shape[0]
    C = GLA_CHUNK
    G = 256
    h = _modulate(x_ref[...], mod_ref, 0).astype(BF)
    nk = GLA_H * GLA_DK
    nv = GLA_H * GLA_DV
    q = _mm(h, wp_ref[:, 0:nk]) * GLA_SCALE
    k = _mm(h, wp_ref[:, nk:2 * nk])
    v_ref[...] = _mm(h, wp_ref[:, 2 * nk:2 * nk + nv]).astype(BF)
    g_ref[...] = _mm(h, wp_ref[:, 2 * nk + nv:2 * nk + 2 * nv])
    lr = _mm(h, wlr_ref[...]).astype(BF)
    ii = lax.broadcasted_iota(jnp.int32, (G, G), 0)
    jj = lax.broadcasted_iota(jnp.int32, (G, G), 1)
    same = (ii // C) == (jj // C)
    ci = lax.broadcasted_iota(jnp.int32, (tm // C, tm), 0)
    cj = lax.broadcasted_iota(jnp.int32, (tm // C, tm), 1)
    sel = jnp.where((cj // C) == ci, 1.0, 0.0).astype(BF)
    for r in range(2):
        z = _mm(lr, wgk2_ref[r]) + bgk_ref[r:r + 1, :]
        gk = (jnp.minimum(z, 0.0) - jnp.log1p(jnp.exp(-jnp.abs(z)))) * GLA_INV_NORMALIZER
        pieces = _split3(gk)
        tri = jnp.where(jnp.logical_and(same, (jj <= ii) if r == 0 else (jj >= ii)), 1.0, 0.0).astype(BF)
        dec_ref[r] = jnp.exp(_mm3(sel, pieces))
        for gi in range(tm // G):
            rows = slice(gi * G, (gi + 1) * G)
            pg = tuple(p[rows] for p in pieces)
            b = _mm3(tri, pg)
            b3 = b.reshape(G // C, C, nk)
            edge = b3[:, C - 1:C, :] if r == 0 else b3[:, 0:1, :]
            tot = jnp.broadcast_to(edge, b3.shape).reshape(G, nk)
            qt_ref[r, rows, :] = (q[rows] * jnp.exp(b)).astype(BF)
            kt_ref[r, rows, :] = (k[rows] * jnp.exp(-b)).astype(BF)
            kd_ref[r, rows, :] = (k[rows] * jnp.exp(tot - b)).astype(BF)


def _gla_scan_kernel(*refs, L, S, has_h0, emit_state):
    qt_ref, kt_ref, kd_ref, v_ref, g_ref, dec_ref, gn_ref = refs[:7]
    pos = 7
    h0_ref = None
    if has_h0:
        h0_ref = refs[pos]
        pos += 1
    y_ref = refs[pos]
    pos += 1
    st_ref = None
    if emit_state:
        st_ref = refs[pos]
        pos += 1
    st_s, of_s, ob_s = refs[pos], refs[pos + 1], refs[pos + 2]

    C = GLA_CHUNK
    n_chunks = L // C
    ii = lax.broadcasted_iota(jnp.int32, (C, C), 0)
    jj = lax.broadcasted_iota(jnp.int32, (C, C), 1)
    keeps = ((jj <= ii), (jj >= ii))
    chains = [(sq, r, hd) for sq in range(S) for r in range(2) for hd in range(GLA_H)]
    slot = lambda sq, r, hd: (sq * 2 + r) * GLA_H + hd
    for sq, r, hd in chains:
        if has_h0:
            st_s[slot(sq, r, hd)] = h0_ref[sq, r, hd].T
        else:
            st_s[slot(sq, r, hd)] = jnp.zeros((GLA_DV, GLA_DK), F32)

    def step(i):
        for sq, r, hd in chains:
            n = i if r == 0 else n_chunks - 1 - i
            start = sq * L + n * C
            if not isinstance(start, int):
                start = pl.multiple_of(start, C)
            rows = pl.ds(start, C)
            o_s = of_s if r == 0 else ob_s
            kc = slice(hd * GLA_DK, (hd + 1) * GLA_DK)
            vc = slice(hd * GLA_DV, (hd + 1) * GLA_DV)
            qt = qt_ref[r, rows, kc]
            v = v_ref[rows, vc]
            att = jnp.where(keeps[r], _mm_nt(qt, kt_ref[r, rows, kc]), 0.0).astype(BF)
            st = st_s[slot(sq, r, hd)]
            o_s[rows, vc] = _mm_nt(qt, st.astype(BF)) + _mm(att, v)
            st_s[slot(sq, r, hd)] = st * dec_ref[r, sq, n, :, kc] + _mm_tn(v, kd_ref[r, rows, kc])

    if n_chunks <= 4:
        for i in range(n_chunks):
            step(i)
    else:
        def body(i, carry):
            step(i)
            return carry
        lax.fori_loop(0, n_chunks, body, 0)

    RC = 256
    for c in range(S * L // RC):
        rows = slice(c * RC, (c + 1) * RC)
        for hd in range(GLA_H):
            vc = slice(hd * GLA_DV, (hd + 1) * GLA_DV)
            y = _rms(of_s[rows, vc] + ob_s[rows, vc]) * gn_ref[...]
            y_ref[rows, vc] = (y * _silu(g_ref[rows, vc])).astype(BF)
    if emit_state:
        for sq, r, hd in chains:
            st_ref[sq, r, hd] = st_s[slot(sq, r, hd)].T


def _residual_out_kernel(y_ref, x_ref, mod_ref, w_ref, o_ref, *, gate_j):
    o_ref[...] = x_ref[...] + _gate(mod_ref, gate_j) * _mm(y_ref[...], w_ref[...])


def gla_mixer(x, mod, w, L, h0):
    M = x.shape[0]
    B = M // L
    tm = 512
    nk, nv = GLA_H * GLA_DK, GLA_H * GLA_DV
    row = lambda n: pl.BlockSpec((tm, n), lambda i: (i, 0))
    dirs = lambda: pl.BlockSpec((2, tm, nk), lambda i: (0, i, 0))
    n_chunks = L // GLA_CHUNK
    qt, kt, kd, v, g, dec = pl.pallas_call(
        _gla_pre_kernel,
        grid=(M // tm,),
        in_specs=[row(D), _mod_spec(tm, L if mod.shape[0] > 1 else M),
                  _const_spec(w["wp"]), _const_spec(w["wlr"]), _const_spec(w["wgk2"]),
                  _const_spec(w["bgk"])],
        out_specs=[dirs(), dirs(), dirs(), row(nv), row(nv),
                   pl.BlockSpec((2, tm // GLA_CHUNK, nk), lambda i: (0, i, 0))],
        out_shape=[jax.ShapeDtypeStruct((2, M, nk), BF), jax.ShapeDtypeStruct((2, M, nk), BF),
                   jax.ShapeDtypeStruct((2, M, nk), BF),
                   jax.ShapeDtypeStruct((M, nv), BF), jax.ShapeDtypeStruct((M, nv), F32),
                   jax.ShapeDtypeStruct((2, M // GLA_CHUNK, nk), F32)],
        compiler_params=_cp("arbitrary"),
        name="gla_pre",
    )(x, mod, w["wp"], w["wlr"], w["wgk2"], w["bgk"])
    dec = dec.reshape(2, B, n_chunks, 1, nk)

    has_h0 = h0 is not None
    emit_state = not has_h0
    S = 2 if (L <= 256 and B % 2 == 0) else 1
    SL = S * L
    dir_spec = lambda: pl.BlockSpec((2, SL, nk), lambda b: (0, b, 0))
    in_specs = [dir_spec(), dir_spec(), dir_spec(),
                pl.BlockSpec((SL, nv), lambda b: (b, 0)),
                pl.BlockSpec((SL, nv), lambda b: (b, 0)),
                pl.BlockSpec((2, S, n_chunks, 1, nk), lambda b: (0, b, 0, 0, 0)),
                pl.BlockSpec((1, GLA_DV), lambda b: (0, 0))]
    args = [qt, kt, kd, v, g, dec, w["gnorm"]]
    st_spec = pl.BlockSpec((S, 2, GLA_H, GLA_DK, GLA_DV), lambda b: (b, 0, 0, 0, 0))
    if has_h0:
        in_specs.append(st_spec)
        args.append(h0)
    out_specs = [pl.BlockSpec((SL, nv), lambda b: (b, 0))]
    out_shape = [jax.ShapeDtypeStruct((M, nv), BF)]
    if emit_state:
        out_specs.append(st_spec)
        out_shape.append(jax.ShapeDtypeStruct((B, 2, GLA_H, GLA_DK, GLA_DV), F32))
    res = pl.pallas_call(
        functools.partial(_gla_scan_kernel, L=L, S=S, has_h0=has_h0, emit_state=emit_state),
        grid=(B // S,),
        in_specs=in_specs,
        out_specs=out_specs,
        out_shape=out_shape,
        scratch_shapes=[pltpu.VMEM((S * 2 * GLA_H, GLA_DV, GLA_DK), F32), pltpu.VMEM((SL, nv), F32),
                        pltpu.VMEM((SL, nv), F32)],
        compiler_params=_cp("arbitrary"),
        name="gla_scan",
    )(*args)
    y = res[0]
    state = res[1] if emit_state else None

    x_new = pl.pallas_call(
        functools.partial(_residual_out_kernel, gate_j=0),
        grid=(M // tm,),
        in_specs=[row(nv), row(D), _mod_spec(tm, L if mod.shape[0] > 1 else M), _const_spec(w["wo"])],
        out_specs=row(D),
        out_shape=jax.ShapeDtypeStruct((M, D), F32),
        compiler_params=_cp("arbitrary"),
        name="gla_post",
    )(y, x, mod, w["wo"])
    return x_new, state


def _softmax_sink_heads(q_of, k_of, v_of, sink_ref, o_s):
    for hq in range(SWA_H):
        kh = hq // (SWA_H // SWA_KVH)
        s = _mm_nt(q_of(hq), k_of(kh))
        sink = sink_ref[hq]
        m = jnp.maximum(jnp.max(s, axis=-1, keepdims=True), sink)
        e = jnp.exp(s - m)
        p = e / (jnp.sum(e, axis=-1, keepdims=True) + jnp.exp(sink - m))
        o_s[:, hq * SWA_HD:(hq + 1) * SWA_HD] = _mm(p.astype(BF), v_of(kh)).astype(BF)


def _swa_ctx_kernel(sink_ref, x_ref, mod_ref, wqkv_ref, wo_ref, o_ref, k_out, v_out, o_s):
    x = x_ref[...]
    h = _modulate(x, mod_ref, 0).astype(BF)
    qkv = _mm(h, wqkv_ref[...])
    k_out[...] = qkv[:, SWA_NQ:SWA_NQ + SWA_NK]
    v_out[...] = qkv[:, SWA_NQ + SWA_NK:]
    q_of = lambda hq: (qkv[:, hq * SWA_HD:(hq + 1) * SWA_HD] * SWA_SCALE).astype(BF)
    k_of = lambda kh: qkv[:, SWA_NQ + kh * SWA_HD:SWA_NQ + (kh + 1) * SWA_HD].astype(BF)
    v_of = lambda kh: qkv[:, SWA_NQ + SWA_NK + kh * SWA_HD:SWA_NQ + SWA_NK + (kh + 1) * SWA_HD].astype(BF)
    _softmax_sink_heads(q_of, k_of, v_of, sink_ref, o_s)
    o_ref[...] = x + _gate(mod_ref, 0) * _mm(o_s[...], wo_ref[...])


def swa_ctx_mixer(x, mod, w, L):
    M = x.shape[0]
    row = lambda n: pl.BlockSpec((L, n), lambda i: (i, 0))
    return pl.pallas_call(
        _swa_ctx_kernel,
        grid=(M // L,),
        in_specs=[pl.BlockSpec(memory_space=pltpu.SMEM), row(D), _mod_spec(L, M),
                  _const_spec(w["wqkv"]), _const_spec(w["wo"])],
        out_specs=[row(D), row(SWA_NK), row(SWA_NK)],
        out_shape=[jax.ShapeDtypeStruct((M, D), F32), jax.ShapeDtypeStruct((M, SWA_NK), F32),
                   jax.ShapeDtypeStruct((M, SWA_NK), F32)],
        scratch_shapes=[pltpu.VMEM((L, SWA_NQ), BF)],
        compiler_params=_cp("arbitrary"),
        name="swa_ctx",
    )(w["sink"], x, mod, w["wqkv"], w["wo"])


def _swa_lat_kernel(sink_ref, x_ref, mod_ref, wqkv_ref, wsw_ref, cos_ref, sin_ref, kc_ref, vc_ref,
                    wo_ref, o_ref, q_s, k_s, v_s, o_s, *, L):
    n = pl.program_id(1)
    RC = 256

    @pl.when(n == 0)
    def _():
        for c in range(L // RC):
            rows = slice(c * RC, (c + 1) * RC)
            h = _modulate(x_ref[rows, :], mod_ref, 0).astype(BF)
            qkv = _mm(h, wqkv_ref[...])
            sw = _mm(h, wsw_ref[...])
            cos = cos_ref[rows, :]
            sin = sin_ref[rows, :]
            cq = jnp.concatenate([cos] * (SWA_NQ // 128), axis=1)
            sq = jnp.concatenate([sin] * (SWA_NQ // 128), axis=1)
            ck = jnp.concatenate([cos] * (SWA_NK // 128), axis=1)
            sk = jnp.concatenate([sin] * (SWA_NK // 128), axis=1)
            q = (qkv[:, :SWA_NQ] * SWA_SCALE) * cq + (sw[:, :SWA_NQ] * SWA_SCALE) * sq
            q_s[rows, :] = q.astype(BF)
            k = qkv[:, SWA_NQ:SWA_NQ + SWA_NK] * ck + sw[:, SWA_NQ:SWA_NQ + SWA_NK] * sk
            k_s[rows, :] = k.astype(BF)
            v_s[rows, :] = qkv[:, SWA_NQ + SWA_NK:].astype(BF)

    QB = SWA_BLOCK
    KW = 3 * SWA_BLOCK
    r0 = pl.multiple_of(n * QB, QB)
    ws = pl.multiple_of(jnp.clip((n - 1) * QB, 0, L - KW), QB)
    qpos = r0 + lax.broadcasted_iota(jnp.int32, (QB, KW), 0)
    kpos = ws + lax.broadcasted_iota(jnp.int32, (QB, KW), 1)
    valid = jnp.abs(kpos - qpos) <= SWA_WINDOW
    for hq in range(SWA_H):
        kh = hq // (SWA_H // SWA_KVH)
        hs = slice(kh * SWA_HD, (kh + 1) * SWA_HD)
        q = q_s[pl.ds(r0, QB), hq * SWA_HD:(hq + 1) * SWA_HD]
        s1 = jnp.where(valid, _mm_nt(q, k_s[pl.ds(ws, KW), hs]), NEG)
        s2 = _mm_nt(q, kc_ref[0, :, hs].astype(BF))
        sink = sink_ref[hq]
        m = jnp.maximum(jnp.maximum(jnp.max(s1, axis=-1, keepdims=True),
                                    jnp.max(s2, axis=-1, keepdims=True)), sink)
        e1 = jnp.exp(s1 - m)
        e2 = jnp.exp(s2 - m)
        den = (jnp.sum(e1, axis=-1, keepdims=True) + jnp.sum(e2, axis=-1, keepdims=True)
               + jnp.exp(sink - m))
        o = (_mm(e1.astype(BF), v_s[pl.ds(ws, KW), hs])
             + _mm(e2.astype(BF), vc_ref[0, :, hs].astype(BF)))
        o_s[:, hq * SWA_HD:(hq + 1) * SWA_HD] = (o * (1.0 / den)).astype(BF)
    o_ref[...] = x_ref[pl.ds(r0, QB), :] + _gate(mod_ref, 0) * _mm(o_s[...], wo_ref[...])


def swa_lat_mixer(x, mod, w, L, kc, vc, cos128, sin128):
    M = x.shape[0]
    B = M // L
    P = kc.shape[1]
    nb = L // SWA_BLOCK
    return pl.pallas_call(
        functools.partial(_swa_lat_kernel, L=L),
        grid=(B, nb),
        in_specs=[pl.BlockSpec(memory_space=pltpu.SMEM),
                  pl.BlockSpec((L, D), lambda b, n: (b, 0)),
                  pl.BlockSpec((1, 6, D), lambda b, n: (b, 0, 0)),
                  _const_spec(w["wqkv"]), _const_spec(w["wsw"]),
                  _const_spec(cos128), _const_spec(sin128),
                  pl.BlockSpec((1, P, SWA_NK), lambda b, n: (b, 0, 0)),
                  pl.BlockSpec((1, P, SWA_NK), lambda b, n: (b, 0, 0)),
                  _const_spec(w["wo"])],
        out_specs=pl.BlockSpec((SWA_BLOCK, D), lambda b, n: (b * nb + n, 0)),
        out_shape=jax.ShapeDtypeStruct((M, D), F32),
        scratch_shapes=[pltpu.VMEM((L, SWA_NQ), BF), pltpu.VMEM((L, SWA_NK), BF),
                        pltpu.VMEM((L, SWA_NK), BF), pltpu.VMEM((SWA_BLOCK, SWA_NQ), BF)],
        compiler_params=_cp("arbitrary", "arbitrary"),
        name="swa_lat",
    )(w["sink"], x, mod, w["wqkv"], w["wsw"], cos128, sin128, kc, vc, w["wo"])


def _conv_kernel(x_ref, mod_ref, win_ref, cw_ref, wout_ref, o_ref, y_s, *, L):
    x = x_ref[...]
    rows = x.shape[0]
    h = _modulate(x, mod_ref, 0).astype(BF)
    CC = 256
    row = lax.broadcasted_iota(jnp.int32, (rows, CC), 0) % L
    for c in range(D // CC):
        cols = slice(c * CC, (c + 1) * CC)
        bg = _mm(h, win_ref[:, c * CC:(c + 1) * CC])
        cg = _mm(h, win_ref[:, D + c * CC:D + (c + 1) * CC])
        u = _mm(h, win_ref[:, 2 * D + c * CC:2 * D + (c + 1) * CC])
        cu = cg * u
        prev = jnp.where(row == 0, 0.0, pltpu.roll(cu, 1, 0))
        nxt = jnp.where(row == L - 1, 0.0, pltpu.roll(cu, rows - 1, 0))
        conv = prev * cw_ref[0:1, cols] + cu * cw_ref[1:2, cols] + nxt * cw_ref[2:3, cols]
        y_s[:, cols] = (bg * conv).astype(BF)
    o_ref[...] = x + _gate(mod_ref, 0) * _mm(y_s[...], wout_ref[...])


def conv_mixer(x, mod, w, L):
    M = x.shape[0]
    shared_mod = mod.shape[0] == 1
    tm = min(M, 1024) if shared_mod else L
    row = pl.BlockSpec((tm, D), lambda i: (i, 0))
    return pl.pallas_call(
        functools.partial(_conv_kernel, L=L),
        grid=(M // tm,),
        in_specs=[row, _mod_spec(tm, M if shared_mod else L), _const_spec(w["win"]),
                  _const_spec(w["cw"]), _const_spec(w["wout"])],
        out_specs=row,
        out_shape=jax.ShapeDtypeStruct((M, D), F32),
        scratch_shapes=[pltpu.VMEM((tm, D), BF)],
        compiler_params=_cp("arbitrary"),
        name="conv_mix",
    )(x, mod, w["win"], w["cw"], w["wout"])


def _mla_ctx_kernel(x_ref, mod_ref, wdq_ref, qn_ref, wuq_ref, wdkv_ref, kvn_ref, wukv_ref, wo_ref,
                    o_ref, ckv_out, kr_out, o_s):
    x = x_ref[...]
    h = _modulate(x, mod_ref, 0).astype(BF)
    cq = (_rms(_mm(h, wdq_ref[...])) * qn_ref[...]).astype(BF)
    q = _mm(cq, wuq_ref[...]).astype(BF)
    kvc = _mm(h, wdkv_ref[...])
    ckv = _rms(kvc[:, :KV_LORA]) * kvn_ref[...]
    kr = kvc[:, KV_LORA:]
    ckv_out[...] = ckv
    kr_out[...] = kr
    kv = _mm(ckv.astype(BF), wukv_ref[...]).astype(BF)
    krb = kr.astype(BF)
    nn = MLA_H * QK_NOPE
    for hh in range(MLA_H):
        qh = q[:, hh * MLA_QW:(hh + 1) * MLA_QW]
        kh = jnp.concatenate([kv[:, hh * QK_NOPE:(hh + 1) * QK_NOPE], krb], axis=1)
        s = _mm_nt(qh, kh) * MLA_SCALE
        m = jnp.max(s, axis=-1, keepdims=True)
        e = jnp.exp(s - m)
        inv = 1.0 / jnp.sum(e, axis=-1, keepdims=True)
        vh = kv[:, nn + hh * V_HEAD:nn + (hh + 1) * V_HEAD]
        o_s[:, hh * V_HEAD:(hh + 1) * V_HEAD] = (_mm(e.astype(BF), vh) * inv).astype(BF)
    o_ref[...] = x + _gate(mod_ref, 0) * _mm(o_s[...], wo_ref[...])


def mla_ctx_mixer(x, mod, w, L):
    M = x.shape[0]
    row = lambda n: pl.BlockSpec((L, n), lambda i: (i, 0))
    ws = [w["wdq"], w["qnorm"], w["wuq"], w["wdkv"], w["kvnorm"], w["wukv"], w["wo"]]
    return pl.pallas_call(
        _mla_ctx_kernel,
        grid=(M // L,),
        in_specs=[row(D), _mod_spec(L, M)] + [_const_spec(a) for a in ws],
        out_specs=[row(D), row(KV_LORA), row(128)],
        out_shape=[jax.ShapeDtypeStruct((M, D), F32), jax.ShapeDtypeStruct((M, KV_LORA), F32),
                   jax.ShapeDtypeStruct((M, 128), F32)],
        scratch_shapes=[pltpu.VMEM((L, MLA_H * V_HEAD), BF)],
        compiler_params=_cp("arbitrary"),
        name="mla_ctx",
    )(x, mod, *ws)


def _mla_lat_kernel(x_ref, mod_ref, wdq_ref, qn_ref, wuq_ref, wuqsw_ref, wdkv_ref, wdkvsw_ref,
                    kvn_ref, wukv_ref, cq_ref, sq_ref, ck_ref, sk_ref, ckvc_ref, krc_ref, wo_ref,
                    o_ref, q_s, kn_s, v_s, kr_s, knc_s, vc_s, o_s, *, L, QB):
    n = pl.program_id(1)
    RC = 256
    nn = MLA_H * QK_NOPE

    @pl.when(n == 0)
    def _():
        for c in range(L // RC):
            rows = slice(c * RC, (c + 1) * RC)
            h = _modulate(x_ref[rows, :], mod_ref, 0).astype(BF)
            cq = (_rms(_mm(h, wdq_ref[...])) * qn_ref[...]).astype(BF)
            cosq = jnp.concatenate([cq_ref[rows, :]] * MLA_H, axis=1)
            sinq = jnp.concatenate([sq_ref[rows, :]] * MLA_H, axis=1)
            q = _mm(cq, wuq_ref[...]) * cosq + _mm(cq, wuqsw_ref[...]) * sinq
            q_s[rows, :] = q.astype(BF)
            kvc = _mm(h, wdkv_ref[...])
            ksw = _mm(h, wdkvsw_ref[...])
            ckv = _rms(kvc[:, :KV_LORA]) * kvn_ref[...]
            kr_s[rows, :] = (kvc[:, KV_LORA:] * ck_ref[rows, :] + ksw * sk_ref[rows, :]).astype(BF)
            kv = _mm(ckv.astype(BF), wukv_ref[...])
            kn_s[rows, :] = kv[:, :nn].astype(BF)
            v_s[rows, :] = kv[:, nn:].astype(BF)
        kvp = _mm(ckvc_ref[0].astype(BF), wukv_ref[...])
        knc_s[...] = kvp[:, :nn].astype(BF)
        vc_s[...] = kvp[:, nn:].astype(BF)

    r0 = pl.multiple_of(n * QB, QB)
    krc = krc_ref[0].astype(BF)
    for hh in range(MLA_H):
        ns = slice(hh * QK_NOPE, (hh + 1) * QK_NOPE)
        qh = q_s[pl.ds(r0, QB), hh * MLA_QW:(hh + 1) * MLA_QW]
        k1 = jnp.concatenate([kn_s[:, ns], kr_s[...]], axis=1)
        k2 = jnp.concatenate([knc_s[:, ns], krc], axis=1)
        s1 = _mm_nt(qh, k1) * MLA_SCALE
        s2 = _mm_nt(qh, k2) * MLA_SCALE
        m = jnp.maximum(jnp.max(s1, axis=-1, keepdims=True), jnp.max(s2, axis=-1, keepdims=True))
        e1 = jnp.exp(s1 - m)
        e2 = jnp.exp(s2 - m)
        den = jnp.sum(e1, axis=-1, keepdims=True) + jnp.sum(e2, axis=-1, keepdims=True)
        vs = slice(hh * V_HEAD, (hh + 1) * V_HEAD)
        o = _mm(e1.astype(BF), v_s[:, vs]) + _mm(e2.astype(BF), vc_s[:, vs])
        o_s[:, vs] = (o * (1.0 / den)).astype(BF)
    o_ref[...] = x_ref[pl.ds(r0, QB), :] + _gate(mod_ref, 0) * _mm(o_s[...], wo_ref[...])


def mla_lat_mixer(x, mod, w, L, ckv_c, kr_c128, tabs):
    M = x.shape[0]
    B = M // L
    P = ckv_c.shape[1]
    QB = 256
    nb = L // QB
    ws1 = [w["wdq"], w["qnorm"], w["wuq"], w["wuqsw"], w["wdkv"], w["wdkvsw"], w["kvnorm"], w["wukv"]]
    nv = MLA_H * V_HEAD
    return pl.pallas_call(
        functools.partial(_mla_lat_kernel, L=L, QB=QB),
        grid=(B, nb),
        in_specs=[pl.BlockSpec((L, D), lambda b, n: (b, 0)),
                  pl.BlockSpec((1, 6, D), lambda b, n: (b, 0, 0))]
                 + [_const_spec(a) for a in ws1] + [_const_spec(a) for a in tabs]
                 + [pl.BlockSpec((1, P, KV_LORA), lambda b, n: (b, 0, 0)),
                    pl.BlockSpec((1, P, 128), lambda b, n: (b, 0, 0)),
                    _const_spec(w["wo"])],
        out_specs=pl.BlockSpec((QB, D), lambda b, n: (b * nb + n, 0)),
        out_shape=jax.ShapeDtypeStruct((M, D), F32),
        scratch_shapes=[pltpu.VMEM((L, MLA_H * MLA_QW), BF), pltpu.VMEM((L, nv), BF),
                        pltpu.VMEM((L, nv), BF), pltpu.VMEM((L, 128), BF),
                        pltpu.VMEM((P, nv), BF), pltpu.VMEM((P, nv), BF), pltpu.VMEM((QB, nv), BF)],
        compiler_params=_cp("arbitrary", "arbitrary"),
        name="mla_lat",
    )(x, mod, *ws1, *tabs, ckv_c, kr_c128, w["wo"])


def _swiglu_partial(xb, wg_ref, wu_ref, wd_ref):
    hg = _mm(xb, wg_ref[...].astype(BF))
    hu = _mm(xb, wu_ref[...].astype(BF))
    a = (_silu(hg) * hu).astype(BF)
    return _mm(a, wd_ref[...].astype(BF))


def _dense_ffn_kernel(x_hbm, mod_ref, wga_ref, wgb_ref, wua_ref, wub_ref, wda_ref, wdb_ref, o_ref, x_s, xm_s,
                      sem, *, odd_blocks):
    i = pl.program_id(0)
    f = pl.program_id(1)
    last = pl.num_programs(1) - 1
    tm = x_s.shape[0]

    @pl.when(f == 0)
    def _():
        cp = pltpu.make_async_copy(x_hbm.at[pl.ds(pl.multiple_of(i * tm, tm), tm), :], x_s, sem)
        cp.start()
        cp.wait()
        RC = 512
        for c in range(tm // RC):
            rows = slice(c * RC, (c + 1) * RC)
            xm_s[rows, :] = _modulate(x_s[rows, :], mod_ref, 1).astype(BF)
        o_ref[...] = jnp.zeros_like(o_ref)

    def pair():
        xb = xm_s[...]
        acts = []
        for wg_ref, wu_ref in ((wga_ref, wua_ref), (wgb_ref, wub_ref)):
            hg = _mm(xb, wg_ref[...].astype(BF))
            hu = _mm(xb, wu_ref[...].astype(BF))
            acts.append((_silu(hg) * hu).astype(BF))
        wd = jnp.concatenate([wda_ref[...].astype(BF), wdb_ref[...].astype(BF)], axis=0)
        return _mm(jnp.concatenate(acts, axis=1), wd)

    if odd_blocks:
        @pl.when(f < last)
        def _():
            o_ref[...] += pair()

        @pl.when(f == last)
        def _():
            o_ref[...] += _swiglu_partial(xm_s[...], wga_ref, wua_ref, wda_ref)
    else:
        o_ref[...] += pair()

    @pl.when(f == last)
    def _():
        o_ref[...] = x_s[...] + _gate(mod_ref, 1) * o_ref[...]


def dense_ffn(x, mod, wg, wu, wd, j, L):
    M = x.shape[0]
    F = wg.shape[-1]
    shared_mod = mod.shape[0] == 1
    tm = min(M, 2 * DENSE_TM) if shared_mod else DENSE_TM
    tf = DENSE_TF
    nfb = F // tf
    blk_a = lambda f: 2 * f
    blk_b = lambda f: jnp.minimum(2 * f + 1, nfb - 1)
    col = lambda blk: pl.BlockSpec((None, D, tf), lambda i, f: (j, 0, blk(f)))
    row = lambda blk: pl.BlockSpec((None, tf, D), lambda i, f: (j, blk(f), 0))
    return pl.pallas_call(
        functools.partial(_dense_ffn_kernel, odd_blocks=nfb % 2 == 1),
        grid=(M // tm, (nfb + 1) // 2),
        in_specs=[pl.BlockSpec(memory_space=pl.ANY),
                  pl.BlockSpec((1, 6, D), lambda i, f: ((i * tm) // (M if shared_mod else L), 0, 0)),
                  col(blk_a), col(blk_b), col(blk_a), col(blk_b), row(blk_a), row(blk_b)],
        out_specs=pl.BlockSpec((tm, D), lambda i, f: (i, 0)),
        out_shape=jax.ShapeDtypeStruct((M, D), F32),
        scratch_shapes=[pltpu.VMEM((tm, D), F32), pltpu.VMEM((tm, D), BF), pltpu.SemaphoreType.DMA(())],
        compiler_params=_cp("arbitrary", "arbitrary"),
        name="dense_ffn",
    )(x, mod, wg, wg, wu, wu, wd, wd)


def _router_kernel(xp_ref, xs_ref, mod_ref, wr_ref, xm_ref, idx_ref, w_ref, cnt_ref, cnt_s, *, np_tiles):
    i = pl.program_id(0)
    tm = xm_ref.shape[0]

    @pl.when(i == 0)
    def _():
        cnt_s[...] = jnp.zeros_like(cnt_s)

    x = jnp.where(i < np_tiles, xp_ref[...], xs_ref[...])
    xmod = _modulate(x, mod_ref, 1)
    xm_ref[...] = xmod
    xm = xmod.astype(BF)
    lane = lax.broadcasted_iota(jnp.int32, idx_ref.shape, 1)
    logits = jnp.where(lane < N_EXPERTS, _mm(xm, wr_ref[...]), -jnp.inf)
    m1 = jnp.max(logits, axis=-1, keepdims=True)
    i1 = jnp.min(jnp.where(logits == m1, lane, 128), axis=-1, keepdims=True)
    rest = jnp.where(lane == i1, -jnp.inf, logits)
    m2 = jnp.max(rest, axis=-1, keepdims=True)
    i2 = jnp.min(jnp.where(rest == m2, lane, 128), axis=-1, keepdims=True)
    e = jnp.exp(m2 - m1)
    w_ref[...] = jnp.where(lane == 0, 1.0 / (1.0 + e), e / (1.0 + e))
    oh1 = jnp.where(lane == i1, 1.0, 0.0)
    oh2 = jnp.where(lane == i2, 1.0, 0.0)
    rr = lax.broadcasted_iota(jnp.int32, (tm, tm), 0)
    cc = lax.broadcasted_iota(jnp.int32, (tm, tm), 1)
    below = jnp.where(cc < rr, 1.0, 0.0).astype(BF)
    run = cnt_s[0:1, :]
    tot1 = jnp.sum(oh1, axis=0, keepdims=True)
    tot2 = jnp.sum(oh2, axis=0, keepdims=True)
    r1 = jnp.sum(oh1 * (run + _mm(below, oh1.astype(BF))), axis=-1, keepdims=True)
    r2 = jnp.sum(oh2 * (run + tot1 + _mm(below, oh2.astype(BF))), axis=-1, keepdims=True)
    idx_ref[...] = jnp.where(lane == 0, i1, jnp.where(lane == 1, i2, jnp.where(
        lane == 2, r1.astype(jnp.int32), r2.astype(jnp.int32))))
    total = run + tot1 + tot2
    cnt_s[...] = jnp.broadcast_to(total, cnt_s.shape)
    cnt_ref[...] = jnp.broadcast_to(total, cnt_ref.shape)


def route(xp, xs, mod3, wr, Ls):
    Mp, Ms = xp.shape[0], xs.shape[0]
    T = Mp + Ms
    tm = 512
    npt = Mp // tm
    row = lambda n: pl.BlockSpec((tm, n), lambda i: (i, 0))
    return pl.pallas_call(
        functools.partial(_router_kernel, np_tiles=npt),
        grid=(T // tm,),
        in_specs=[pl.BlockSpec((tm, D), lambda i: (jnp.minimum(i, npt - 1), 0)),
                  pl.BlockSpec((tm, D), lambda i: (jnp.maximum(i - npt, 0), 0)),
                  pl.BlockSpec((1, 6, D), lambda i: (jnp.where(i < npt, 0, 1 + ((i - npt) * tm) // Ls), 0, 0)),
                  _const_spec(wr)],
        out_specs=[row(D), row(128), row(128), pl.BlockSpec((8, 128), lambda i: (0, 0))],
        out_shape=[jax.ShapeDtypeStruct((T, D), F32), jax.ShapeDtypeStruct((T, 128), jnp.int32),
                   jax.ShapeDtypeStruct((T, 128), F32), jax.ShapeDtypeStruct((8, 128), F32)],
        scratch_shapes=[pltpu.VMEM((8, 128), F32)],
        compiler_params=_cp("arbitrary"),
        name="moe_route",
    )(xp, xs, mod3, wr)


def _moe_ffn_kernel(vb_ref, ve_ref, vm_ref, nv_ref, xa_ref, xb_ref, wg_ref, wu_ref, wd_ref, o_ref, x_s,
                    *, half_blocks):
    v = pl.program_id(0)
    f = pl.program_id(1)
    mode = vm_ref[v]
    H = MOE_TM

    @pl.when(jnp.logical_and(v < nv_ref[0], f == 0))
    def _():
        x_s[...] = jnp.where(vb_ref[v] < half_blocks, xa_ref[...], xb_ref[...]).astype(BF)

    def run(rows):
        @pl.when(f == 0)
        def _():
            o_ref[rows, :] = jnp.zeros((rows.stop - rows.start, D), F32)

        o_ref[rows, :] += _swiglu_partial(x_s[rows, :], wg_ref, wu_ref, wd_ref)

    @pl.when(mode == MOE_FULL)
    def _():
        run(slice(0, 2 * H))

    @pl.when(jnp.logical_or(mode == MOE_LO, mode == MOE_LO_ZERO_HI))
    def _():
        run(slice(0, H))

    @pl.when(mode == MOE_HI)
    def _():
        run(slice(H, 2 * H))

    @pl.when(jnp.logical_and(mode == MOE_LO_ZERO_HI, f == 0))
    def _():
        o_ref[H:2 * H, :] = jnp.zeros((H, D), F32)

    @pl.when(jnp.logical_and(mode == MOE_ZERO, f == 0))
    def _():
        o_ref[...] = jnp.zeros_like(o_ref)


def moe_ffn(xa, xb, tile_expert, n_valid, wg, wu, wd, j):
    R = 2 * xa.shape[0]
    F = wg.shape[-1]
    bm, tf = 2 * MOE_TM, MOE_TF
    nf = F // tf
    nb = R // bm
    nhb = nb // 2
    n_vis = nb + N_EXPERTS - 1
    blocks = jnp.arange(nb, dtype=jnp.int32)
    t0, t1 = tile_expert[0::2], tile_expert[1::2]
    valid0 = 2 * blocks < n_valid[0]
    valid1 = 2 * blocks + 1 < n_valid[0]
    two = jnp.logical_and(valid1, t0 != t1)
    cnt = jnp.where(valid0, 1 + two.astype(jnp.int32), 0)
    cum = jnp.cumsum(cnt)
    nv = cum[-1:].astype(jnp.int32)
    v_eff = jnp.minimum(jnp.arange(n_vis, dtype=jnp.int32), nv[0] - 1)
    vb = jnp.sum((v_eff[:, None] >= cum[None, :]).astype(jnp.int32), axis=1)
    pick = lambda a: jnp.sum(jnp.where(vb[:, None] == blocks[None, :], a[None, :].astype(jnp.int32), 0), axis=1)
    k = v_eff - pick(cum - cnt)
    two_v, t0_v, t1_v, valid1_v = pick(two), pick(t0), pick(t1), pick(valid1)
    vm = jnp.where(two_v == 1, jnp.where(k == 0, MOE_LO, MOE_HI),
                   jnp.where(valid1_v == 1, MOE_FULL, MOE_LO_ZERO_HI))
    ve = jnp.where(jnp.logical_and(two_v == 1, k == 1), t1_v, t0_v).astype(jnp.int32)
    spare = jnp.arange(n_vis, dtype=jnp.int32) - nv[0]
    n_used = jnp.sum(valid0.astype(jnp.int32))
    fill = jnp.logical_and(spare >= 0, n_used + spare < nb)
    idle = jnp.logical_and(spare >= 0, jnp.logical_not(fill))
    vb = jnp.where(fill, n_used + spare, jnp.where(jnp.logical_and(idle, n_used < nb), nb - 1, vb))
    vm = jnp.where(fill, MOE_ZERO, jnp.where(idle, MOE_IDLE, vm)).astype(jnp.int32)

    def fidx(v, f, nv):
        return jnp.where(v < nv[0], f, nf - 1)

    grid_spec = pltpu.PrefetchScalarGridSpec(
        num_scalar_prefetch=4,
        grid=(n_vis, nf),
        in_specs=[pl.BlockSpec((bm, D), lambda v, f, vb, ve, vm, nv: (jnp.minimum(vb[v], nhb - 1), 0)),
                  pl.BlockSpec((bm, D), lambda v, f, vb, ve, vm, nv: (jnp.maximum(vb[v] - nhb, 0), 0)),
                  pl.BlockSpec((None, None, D, tf), lambda v, f, vb, ve, vm, nv: (j, ve[v], 0, fidx(v, f, nv))),
                  pl.BlockSpec((None, None, D, tf), lambda v, f, vb, ve, vm, nv: (j, ve[v], 0, fidx(v, f, nv))),
                  pl.BlockSpec((None, None, tf, D), lambda v, f, vb, ve, vm, nv: (j, ve[v], fidx(v, f, nv), 0))],
        out_specs=pl.BlockSpec((bm, D), lambda v, f, vb, ve, vm, nv: (vb[v], 0)),
        scratch_shapes=[pltpu.VMEM((bm, D), BF)],
    )
    return pl.pallas_call(
        functools.partial(_moe_ffn_kernel, half_blocks=nhb),
        grid_spec=grid_spec,
        out_shape=jax.ShapeDtypeStruct((R, D), F32),
        compiler_params=_cp("arbitrary", "arbitrary"),
        name="moe_ffn",
    )(vb.astype(jnp.int32), ve, vm, nv, xa, xb, wg, wu, wd)


def _combine_kernel(x_ref, mod_ref, ya_ref, yb_ref, w_ref, fn_ref, o_ref, *, final):
    y = w_ref[:, 0:1] * ya_ref[...] + w_ref[:, 1:2] * yb_ref[...]
    o = x_ref[...] + _gate(mod_ref, 1) * y
    if final:
        o = _rms(o) * fn_ref[...]
    o_ref[...] = o


def moe_combine(x, mod, ya, yb, w, fn, L, final, row_off):
    M = x.shape[0]
    tm = 512
    off = row_off // tm
    row = pl.BlockSpec((tm, D), lambda i: (i, 0))
    row_o = lambda n: pl.BlockSpec((tm, n), lambda i: (i + off, 0))
    return pl.pallas_call(
        functools.partial(_combine_kernel, final=final),
        grid=(M // tm,),
        in_specs=[row, _mod_spec(tm, L if mod.shape[0] > 1 else M), row_o(D), row_o(D), row_o(128),
                  _const_spec(fn)],
        out_specs=row,
        out_shape=jax.ShapeDtypeStruct((M, D), F32),
        compiler_params=_cp("arbitrary"),
        name="moe_combine",
    )(x, mod, ya, yb, w, fn)


def moe_layer(xp, xs, mod3, wr, wg, wu, wd, j, fn, Lp, Ls, final):
    Mp = xp.shape[0]
    xm, idx, wts, cnt = route(xp, xs, mod3, wr, Ls)
    T = xm.shape[0]
    tm = MOE_TM
    R = 2 * T + N_EXPERTS * tm
    counts = cnt[0, :N_EXPERTS].astype(jnp.int32)
    tiles_per = (counts + tm - 1) // tm
    tile_end = jnp.cumsum(tiles_per)
    offs = (tile_end - tiles_per) * tm
    experts = jnp.arange(N_EXPERTS, dtype=jnp.int32)
    dest = jnp.sum(jnp.where(idx[:, 0:2, None] == experts, offs, 0), axis=-1) + idx[:, 2:4]
    tok = jnp.arange(2 * T, dtype=jnp.int32) // 2
    src = (jnp.arange(R, dtype=jnp.int32) % T).at[dest.reshape(-1)].set(
        tok, unique_indices=True, mode="promise_in_bounds")
    n_tiles = R // tm
    n_valid = tile_end[-1:].astype(jnp.int32)
    t_ids = jnp.arange(n_tiles, dtype=jnp.int32)
    te = jnp.sum((t_ids[:, None] >= tile_end[None, :]).astype(jnp.int32), axis=1)
    te_last = jnp.sum((n_valid - 1 >= tile_end).astype(jnp.int32))
    te = jnp.where(t_ids < n_valid[0], te, te_last).astype(jnp.int32)
    xa = xm.at[src[:R // 2]].get(mode="promise_in_bounds")
    xb = xm.at[src[R // 2:]].get(mode="promise_in_bounds")
    y_sorted = moe_ffn(xa, xb, te, n_valid, wg, wu, wd, j)
    ya = y_sorted.at[dest[:, 0]].get(mode="promise_in_bounds")
    yb = y_sorted.at[dest[:, 1]].get(mode="promise_in_bounds")
    xp_new = moe_combine(xp, mod3[0:1], ya, yb, wts, fn, Lp, final, 0)
    xs_new = moe_combine(xs, mod3[1:], ya, yb, wts, fn, Ls, final, Mp)
    return xp_new, xs_new


def _rope_partner_perm(n):
    c = np.arange(n)
    return np.where((c % 32) < 16, c + 16, c - 16)


def _rope_tables(L):
    nf = 16
    inv = ROPE_THETA ** (-np.arange(nf, dtype=np.float32) / nf)
    pos = np.arange(L)
    ang = np.stack([(pos // GRID_W).astype(np.float32), (pos % GRID_W).astype(np.float32)],
                   axis=-1)[:, :, None] * inv
    cos = jnp.cos(jnp.asarray(ang, F32))
    sin = jnp.sin(jnp.asarray(ang, F32))
    c64 = jnp.stack([cos, cos], axis=2).reshape(L, 64)
    s64 = jnp.stack([-sin, sin], axis=2).reshape(L, 64)
    return c64, s64


def kernel(x_prompt, x_sample, state_l0_gla, cache_l1_k, cache_l1_v, cache_l3_ckv, cache_l3_krope, c, c_ctx, w_mod, b_mod, gla_wq, gla_wk, gla_wv, gla_wg, gla_wgk1, gla_wgk2, gla_bgk, gla_norm, gla_wo, swa_wqkv, swa_sink, swa_wo, conv_win, conv_w, conv_wout, mla_wdq, mla_qnorm, mla_wuq, mla_wdkv, mla_kvnorm, mla_wukv, mla_wo, dense_w_gate, dense_w_up, dense_w_down, moe_router, moe_w_gate, moe_w_up, moe_w_down, final_norm):
    Bp, Lp, _ = x_prompt.shape
    Bs, Ls, _ = x_sample.shape
    P = cache_l1_k.shape[1]
    xp = x_prompt.reshape(Bp * Lp, D)
    xs = x_sample.reshape(Bs * Ls, D)

    cvec = jnp.zeros((8, D), F32).at[0].set(c_ctx).at[1:1 + Bs].set(c)
    mod_all = adaln_all(cvec, w_mod, b_mod).reshape(w_mod.shape[0], 8, 6, D)

    wlr = jnp.zeros((D, 128), F32).at[:, :GLA_RANK].set(gla_wgk1[0]).at[:, GLA_RANK:2 * GLA_RANK].set(gla_wgk1[1])
    wgk2 = jnp.zeros((2, 128, GLA_H * GLA_DK), F32)
    wgk2 = wgk2.at[0, :GLA_RANK].set(gla_wgk2[0]).at[1, GLA_RANK:2 * GLA_RANK].set(gla_wgk2[1])
    gla_w = dict(wp=jnp.concatenate([gla_wq, gla_wk, gla_wv, gla_wg], axis=1).astype(BF),
                 wlr=wlr.astype(BF), wgk2=wgk2.astype(BF), bgk=gla_bgk,
                 gnorm=gla_norm.reshape(1, GLA_DV), wo=gla_wo.astype(BF))
    modp, mods = mod_all[0, 0:1], mod_all[0, 1:1 + Bs]
    xp, new_state = gla_mixer(xp, modp, gla_w, Lp, None)
    xs, _ = gla_mixer(xs, mods, gla_w, Ls, state_l0_gla)
    xp = dense_ffn(xp, modp, dense_w_gate, dense_w_up, dense_w_down, 0, Lp)
    xs = dense_ffn(xs, mods, dense_w_gate, dense_w_up, dense_w_down, 0, Ls)

    c64, s64 = _rope_tables(Ls)
    cos128 = jnp.concatenate([c64, c64], axis=1)
    sin128 = jnp.concatenate([s64, s64], axis=1)
    perm = _rope_partner_perm(SWA_NQ + SWA_NK)
    swa_w = dict(wqkv=swa_wqkv.astype(BF), wsw=swa_wqkv[:, perm].astype(BF), wo=swa_wo.astype(BF),
                 sink=swa_sink)
    modp, mods = mod_all[1, 0:1], mod_all[1, 1:1 + Bs]
    xp, k1, v1 = swa_ctx_mixer(xp, modp, swa_w, Lp)
    xs = swa_lat_mixer(xs, mods, swa_w, Ls, cache_l1_k.reshape(Bs, P, SWA_NK),
                       cache_l1_v.reshape(Bs, P, SWA_NK), cos128, sin128)
    wr = jnp.zeros((moe_router.shape[0], D, 128), F32).at[:, :, :N_EXPERTS].set(moe_router).astype(BF)
    xp, xs = moe_layer(xp, xs, mod_all[1, 0:1 + Bs], wr[0], moe_w_gate, moe_w_up, moe_w_down, 0,
                       final_norm.reshape(1, D), Lp, Ls, False)

    conv_wts = dict(win=conv_win.astype(BF), cw=conv_w, wout=conv_wout.astype(BF))
    modp, mods = mod_all[2, 0:1], mod_all[2, 1:1 + Bs]
    xp = conv_mixer(xp, modp, conv_wts, Lp)
    xs = conv_mixer(xs, mods, conv_wts, Ls)
    xp = dense_ffn(xp, modp, dense_w_gate, dense_w_up, dense_w_down, 1, Lp)
    xs = dense_ffn(xs, mods, dense_w_gate, dense_w_up, dense_w_down, 1, Ls)

    wuq3 = mla_wuq.reshape(Q_LORA, MLA_H, QK_NOPE + QK_ROPE)
    zpad = jnp.zeros((Q_LORA, MLA_H, MLA_QW - QK_NOPE - QK_ROPE), F32)
    wuq = jnp.concatenate([wuq3, zpad], axis=2).reshape(Q_LORA, MLA_H * MLA_QW)
    rperm = _rope_partner_perm(QK_ROPE)
    wuq_sw = jnp.concatenate([jnp.zeros((Q_LORA, MLA_H, QK_NOPE), F32), wuq3[:, :, QK_NOPE:][:, :, rperm], zpad],
                             axis=2).reshape(Q_LORA, MLA_H * MLA_QW)
    kpad = jnp.zeros((D, 128 - QK_ROPE), F32)
    wdkv = jnp.concatenate([mla_wdkv, kpad], axis=1)
    wdkv_sw = jnp.concatenate([mla_wdkv[:, KV_LORA:][:, rperm], kpad], axis=1)
    wukv3 = mla_wukv.reshape(KV_LORA, MLA_H, QK_NOPE + V_HEAD)
    wukv = jnp.concatenate([wukv3[:, :, :QK_NOPE].reshape(KV_LORA, MLA_H * QK_NOPE),
                            wukv3[:, :, QK_NOPE:].reshape(KV_LORA, MLA_H * V_HEAD)], axis=1)
    mla_w = dict(wdq=mla_wdq.astype(BF), qnorm=mla_qnorm.reshape(1, Q_LORA), wuq=wuq.astype(BF),
                 wuqsw=wuq_sw.astype(BF), wdkv=wdkv.astype(BF), wdkvsw=wdkv_sw.astype(BF),
                 kvnorm=mla_kvnorm.reshape(1, KV_LORA), wukv=wukv.astype(BF), wo=mla_wo.astype(BF))
    ones = jnp.ones((Ls, QK_NOPE), F32)
    z128 = jnp.zeros((Ls, QK_NOPE), F32)
    z64 = jnp.zeros((Ls, 64), F32)
    tabs = [jnp.concatenate([ones, c64, z64], axis=1), jnp.concatenate([z128, s64, z64], axis=1),
            jnp.concatenate([c64, z64], axis=1), jnp.concatenate([s64, z64], axis=1)]
    modp, mods = mod_all[3, 0:1], mod_all[3, 1:1 + Bs]
    xp, ckv3, kr3 = mla_ctx_mixer(xp, modp, mla_w, Lp)
    kr_c128 = jnp.concatenate([cache_l3_krope, jnp.zeros((Bs, P, 128 - QK_ROPE), F32)], axis=2)
    xs = mla_lat_mixer(xs, mods, mla_w, Ls, cache_l3_ckv, kr_c128, tabs)
    xp, xs = moe_layer(xp, xs, mod_all[3, 0:1 + Bs], wr[1], moe_w_gate, moe_w_up, moe_w_down, 1,
                       final_norm.reshape(1, D), Lp, Ls, True)

    return (xp.reshape(Bp, Lp, D), xs.reshape(Bs, Ls, D), new_state,
            k1.reshape(Bp, Lp, SWA_KVH, SWA_HD), v1.reshape(Bp, Lp, SWA_KVH, SWA_HD),
            ckv3.reshape(Bp, Lp, KV_LORA), kr3[:, :QK_ROPE].reshape(Bp, Lp, QK_ROPE))
```

```python
import functools

import numpy as np
import jax
import jax.numpy as jnp
from jax import lax
from jax.experimental import pallas as pl
from jax.experimental.pallas import tpu as pltpu

BF = jnp.bfloat16
F32 = jnp.float32

D = 1024
EPS = 1e-6
NEG = -1e30
ROPE_THETA = 10000.0
GRID_W = 64

GLA_H, GLA_DK, GLA_DV, GLA_RANK, GLA_CHUNK = 4, 128, 256, 16, 64
GLA_SCALE = GLA_DK ** -0.5
GLA_INV_NORMALIZER = 1.0 / 16.0

SWA_H, SWA_KVH, SWA_HD, SWA_WINDOW, SWA_BLOCK = 16, 4, 64, 128, 128
SWA_SCALE = SWA_HD ** -0.5
SWA_NQ = SWA_H * SWA_HD
SWA_NK = SWA_KVH * SWA_HD

MLA_H, Q_LORA, KV_LORA, QK_NOPE, QK_ROPE, V_HEAD = 8, 384, 256, 128, 64, 128
MLA_SCALE = (QK_NOPE + QK_ROPE) ** -0.5
MLA_QW = 256

N_EXPERTS = 8
MOE_TM = 512
MOE_TF = 512
MOE_FULL, MOE_LO, MOE_HI, MOE_LO_ZERO_HI, MOE_ZERO, MOE_IDLE = range(6)
ROUTE_TM = 1024
DENSE_TM = 1024
DENSE_TF = 256

VMEM_LIMIT = 56 * 1024 * 1024


def _cp(*sem):
    return pltpu.CompilerParams(dimension_semantics=sem, vmem_limit_bytes=VMEM_LIMIT)


def _mm(a, b):
    return jnp.dot(a, b, preferred_element_type=F32)


def _mm_nt(a, b):
    return lax.dot_general(a, b, (((1,), (1,)), ((), ())), preferred_element_type=F32)


def _mm_tn(a, b):
    return lax.dot_general(a, b, (((0,), (0,)), ((), ())), preferred_element_type=F32)


def _rms(x):
    return x * lax.rsqrt(jnp.mean(x * x, axis=-1, keepdims=True) + EPS)


def _silu(x):
    return x * (1.0 / (1.0 + jnp.exp(-x)))


def _modulate(x, mod_ref, j):
    shift = mod_ref[0, 3 * j:3 * j + 1, :]
    scale = mod_ref[0, 3 * j + 1:3 * j + 2, :]
    return _rms(x) * (1.0 + scale) + shift


def _gate(mod_ref, j):
    return mod_ref[0, 3 * j + 2:3 * j + 3, :]


def _const_spec(a):
    nd = a.ndim
    return pl.BlockSpec(a.shape, lambda *_: (0,) * nd)


def _mod_spec(tm, rows_per_mod):
    return pl.BlockSpec((1, 6, D), lambda i, *_: ((i * tm) // rows_per_mod, 0, 0))


def _adaln_kernel(c_ref, w_ref, b_ref, o_ref):
    s = _silu(c_ref[...]).astype(BF)
    o_ref[0] = _mm(s, w_ref[0].astype(BF)) + b_ref[0]


def adaln_all(cvec8, w_mod, b_mod):
    nl, d, n = w_mod.shape
    tn = 1536
    return pl.pallas_call(
        _adaln_kernel,
        grid=(nl, n // tn),
        in_specs=[pl.BlockSpec((8, d), lambda l, j: (0, 0)),
                  pl.BlockSpec((1, d, tn), lambda l, j: (l, 0, j)),
                  pl.BlockSpec((1, 1, tn), lambda l, j: (l, 0, j))],
        out_specs=pl.BlockSpec((1, 8, tn), lambda l, j: (l, 0, j)),
        out_shape=jax.ShapeDtypeStruct((nl, 8, n), F32),
        compiler_params=_cp("arbitrary", "arbitrary"),
        name="adaln",
    )(cvec8, w_mod, b_mod.reshape(nl, 1, n))


def _split3(x):
    hi = x.astype(BF)
    rem = x - hi.astype(F32)
    mid = rem.astype(BF)
    lo = (rem - mid.astype(F32)).astype(BF)
    return hi, mid, lo


def _mm3(a, pieces):
    return _mm(a, pieces[0]) + _mm(a, pieces[1]) + _mm(a, pieces[2])


def _gla_pre_kernel(x_ref, mod_ref, wp_ref, wlr_ref, wgk2_ref, bgk_ref,
                    qt_ref, kt_ref, kd_ref, v_ref, g_ref, dec_ref):
    tm = x_ref.shape[0]
    C = GLA_CHUNK
    G = 256
    h = _modulate(x_ref[...], mod_ref, 0).astype(BF)
    nk = GLA_H * GLA_DK
    nv = GLA_H * GLA_DV
    q = _mm(h, wp_ref[:, 0:nk]) * GLA_SCALE
    k = _mm(h, wp_ref[:, nk:2 * nk])
    v_ref[...] = _mm(h, wp_ref[:, 2 * nk:2 * nk + nv]).astype(BF)
    g_ref[...] = _mm(h, wp_ref[:, 2 * nk + nv:2 * nk + 2 * nv])
    lr = _mm(h, wlr_ref[...]).astype(BF)
    ii = lax.broadcasted_iota(jnp.int32, (G, G), 0)
    jj = lax.broadcasted_iota(jnp.int32, (G, G), 1)
    same = (ii // C) == (jj // C)
    ci = lax.broadcasted_iota(jnp.int32, (tm // C, tm), 0)
    cj = lax.broadcasted_iota(jnp.int32, (tm // C, tm), 1)
    sel = jnp.where((cj // C) == ci, 1.0, 0.0).astype(BF)
    for r in range(2):
        z = _mm(lr, wgk2_ref[r]) + bgk_ref[r:r + 1, :]
        gk = (jnp.minimum(z, 0.0) - jnp.log1p(jnp.exp(-jnp.abs(z)))) * GLA_INV_NORMALIZER
        pieces = _split3(gk)
        tri = jnp.where(jnp.logical_and(same, (jj <= ii) if r == 0 else (jj >= ii)), 1.0, 0.0).astype(BF)
        dec_ref[r] = jnp.exp(_mm3(sel, pieces))
        for gi in range(tm // G):
            rows = slice(gi * G, (gi + 1) * G)
            pg = tuple(p[rows] for p in pieces)
            b = _mm3(tri, pg)
            b3 = b.reshape(G // C, C, nk)
            edge = b3[:, C - 1:C, :] if r == 0 else b3[:, 0:1, :]
            tot = jnp.broadcast_to(edge, b3.shape).reshape(G, nk)
            qt_ref[r, rows, :] = (q[rows] * jnp.exp(b)).astype(BF)
            kt_ref[r, rows, :] = (k[rows] * jnp.exp(-b)).astype(BF)
            kd_ref[r, rows, :] = (k[rows] * jnp.exp(tot - b)).astype(BF)


def _gla_scan_kernel(*refs, L, S, has_h0, emit_state):
    qt_ref, kt_ref, kd_ref, v_ref, g_ref, dec_ref, gn_ref = refs[:7]
    pos = 7
    h0_ref = None
    if has_h0:
        h0_ref = refs[pos]
        pos += 1
    y_ref = refs[pos]
    pos += 1
    st_ref = None
    if emit_state:
        st_ref = refs[pos]
        pos += 1
    st_s, of_s, ob_s = refs[pos], refs[pos + 1], refs[pos + 2]

    C = GLA_CHUNK
    n_chunks = L // C
    ii = lax.broadcasted_iota(jnp.int32, (C, C), 0)
    jj = lax.broadcasted_iota(jnp.int32, (C, C), 1)
    keeps = ((jj <= ii), (jj >= ii))
    chains = [(sq, r, hd) for sq in range(S) for r in range(2) for hd in range(GLA_H)]
    slot = lambda sq, r, hd: (sq * 2 + r) * GLA_H + hd
    for sq, r, hd in chains:
        if has_h0:
            st_s[slot(sq, r, hd)] = h0_ref[sq, r, hd].T
        else:
            st_s[slot(sq, r, hd)] = jnp.zeros((GLA_DV, GLA_DK), F32)

    def step(i):
        for sq, r, hd in chains:
            n = i if r == 0 else n_chunks - 1 - i
            start = sq * L + n * C
            if not isinstance(start, int):
                start = pl.multiple_of(start, C)
            rows = pl.ds(start, C)
            o_s = of_s if r == 0 else ob_s
            kc = slice(hd * GLA_DK, (hd + 1) * GLA_DK)
            vc = slice(hd * GLA_DV, (hd + 1) * GLA_DV)
            qt = qt_ref[r, rows, kc]
            v = v_ref[rows, vc]
            att = jnp.where(keeps[r], _mm_nt(qt, kt_ref[r, rows, kc]), 0.0).astype(BF)
            st = st_s[slot(sq, r, hd)]
            o_s[rows, vc] = _mm_nt(qt, st.astype(BF)) + _mm(att, v)
            st_s[slot(sq, r, hd)] = st * dec_ref[r, sq, n, :, kc] + _mm_tn(v, kd_ref[r, rows, kc])

    if n_chunks <= 4:
        for i in range(n_chunks):
            step(i)
    else:
        def body(i, carry):
            step(i)
            return carry
        lax.fori_loop(0, n_chunks, body, 0)

    RC = 256
    for c in range(S * L // RC):
        rows = slice(c * RC, (c + 1) * RC)
        for hd in range(GLA_H):
            vc = slice(hd * GLA_DV, (hd + 1) * GLA_DV)
            y = _rms(of_s[rows, vc] + ob_s[rows, vc]) * gn_ref[...]
            y_ref[rows, vc] = (y * _silu(g_ref[rows, vc])).astype(BF)
    if emit_state:
        for sq, r, hd in chains:
            st_ref[sq, r, hd] = st_s[slot(sq, r, hd)].T


def _residual_out_kernel(y_ref, x_ref, mod_ref, w_ref, o_ref, *, gate_j):
    o_ref[...] = x_ref[...] + _gate(mod_ref, gate_j) * _mm(y_ref[...], w_ref[...])


def gla_mixer(x, mod, w, L, h0):
    M = x.shape[0]
    B = M // L
    tm = 512
    nk, nv = GLA_H * GLA_DK, GLA_H * GLA_DV
    row = lambda n: pl.BlockSpec((tm, n), lambda i: (i, 0))
    dirs = lambda: pl.BlockSpec((2, tm, nk), lambda i: (0, i, 0))
    n_chunks = L // GLA_CHUNK
    qt, kt, kd, v, g, dec = pl.pallas_call(
        _gla_pre_kernel,
        grid=(M // tm,),
        in_specs=[row(D), _mod_spec(tm, L if mod.shape[0] > 1 else M),
                  _const_spec(w["wp"]), _const_spec(w["wlr"]), _const_spec(w["wgk2"]),
                  _const_spec(w["bgk"])],
        out_specs=[dirs(), dirs(), dirs(), row(nv), row(nv),
                   pl.BlockSpec((2, tm // GLA_CHUNK, nk), lambda i: (0, i, 0))],
        out_shape=[jax.ShapeDtypeStruct((2, M, nk), BF), jax.ShapeDtypeStruct((2, M, nk), BF),
                   jax.ShapeDtypeStruct((2, M, nk), BF),
                   jax.ShapeDtypeStruct((M, nv), BF), jax.ShapeDtypeStruct((M, nv), F32),
                   jax.ShapeDtypeStruct((2, M // GLA_CHUNK, nk), F32)],
        compiler_params=_cp("arbitrary"),
        name="gla_pre",
    )(x, mod, w["wp"], w["wlr"], w["wgk2"], w["bgk"])
    dec = dec.reshape(2, B, n_chunks, 1, nk)

    has_h0 = h0 is not None
    emit_state = not has_h0
    S = 2 if (L <= 256 and B % 2 == 0) else 1
    SL = S * L
    dir_spec = lambda: pl.BlockSpec((2, SL, nk), lambda b: (0, b, 0))
    in_specs = [dir_spec(), dir_spec(), dir_spec(),
                pl.BlockSpec((SL, nv), lambda b: (b, 0)),
                pl.BlockSpec((SL, nv), lambda b: (b, 0)),
                pl.BlockSpec((2, S, n_chunks, 1, nk), lambda b: (0, b, 0, 0, 0)),
                pl.BlockSpec((1, GLA_DV), lambda b: (0, 0))]
    args = [qt, kt, kd, v, g, dec, w["gnorm"]]
    st_spec = pl.BlockSpec((S, 2, GLA_H, GLA_DK, GLA_DV), lambda b: (b, 0, 0, 0, 0))
    if has_h0:
        in_specs.append(st_spec)
        args.append(h0)
    out_specs = [pl.BlockSpec((SL, nv), lambda b: (b, 0))]
    out_shape = [jax.ShapeDtypeStruct((M, nv), BF)]
    if emit_state:
        out_specs.append(st_spec)
        out_shape.append(jax.ShapeDtypeStruct((B, 2, GLA_H, GLA_DK, GLA_DV), F32))
    res = pl.pallas_call(
        functools.partial(_gla_scan_kernel, L=L, S=S, has_h0=has_h0, emit_state=emit_state),
        grid=(B // S,),
        in_specs=in_specs,
        out_specs=out_specs,
        out_shape=out_shape,
        scratch_shapes=[pltpu.VMEM((S * 2 * GLA_H, GLA_DV, GLA_DK), F32), pltpu.VMEM((SL, nv), F32),
                        pltpu.VMEM((SL, nv), F32)],
        compiler_params=_cp("arbitrary"),
        name="gla_scan",
    )(*args)
    y = res[0]
    state = res[1] if emit_state else None

    x_new = pl.pallas_call(
        functools.partial(_residual_out_kernel, gate_j=0),
        grid=(M // tm,),
        in_specs=[row(nv), row(D), _mod_spec(tm, L if mod.shape[0] > 1 else M), _const_spec(w["wo"])],
        out_specs=row(D),
        out_shape=jax.ShapeDtypeStruct((M, D), F32),
        compiler_params=_cp("arbitrary"),
        name="gla_post",
    )(y, x, mod, w["wo"])
    return x_new, state


def _softmax_sink_heads(q_of, k_of, v_of, sink_ref, o_s):
    for hq in range(SWA_H):
        kh = hq // (SWA_H // SWA_KVH)
        s = _mm_nt(q_of(hq), k_of(kh))
        sink = sink_ref[hq]
        m = jnp.maximum(jnp.max(s, axis=-1, keepdims=True), sink)
        e = jnp.exp(s - m)
        p = e / (jnp.sum(e, axis=-1, keepdims=True) + jnp.exp(sink - m))
        o_s[:, hq * SWA_HD:(hq + 1) * SWA_HD] = _mm(p.astype(BF), v_of(kh)).astype(BF)


def _swa_ctx_kernel(sink_ref, x_ref, mod_ref, wqkv_ref, wo_ref, o_ref, k_out, v_out, o_s):
    x = x_ref[...]
    h = _modulate(x, mod_ref, 0).astype(BF)
    qkv = _mm(h, wqkv_ref[...])
    k_out[...] = qkv[:, SWA_NQ:SWA_NQ + SWA_NK]
    v_out[...] = qkv[:, SWA_NQ + SWA_NK:]
    q_of = lambda hq: (qkv[:, hq * SWA_HD:(hq + 1) * SWA_HD] * SWA_SCALE).astype(BF)
    k_of = lambda kh: qkv[:, SWA_NQ + kh * SWA_HD:SWA_NQ + (kh + 1) * SWA_HD].astype(BF)
    v_of = lambda kh: qkv[:, SWA_NQ + SWA_NK + kh * SWA_HD:SWA_NQ + SWA_NK + (kh + 1) * SWA_HD].astype(BF)
    _softmax_sink_heads(q_of, k_of, v_of, sink_ref, o_s)
    o_ref[...] = x + _gate(mod_ref, 0) * _mm(o_s[...], wo_ref[...])


def swa_ctx_mixer(x, mod, w, L):
    M = x.shape[0]
    row = lambda n: pl.BlockSpec((L, n), lambda i: (i, 0))
    return pl.pallas_call(
        _swa_ctx_kernel,
        grid=(M // L,),
        in_specs=[pl.BlockSpec(memory_space=pltpu.SMEM), row(D), _mod_spec(L, M),
                  _const_spec(w["wqkv"]), _const_spec(w["wo"])],
        out_specs=[row(D), row(SWA_NK), row(SWA_NK)],
        out_shape=[jax.ShapeDtypeStruct((M, D), F32), jax.ShapeDtypeStruct((M, SWA_NK), F32),
                   jax.ShapeDtypeStruct((M, SWA_NK), F32)],
        scratch_shapes=[pltpu.VMEM((L, SWA_NQ), BF)],
        compiler_params=_cp("arbitrary"),
        name="swa_ctx",
    )(w["sink"], x, mod, w["wqkv"], w["wo"])


def _swa_lat_kernel(sink_ref, x_ref, mod_ref, wqkv_ref, wsw_ref, cos_ref, sin_ref, kc_ref, vc_ref,
                    wo_ref, o_ref, q_s, k_s, v_s, o_s, *, L):
    n = pl.program_id(1)
    RC = 256

    @pl.when(n == 0)
    def _():
        for c in range(L // RC):
            rows = slice(c * RC, (c + 1) * RC)
            h = _modulate(x_ref[rows, :], mod_ref, 0).astype(BF)
            qkv = _mm(h, wqkv_ref[...])
            sw = _mm(h, wsw_ref[...])
            cos = cos_ref[rows, :]
            sin = sin_ref[rows, :]
            cq = jnp.concatenate([cos] * (SWA_NQ // 128), axis=1)
            sq = jnp.concatenate([sin] * (SWA_NQ // 128), axis=1)
            ck = jnp.concatenate([cos] * (SWA_NK // 128), axis=1)
            sk = jnp.concatenate([sin] * (SWA_NK // 128), axis=1)
            q = (qkv[:, :SWA_NQ] * SWA_SCALE) * cq + (sw[:, :SWA_NQ] * SWA_SCALE) * sq
            q_s[rows, :] = q.astype(BF)
            k = qkv[:, SWA_NQ:SWA_NQ + SWA_NK] * ck + sw[:, SWA_NQ:SWA_NQ + SWA_NK] * sk
            k_s[rows, :] = k.astype(BF)
            v_s[rows, :] = qkv[:, SWA_NQ + SWA_NK:].astype(BF)

    QB = SWA_BLOCK
    KW = 3 * SWA_BLOCK
    r0 = pl.multiple_of(n * QB, QB)
    ws = pl.multiple_of(jnp.clip((n - 1) * QB, 0, L - KW), QB)
    qpos = r0 + lax.broadcasted_iota(jnp.int32, (QB, KW), 0)
    kpos = ws + lax.broadcasted_iota(jnp.int32, (QB, KW), 1)
    valid = jnp.abs(kpos - qpos) <= SWA_WINDOW
    for hq in range(SWA_H):
        kh = hq // (SWA_H // SWA_KVH)
        hs = slice(kh * SWA_HD, (kh + 1) * SWA_HD)
        q = q_s[pl.ds(r0, QB), hq * SWA_HD:(hq + 1) * SWA_HD]
        s1 = jnp.where(valid, _mm_nt(q, k_s[pl.ds(ws, KW), hs]), NEG)
        s2 = _mm_nt(q, kc_ref[0, :, hs].astype(BF))
        sink = sink_ref[hq]
        m = jnp.maximum(jnp.maximum(jnp.max(s1, axis=-1, keepdims=True),
                                    jnp.max(s2, axis=-1, keepdims=True)), sink)
        e1 = jnp.exp(s1 - m)
        e2 = jnp.exp(s2 - m)
        den = (jnp.sum(e1, axis=-1, keepdims=True) + jnp.sum(e2, axis=-1, keepdims=True)
               + jnp.exp(sink - m))
        o = (_mm(e1.astype(BF), v_s[pl.ds(ws, KW), hs])
             + _mm(e2.astype(BF), vc_ref[0, :, hs].astype(BF)))
        o_s[:, hq * SWA_HD:(hq + 1) * SWA_HD] = (o * (1.0 / den)).astype(BF)
    o_ref[...] = x_ref[pl.ds(r0, QB), :] + _gate(mod_ref, 0) * _mm(o_s[...], wo_ref[...])


def swa_lat_mixer(x, mod, w, L, kc, vc, cos128, sin128):
    M = x.shape[0]
    B = M // L
    P = kc.shape[1]
    nb = L // SWA_BLOCK
    return pl.pallas_call(
        functools.partial(_swa_lat_kernel, L=L),
        grid=(B, nb),
        in_specs=[pl.BlockSpec(memory_space=pltpu.SMEM),
                  pl.BlockSpec((L, D), lambda b, n: (b, 0)),
                  pl.BlockSpec((1, 6, D), lambda b, n: (b, 0, 0)),
                  _const_spec(w["wqkv"]), _const_spec(w["wsw"]),
                  _const_spec(cos128), _const_spec(sin128),
                  pl.BlockSpec((1, P, SWA_NK), lambda b, n: (b, 0, 0)),
                  pl.BlockSpec((1, P, SWA_NK), lambda b, n: (b, 0, 0)),
                  _const_spec(w["wo"])],
        out_specs=pl.BlockSpec((SWA_BLOCK, D), lambda b, n: (b * nb + n, 0)),
        out_shape=jax.ShapeDtypeStruct((M, D), F32),
        scratch_shapes=[pltpu.VMEM((L, SWA_NQ), BF), pltpu.VMEM((L, SWA_NK), BF),
                        pltpu.VMEM((L, SWA_NK), BF), pltpu.VMEM((SWA_BLOCK, SWA_NQ), BF)],
        compiler_params=_cp("arbitrary", "arbitrary"),
        name="swa_lat",
    )(w["sink"], x, mod, w["wqkv"], w["wsw"], cos128, sin128, kc, vc, w["wo"])


def _conv_kernel(x_ref, mod_ref, win_ref, cw_ref, wout_ref, o_ref, y_s, *, L):
    x = x_ref[...]
    rows = x.shape[0]
    h = _modulate(x, mod_ref, 0).astype(BF)
    CC = 256
    row = lax.broadcasted_iota(jnp.int32, (rows, CC), 0) % L
    for c in range(D // CC):
        cols = slice(c * CC, (c + 1) * CC)
        bg = _mm(h, win_ref[:, c * CC:(c + 1) * CC])
        cg = _mm(h, win_ref[:, D + c * CC:D + (c + 1) * CC])
        u = _mm(h, win_ref[:, 2 * D + c * CC:2 * D + (c + 1) * CC])
        cu = cg * u
        prev = jnp.where(row == 0, 0.0, pltpu.roll(cu, 1, 0))
        nxt = jnp.where(row == L - 1, 0.0, pltpu.roll(cu, rows - 1, 0))
        conv = prev * cw_ref[0:1, cols] + cu * cw_ref[1:2, cols] + nxt * cw_ref[2:3, cols]
        y_s[:, cols] = (bg * conv).astype(BF)
    o_ref[...] = x + _gate(mod_ref, 0) * _mm(y_s[...], wout_ref[...])


def conv_mixer(x, mod, w, L):
    M = x.shape[0]
    shared_mod = mod.shape[0] == 1
    tm = min(M, 1024) if shared_mod else L
    row = pl.BlockSpec((tm, D), lambda i: (i, 0))
    return pl.pallas_call(
        functools.partial(_conv_kernel, L=L),
        grid=(M // tm,),
        in_specs=[row, _mod_spec(tm, M if shared_mod else L), _const_spec(w["win"]),
                  _const_spec(w["cw"]), _const_spec(w["wout"])],
        out_specs=row,
        out_shape=jax.ShapeDtypeStruct((M, D), F32),
        scratch_shapes=[pltpu.VMEM((tm, D), BF)],
        compiler_params=_cp("arbitrary"),
        name="conv_mix",
    )(x, mod, w["win"], w["cw"], w["wout"])


def _mla_ctx_kernel(x_ref, mod_ref, wdq_ref, qn_ref, wuq_ref, wdkv_ref, kvn_ref, wukv_ref, wo_ref,
                    o_ref, ckv_out, kr_out, o_s):
    x = x_ref[...]
    h = _modulate(x, mod_ref, 0).astype(BF)
    cq = (_rms(_mm(h, wdq_ref[...])) * qn_ref[...]).astype(BF)
    q = _mm(cq, wuq_ref[...]).astype(BF)
    kvc = _mm(h, wdkv_ref[...])
    ckv = _rms(kvc[:, :KV_LORA]) * kvn_ref[...]
    kr = kvc[:, KV_LORA:]
    ckv_out[...] = ckv
    kr_out[...] = kr
    kv = _mm(ckv.astype(BF), wukv_ref[...]).astype(BF)
    krb = kr.astype(BF)
    nn = MLA_H * QK_NOPE
    for hh in range(MLA_H):
        qh = q[:, hh * MLA_QW:(hh + 1) * MLA_QW]
        kh = jnp.concatenate([kv[:, hh * QK_NOPE:(hh + 1) * QK_NOPE], krb], axis=1)
        s = _mm_nt(qh, kh) * MLA_SCALE
        m = jnp.max(s, axis=-1, keepdims=True)
        e = jnp.exp(s - m)
        inv = 1.0 / jnp.sum(e, axis=-1, keepdims=True)
        vh = kv[:, nn + hh * V_HEAD:nn + (hh + 1) * V_HEAD]
        o_s[:, hh * V_HEAD:(hh + 1) * V_HEAD] = (_mm(e.astype(BF), vh) * inv).astype(BF)
    o_ref[...] = x + _gate(mod_ref, 0) * _mm(o_s[...], wo_ref[...])


def mla_ctx_mixer(x, mod, w, L):
    M = x.shape[0]
    row = lambda n: pl.BlockSpec((L, n), lambda i: (i, 0))
    ws = [w["wdq"], w["qnorm"], w["wuq"], w["wdkv"], w["kvnorm"], w["wukv"], w["wo"]]
    return pl.pallas_call(
        _mla_ctx_kernel,
        grid=(M // L,),
        in_specs=[row(D), _mod_spec(L, M)] + [_const_spec(a) for a in ws],
        out_specs=[row(D), row(KV_LORA), row(128)],
        out_shape=[jax.ShapeDtypeStruct((M, D), F32), jax.ShapeDtypeStruct((M, KV_LORA), F32),
                   jax.ShapeDtypeStruct((M, 128), F32)],
        scratch_shapes=[pltpu.VMEM((L, MLA_H * V_HEAD), BF)],
        compiler_params=_cp("arbitrary"),
        name="mla_ctx",
    )(x, mod, *ws)


def _mla_lat_kernel(x_ref, mod_ref, wdq_ref, qn_ref, wuq_ref, wuqsw_ref, wdkv_ref, wdkvsw_ref,
                    kvn_ref, wukv_ref, cq_ref, sq_ref, ck_ref, sk_ref, ckvc_ref, krc_ref, wo_ref,
                    o_ref, q_s, kn_s, v_s, kr_s, knc_s, vc_s, o_s, *, L, QB):
    n = pl.program_id(1)
    RC = 256
    nn = MLA_H * QK_NOPE

    @pl.when(n == 0)
    def _():
        for c in range(L // RC):
            rows = slice(c * RC, (c + 1) * RC)
            h = _modulate(x_ref[rows, :], mod_ref, 0).astype(BF)
            cq = (_rms(_mm(h, wdq_ref[...])) * qn_ref[...]).astype(BF)
            cosq = jnp.concatenate([cq_ref[rows, :]] * MLA_H, axis=1)
            sinq = jnp.concatenate([sq_ref[rows, :]] * MLA_H, axis=1)
            q = _mm(cq, wuq_ref[...]) * cosq + _mm(cq, wuqsw_ref[...]) * sinq
            q_s[rows, :] = q.astype(BF)
            kvc = _mm(h, wdkv_ref[...])
            ksw = _mm(h, wdkvsw_ref[...])
            ckv = _rms(kvc[:, :KV_LORA]) * kvn_ref[...]
            kr_s[rows, :] = (kvc[:, KV_LORA:] * ck_ref[rows, :] + ksw * sk_ref[rows, :]).astype(BF)
            kv = _mm(ckv.astype(BF), wukv_ref[...])
            kn_s[rows, :] = kv[:, :nn].astype(BF)
            v_s[rows, :] = kv[:, nn:].astype(BF)
        kvp = _mm(ckvc_ref[0].astype(BF), wukv_ref[...])
        knc_s[...] = kvp[:, :nn].astype(BF)
        vc_s[...] = kvp[:, nn:].astype(BF)

    r0 = pl.multiple_of(n * QB, QB)
    krc = krc_ref[0].astype(BF)
    for hh in range(MLA_H):
        ns = slice(hh * QK_NOPE, (hh + 1) * QK_NOPE)
        qh = q_s[pl.ds(r0, QB), hh * MLA_QW:(hh + 1) * MLA_QW]
        k1 = jnp.concatenate([kn_s[:, ns], kr_s[...]], axis=1)
        k2 = jnp.concatenate([knc_s[:, ns], krc], axis=1)
        s1 = _mm_nt(qh, k1) * MLA_SCALE
        s2 = _mm_nt(qh, k2) * MLA_SCALE
        m = jnp.maximum(jnp.max(s1, axis=-1, keepdims=True), jnp.max(s2, axis=-1, keepdims=True))
        e1 = jnp.exp(s1 - m)
        e2 = jnp.exp(s2 - m)
        den = jnp.sum(e1, axis=-1, keepdims=True) + jnp.sum(e2, axis=-1, keepdims=True)
        vs = slice(hh * V_HEAD, (hh + 1) * V_HEAD)
        o = _mm(e1.astype(BF), v_s[:, vs]) + _mm(e2.astype(BF), vc_s[:, vs])
        o_s[:, vs] = (o * (1.0 / den)).astype(BF)
    o_ref[...] = x_ref[pl.ds(r0, QB), :] + _gate(mod_ref, 0) * _mm(o_s[...], wo_ref[...])


def mla_lat_mixer(x, mod, w, L, ckv_c, kr_c128, tabs):
    M = x.shape[0]
    B = M // L
    P = ckv_c.shape[1]
    QB = 256
    nb = L // QB
    ws1 = [w["wdq"], w["qnorm"], w["wuq"], w["wuqsw"], w["wdkv"], w["wdkvsw"], w["kvnorm"], w["wukv"]]
    nv = MLA_H * V_HEAD
    return pl.pallas_call(
        functools.partial(_mla_lat_kernel, L=L, QB=QB),
        grid=(B, nb),
        in_specs=[pl.BlockSpec((L, D), lambda b, n: (b, 0)),
                  pl.BlockSpec((1, 6, D), lambda b, n: (b, 0, 0))]
                 + [_const_spec(a) for a in ws1] + [_const_spec(a) for a in tabs]
                 + [pl.BlockSpec((1, P, KV_LORA), lambda b, n: (b, 0, 0)),
                    pl.BlockSpec((1, P, 128), lambda b, n: (b, 0, 0)),
                    _const_spec(w["wo"])],
        out_specs=pl.BlockSpec((QB, D), lambda b, n: (b * nb + n, 0)),
        out_shape=jax.ShapeDtypeStruct((M, D), F32),
        scratch_shapes=[pltpu.VMEM((L, MLA_H * MLA_QW), BF), pltpu.VMEM((L, nv), BF),
                        pltpu.VMEM((L, nv), BF), pltpu.VMEM((L, 128), BF),
                        pltpu.VMEM((P, nv), BF), pltpu.VMEM((P, nv), BF), pltpu.VMEM((QB, nv), BF)],
        compiler_params=_cp("arbitrary", "arbitrary"),
        name="mla_lat",
    )(x, mod, *ws1, *tabs, ckv_c, kr_c128, w["wo"])


def _swiglu_partial(xb, wg_ref, wu_ref, wd_ref):
    hg = _mm(xb, wg_ref[...].astype(BF))
    hu = _mm(xb, wu_ref[...].astype(BF))
    a = (_silu(hg) * hu).astype(BF)
    return _mm(a, wd_ref[...].astype(BF))


def _dense_ffn_kernel(x_ref, mod_ref, wga_ref, wgb_ref, wua_ref, wub_ref, wda_ref, wdb_ref, o_ref, xm_s,
                      *, odd_blocks):
    f = pl.program_id(1)
    last = pl.num_programs(1) - 1

    @pl.when(f == 0)
    def _():
        xm_s[...] = _modulate(x_ref[...], mod_ref, 1).astype(BF)
        o_ref[...] = jnp.zeros_like(o_ref)

    def pair():
        xb = xm_s[...]
        acts = []
        for wg_ref, wu_ref in ((wga_ref, wua_ref), (wgb_ref, wub_ref)):
            hg = _mm(xb, wg_ref[...].astype(BF))
            hu = _mm(xb, wu_ref[...].astype(BF))
            acts.append((_silu(hg) * hu).astype(BF))
        wd = jnp.concatenate([wda_ref[...].astype(BF), wdb_ref[...].astype(BF)], axis=0)
        return _mm(jnp.concatenate(acts, axis=1), wd)

    if odd_blocks:
        @pl.when(f < last)
        def _():
            o_ref[...] += pair()

        @pl.when(f == last)
        def _():
            o_ref[...] += _swiglu_partial(xm_s[...], wga_ref, wua_ref, wda_ref)
    else:
        o_ref[...] += pair()

    @pl.when(f == last)
    def _():
        o_ref[...] = x_ref[...] + _gate(mod_ref, 1) * o_ref[...]


def dense_ffn(x, mod, wg, wu, wd, j, L):
    M = x.shape[0]
    F = wg.shape[-1]
    tm, tf = DENSE_TM, DENSE_TF
    nfb = F // tf
    blk_a = lambda f: 2 * f
    blk_b = lambda f: jnp.minimum(2 * f + 1, nfb - 1)
    col = lambda blk: pl.BlockSpec((None, D, tf), lambda i, f: (j, 0, blk(f)))
    row = lambda blk: pl.BlockSpec((None, tf, D), lambda i, f: (j, blk(f), 0))
    return pl.pallas_call(
        functools.partial(_dense_ffn_kernel, odd_blocks=nfb % 2 == 1),
        grid=(M // tm, (nfb + 1) // 2),
        in_specs=[pl.BlockSpec((tm, D), lambda i, f: (i, 0)),
                  pl.BlockSpec((1, 6, D), lambda i, f: ((i * tm) // (L if mod.shape[0] > 1 else M), 0, 0)),
                  col(blk_a), col(blk_b), col(blk_a), col(blk_b), row(blk_a), row(blk_b)],
        out_specs=pl.BlockSpec((tm, D), lambda i, f: (i, 0)),
        out_shape=jax.ShapeDtypeStruct((M, D), F32),
        scratch_shapes=[pltpu.VMEM((tm, D), BF)],
        compiler_params=_cp("arbitrary", "arbitrary"),
        name="dense_ffn",
    )(x, mod, wg, wg, wu, wu, wd, wd)


def _router_kernel(xp_ref, xs_ref, mod_ref, wr_ref, xm_ref, idx_ref, w_ref, cnt_ref, cnt_s, *, np_tiles):
    i = pl.program_id(0)
    tm = xm_ref.shape[0]

    @pl.when(i == 0)
    def _():
        cnt_s[...] = jnp.zeros_like(cnt_s)

    x = jnp.where(i < np_tiles, xp_ref[...], xs_ref[...])
    xmod = _modulate(x, mod_ref, 1)
    xm_ref[...] = xmod
    xm = xmod.astype(BF)
    lane = lax.broadcasted_iota(jnp.int32, idx_ref.shape, 1)
    logits = jnp.where(lane < N_EXPERTS, _mm(xm, wr_ref[...]), -jnp.inf)
    m1 = jnp.max(logits, axis=-1, keepdims=True)
    i1 = jnp.min(jnp.where(logits == m1, lane, 128), axis=-1, keepdims=True)
    rest = jnp.where(lane == i1, -jnp.inf, logits)
    m2 = jnp.max(rest, axis=-1, keepdims=True)
    i2 = jnp.min(jnp.where(rest == m2, lane, 128), axis=-1, keepdims=True)
    e = jnp.exp(m2 - m1)
    w_ref[...] = jnp.where(lane == 0, 1.0 / (1.0 + e), e / (1.0 + e))
    oh1 = jnp.where(lane == i1, 1.0, 0.0)
    oh2 = jnp.where(lane == i2, 1.0, 0.0)
    rr = lax.broadcasted_iota(jnp.int32, (tm, tm), 0)
    cc = lax.broadcasted_iota(jnp.int32, (tm, tm), 1)
    below = jnp.where(cc < rr, 1.0, 0.0).astype(BF)
    run = cnt_s[0:1, :]
    tot1 = jnp.sum(oh1, axis=0, keepdims=True)
    tot2 = jnp.sum(oh2, axis=0, keepdims=True)
    r1 = jnp.sum(oh1 * (run + _mm(below, oh1.astype(BF))), axis=-1, keepdims=True)
    r2 = jnp.sum(oh2 * (run + tot1 + _mm(below, oh2.astype(BF))), axis=-1, keepdims=True)
    idx_ref[...] = jnp.where(lane == 0, i1, jnp.where(lane == 1, i2, jnp.where(
        lane == 2, r1.astype(jnp.int32), r2.astype(jnp.int32))))
    total = run + tot1 + tot2
    cnt_s[...] = jnp.broadcast_to(total, cnt_s.shape)
    cnt_ref[...] = jnp.broadcast_to(total, cnt_ref.shape)


def route(xp, xs, mod3, wr, Ls):
    Mp, Ms = xp.shape[0], xs.shape[0]
    T = Mp + Ms
    tm = ROUTE_TM
    npt = Mp // tm
    row = lambda n: pl.BlockSpec((tm, n), lambda i: (i, 0))
    return pl.pallas_call(
        functools.partial(_router_kernel, np_tiles=npt),
        grid=(T // tm,),
        in_specs=[pl.BlockSpec((tm, D), lambda i: (jnp.minimum(i, npt - 1), 0)),
                  pl.BlockSpec((tm, D), lambda i: (jnp.maximum(i - npt, 0), 0)),
                  pl.BlockSpec((1, 6, D), lambda i: (jnp.where(i < npt, 0, 1 + ((i - npt) * tm) // Ls), 0, 0)),
                  _const_spec(wr)],
        out_specs=[row(D), row(128), row(128), pl.BlockSpec((8, 128), lambda i: (0, 0))],
        out_shape=[jax.ShapeDtypeStruct((T, D), F32), jax.ShapeDtypeStruct((T, 128), jnp.int32),
                   jax.ShapeDtypeStruct((T, 128), F32), jax.ShapeDtypeStruct((8, 128), F32)],
        scratch_shapes=[pltpu.VMEM((8, 128), F32)],
        compiler_params=_cp("arbitrary"),
        name="moe_route",
    )(xp, xs, mod3, wr)


def _moe_ffn_kernel(vb_ref, ve_ref, vm_ref, nv_ref, xa_ref, xb_ref, wg_ref, wu_ref, wd_ref, o_ref, x_s,
                    *, half_blocks):
    v = pl.program_id(0)
    f = pl.program_id(1)
    mode = vm_ref[v]
    H = MOE_TM

    @pl.when(jnp.logical_and(v < nv_ref[0], f == 0))
    def _():
        x_s[...] = jnp.where(vb_ref[v] < half_blocks, xa_ref[...], xb_ref[...]).astype(BF)

    def run(rows):
        @pl.when(f == 0)
        def _():
            o_ref[rows, :] = jnp.zeros((rows.stop - rows.start, D), F32)

        o_ref[rows, :] += _swiglu_partial(x_s[rows, :], wg_ref, wu_ref, wd_ref)

    @pl.when(mode == MOE_FULL)
    def _():
        run(slice(0, 2 * H))

    @pl.when(jnp.logical_or(mode == MOE_LO, mode == MOE_LO_ZERO_HI))
    def _():
        run(slice(0, H))

    @pl.when(mode == MOE_HI)
    def _():
        run(slice(H, 2 * H))

    @pl.when(jnp.logical_and(mode == MOE_LO_ZERO_HI, f == 0))
    def _():
        o_ref[H:2 * H, :] = jnp.zeros((H, D), F32)

    @pl.when(jnp.logical_and(mode == MOE_ZERO, f == 0))
    def _():
        o_ref[...] = jnp.zeros_like(o_ref)


def moe_ffn(xa, xb, tile_expert, n_valid, wg, wu, wd, j):
    R = 2 * xa.shape[0]
    F = wg.shape[-1]
    bm, tf = 2 * MOE_TM, MOE_TF
    nf = F // tf
    nb = R // bm
    nhb = nb // 2
    n_vis = nb + N_EXPERTS - 1
    blocks = jnp.arange(nb, dtype=jnp.int32)
    t0, t1 = tile_expert[0::2], tile_expert[1::2]
    valid0 = 2 * blocks < n_valid[0]
    valid1 = 2 * blocks + 1 < n_valid[0]
    two = jnp.logical_and(valid1, t0 != t1)
    cnt = jnp.where(valid0, 1 + two.astype(jnp.int32), 0)
    cum = jnp.cumsum(cnt)
    nv = cum[-1:].astype(jnp.int32)
    v_eff = jnp.minimum(jnp.arange(n_vis, dtype=jnp.int32), nv[0] - 1)
    vb = jnp.sum((v_eff[:, None] >= cum[None, :]).astype(jnp.int32), axis=1)
    pick = lambda a: jnp.sum(jnp.where(vb[:, None] == blocks[None, :], a[None, :].astype(jnp.int32), 0), axis=1)
    k = v_eff - pick(cum - cnt)
    two_v, t0_v, t1_v, valid1_v = pick(two), pick(t0), pick(t1), pick(valid1)
    vm = jnp.where(two_v == 1, jnp.where(k == 0, MOE_LO, MOE_HI),
                   jnp.where(valid1_v == 1, MOE_FULL, MOE_LO_ZERO_HI))
    ve = jnp.where(jnp.logical_and(two_v == 1, k == 1), t1_v, t0_v).astype(jnp.int32)
    spare = jnp.arange(n_vis, dtype=jnp.int32) - nv[0]
    n_used = jnp.sum(valid0.astype(jnp.int32))
    fill = jnp.logical_and(spare >= 0, n_used + spare < nb)
    idle = jnp.logical_and(spare >= 0, jnp.logical_not(fill))
    vb = jnp.where(fill, n_used + spare, jnp.where(jnp.logical_and(idle, n_used < nb), nb - 1, vb))
    vm = jnp.where(fill, MOE_ZERO, jnp.where(idle, MOE_IDLE, vm)).astype(jnp.int32)

    def fidx(v, f, nv):
        return jnp.where(v < nv[0], f, nf - 1)

    grid_spec = pltpu.PrefetchScalarGridSpec(
        num_scalar_prefetch=4,
        grid=(n_vis, nf),
        in_specs=[pl.BlockSpec((bm, D), lambda v, f, vb, ve, vm, nv: (jnp.minimum(vb[v], nhb - 1), 0)),
                  pl.BlockSpec((bm, D), lambda v, f, vb, ve, vm, nv: (jnp.maximum(vb[v] - nhb, 0), 0)),
                  pl.BlockSpec((None, None, D, tf), lambda v, f, vb, ve, vm, nv: (j, ve[v], 0, fidx(v, f, nv))),
                  pl.BlockSpec((None, None, D, tf), lambda v, f, vb, ve, vm, nv: (j, ve[v], 0, fidx(v, f, nv))),
                  pl.BlockSpec((None, None, tf, D), lambda v, f, vb, ve, vm, nv: (j, ve[v], fidx(v, f, nv), 0))],
        out_specs=pl.BlockSpec((bm, D), lambda v, f, vb, ve, vm, nv: (vb[v], 0)),
        scratch_shapes=[pltpu.VMEM((bm, D), BF)],
    )
    return pl.pallas_call(
        functools.partial(_moe_ffn_kernel, half_blocks=nhb),
        grid_spec=grid_spec,
        out_shape=jax.ShapeDtypeStruct((R, D), F32),
        compiler_params=_cp("arbitrary", "arbitrary"),
        name="moe_ffn",
    )(vb.astype(jnp.int32), ve, vm, nv, xa, xb, wg, wu, wd)


def _combine_kernel(x_ref, mod_ref, ya_ref, yb_ref, w_ref, fn_ref, o_ref, *, final):
    y = w_ref[:, 0:1] * ya_ref[...] + w_ref[:, 1:2] * yb_ref[...]
    o = x_ref[...] + _gate(mod_ref, 1) * y
    if final:
        o = _rms(o) * fn_ref[...]
    o_ref[...] = o


def moe_combine(x, mod, ya, yb, w, fn, L, final, row_off):
    M = x.shape[0]
    tm = 512
    off = row_off // tm
    row = pl.BlockSpec((tm, D), lambda i: (i, 0))
    row_o = lambda n: pl.BlockSpec((tm, n), lambda i: (i + off, 0))
    return pl.pallas_call(
        functools.partial(_combine_kernel, final=final),
        grid=(M // tm,),
        in_specs=[row, _mod_spec(tm, L if mod.shape[0] > 1 else M), row_o(D), row_o(D), row_o(128),
                  _const_spec(fn)],
        out_specs=row,
        out_shape=jax.ShapeDtypeStruct((M, D), F32),
        compiler_params=_cp("arbitrary"),
        name="moe_combine",
    )(x, mod, ya, yb, w, fn)


def moe_layer(xp, xs, mod3, wr, wg, wu, wd, j, fn, Lp, Ls, final):
    Mp = xp.shape[0]
    xm, idx, wts, cnt = route(xp, xs, mod3, wr, Ls)
    T = xm.shape[0]
    tm = MOE_TM
    R = 2 * T + N_EXPERTS * tm
    counts = cnt[0, :N_EXPERTS].astype(jnp.int32)
    tiles_per = (counts + tm - 1) // tm
    tile_end = jnp.cumsum(tiles_per)
    offs = (tile_end - tiles_per) * tm
    experts = jnp.arange(N_EXPERTS, dtype=jnp.int32)
    dest = jnp.sum(jnp.where(idx[:, 0:2, None] == experts, offs, 0), axis=-1) + idx[:, 2:4]
    tok = jnp.arange(2 * T, dtype=jnp.int32) // 2
    src = (jnp.arange(R, dtype=jnp.int32) % T).at[dest.reshape(-1)].set(
        tok, unique_indices=True, mode="promise_in_bounds")
    n_tiles = R // tm
    n_valid = tile_end[-1:].astype(jnp.int32)
    t_ids = jnp.arange(n_tiles, dtype=jnp.int32)
    te = jnp.sum((t_ids[:, None] >= tile_end[None, :]).astype(jnp.int32), axis=1)
    te_last = jnp.sum((n_valid - 1 >= tile_end).astype(jnp.int32))
    te = jnp.where(t_ids < n_valid[0], te, te_last).astype(jnp.int32)
    xa = xm.at[src[:R // 2]].get(mode="promise_in_bounds")
    xb = xm.at[src[R // 2:]].get(mode="promise_in_bounds")
    y_sorted = moe_ffn(xa, xb, te, n_valid, wg, wu, wd, j)
    ya = y_sorted.at[dest[:, 0]].get(mode="promise_in_bounds")
    yb = y_sorted.at[dest[:, 1]].get(mode="promise_in_bounds")
    xp_new = moe_combine(xp, mod3[0:1], ya, yb, wts, fn, Lp, final, 0)
    xs_new = moe_combine(xs, mod3[1:], ya, yb, wts, fn, Ls, final, Mp)
    return xp_new, xs_new


def _rope_partner_perm(n):
    c = np.arange(n)
    return np.where((c % 32) < 16, c + 16, c - 16)


def _rope_tables(L):
    nf = 16
    inv = ROPE_THETA ** (-np.arange(nf, dtype=np.float32) / nf)
    pos = np.arange(L)
    ang = np.stack([(pos // GRID_W).astype(np.float32), (pos % GRID_W).astype(np.float32)],
                   axis=-1)[:, :, None] * inv
    cos = jnp.cos(jnp.asarray(ang, F32))
    sin = jnp.sin(jnp.asarray(ang, F32))
    c64 = jnp.stack([cos, cos], axis=2).reshape(L, 64)
    s64 = jnp.stack([-sin, sin], axis=2).reshape(L, 64)
    return c64, s64


def kernel(x_prompt, x_sample, state_l0_gla, cache_l1_k, cache_l1_v, cache_l3_ckv, cache_l3_krope, c, c_ctx, w_mod, b_mod, gla_wq, gla_wk, gla_wv, gla_wg, gla_wgk1, gla_wgk2, gla_bgk, gla_norm, gla_wo, swa_wqkv, swa_sink, swa_wo, conv_win, conv_w, conv_wout, mla_wdq, mla_qnorm, mla_wuq, mla_wdkv, mla_kvnorm, mla_wukv, mla_wo, dense_w_gate, dense_w_up, dense_w_down, moe_router, moe_w_gate, moe_w_up, moe_w_down, final_norm):
    Bp, Lp, _ = x_prompt.shape
    Bs, Ls, _ = x_sample.shape
    P = cache_l1_k.shape[1]
    xp = x_prompt.reshape(Bp * Lp, D)
    xs = x_sample.reshape(Bs * Ls, D)

    cvec = jnp.zeros((8, D), F32).at[0].set(c_ctx).at[1:1 + Bs].set(c)
    mod_all = adaln_all(cvec, w_mod, b_mod).reshape(w_mod.shape[0], 8, 6, D)

    wlr = jnp.zeros((D, 128), F32).at[:, :GLA_RANK].set(gla_wgk1[0]).at[:, GLA_RANK:2 * GLA_RANK].set(gla_wgk1[1])
    wgk2 = jnp.zeros((2, 128, GLA_H * GLA_DK), F32)
    wgk2 = wgk2.at[0, :GLA_RANK].set(gla_wgk2[0]).at[1, GLA_RANK:2 * GLA_RANK].set(gla_wgk2[1])
    gla_w = dict(wp=jnp.concatenate([gla_wq, gla_wk, gla_wv, gla_wg], axis=1).astype(BF),
                 wlr=wlr.astype(BF), wgk2=wgk2.astype(BF), bgk=gla_bgk,
                 gnorm=gla_norm.reshape(1, GLA_DV), wo=gla_wo.astype(BF))
    modp, mods = mod_all[0, 0:1], mod_all[0, 1:1 + Bs]
    xp, new_state = gla_mixer(xp, modp, gla_w, Lp, None)
    xs, _ = gla_mixer(xs, mods, gla_w, Ls, state_l0_gla)
    xp = dense_ffn(xp, modp, dense_w_gate, dense_w_up, dense_w_down, 0, Lp)
    xs = dense_ffn(xs, mods, dense_w_gate, dense_w_up, dense_w_down, 0, Ls)

    c64, s64 = _rope_tables(Ls)
    cos128 = jnp.concatenate([c64, c64], axis=1)
    sin128 = jnp.concatenate([s64, s64], axis=1)
    perm = _rope_partner_perm(SWA_NQ + SWA_NK)
    swa_w = dict(wqkv=swa_wqkv.astype(BF), wsw=swa_wqkv[:, perm].astype(BF), wo=swa_wo.astype(BF),
                 sink=swa_sink)
    modp, mods = mod_all[1, 0:1], mod_all[1, 1:1 + Bs]
    xp, k1, v1 = swa_ctx_mixer(xp, modp, swa_w, Lp)
    xs = swa_lat_mixer(xs, mods, swa_w, Ls, cache_l1_k.reshape(Bs, P, SWA_NK),
                       cache_l1_v.reshape(Bs, P, SWA_NK), cos128, sin128)
    wr = jnp.zeros((moe_router.shape[0], D, 128), F32).at[:, :, :N_EXPERTS].set(moe_router).astype(BF)
    xp, xs = moe_layer(xp, xs, mod_all[1, 0:1 + Bs], wr[0], moe_w_gate, moe_w_up, moe_w_down, 0,
                       final_norm.reshape(1, D), Lp, Ls, False)

    conv_wts = dict(win=conv_win.astype(BF), cw=conv_w, wout=conv_wout.astype(BF))
    modp, mods = mod_all[2, 0:1], mod_all[2, 1:1 + Bs]
    xp = conv_mixer(xp, modp, conv_wts, Lp)
    xs = conv_mixer(xs, mods, conv_wts, Ls)
    xp = dense_ffn(xp, modp, dense_w_gate, dense_w_up, dense_w_down, 1, Lp)
    xs = dense_ffn(xs, mods, dense_w_gate, dense_w_up, dense_w_down, 1, Ls)

    wuq3 = mla_wuq.reshape(Q_LORA, MLA_H, QK_NOPE + QK_ROPE)
    zpad = jnp.zeros((Q_LORA, MLA_H, MLA_QW - QK_NOPE - QK_ROPE), F32)
    wuq = jnp.concatenate([wuq3, zpad], axis=2).reshape(Q_LORA, MLA_H * MLA_QW)
    rperm = _rope_partner_perm(QK_ROPE)
    wuq_sw = jnp.concatenate([jnp.zeros((Q_LORA, MLA_H, QK_NOPE), F32), wuq3[:, :, QK_NOPE:][:, :, rperm], zpad],
                             axis=2).reshape(Q_LORA, MLA_H * MLA_QW)
    kpad = jnp.zeros((D, 128 - QK_ROPE), F32)
    wdkv = jnp.concatenate([mla_wdkv, kpad], axis=1)
    wdkv_sw = jnp.concatenate([mla_wdkv[:, KV_LORA:][:, rperm], kpad], axis=1)
    wukv3 = mla_wukv.reshape(KV_LORA, MLA_H, QK_NOPE + V_HEAD)
    wukv = jnp.concatenate([wukv3[:, :, :QK_NOPE].reshape(KV_LORA, MLA_H * QK_NOPE),
                            wukv3[:, :, QK_NOPE:].reshape(KV_LORA, MLA_H * V_HEAD)], axis=1)
    mla_w = dict(wdq=mla_wdq.astype(BF), qnorm=mla_qnorm.reshape(1, Q_LORA), wuq=wuq.astype(BF),
                 wuqsw=wuq_sw.astype(BF), wdkv=wdkv.astype(BF), wdkvsw=wdkv_sw.astype(BF),
                 kvnorm=mla_kvnorm.reshape(1, KV_LORA), wukv=wukv.astype(BF), wo=mla_wo.astype(BF))
    ones = jnp.ones((Ls, QK_NOPE), F32)
    z128 = jnp.zeros((Ls, QK_NOPE), F32)
    z64 = jnp.zeros((Ls, 64), F32)
    tabs = [jnp.concatenate([ones, c64, z64], axis=1), jnp.concatenate([z128, s64, z64], axis=1),
            jnp.concatenate([c64, z64], axis=1), jnp.concatenate([s64, z64], axis=1)]
    modp, mods = mod_all[3, 0:1], mod_all[3, 1:1 + Bs]
    xp, ckv3, kr3 = mla_ctx_mixer(xp, modp, mla_w, Lp)
    kr_c128 = jnp.concatenate([cache_l3_krope, jnp.zeros((Bs, P, 128 - QK_ROPE), F32)], axis=2)
    xs = mla_lat_mixer(xs, mods, mla_w, Ls, cache_l3_ckv, kr_c128, tabs)
    xp, xs = moe_layer(xp, xs, mod_all[3, 0:1 + Bs], wr[1], moe_w_gate, moe_w_up, moe_w_down, 1,
                       final_norm.reshape(1, D), Lp, Ls, True)

    return (xp.reshape(Bp, Lp, D), xs.reshape(Bs, Ls, D), new_state,
            k1.reshape(Bp, Lp, SWA_KVH, SWA_HD), v1.reshape(Bp, Lp, SWA_KVH, SWA_HD),
            ckv3.reshape(Bp, Lp, KV_LORA), kr3[:, :QK_ROPE].reshape(Bp, Lp, QK_ROPE))
```

```python
import functools

import numpy as np
import jax
import jax.numpy as jnp
from jax import lax
from jax.experimental import pallas as pl
from jax.experimental.pallas import tpu as pltpu

BF = jnp.bfloat16
F32 = jnp.float32

D = 1024
EPS = 1e-6
NEG = -1e30
ROPE_THETA = 10000.0
GRID_W = 64

GLA_H, GLA_DK, GLA_DV, GLA_RANK, GLA_CHUNK = 4, 128, 256, 16, 64
GLA_SCALE = GLA_DK ** -0.5
GLA_INV_NORMALIZER = 1.0 / 16.0

SWA_H, SWA_KVH, SWA_HD, SWA_WINDOW, SWA_BLOCK = 16, 4, 64, 128, 128
SWA_SCALE = SWA_HD ** -0.5
SWA_NQ = SWA_H * SWA_HD
SWA_NK = SWA_KVH * SWA_HD

MLA_H, Q_LORA, KV_LORA, QK_NOPE, QK_ROPE, V_HEAD = 8, 384, 256, 128, 64, 128
MLA_SCALE = (QK_NOPE + QK_ROPE) ** -0.5
MLA_QW = 256

N_EXPERTS = 8
MOE_TM = 512
MOE_TF = 512
MOE_FULL, MOE_LO, MOE_HI, MOE_LO_ZERO_HI, MOE_ZERO, MOE_IDLE = range(6)
ROUTE_TM = 1024
DENSE_TM = 1024
DENSE_TF = 256

LANES = 128
SUBLANES = 8
V7X_VMEM_BYTES = 64 * 1024 * 1024
VMEM_LIMIT = V7X_VMEM_BYTES * 7 // 8
ROW_TM = 512


def _cp(*sem):
    return pltpu.CompilerParams(dimension_semantics=sem, vmem_limit_bytes=VMEM_LIMIT)


def _mm(a, b):
    return jnp.dot(a, b, preferred_element_type=F32)


def _mm_nt(a, b):
    return lax.dot_general(a, b, (((1,), (1,)), ((), ())), preferred_element_type=F32)


def _mm_tn(a, b):
    return lax.dot_general(a, b, (((0,), (0,)), ((), ())), preferred_element_type=F32)


def _rms(x):
    return x * lax.rsqrt(jnp.mean(x * x, axis=-1, keepdims=True) + EPS)


def _silu(x):
    return x * (1.0 / (1.0 + jnp.exp(-x)))


def _modulate(x, mod_ref, j):
    shift = mod_ref[0, 3 * j:3 * j + 1, :]
    scale = mod_ref[0, 3 * j + 1:3 * j + 2, :]
    return _rms(x) * (1.0 + scale) + shift


def _gate(mod_ref, j):
    return mod_ref[0, 3 * j + 2:3 * j + 3, :]


def _const_spec(a):
    nd = a.ndim
    return pl.BlockSpec(a.shape, lambda *_: (0,) * nd)


def _mod_spec(tm, rows_per_mod):
    return pl.BlockSpec((1, 6, D), lambda i, *_: ((i * tm) // rows_per_mod, 0, 0))


def _adaln_kernel(c_ref, w_ref, b_ref, o_ref):
    s = _silu(c_ref[...]).astype(BF)
    o_ref[0] = _mm(s, w_ref[0].astype(BF)) + b_ref[0]


def adaln_all(cvec8, w_mod, b_mod):
    nl, d, n = w_mod.shape
    tn = 1536
    return pl.pallas_call(
        _adaln_kernel,
        grid=(nl, n // tn),
        in_specs=[pl.BlockSpec((8, d), lambda l, j: (0, 0)),
                  pl.BlockSpec((1, d, tn), lambda l, j: (l, 0, j)),
                  pl.BlockSpec((1, 1, tn), lambda l, j: (l, 0, j))],
        out_specs=pl.BlockSpec((1, 8, tn), lambda l, j: (l, 0, j)),
        out_shape=jax.ShapeDtypeStruct((nl, 8, n), F32),
        compiler_params=_cp("arbitrary", "arbitrary"),
        name="adaln",
    )(cvec8, w_mod, b_mod.reshape(nl, 1, n))


def _split3(x):
    hi = x.astype(BF)
    rem = x - hi.astype(F32)
    mid = rem.astype(BF)
    lo = (rem - mid.astype(F32)).astype(BF)
    return hi, mid, lo


def _mm3(a, pieces):
    return _mm(a, pieces[0]) + _mm(a, pieces[1]) + _mm(a, pieces[2])


def _gla_pre_kernel(x_ref, mod_ref, wp_ref, wlr_ref, wgk2_ref, bgk_ref,
                    qt_ref, kt_ref, kd_ref, v_ref, g_ref, dec_ref):
    tm = x_ref.shape[0]
    C = GLA_CHUNK
    G = 256
    h = _modulate(x_ref[...], mod_ref, 0).astype(BF)
    nk = GLA_H * GLA_DK
    nv = GLA_H * GLA_DV
    q = _mm(h, wp_ref[:, 0:nk]) * GLA_SCALE
    k = _mm(h, wp_ref[:, nk:2 * nk])
    v_ref[...] = _mm(h, wp_ref[:, 2 * nk:2 * nk + nv]).astype(BF)
    g_ref[...] = _mm(h, wp_ref[:, 2 * nk + nv:2 * nk + 2 * nv])
    lr = _mm(h, wlr_ref[...]).astype(BF)
    ii = lax.broadcasted_iota(jnp.int32, (G, G), 0)
    jj = lax.broadcasted_iota(jnp.int32, (G, G), 1)
    same = (ii // C) == (jj // C)
    ci = lax.broadcasted_iota(jnp.int32, (tm // C, tm), 0)
    cj = lax.broadcasted_iota(jnp.int32, (tm // C, tm), 1)
    sel = jnp.where((cj // C) == ci, 1.0, 0.0).astype(BF)
    for r in range(2):
        z = _mm(lr, wgk2_ref[r]) + bgk_ref[r:r + 1, :]
        gk = (jnp.minimum(z, 0.0) - jnp.log1p(jnp.exp(-jnp.abs(z)))) * GLA_INV_NORMALIZER
        pieces = _split3(gk)
        tri = jnp.where(jnp.logical_and(same, (jj <= ii) if r == 0 else (jj >= ii)), 1.0, 0.0).astype(BF)
        dec_ref[r] = jnp.exp(_mm3(sel, pieces))
        for gi in range(tm // G):
            rows = slice(gi * G, (gi + 1) * G)
            pg = tuple(p[rows] for p in pieces)
            b = _mm3(tri, pg)
            b3 = b.reshape(G // C, C, nk)
            edge = b3[:, C - 1:C, :] if r == 0 else b3[:, 0:1, :]
            tot = jnp.broadcast_to(edge, b3.shape).reshape(G, nk)
            qt_ref[r, rows, :] = (q[rows] * jnp.exp(b)).astype(BF)
            kt_ref[r, rows, :] = (k[rows] * jnp.exp(-b)).astype(BF)
            kd_ref[r, rows, :] = (k[rows] * jnp.exp(tot - b)).astype(BF)


def _gla_scan_kernel(*refs, L, S, has_h0, emit_state):
    qt_ref, kt_ref, kd_ref, v_ref, g_ref, dec_ref, gn_ref = refs[:7]
    pos = 7
    h0_ref = None
    if has_h0:
        h0_ref = refs[pos]
        pos += 1
    y_ref = refs[pos]
    pos += 1
    st_ref = None
    if emit_state:
        st_ref = refs[pos]
        pos += 1
    st_s, of_s, ob_s = refs[pos], refs[pos + 1], refs[pos + 2]

    C = GLA_CHUNK
    n_chunks = L // C
    ii = lax.broadcasted_iota(jnp.int32, (C, C), 0)
    jj = lax.broadcasted_iota(jnp.int32, (C, C), 1)
    keeps = ((jj <= ii), (jj >= ii))
    chains = [(sq, r, hd) for sq in range(S) for r in range(2) for hd in range(GLA_H)]
    slot = lambda sq, r, hd: (sq * 2 + r) * GLA_H + hd
    for sq, r, hd in chains:
        if has_h0:
            st_s[slot(sq, r, hd)] = h0_ref[sq, r, hd].T
        else:
            st_s[slot(sq, r, hd)] = jnp.zeros((GLA_DV, GLA_DK), F32)

    def step(i):
        for sq, r, hd in chains:
            n = i if r == 0 else n_chunks - 1 - i
            start = sq * L + n * C
            if not isinstance(start, int):
                start = pl.multiple_of(start, C)
            rows = pl.ds(start, C)
            o_s = of_s if r == 0 else ob_s
            kc = slice(hd * GLA_DK, (hd + 1) * GLA_DK)
            vc = slice(hd * GLA_DV, (hd + 1) * GLA_DV)
            qt = qt_ref[r, rows, kc]
            v = v_ref[rows, vc]
            att = jnp.where(keeps[r], _mm_nt(qt, kt_ref[r, rows, kc]), 0.0).astype(BF)
            st = st_s[slot(sq, r, hd)]
            o_s[rows, vc] = _mm_nt(qt, st.astype(BF)) + _mm(att, v)
            st_s[slot(sq, r, hd)] = st * dec_ref[r, sq, n, :, kc] + _mm_tn(v, kd_ref[r, rows, kc])

    if n_chunks <= 4:
        for i in range(n_chunks):
            step(i)
    else:
        def body(i, carry):
            step(i)
            return carry
        lax.fori_loop(0, n_chunks, body, 0)

    RC = 256
    for c in range(S * L // RC):
        rows = slice(c * RC, (c + 1) * RC)
        for hd in range(GLA_H):
            vc = slice(hd * GLA_DV, (hd + 1) * GLA_DV)
            y = _rms(of_s[rows, vc] + ob_s[rows, vc]) * gn_ref[...]
            y_ref[rows, vc] = (y * _silu(g_ref[rows, vc])).astype(BF)
    if emit_state:
        for sq, r, hd in chains:
            st_ref[sq, r, hd] = st_s[slot(sq, r, hd)].T


def _residual_out_kernel(y_ref, x_ref, mod_ref, w_ref, o_ref, *, gate_j):
    o_ref[...] = x_ref[...] + _gate(mod_ref, gate_j) * _mm(y_ref[...], w_ref[...])


def gla_mixer(x, mod, w, L, h0):
    M = x.shape[0]
    B = M // L
    tm = ROW_TM
    nk, nv = GLA_H * GLA_DK, GLA_H * GLA_DV
    row = lambda n: pl.BlockSpec((tm, n), lambda i: (i, 0))
    dirs = lambda: pl.BlockSpec((2, tm, nk), lambda i: (0, i, 0))
    n_chunks = L // GLA_CHUNK
    qt, kt, kd, v, g, dec = pl.pallas_call(
        _gla_pre_kernel,
        grid=(M // tm,),
        in_specs=[row(D), _mod_spec(tm, L if mod.shape[0] > 1 else M),
                  _const_spec(w["wp"]), _const_spec(w["wlr"]), _const_spec(w["wgk2"]),
                  _const_spec(w["bgk"])],
        out_specs=[dirs(), dirs(), dirs(), row(nv), row(nv),
                   pl.BlockSpec((2, tm // GLA_CHUNK, nk), lambda i: (0, i, 0))],
        out_shape=[jax.ShapeDtypeStruct((2, M, nk), BF), jax.ShapeDtypeStruct((2, M, nk), BF),
                   jax.ShapeDtypeStruct((2, M, nk), BF),
                   jax.ShapeDtypeStruct((M, nv), BF), jax.ShapeDtypeStruct((M, nv), F32),
                   jax.ShapeDtypeStruct((2, M // GLA_CHUNK, nk), F32)],
        compiler_params=_cp("arbitrary"),
        name="gla_pre",
    )(x, mod, w["wp"], w["wlr"], w["wgk2"], w["bgk"])
    dec = dec.reshape(2, B, n_chunks, 1, nk)

    has_h0 = h0 is not None
    emit_state = not has_h0
    S = max(s for s in (1, 2, 4) if B % s == 0 and s * L <= 1024)
    SL = S * L
    dir_spec = lambda: pl.BlockSpec((2, SL, nk), lambda b: (0, b, 0))
    in_specs = [dir_spec(), dir_spec(), dir_spec(),
                pl.BlockSpec((SL, nv), lambda b: (b, 0)),
                pl.BlockSpec((SL, nv), lambda b: (b, 0)),
                pl.BlockSpec((2, S, n_chunks, 1, nk), lambda b: (0, b, 0, 0, 0)),
                pl.BlockSpec((1, GLA_DV), lambda b: (0, 0))]
    args = [qt, kt, kd, v, g, dec, w["gnorm"]]
    st_spec = pl.BlockSpec((S, 2, GLA_H, GLA_DK, GLA_DV), lambda b: (b, 0, 0, 0, 0))
    if has_h0:
        in_specs.append(st_spec)
        args.append(h0)
    out_specs = [pl.BlockSpec((SL, nv), lambda b: (b, 0))]
    out_shape = [jax.ShapeDtypeStruct((M, nv), BF)]
    if emit_state:
        out_specs.append(st_spec)
        out_shape.append(jax.ShapeDtypeStruct((B, 2, GLA_H, GLA_DK, GLA_DV), F32))
    res = pl.pallas_call(
        functools.partial(_gla_scan_kernel, L=L, S=S, has_h0=has_h0, emit_state=emit_state),
        grid=(B // S,),
        in_specs=in_specs,
        out_specs=out_specs,
        out_shape=out_shape,
        scratch_shapes=[pltpu.VMEM((S * 2 * GLA_H, GLA_DV, GLA_DK), F32), pltpu.VMEM((SL, nv), F32),
                        pltpu.VMEM((SL, nv), F32)],
        compiler_params=_cp("arbitrary"),
        name="gla_scan",
    )(*args)
    y = res[0]
    state = res[1] if emit_state else None

    x_new = pl.pallas_call(
        functools.partial(_residual_out_kernel, gate_j=0),
        grid=(M // tm,),
        in_specs=[row(nv), row(D), _mod_spec(tm, L if mod.shape[0] > 1 else M), _const_spec(w["wo"])],
        out_specs=row(D),
        out_shape=jax.ShapeDtypeStruct((M, D), F32),
        compiler_params=_cp("arbitrary"),
        name="gla_post",
    )(y, x, mod, w["wo"])
    return x_new, state


def _softmax_sink_heads(q_of, k_of, v_of, sink_ref, o_s):
    for hq in range(SWA_H):
        kh = hq // (SWA_H // SWA_KVH)
        s = _mm_nt(q_of(hq), k_of(kh))
        sink = sink_ref[hq]
        m = jnp.maximum(jnp.max(s, axis=-1, keepdims=True), sink)
        e = jnp.exp(s - m)
        p = e / (jnp.sum(e, axis=-1, keepdims=True) + jnp.exp(sink - m))
        o_s[:, hq * SWA_HD:(hq + 1) * SWA_HD] = _mm(p.astype(BF), v_of(kh)).astype(BF)


def _swa_ctx_kernel(sink_ref, x_ref, mod_ref, wqkv_ref, wo_ref, o_ref, k_out, v_out, o_s):
    x = x_ref[...]
    h = _modulate(x, mod_ref, 0).astype(BF)
    qkv = _mm(h, wqkv_ref[...])
    k_out[...] = qkv[:, SWA_NQ:SWA_NQ + SWA_NK]
    v_out[...] = qkv[:, SWA_NQ + SWA_NK:]
    q_of = lambda hq: (qkv[:, hq * SWA_HD:(hq + 1) * SWA_HD] * SWA_SCALE).astype(BF)
    k_of = lambda kh: qkv[:, SWA_NQ + kh * SWA_HD:SWA_NQ + (kh + 1) * SWA_HD].astype(BF)
    v_of = lambda kh: qkv[:, SWA_NQ + SWA_NK + kh * SWA_HD:SWA_NQ + SWA_NK + (kh + 1) * SWA_HD].astype(BF)
    _softmax_sink_heads(q_of, k_of, v_of, sink_ref, o_s)
    o_ref[...] = x + _gate(mod_ref, 0) * _mm(o_s[...], wo_ref[...])


def swa_ctx_mixer(x, mod, w, L):
    M = x.shape[0]
    row = lambda n: pl.BlockSpec((L, n), lambda i: (i, 0))
    return pl.pallas_call(
        _swa_ctx_kernel,
        grid=(M // L,),
        in_specs=[pl.BlockSpec(memory_space=pltpu.SMEM), row(D), _mod_spec(L, M),
                  _const_spec(w["wqkv"]), _const_spec(w["wo"])],
        out_specs=[row(D), row(SWA_NK), row(SWA_NK)],
        out_shape=[jax.ShapeDtypeStruct((M, D), F32), jax.ShapeDtypeStruct((M, SWA_NK), F32),
                   jax.ShapeDtypeStruct((M, SWA_NK), F32)],
        scratch_shapes=[pltpu.VMEM((L, SWA_NQ), BF)],
        compiler_params=_cp("arbitrary"),
        name="swa_ctx",
    )(w["sink"], x, mod, w["wqkv"], w["wo"])


def _swa_lat_kernel(sink_ref, x_ref, mod_ref, wqkv_ref, wsw_ref, cos_ref, sin_ref, kc_ref, vc_ref,
                    wo_ref, o_ref, q_s, k_s, v_s, o_s, *, L):
    n = pl.program_id(1)
    RC = 256

    @pl.when(n == 0)
    def _():
        for c in range(L // RC):
            rows = slice(c * RC, (c + 1) * RC)
            h = _modulate(x_ref[rows, :], mod_ref, 0).astype(BF)
            qkv = _mm(h, wqkv_ref[...])
            sw = _mm(h, wsw_ref[...])
            cos = cos_ref[rows, :]
            sin = sin_ref[rows, :]
            cq = jnp.concatenate([cos] * (SWA_NQ // 128), axis=1)
            sq = jnp.concatenate([sin] * (SWA_NQ // 128), axis=1)
            ck = jnp.concatenate([cos] * (SWA_NK // 128), axis=1)
            sk = jnp.concatenate([sin] * (SWA_NK // 128), axis=1)
            q = (qkv[:, :SWA_NQ] * SWA_SCALE) * cq + (sw[:, :SWA_NQ] * SWA_SCALE) * sq
            q_s[rows, :] = q.astype(BF)
            k = qkv[:, SWA_NQ:SWA_NQ + SWA_NK] * ck + sw[:, SWA_NQ:SWA_NQ + SWA_NK] * sk
            k_s[rows, :] = k.astype(BF)
            v_s[rows, :] = qkv[:, SWA_NQ + SWA_NK:].astype(BF)

    QB = SWA_BLOCK
    KW = 3 * SWA_BLOCK
    r0 = pl.multiple_of(n * QB, QB)
    ws = pl.multiple_of(jnp.clip((n - 1) * QB, 0, L - KW), QB)
    qpos = r0 + lax.broadcasted_iota(jnp.int32, (QB, KW), 0)
    kpos = ws + lax.broadcasted_iota(jnp.int32, (QB, KW), 1)
    valid = jnp.abs(kpos - qpos) <= SWA_WINDOW
    for hq in range(SWA_H):
        kh = hq // (SWA_H // SWA_KVH)
        hs = slice(kh * SWA_HD, (kh + 1) * SWA_HD)
        q = q_s[pl.ds(r0, QB), hq * SWA_HD:(hq + 1) * SWA_HD]
        s1 = jnp.where(valid, _mm_nt(q, k_s[pl.ds(ws, KW), hs]), NEG)
        s2 = _mm_nt(q, kc_ref[0, :, hs].astype(BF))
        sink = sink_ref[hq]
        m = jnp.maximum(jnp.maximum(jnp.max(s1, axis=-1, keepdims=True),
                                    jnp.max(s2, axis=-1, keepdims=True)), sink)
        e1 = jnp.exp(s1 - m)
        e2 = jnp.exp(s2 - m)
        den = (jnp.sum(e1, axis=-1, keepdims=True) + jnp.sum(e2, axis=-1, keepdims=True)
               + jnp.exp(sink - m))
        o = (_mm(e1.astype(BF), v_s[pl.ds(ws, KW), hs])
             + _mm(e2.astype(BF), vc_ref[0, :, hs].astype(BF)))
        o_s[:, hq * SWA_HD:(hq + 1) * SWA_HD] = (o * (1.0 / den)).astype(BF)
    o_ref[...] = x_ref[pl.ds(r0, QB), :] + _gate(mod_ref, 0) * _mm(o_s[...], wo_ref[...])


def swa_lat_mixer(x, mod, w, L, kc, vc, cos128, sin128):
    M = x.shape[0]
    B = M // L
    P = kc.shape[1]
    nb = L // SWA_BLOCK
    return pl.pallas_call(
        functools.partial(_swa_lat_kernel, L=L),
        grid=(B, nb),
        in_specs=[pl.BlockSpec(memory_space=pltpu.SMEM),
                  pl.BlockSpec((L, D), lambda b, n: (b, 0)),
                  pl.BlockSpec((1, 6, D), lambda b, n: (b, 0, 0)),
                  _const_spec(w["wqkv"]), _const_spec(w["wsw"]),
                  _const_spec(cos128), _const_spec(sin128),
                  pl.BlockSpec((1, P, SWA_NK), lambda b, n: (b, 0, 0)),
                  pl.BlockSpec((1, P, SWA_NK), lambda b, n: (b, 0, 0)),
                  _const_spec(w["wo"])],
        out_specs=pl.BlockSpec((SWA_BLOCK, D), lambda b, n: (b * nb + n, 0)),
        out_shape=jax.ShapeDtypeStruct((M, D), F32),
        scratch_shapes=[pltpu.VMEM((L, SWA_NQ), BF), pltpu.VMEM((L, SWA_NK), BF),
                        pltpu.VMEM((L, SWA_NK), BF), pltpu.VMEM((SWA_BLOCK, SWA_NQ), BF)],
        compiler_params=_cp("arbitrary", "arbitrary"),
        name="swa_lat",
    )(w["sink"], x, mod, w["wqkv"], w["wsw"], cos128, sin128, kc, vc, w["wo"])


def _conv_kernel(x_ref, mod_ref, win_ref, cw_ref, wout_ref, o_ref, y_s, *, L):
    x = x_ref[...]
    rows = x.shape[0]
    h = _modulate(x, mod_ref, 0).astype(BF)
    CC = 256
    row = lax.broadcasted_iota(jnp.int32, (rows, CC), 0) % L
    for c in range(D // CC):
        cols = slice(c * CC, (c + 1) * CC)
        bg = _mm(h, win_ref[:, c * CC:(c + 1) * CC])
        cg = _mm(h, win_ref[:, D + c * CC:D + (c + 1) * CC])
        u = _mm(h, win_ref[:, 2 * D + c * CC:2 * D + (c + 1) * CC])
        cu = cg * u
        prev = jnp.where(row == 0, 0.0, pltpu.roll(cu, 1, 0))
        nxt = jnp.where(row == L - 1, 0.0, pltpu.roll(cu, rows - 1, 0))
        conv = prev * cw_ref[0:1, cols] + cu * cw_ref[1:2, cols] + nxt * cw_ref[2:3, cols]
        y_s[:, cols] = (bg * conv).astype(BF)
    o_ref[...] = x + _gate(mod_ref, 0) * _mm(y_s[...], wout_ref[...])


def conv_mixer(x, mod, w, L):
    M = x.shape[0]
    shared_mod = mod.shape[0] == 1
    tm = min(M, 1024) if shared_mod else L
    row = pl.BlockSpec((tm, D), lambda i: (i, 0))
    return pl.pallas_call(
        functools.partial(_conv_kernel, L=L),
        grid=(M // tm,),
        in_specs=[row, _mod_spec(tm, M if shared_mod else L), _const_spec(w["win"]),
                  _const_spec(w["cw"]), _const_spec(w["wout"])],
        out_specs=row,
        out_shape=jax.ShapeDtypeStruct((M, D), F32),
        scratch_shapes=[pltpu.VMEM((tm, D), BF)],
        compiler_params=_cp("arbitrary"),
        name="conv_mix",
    )(x, mod, w["win"], w["cw"], w["wout"])


def _mla_ctx_kernel(x_ref, mod_ref, wdq_ref, qn_ref, wuq_ref, wdkv_ref, kvn_ref, wukv_ref, wo_ref,
                    o_ref, ckv_out, kr_out, o_s):
    x = x_ref[...]
    h = _modulate(x, mod_ref, 0).astype(BF)
    cq = (_rms(_mm(h, wdq_ref[...])) * qn_ref[...]).astype(BF)
    q = _mm(cq, wuq_ref[...]).astype(BF)
    kvc = _mm(h, wdkv_ref[...])
    ckv = _rms(kvc[:, :KV_LORA]) * kvn_ref[...]
    kr = kvc[:, KV_LORA:]
    ckv_out[...] = ckv
    kr_out[...] = kr
    kv = _mm(ckv.astype(BF), wukv_ref[...]).astype(BF)
    krb = kr.astype(BF)
    nn = MLA_H * QK_NOPE
    for hh in range(MLA_H):
        qh = q[:, hh * MLA_QW:(hh + 1) * MLA_QW]
        kh = jnp.concatenate([kv[:, hh * QK_NOPE:(hh + 1) * QK_NOPE], krb], axis=1)
        s = _mm_nt(qh, kh) * MLA_SCALE
        m = jnp.max(s, axis=-1, keepdims=True)
        e = jnp.exp(s - m)
        inv = 1.0 / jnp.sum(e, axis=-1, keepdims=True)
        vh = kv[:, nn + hh * V_HEAD:nn + (hh + 1) * V_HEAD]
        o_s[:, hh * V_HEAD:(hh + 1) * V_HEAD] = (_mm(e.astype(BF), vh) * inv).astype(BF)
    o_ref[...] = x + _gate(mod_ref, 0) * _mm(o_s[...], wo_ref[...])


def mla_ctx_mixer(x, mod, w, L):
    M = x.shape[0]
    row = lambda n: pl.BlockSpec((L, n), lambda i: (i, 0))
    ws = [w["wdq"], w["qnorm"], w["wuq"], w["wdkv"], w["kvnorm"], w["wukv"], w["wo"]]
    return pl.pallas_call(
        _mla_ctx_kernel,
        grid=(M // L,),
        in_specs=[row(D), _mod_spec(L, M)] + [_const_spec(a) for a in ws],
        out_specs=[row(D), row(KV_LORA), row(128)],
        out_shape=[jax.ShapeDtypeStruct((M, D), F32), jax.ShapeDtypeStruct((M, KV_LORA), F32),
                   jax.ShapeDtypeStruct((M, LANES), F32)],
        scratch_shapes=[pltpu.VMEM((L, MLA_H * V_HEAD), BF)],
        compiler_params=_cp("arbitrary"),
        name="mla_ctx",
    )(x, mod, *ws)


def _mla_lat_kernel(x_ref, mod_ref, wdq_ref, qn_ref, wuq_ref, wuqsw_ref, wdkv_ref, wdkvsw_ref,
                    kvn_ref, wukv_ref, cq_ref, sq_ref, ck_ref, sk_ref, ckvc_ref, krc_ref, wo_ref,
                    o_ref, q_s, kn_s, v_s, kr_s, knc_s, vc_s, o_s, *, L, QB):
    n = pl.program_id(1)
    RC = 256
    nn = MLA_H * QK_NOPE

    @pl.when(n == 0)
    def _():
        for c in range(L // RC):
            rows = slice(c * RC, (c + 1) * RC)
            h = _modulate(x_ref[rows, :], mod_ref, 0).astype(BF)
            cq = (_rms(_mm(h, wdq_ref[...])) * qn_ref[...]).astype(BF)
            cosq = jnp.concatenate([cq_ref[rows, :]] * MLA_H, axis=1)
            sinq = jnp.concatenate([sq_ref[rows, :]] * MLA_H, axis=1)
            q = _mm(cq, wuq_ref[...]) * cosq + _mm(cq, wuqsw_ref[...]) * sinq
            q_s[rows, :] = q.astype(BF)
            kvc = _mm(h, wdkv_ref[...])
            ksw = _mm(h, wdkvsw_ref[...])
            ckv = _rms(kvc[:, :KV_LORA]) * kvn_ref[...]
            kr_s[rows, :] = (kvc[:, KV_LORA:] * ck_ref[rows, :] + ksw * sk_ref[rows, :]).astype(BF)
            kv = _mm(ckv.astype(BF), wukv_ref[...])
            kn_s[rows, :] = kv[:, :nn].astype(BF)
            v_s[rows, :] = kv[:, nn:].astype(BF)
        kvp = _mm(ckvc_ref[0].astype(BF), wukv_ref[...])
        knc_s[...] = kvp[:, :nn].astype(BF)
        vc_s[...] = kvp[:, nn:].astype(BF)

    r0 = pl.multiple_of(n * QB, QB)
    krc = krc_ref[0].astype(BF)
    for hh in range(MLA_H):
        ns = slice(hh * QK_NOPE, (hh + 1) * QK_NOPE)
        qh = q_s[pl.ds(r0, QB), hh * MLA_QW:(hh + 1) * MLA_QW]
        k1 = jnp.concatenate([kn_s[:, ns], kr_s[...]], axis=1)
        k2 = jnp.concatenate([knc_s[:, ns], krc], axis=1)
        s1 = _mm_nt(qh, k1) * MLA_SCALE
        s2 = _mm_nt(qh, k2) * MLA_SCALE
        m = jnp.maximum(jnp.max(s1, axis=-1, keepdims=True), jnp.max(s2, axis=-1, keepdims=True))
        e1 = jnp.exp(s1 - m)
        e2 = jnp.exp(s2 - m)
        den = jnp.sum(e1, axis=-1, keepdims=True) + jnp.sum(e2, axis=-1, keepdims=True)
        vs = slice(hh * V_HEAD, (hh + 1) * V_HEAD)
        o = _mm(e1.astype(BF), v_s[:, vs]) + _mm(e2.astype(BF), vc_s[:, vs])
        o_s[:, vs] = (o * (1.0 / den)).astype(BF)
    o_ref[...] = x_ref[pl.ds(r0, QB), :] + _gate(mod_ref, 0) * _mm(o_s[...], wo_ref[...])


def mla_lat_mixer(x, mod, w, L, ckv_c, kr_c128, tabs):
    M = x.shape[0]
    B = M // L
    P = ckv_c.shape[1]
    QB = 256
    nb = L // QB
    ws1 = [w["wdq"], w["qnorm"], w["wuq"], w["wuqsw"], w["wdkv"], w["wdkvsw"], w["kvnorm"], w["wukv"]]
    nv = MLA_H * V_HEAD
    return pl.pallas_call(
        functools.partial(_mla_lat_kernel, L=L, QB=QB),
        grid=(B, nb),
        in_specs=[pl.BlockSpec((L, D), lambda b, n: (b, 0)),
                  pl.BlockSpec((1, 6, D), lambda b, n: (b, 0, 0))]
                 + [_const_spec(a) for a in ws1] + [_const_spec(a) for a in tabs]
                 + [pl.BlockSpec((1, P, KV_LORA), lambda b, n: (b, 0, 0)),
                    pl.BlockSpec((1, P, LANES), lambda b, n: (b, 0, 0)),
                    _const_spec(w["wo"])],
        out_specs=pl.BlockSpec((QB, D), lambda b, n: (b * nb + n, 0)),
        out_shape=jax.ShapeDtypeStruct((M, D), F32),
        scratch_shapes=[pltpu.VMEM((L, MLA_H * MLA_QW), BF), pltpu.VMEM((L, nv), BF),
                        pltpu.VMEM((L, nv), BF), pltpu.VMEM((L, LANES), BF),
                        pltpu.VMEM((P, nv), BF), pltpu.VMEM((P, nv), BF), pltpu.VMEM((QB, nv), BF)],
        compiler_params=_cp("arbitrary", "arbitrary"),
        name="mla_lat",
    )(x, mod, *ws1, *tabs, ckv_c, kr_c128, w["wo"])


def _swiglu_partial(xb, wg_ref, wu_ref, wd_ref):
    hg = _mm(xb, wg_ref[...].astype(BF))
    hu = _mm(xb, wu_ref[...].astype(BF))
    a = (_silu(hg) * hu).astype(BF)
    return _mm(a, wd_ref[...].astype(BF))


def _dense_ffn_kernel(x_ref, mod_ref, wga_ref, wgb_ref, wua_ref, wub_ref, wda_ref, wdb_ref, o_ref, xm_s,
                      *, odd_blocks):
    f = pl.program_id(1)
    last = pl.num_programs(1) - 1

    @pl.when(f == 0)
    def _():
        xm_s[...] = _modulate(x_ref[...], mod_ref, 1).astype(BF)
        o_ref[...] = jnp.zeros_like(o_ref)

    def pair():
        xb = xm_s[...]
        acts = []
        for wg_ref, wu_ref in ((wga_ref, wua_ref), (wgb_ref, wub_ref)):
            hg = _mm(xb, wg_ref[...].astype(BF))
            hu = _mm(xb, wu_ref[...].astype(BF))
            acts.append((_silu(hg) * hu).astype(BF))
        wd = jnp.concatenate([wda_ref[...].astype(BF), wdb_ref[...].astype(BF)], axis=0)
        return _mm(jnp.concatenate(acts, axis=1), wd)

    if odd_blocks:
        @pl.when(f < last)
        def _():
            o_ref[...] += pair()

        @pl.when(f == last)
        def _():
            o_ref[...] += _swiglu_partial(xm_s[...], wga_ref, wua_ref, wda_ref)
    else:
        o_ref[...] += pair()

    @pl.when(f == last)
    def _():
        o_ref[...] = x_ref[...] + _gate(mod_ref, 1) * o_ref[...]


def dense_ffn(x, mod, wg, wu, wd, j, L):
    M = x.shape[0]
    F = wg.shape[-1]
    tm, tf = DENSE_TM, DENSE_TF
    nfb = F // tf
    blk_a = lambda f: 2 * f
    blk_b = lambda f: jnp.minimum(2 * f + 1, nfb - 1)
    col = lambda blk: pl.BlockSpec((None, D, tf), lambda i, f: (j, 0, blk(f)))
    row = lambda blk: pl.BlockSpec((None, tf, D), lambda i, f: (j, blk(f), 0))
    return pl.pallas_call(
        functools.partial(_dense_ffn_kernel, odd_blocks=nfb % 2 == 1),
        grid=(M // tm, (nfb + 1) // 2),
        in_specs=[pl.BlockSpec((tm, D), lambda i, f: (i, 0)),
                  pl.BlockSpec((1, 6, D), lambda i, f: ((i * tm) // (L if mod.shape[0] > 1 else M), 0, 0)),
                  col(blk_a), col(blk_b), col(blk_a), col(blk_b), row(blk_a), row(blk_b)],
        out_specs=pl.BlockSpec((tm, D), lambda i, f: (i, 0)),
        out_shape=jax.ShapeDtypeStruct((M, D), F32),
        scratch_shapes=[pltpu.VMEM((tm, D), BF)],
        compiler_params=_cp("arbitrary", "arbitrary"),
        name="dense_ffn",
    )(x, mod, wg, wg, wu, wu, wd, wd)


def _router_kernel(xp_ref, xs_ref, mod_ref, wr_ref, xm_ref, idx_ref, w_ref, cnt_ref, cnt_s, *, np_tiles):
    i = pl.program_id(0)
    tm = xm_ref.shape[0]

    @pl.when(i == 0)
    def _():
        cnt_s[...] = jnp.zeros_like(cnt_s)

    x = jnp.where(i < np_tiles, xp_ref[...], xs_ref[...])
    xmod = _modulate(x, mod_ref, 1)
    xm_ref[...] = xmod
    xm = xmod.astype(BF)
    lane = lax.broadcasted_iota(jnp.int32, idx_ref.shape, 1)
    logits = jnp.where(lane < N_EXPERTS, _mm(xm, wr_ref[...]), -jnp.inf)
    m1 = jnp.max(logits, axis=-1, keepdims=True)
    i1 = jnp.min(jnp.where(logits == m1, lane, LANES), axis=-1, keepdims=True)
    rest = jnp.where(lane == i1, -jnp.inf, logits)
    m2 = jnp.max(rest, axis=-1, keepdims=True)
    i2 = jnp.min(jnp.where(rest == m2, lane, LANES), axis=-1, keepdims=True)
    e = jnp.exp(m2 - m1)
    w_ref[...] = jnp.where(lane == 0, 1.0 / (1.0 + e), e / (1.0 + e))
    oh1 = jnp.where(lane == i1, 1.0, 0.0)
    oh2 = jnp.where(lane == i2, 1.0, 0.0)
    rr = lax.broadcasted_iota(jnp.int32, (tm, tm), 0)
    cc = lax.broadcasted_iota(jnp.int32, (tm, tm), 1)
    below = jnp.where(cc < rr, 1.0, 0.0).astype(BF)
    run = cnt_s[0:1, :]
    tot1 = jnp.sum(oh1, axis=0, keepdims=True)
    tot2 = jnp.sum(oh2, axis=0, keepdims=True)
    r1 = jnp.sum(oh1 * (run + _mm(below, oh1.astype(BF))), axis=-1, keepdims=True)
    r2 = jnp.sum(oh2 * (run + tot1 + _mm(below, oh2.astype(BF))), axis=-1, keepdims=True)
    idx_ref[...] = jnp.where(lane == 0, i1, jnp.where(lane == 1, i2, jnp.where(
        lane == 2, r1.astype(jnp.int32), r2.astype(jnp.int32))))
    total = run + tot1 + tot2
    cnt_s[...] = jnp.broadcast_to(total, cnt_s.shape)
    cnt_ref[...] = jnp.broadcast_to(total, cnt_ref.shape)


def route(xp, xs, mod3, wr, Ls):
    Mp, Ms = xp.shape[0], xs.shape[0]
    T = Mp + Ms
    tm = ROUTE_TM
    npt = Mp // tm
    row = lambda n: pl.BlockSpec((tm, n), lambda i: (i, 0))
    return pl.pallas_call(
        functools.partial(_router_kernel, np_tiles=npt),
        grid=(T // tm,),
        in_specs=[pl.BlockSpec((tm, D), lambda i: (jnp.minimum(i, npt - 1), 0)),
                  pl.BlockSpec((tm, D), lambda i: (jnp.maximum(i - npt, 0), 0)),
                  pl.BlockSpec((1, 6, D), lambda i: (jnp.where(i < npt, 0, 1 + ((i - npt) * tm) // Ls), 0, 0)),
                  _const_spec(wr)],
        out_specs=[row(D), row(LANES), row(LANES), pl.BlockSpec((SUBLANES, LANES), lambda i: (0, 0))],
        out_shape=[jax.ShapeDtypeStruct((T, D), F32), jax.ShapeDtypeStruct((T, LANES), jnp.int32),
                   jax.ShapeDtypeStruct((T, LANES), F32), jax.ShapeDtypeStruct((SUBLANES, LANES), F32)],
        scratch_shapes=[pltpu.VMEM((SUBLANES, LANES), F32)],
        compiler_params=_cp("arbitrary"),
        name="moe_route",
    )(xp, xs, mod3, wr)


def _moe_ffn_kernel(vb_ref, ve_ref, vm_ref, nv_ref, xa_ref, xb_ref, wg_ref, wu_ref, wd_ref, o_ref, x_s,
                    *, half_blocks):
    v = pl.program_id(0)
    f = pl.program_id(1)
    mode = vm_ref[v]
    H = MOE_TM

    @pl.when(jnp.logical_and(v < nv_ref[0], f == 0))
    def _():
        x_s[...] = jnp.where(vb_ref[v] < half_blocks, xa_ref[...], xb_ref[...]).astype(BF)

    def run(rows):
        @pl.when(f == 0)
        def _():
            o_ref[rows, :] = jnp.zeros((rows.stop - rows.start, D), F32)

        o_ref[rows, :] += _swiglu_partial(x_s[rows, :], wg_ref, wu_ref, wd_ref)

    @pl.when(mode == MOE_FULL)
    def _():
        run(slice(0, 2 * H))

    @pl.when(jnp.logical_or(mode == MOE_LO, mode == MOE_LO_ZERO_HI))
    def _():
        run(slice(0, H))

    @pl.when(mode == MOE_HI)
    def _():
        run(slice(H, 2 * H))

    @pl.when(jnp.logical_and(mode == MOE_LO_ZERO_HI, f == 0))
    def _():
        o_ref[H:2 * H, :] = jnp.zeros((H, D), F32)

    @pl.when(jnp.logical_and(mode == MOE_ZERO, f == 0))
    def _():
        o_ref[...] = jnp.zeros_like(o_ref)


def moe_ffn(xa, xb, tile_expert, n_valid, wg, wu, wd, j):
    R = 2 * xa.shape[0]
    F = wg.shape[-1]
    bm, tf = 2 * MOE_TM, MOE_TF
    nf = F // tf
    nb = R // bm
    nhb = nb // 2
    n_vis = nb + N_EXPERTS - 1
    blocks = jnp.arange(nb, dtype=jnp.int32)
    t0, t1 = tile_expert[0::2], tile_expert[1::2]
    valid0 = 2 * blocks < n_valid[0]
    valid1 = 2 * blocks + 1 < n_valid[0]
    two = jnp.logical_and(valid1, t0 != t1)
    cnt = jnp.where(valid0, 1 + two.astype(jnp.int32), 0)
    cum = jnp.cumsum(cnt)
    nv = cum[-1:].astype(jnp.int32)
    v_eff = jnp.minimum(jnp.arange(n_vis, dtype=jnp.int32), nv[0] - 1)
    vb = jnp.sum((v_eff[:, None] >= cum[None, :]).astype(jnp.int32), axis=1)
    pick = lambda a: jnp.sum(jnp.where(vb[:, None] == blocks[None, :], a[None, :].astype(jnp.int32), 0), axis=1)
    k = v_eff - pick(cum - cnt)
    two_v, t0_v, t1_v, valid1_v = pick(two), pick(t0), pick(t1), pick(valid1)
    vm = jnp.where(two_v == 1, jnp.where(k == 0, MOE_LO, MOE_HI),
                   jnp.where(valid1_v == 1, MOE_FULL, MOE_LO_ZERO_HI))
    ve = jnp.where(jnp.logical_and(two_v == 1, k == 1), t1_v, t0_v).astype(jnp.int32)
    spare = jnp.arange(n_vis, dtype=jnp.int32) - nv[0]
    n_used = jnp.sum(valid0.astype(jnp.int32))
    fill = jnp.logical_and(spare >= 0, n_used + spare < nb)
    idle = jnp.logical_and(spare >= 0, jnp.logical_not(fill))
    vb = jnp.where(fill, n_used + spare, jnp.where(jnp.logical_and(idle, n_used < nb), nb - 1, vb))
    vm = jnp.where(fill, MOE_ZERO, jnp.where(idle, MOE_IDLE, vm)).astype(jnp.int32)

    def fidx(v, f, nv):
        return jnp.where(v < nv[0], f, nf - 1)

    grid_spec = pltpu.PrefetchScalarGridSpec(
        num_scalar_prefetch=4,
        grid=(n_vis, nf),
        in_specs=[pl.BlockSpec((bm, D), lambda v, f, vb, ve, vm, nv: (jnp.minimum(vb[v], nhb - 1), 0)),
                  pl.BlockSpec((bm, D), lambda v, f, vb, ve, vm, nv: (jnp.maximum(vb[v] - nhb, 0), 0)),
                  pl.BlockSpec((None, None, D, tf), lambda v, f, vb, ve, vm, nv: (j, ve[v], 0, fidx(v, f, nv))),
                  pl.BlockSpec((None, None, D, tf), lambda v, f, vb, ve, vm, nv: (j, ve[v], 0, fidx(v, f, nv))),
                  pl.BlockSpec((None, None, tf, D), lambda v, f, vb, ve, vm, nv: (j, ve[v], fidx(v, f, nv), 0))],
        out_specs=pl.BlockSpec((bm, D), lambda v, f, vb, ve, vm, nv: (vb[v], 0)),
        scratch_shapes=[pltpu.VMEM((bm, D), BF)],
    )
    return pl.pallas_call(
        functools.partial(_moe_ffn_kernel, half_blocks=nhb),
        grid_spec=grid_spec,
        out_shape=jax.ShapeDtypeStruct((R, D), F32),
        compiler_params=_cp("arbitrary", "arbitrary"),
        name="moe_ffn",
    )(vb.astype(jnp.int32), ve, vm, nv, xa, xb, wg, wu, wd)


def _combine_kernel(x_ref, mod_ref, ya_ref, yb_ref, w_ref, fn_ref, o_ref, *, final):
    y = w_ref[:, 0:1] * ya_ref[...] + w_ref[:, 1:2] * yb_ref[...]
    o = x_ref[...] + _gate(mod_ref, 1) * y
    if final:
        o = _rms(o) * fn_ref[...]
    o_ref[...] = o


def moe_combine(x, mod, ya, yb, w, fn, L, final, row_off):
    M = x.shape[0]
    tm = ROW_TM
    off = row_off // tm
    row = pl.BlockSpec((tm, D), lambda i: (i, 0))
    row_o = lambda n: pl.BlockSpec((tm, n), lambda i: (i + off, 0))
    return pl.pallas_call(
        functools.partial(_combine_kernel, final=final),
        grid=(M // tm,),
        in_specs=[row, _mod_spec(tm, L if mod.shape[0] > 1 else M), row_o(D), row_o(D), row_o(LANES),
                  _const_spec(fn)],
        out_specs=row,
        out_shape=jax.ShapeDtypeStruct((M, D), F32),
        compiler_params=_cp("arbitrary"),
        name="moe_combine",
    )(x, mod, ya, yb, w, fn)


def moe_layer(xp, xs, mod3, wr, wg, wu, wd, j, fn, Lp, Ls, final):
    Mp = xp.shape[0]
    xm, idx, wts, cnt = route(xp, xs, mod3, wr, Ls)
    T = xm.shape[0]
    tm = MOE_TM
    R = 2 * T + N_EXPERTS * tm
    counts = cnt[0, :N_EXPERTS].astype(jnp.int32)
    tiles_per = (counts + tm - 1) // tm
    tile_end = jnp.cumsum(tiles_per)
    offs = (tile_end - tiles_per) * tm
    experts = jnp.arange(N_EXPERTS, dtype=jnp.int32)
    dest = jnp.sum(jnp.where(idx[:, 0:2, None] == experts, offs, 0), axis=-1) + idx[:, 2:4]
    tok = jnp.arange(2 * T, dtype=jnp.int32) // 2
    src = (jnp.arange(R, dtype=jnp.int32) % T).at[dest.reshape(-1)].set(
        tok, unique_indices=True, mode="promise_in_bounds")
    n_tiles = R // tm
    n_valid = tile_end[-1:].astype(jnp.int32)
    t_ids = jnp.arange(n_tiles, dtype=jnp.int32)
    te = jnp.sum((t_ids[:, None] >= tile_end[None, :]).astype(jnp.int32), axis=1)
    te_last = jnp.sum((n_valid - 1 >= tile_end).astype(jnp.int32))
    te = jnp.where(t_ids < n_valid[0], te, te_last).astype(jnp.int32)
    xa = xm.at[src[:R // 2]].get(mode="promise_in_bounds")
    xb = xm.at[src[R // 2:]].get(mode="promise_in_bounds")
    y_sorted = moe_ffn(xa, xb, te, n_valid, wg, wu, wd, j)
    ya = y_sorted.at[dest[:, 0]].get(mode="promise_in_bounds")
    yb = y_sorted.at[dest[:, 1]].get(mode="promise_in_bounds")
    xp_new = moe_combine(xp, mod3[0:1], ya, yb, wts, fn, Lp, final, 0)
    xs_new = moe_combine(xs, mod3[1:], ya, yb, wts, fn, Ls, final, Mp)
    return xp_new, xs_new


def _rope_partner_perm(n):
    c = np.arange(n)
    return np.where((c % 32) < 16, c + 16, c - 16)


def _rope_tables(L):
    nf = 16
    inv = ROPE_THETA ** (-np.arange(nf, dtype=np.float32) / nf)
    pos = np.arange(L)
    ang = np.stack([(pos // GRID_W).astype(np.float32), (pos % GRID_W).astype(np.float32)],
                   axis=-1)[:, :, None] * inv
    cos = jnp.cos(jnp.asarray(ang, F32))
    sin = jnp.sin(jnp.asarray(ang, F32))
    c64 = jnp.stack([cos, cos], axis=2).reshape(L, 64)
    s64 = jnp.stack([-sin, sin], axis=2).reshape(L, 64)
    return c64, s64


def kernel(x_prompt, x_sample, state_l0_gla, cache_l1_k, cache_l1_v, cache_l3_ckv, cache_l3_krope, c, c_ctx, w_mod, b_mod, gla_wq, gla_wk, gla_wv, gla_wg, gla_wgk1, gla_wgk2, gla_bgk, gla_norm, gla_wo, swa_wqkv, swa_sink, swa_wo, conv_win, conv_w, conv_wout, mla_wdq, mla_qnorm, mla_wuq, mla_wdkv, mla_kvnorm, mla_wukv, mla_wo, dense_w_gate, dense_w_up, dense_w_down, moe_router, moe_w_gate, moe_w_up, moe_w_down, final_norm):
    Bp, Lp, _ = x_prompt.shape
    Bs, Ls, _ = x_sample.shape
    P = cache_l1_k.shape[1]
    xp = x_prompt.reshape(Bp * Lp, D)
    xs = x_sample.reshape(Bs * Ls, D)

    cvec = jnp.zeros((8, D), F32).at[0].set(c_ctx).at[1:1 + Bs].set(c)
    mod_all = adaln_all(cvec, w_mod, b_mod).reshape(w_mod.shape[0], 8, 6, D)

    wlr = jnp.zeros((D, 128), F32).at[:, :GLA_RANK].set(gla_wgk1[0]).at[:, GLA_RANK:2 * GLA_RANK].set(gla_wgk1[1])
    wgk2 = jnp.zeros((2, 128, GLA_H * GLA_DK), F32)
    wgk2 = wgk2.at[0, :GLA_RANK].set(gla_wgk2[0]).at[1, GLA_RANK:2 * GLA_RANK].set(gla_wgk2[1])
    gla_w = dict(wp=jnp.concatenate([gla_wq, gla_wk, gla_wv, gla_wg], axis=1).astype(BF),
                 wlr=wlr.astype(BF), wgk2=wgk2.astype(BF), bgk=gla_bgk,
                 gnorm=gla_norm.reshape(1, GLA_DV), wo=gla_wo.astype(BF))
    modp, mods = mod_all[0, 0:1], mod_all[0, 1:1 + Bs]
    xp, new_state = gla_mixer(xp, modp, gla_w, Lp, None)
    xs, _ = gla_mixer(xs, mods, gla_w, Ls, state_l0_gla)
    xp = dense_ffn(xp, modp, dense_w_gate, dense_w_up, dense_w_down, 0, Lp)
    xs = dense_ffn(xs, mods, dense_w_gate, dense_w_up, dense_w_down, 0, Ls)

    c64, s64 = _rope_tables(Ls)
    cos128 = jnp.concatenate([c64, c64], axis=1)
    sin128 = jnp.concatenate([s64, s64], axis=1)
    perm = _rope_partner_perm(SWA_NQ + SWA_NK)
    swa_w = dict(wqkv=swa_wqkv.astype(BF), wsw=swa_wqkv[:, perm].astype(BF), wo=swa_wo.astype(BF),
                 sink=swa_sink)
    modp, mods = mod_all[1, 0:1], mod_all[1, 1:1 + Bs]
    xp, k1, v1 = swa_ctx_mixer(xp, modp, swa_w, Lp)
    xs = swa_lat_mixer(xs, mods, swa_w, Ls, cache_l1_k.reshape(Bs, P, SWA_NK),
                       cache_l1_v.reshape(Bs, P, SWA_NK), cos128, sin128)
    wr = jnp.zeros((moe_router.shape[0], D, LANES), F32).at[:, :, :N_EXPERTS].set(moe_router).astype(BF)
    xp, xs = moe_layer(xp, xs, mod_all[1, 0:1 + Bs], wr[0], moe_w_gate, moe_w_up, moe_w_down, 0,
                       final_norm.reshape(1, D), Lp, Ls, False)

    conv_wts = dict(win=conv_win.astype(BF), cw=conv_w, wout=conv_wout.astype(BF))
    modp, mods = mod_all[2, 0:1], mod_all[2, 1:1 + Bs]
    xp = conv_mixer(xp, modp, conv_wts, Lp)
    xs = conv_mixer(xs, mods, conv_wts, Ls)
    xp = dense_ffn(xp, modp, dense_w_gate, dense_w_up, dense_w_down, 1, Lp)
    xs = dense_ffn(xs, mods, dense_w_gate, dense_w_up, dense_w_down, 1, Ls)

    wuq3 = mla_wuq.reshape(Q_LORA, MLA_H, QK_NOPE + QK_ROPE)
    zpad = jnp.zeros((Q_LORA, MLA_H, MLA_QW - QK_NOPE - QK_ROPE), F32)
    wuq = jnp.concatenate([wuq3, zpad], axis=2).reshape(Q_LORA, MLA_H * MLA_QW)
    rperm = _rope_partner_perm(QK_ROPE)
    wuq_sw = jnp.concatenate([jnp.zeros((Q_LORA, MLA_H, QK_NOPE), F32), wuq3[:, :, QK_NOPE:][:, :, rperm], zpad],
                             axis=2).reshape(Q_LORA, MLA_H * MLA_QW)
    kpad = jnp.zeros((D, 128 - QK_ROPE), F32)
    wdkv = jnp.concatenate([mla_wdkv, kpad], axis=1)
    wdkv_sw = jnp.concatenate([mla_wdkv[:, KV_LORA:][:, rperm], kpad], axis=1)
    wukv3 = mla_wukv.reshape(KV_LORA, MLA_H, QK_NOPE + V_HEAD)
    wukv = jnp.concatenate([wukv3[:, :, :QK_NOPE].reshape(KV_LORA, MLA_H * QK_NOPE),
                            wukv3[:, :, QK_NOPE:].reshape(KV_LORA, MLA_H * V_HEAD)], axis=1)
    mla_w = dict(wdq=mla_wdq.astype(BF), qnorm=mla_qnorm.reshape(1, Q_LORA), wuq=wuq.astype(BF),
                 wuqsw=wuq_sw.astype(BF), wdkv=wdkv.astype(BF), wdkvsw=wdkv_sw.astype(BF),
                 kvnorm=mla_kvnorm.reshape(1, KV_LORA), wukv=wukv.astype(BF), wo=mla_wo.astype(BF))
    ones = jnp.ones((Ls, QK_NOPE), F32)
    z128 = jnp.zeros((Ls, QK_NOPE), F32)
    z64 = jnp.zeros((Ls, 64), F32)
    tabs = [jnp.concatenate([ones, c64, z64], axis=1), jnp.concatenate([z128, s64, z64], axis=1),
            jnp.concatenate([c64, z64], axis=1), jnp.concatenate([s64, z64], axis=1)]
    modp, mods = mod_all[3, 0:1], mod_all[3, 1:1 + Bs]
    xp, ckv3, kr3 = mla_ctx_mixer(xp, modp, mla_w, Lp)
    kr_c128 = jnp.concatenate([cache_l3_krope, jnp.zeros((Bs, P, 128 - QK_ROPE), F32)], axis=2)
    xs = mla_lat_mixer(xs, mods, mla_w, Ls, cache_l3_ckv, kr_c128, tabs)
    xp, xs = moe_layer(xp, xs, mod_all[3, 0:1 + Bs], wr[1], moe_w_gate, moe_w_up, moe_w_down, 1,
                       final_norm.reshape(1, D), Lp, Ls, True)

    return (xp.reshape(Bp, Lp, D), xs.reshape(Bs, Ls, D), new_state,
            k1.reshape(Bp, Lp, SWA_KVH, SWA_HD), v1.reshape(Bp, Lp, SWA_KVH, SWA_HD),
            ckv3.reshape(Bp, Lp, KV_LORA), kr3[:, :QK_ROPE].reshape(Bp, Lp, QK_ROPE))
```

```python
import functools

import numpy as np
import jax
import jax.numpy as jnp
from jax import lax
from jax.experimental import pallas as pl
from jax.experimental.pallas import tpu as pltpu

BF = jnp.bfloat16
F32 = jnp.float32

D = 1024
EPS = 1e-6
NEG = -1e30
ROPE_THETA = 10000.0
GRID_W = 64

GLA_H, GLA_DK, GLA_DV, GLA_RANK, GLA_CHUNK = 4, 128, 256, 16, 64
GLA_SCALE = GLA_DK ** -0.5
GLA_INV_NORMALIZER = 1.0 / 16.0

SWA_H, SWA_KVH, SWA_HD, SWA_WINDOW, SWA_BLOCK = 16, 4, 64, 128, 128
SWA_SCALE = SWA_HD ** -0.5
SWA_NQ = SWA_H * SWA_HD
SWA_NK = SWA_KVH * SWA_HD

MLA_H, Q_LORA, KV_LORA, QK_NOPE, QK_ROPE, V_HEAD = 8, 384, 256, 128, 64, 128
MLA_SCALE = (QK_NOPE + QK_ROPE) ** -0.5
MLA_QW = 256

N_EXPERTS = 8
MOE_TM = 512
MOE_TF = 512
MOE_FULL, MOE_LO, MOE_HI, MOE_LO_ZERO_HI, MOE_ZERO, MOE_IDLE = range(6)
ROUTE_TM = 1024
DENSE_TM = 1024
DENSE_TF = 256

LANES = 128
SUBLANES = 8
V7X_VMEM_BYTES = 64 * 1024 * 1024
VMEM_LIMIT = V7X_VMEM_BYTES * 7 // 8
ROW_TM = 512


def _cp(*sem):
    return pltpu.CompilerParams(dimension_semantics=sem, vmem_limit_bytes=VMEM_LIMIT)


def _mm(a, b):
    return jnp.dot(a, b, preferred_element_type=F32)


def _mm_nt(a, b):
    return lax.dot_general(a, b, (((1,), (1,)), ((), ())), preferred_element_type=F32)


def _mm_tn(a, b):
    return lax.dot_general(a, b, (((0,), (0,)), ((), ())), preferred_element_type=F32)


def _rms(x):
    return x * lax.rsqrt(jnp.mean(x * x, axis=-1, keepdims=True) + EPS)


def _silu(x):
    return x * (1.0 / (1.0 + jnp.exp(-x)))


def _modulate(x, mod_ref, j):
    shift = mod_ref[0, 3 * j:3 * j + 1, :]
    scale = mod_ref[0, 3 * j + 1:3 * j + 2, :]
    return _rms(x) * (1.0 + scale) + shift


def _gate(mod_ref, j):
    return mod_ref[0, 3 * j + 2:3 * j + 3, :]


def _const_spec(a):
    nd = a.ndim
    return pl.BlockSpec(a.shape, lambda *_: (0,) * nd)


def _mod_spec(tm, rows_per_mod):
    return pl.BlockSpec((1, 6, D), lambda i, *_: ((i * tm) // rows_per_mod, 0, 0))


def _adaln_kernel(c_ref, w_ref, b_ref, o_ref):
    s = _silu(c_ref[...]).astype(BF)
    o_ref[0] = _mm(s, w_ref[0].astype(BF)) + b_ref[0]


def adaln_all(cvec8, w_mod, b_mod):
    nl, d, n = w_mod.shape
    tn = 1536
    return pl.pallas_call(
        _adaln_kernel,
        grid=(nl, n // tn),
        in_specs=[pl.BlockSpec((8, d), lambda l, j: (0, 0)),
                  pl.BlockSpec((1, d, tn), lambda l, j: (l, 0, j)),
                  pl.BlockSpec((1, 1, tn), lambda l, j: (l, 0, j))],
        out_specs=pl.BlockSpec((1, 8, tn), lambda l, j: (l, 0, j)),
        out_shape=jax.ShapeDtypeStruct((nl, 8, n), F32),
        compiler_params=_cp("arbitrary", "arbitrary"),
        name="adaln",
    )(cvec8, w_mod, b_mod.reshape(nl, 1, n))


def _split3(x):
    hi = x.astype(BF)
    rem = x - hi.astype(F32)
    mid = rem.astype(BF)
    lo = (rem - mid.astype(F32)).astype(BF)
    return hi, mid, lo


def _mm3(a, pieces):
    return _mm(a, pieces[0]) + _mm(a, pieces[1]) + _mm(a, pieces[2])


def _gla_pre_kernel(x_ref, mod_ref, wp_ref, wlr_ref, wgk2_ref, bgk_ref,
                    qt_ref, kt_ref, kd_ref, v_ref, g_ref, dec_ref):
    tm = x_ref.shape[0]
    C = GLA_CHUNK
    G = 256
    h = _modulate(x_ref[...], mod_ref, 0).astype(BF)
    nk = GLA_H * GLA_DK
    nv = GLA_H * GLA_DV
    q = _mm(h, wp_ref[:, 0:nk]) * GLA_SCALE
    k = _mm(h, wp_ref[:, nk:2 * nk])
    v_ref[...] = _mm(h, wp_ref[:, 2 * nk:2 * nk + nv]).astype(BF)
    g_ref[...] = _mm(h, wp_ref[:, 2 * nk + nv:2 * nk + 2 * nv])
    lr = _mm(h, wlr_ref[...]).astype(BF)
    ii = lax.broadcasted_iota(jnp.int32, (G, G), 0)
    jj = lax.broadcasted_iota(jnp.int32, (G, G), 1)
    same = (ii // C) == (jj // C)
    ci = lax.broadcasted_iota(jnp.int32, (tm // C, tm), 0)
    cj = lax.broadcasted_iota(jnp.int32, (tm // C, tm), 1)
    sel = jnp.where((cj // C) == ci, 1.0, 0.0).astype(BF)
    for r in range(2):
        z = _mm(lr, wgk2_ref[r]) + bgk_ref[r:r + 1, :]
        gk = (jnp.minimum(z, 0.0) - jnp.log1p(jnp.exp(-jnp.abs(z)))) * GLA_INV_NORMALIZER
        pieces = _split3(gk)
        tri = jnp.where(jnp.logical_and(same, (jj <= ii) if r == 0 else (jj >= ii)), 1.0, 0.0).astype(BF)
        dec_ref[r] = jnp.exp(_mm3(sel, pieces))
        for gi in range(tm // G):
            rows = slice(gi * G, (gi + 1) * G)
            pg = tuple(p[rows] for p in pieces)
            b = _mm3(tri, pg)
            b3 = b.reshape(G // C, C, nk)
            edge = b3[:, C - 1:C, :] if r == 0 else b3[:, 0:1, :]
            tot = jnp.broadcast_to(edge, b3.shape).reshape(G, nk)
            qt_ref[r, rows, :] = (q[rows] * jnp.exp(b)).astype(BF)
            kt_ref[r, rows, :] = (k[rows] * jnp.exp(-b)).astype(BF)
            kd_ref[r, rows, :] = (k[rows] * jnp.exp(tot - b)).astype(BF)


def _gla_scan_kernel(*refs, L, S, has_h0, emit_state):
    qt_ref, kt_ref, kd_ref, v_ref, g_ref, dec_ref, gn_ref = refs[:7]
    pos = 7
    h0_ref = None
    if has_h0:
        h0_ref = refs[pos]
        pos += 1
    y_ref = refs[pos]
    pos += 1
    st_ref = None
    if emit_state:
        st_ref = refs[pos]
        pos += 1
    st_s, of_s, ob_s = refs[pos], refs[pos + 1], refs[pos + 2]

    C = GLA_CHUNK
    n_chunks = L // C
    ii = lax.broadcasted_iota(jnp.int32, (C, C), 0)
    jj = lax.broadcasted_iota(jnp.int32, (C, C), 1)
    keeps = ((jj <= ii), (jj >= ii))
    chains = [(sq, r, hd) for sq in range(S) for r in range(2) for hd in range(GLA_H)]
    slot = lambda sq, r, hd: (sq * 2 + r) * GLA_H + hd
    for sq, r, hd in chains:
        if has_h0:
            st_s[slot(sq, r, hd)] = h0_ref[sq, r, hd].T
        else:
            st_s[slot(sq, r, hd)] = jnp.zeros((GLA_DV, GLA_DK), F32)

    def step(i):
        for sq, r, hd in chains:
            n = i if r == 0 else n_chunks - 1 - i
            start = sq * L + n * C
            if not isinstance(start, int):
                start = pl.multiple_of(start, C)
            rows = pl.ds(start, C)
            o_s = of_s if r == 0 else ob_s
            kc = slice(hd * GLA_DK, (hd + 1) * GLA_DK)
            vc = slice(hd * GLA_DV, (hd + 1) * GLA_DV)
            qt = qt_ref[r, rows, kc]
            v = v_ref[rows, vc]
            att = jnp.where(keeps[r], _mm_nt(qt, kt_ref[r, rows, kc]), 0.0).astype(BF)
            st = st_s[slot(sq, r, hd)]
            o_s[rows, vc] = _mm_nt(qt, st.astype(BF)) + _mm(att, v)
            st_s[slot(sq, r, hd)] = st * dec_ref[r, sq, n, :, kc] + _mm_tn(v, kd_ref[r, rows, kc])

    if n_chunks <= 4:
        for i in range(n_chunks):
            step(i)
    else:
        def body(i, carry):
            step(i)
            return carry
        lax.fori_loop(0, n_chunks, body, 0)

    RC = 256
    for c in range(S * L // RC):
        rows = slice(c * RC, (c + 1) * RC)
        for hd in range(GLA_H):
            vc = slice(hd * GLA_DV, (hd + 1) * GLA_DV)
            y = _rms(of_s[rows, vc] + ob_s[rows, vc]) * gn_ref[...]
            y_ref[rows, vc] = (y * _silu(g_ref[rows, vc])).astype(BF)
    if emit_state:
        for sq, r, hd in chains:
            st_ref[sq, r, hd] = st_s[slot(sq, r, hd)].T


def _residual_out_kernel(y_ref, x_ref, mod_ref, w_ref, o_ref, *, gate_j):
    o_ref[...] = x_ref[...] + _gate(mod_ref, gate_j) * _mm(y_ref[...], w_ref[...])


def gla_mixer(x, mod, w, L, h0):
    M = x.shape[0]
    B = M // L
    tm = ROW_TM
    nk, nv = GLA_H * GLA_DK, GLA_H * GLA_DV
    row = lambda n: pl.BlockSpec((tm, n), lambda i: (i, 0))
    dirs = lambda: pl.BlockSpec((2, tm, nk), lambda i: (0, i, 0))
    n_chunks = L // GLA_CHUNK
    qt, kt, kd, v, g, dec = pl.pallas_call(
        _gla_pre_kernel,
        grid=(M // tm,),
        in_specs=[row(D), _mod_spec(tm, L if mod.shape[0] > 1 else M),
                  _const_spec(w["wp"]), _const_spec(w["wlr"]), _const_spec(w["wgk2"]),
                  _const_spec(w["bgk"])],
        out_specs=[dirs(), dirs(), dirs(), row(nv), row(nv),
                   pl.BlockSpec((2, tm // GLA_CHUNK, nk), lambda i: (0, i, 0))],
        out_shape=[jax.ShapeDtypeStruct((2, M, nk), BF), jax.ShapeDtypeStruct((2, M, nk), BF),
                   jax.ShapeDtypeStruct((2, M, nk), BF),
                   jax.ShapeDtypeStruct((M, nv), BF), jax.ShapeDtypeStruct((M, nv), F32),
                   jax.ShapeDtypeStruct((2, M // GLA_CHUNK, nk), F32)],
        compiler_params=_cp("arbitrary"),
        name="gla_pre",
    )(x, mod, w["wp"], w["wlr"], w["wgk2"], w["bgk"])
    dec = dec.reshape(2, B, n_chunks, 1, nk)

    has_h0 = h0 is not None
    emit_state = not has_h0
    S = max(s for s in (1, 2, 4) if B % s == 0 and s * L <= 1024)
    SL = S * L
    dir_spec = lambda: pl.BlockSpec((2, SL, nk), lambda b: (0, b, 0))
    in_specs = [dir_spec(), dir_spec(), dir_spec(),
                pl.BlockSpec((SL, nv), lambda b: (b, 0)),
                pl.BlockSpec((SL, nv), lambda b: (b, 0)),
                pl.BlockSpec((2, S, n_chunks, 1, nk), lambda b: (0, b, 0, 0, 0)),
                pl.BlockSpec((1, GLA_DV), lambda b: (0, 0))]
    args = [qt, kt, kd, v, g, dec, w["gnorm"]]
    st_spec = pl.BlockSpec((S, 2, GLA_H, GLA_DK, GLA_DV), lambda b: (b, 0, 0, 0, 0))
    if has_h0:
        in_specs.append(st_spec)
        args.append(h0)
    out_specs = [pl.BlockSpec((SL, nv), lambda b: (b, 0))]
    out_shape = [jax.ShapeDtypeStruct((M, nv), BF)]
    if emit_state:
        out_specs.append(st_spec)
        out_shape.append(jax.ShapeDtypeStruct((B, 2, GLA_H, GLA_DK, GLA_DV), F32))
    res = pl.pallas_call(
        functools.partial(_gla_scan_kernel, L=L, S=S, has_h0=has_h0, emit_state=emit_state),
        grid=(B // S,),
        in_specs=in_specs,
        out_specs=out_specs,
        out_shape=out_shape,
        scratch_shapes=[pltpu.VMEM((S * 2 * GLA_H, GLA_DV, GLA_DK), F32), pltpu.VMEM((SL, nv), F32),
                        pltpu.VMEM((SL, nv), F32)],
        compiler_params=_cp("arbitrary"),
        name="gla_scan",
    )(*args)
    y = res[0]
    state = res[1] if emit_state else None

    x_new = pl.pallas_call(
        functools.partial(_residual_out_kernel, gate_j=0),
        grid=(M // tm,),
        in_specs=[row(nv), row(D), _mod_spec(tm, L if mod.shape[0] > 1 else M), _const_spec(w["wo"])],
        out_specs=row(D),
        out_shape=jax.ShapeDtypeStruct((M, D), F32),
        compiler_params=_cp("arbitrary"),
        name="gla_post",
    )(y, x, mod, w["wo"])
    return x_new, state


def _softmax_sink_heads(q_of, k_of, v_of, sink_ref, o_s):
    for hq in range(SWA_H):
        kh = hq // (SWA_H // SWA_KVH)
        s = _mm_nt(q_of(hq), k_of(kh))
        sink = sink_ref[hq]
        m = jnp.maximum(jnp.max(s, axis=-1, keepdims=True), sink)
        e = jnp.exp(s - m)
        p = e / (jnp.sum(e, axis=-1, keepdims=True) + jnp.exp(sink - m))
        o_s[:, hq * SWA_HD:(hq + 1) * SWA_HD] = _mm(p.astype(BF), v_of(kh)).astype(BF)


def _swa_ctx_kernel(sink_ref, x_ref, mod_ref, wqkv_ref, wo_ref, o_ref, k_out, v_out, o_s):
    x = x_ref[...]
    h = _modulate(x, mod_ref, 0).astype(BF)
    qkv = _mm(h, wqkv_ref[...])
    k_out[0] = qkv[:, SWA_NQ:SWA_NQ + SWA_NK].T
    v_out[0] = qkv[:, SWA_NQ + SWA_NK:].T
    q_of = lambda hq: (qkv[:, hq * SWA_HD:(hq + 1) * SWA_HD] * SWA_SCALE).astype(BF)
    k_of = lambda kh: qkv[:, SWA_NQ + kh * SWA_HD:SWA_NQ + (kh + 1) * SWA_HD].astype(BF)
    v_of = lambda kh: qkv[:, SWA_NQ + SWA_NK + kh * SWA_HD:SWA_NQ + SWA_NK + (kh + 1) * SWA_HD].astype(BF)
    _softmax_sink_heads(q_of, k_of, v_of, sink_ref, o_s)
    o_ref[...] = x + _gate(mod_ref, 0) * _mm(o_s[...], wo_ref[...])


def swa_ctx_mixer(x, mod, w, L):
    M = x.shape[0]
    row = lambda n: pl.BlockSpec((L, n), lambda i: (i, 0))
    return pl.pallas_call(
        _swa_ctx_kernel,
        grid=(M // L,),
        in_specs=[pl.BlockSpec(memory_space=pltpu.SMEM), row(D), _mod_spec(L, M),
                  _const_spec(w["wqkv"]), _const_spec(w["wo"])],
        out_specs=[row(D), pl.BlockSpec((1, SWA_NK, L), lambda i: (i, 0, 0)),
                   pl.BlockSpec((1, SWA_NK, L), lambda i: (i, 0, 0))],
        out_shape=[jax.ShapeDtypeStruct((M, D), F32), jax.ShapeDtypeStruct((M // L, SWA_NK, L), F32),
                   jax.ShapeDtypeStruct((M // L, SWA_NK, L), F32)],
        scratch_shapes=[pltpu.VMEM((L, SWA_NQ), BF)],
        compiler_params=_cp("arbitrary"),
        name="swa_ctx",
    )(w["sink"], x, mod, w["wqkv"], w["wo"])


def _swa_lat_kernel(sink_ref, x_ref, mod_ref, wqkv_ref, wsw_ref, cos_ref, sin_ref, kc_ref, vc_ref,
                    wo_ref, o_ref, q_s, k_s, v_s, o_s, *, L):
    n = pl.program_id(1)
    RC = 256

    @pl.when(n == 0)
    def _():
        for c in range(L // RC):
            rows = slice(c * RC, (c + 1) * RC)
            h = _modulate(x_ref[rows, :], mod_ref, 0).astype(BF)
            qkv = _mm(h, wqkv_ref[...])
            sw = _mm(h, wsw_ref[...])
            cos = cos_ref[rows, :]
            sin = sin_ref[rows, :]
            cq = jnp.concatenate([cos] * (SWA_NQ // 128), axis=1)
            sq = jnp.concatenate([sin] * (SWA_NQ // 128), axis=1)
            ck = jnp.concatenate([cos] * (SWA_NK // 128), axis=1)
            sk = jnp.concatenate([sin] * (SWA_NK // 128), axis=1)
            q = (qkv[:, :SWA_NQ] * SWA_SCALE) * cq + (sw[:, :SWA_NQ] * SWA_SCALE) * sq
            q_s[rows, :] = q.astype(BF)
            k = qkv[:, SWA_NQ:SWA_NQ + SWA_NK] * ck + sw[:, SWA_NQ:SWA_NQ + SWA_NK] * sk
            k_s[rows, :] = k.astype(BF)
            v_s[rows, :] = qkv[:, SWA_NQ + SWA_NK:].astype(BF)

    QB = SWA_BLOCK
    KW = 3 * SWA_BLOCK
    r0 = pl.multiple_of(n * QB, QB)
    ws = pl.multiple_of(jnp.clip((n - 1) * QB, 0, L - KW), QB)
    qpos = r0 + lax.broadcasted_iota(jnp.int32, (QB, KW), 0)
    kpos = ws + lax.broadcasted_iota(jnp.int32, (QB, KW), 1)
    valid = jnp.abs(kpos - qpos) <= SWA_WINDOW
    for hq in range(SWA_H):
        kh = hq // (SWA_H // SWA_KVH)
        hs = slice(kh * SWA_HD, (kh + 1) * SWA_HD)
        q = q_s[pl.ds(r0, QB), hq * SWA_HD:(hq + 1) * SWA_HD]
        s1 = jnp.where(valid, _mm_nt(q, k_s[pl.ds(ws, KW), hs]), NEG)
        s2 = _mm_nt(q, kc_ref[0, :, hs].astype(BF))
        sink = sink_ref[hq]
        m = jnp.maximum(jnp.maximum(jnp.max(s1, axis=-1, keepdims=True),
                                    jnp.max(s2, axis=-1, keepdims=True)), sink)
        e1 = jnp.exp(s1 - m)
        e2 = jnp.exp(s2 - m)
        den = (jnp.sum(e1, axis=-1, keepdims=True) + jnp.sum(e2, axis=-1, keepdims=True)
               + jnp.exp(sink - m))
        o = (_mm(e1.astype(BF), v_s[pl.ds(ws, KW), hs])
             + _mm(e2.astype(BF), vc_ref[0, :, hs].astype(BF)))
        o_s[:, hq * SWA_HD:(hq + 1) * SWA_HD] = (o * (1.0 / den)).astype(BF)
    o_ref[...] = x_ref[pl.ds(r0, QB), :] + _gate(mod_ref, 0) * _mm(o_s[...], wo_ref[...])


def swa_lat_mixer(x, mod, w, L, kc, vc, cos128, sin128):
    M = x.shape[0]
    B = M // L
    P = kc.shape[1]
    nb = L // SWA_BLOCK
    return pl.pallas_call(
        functools.partial(_swa_lat_kernel, L=L),
        grid=(B, nb),
        in_specs=[pl.BlockSpec(memory_space=pltpu.SMEM),
                  pl.BlockSpec((L, D), lambda b, n: (b, 0)),
                  pl.BlockSpec((1, 6, D), lambda b, n: (b, 0, 0)),
                  _const_spec(w["wqkv"]), _const_spec(w["wsw"]),
                  _const_spec(cos128), _const_spec(sin128),
                  pl.BlockSpec((1, P, SWA_NK), lambda b, n: (b, 0, 0)),
                  pl.BlockSpec((1, P, SWA_NK), lambda b, n: (b, 0, 0)),
                  _const_spec(w["wo"])],
        out_specs=pl.BlockSpec((SWA_BLOCK, D), lambda b, n: (b * nb + n, 0)),
        out_shape=jax.ShapeDtypeStruct((M, D), F32),
        scratch_shapes=[pltpu.VMEM((L, SWA_NQ), BF), pltpu.VMEM((L, SWA_NK), BF),
                        pltpu.VMEM((L, SWA_NK), BF), pltpu.VMEM((SWA_BLOCK, SWA_NQ), BF)],
        compiler_params=_cp("arbitrary", "arbitrary"),
        name="swa_lat",
    )(w["sink"], x, mod, w["wqkv"], w["wsw"], cos128, sin128, kc, vc, w["wo"])


def _conv_kernel(x_ref, mod_ref, win_ref, cw_ref, wout_ref, o_ref, y_s, *, L):
    x = x_ref[...]
    rows = x.shape[0]
    h = _modulate(x, mod_ref, 0).astype(BF)
    CC = 256
    row = lax.broadcasted_iota(jnp.int32, (rows, CC), 0) % L
    for c in range(D // CC):
        cols = slice(c * CC, (c + 1) * CC)
        bg = _mm(h, win_ref[:, c * CC:(c + 1) * CC])
        cg = _mm(h, win_ref[:, D + c * CC:D + (c + 1) * CC])
        u = _mm(h, win_ref[:, 2 * D + c * CC:2 * D + (c + 1) * CC])
        cu = cg * u
        prev = jnp.where(row == 0, 0.0, pltpu.roll(cu, 1, 0))
        nxt = jnp.where(row == L - 1, 0.0, pltpu.roll(cu, rows - 1, 0))
        conv = prev * cw_ref[0:1, cols] + cu * cw_ref[1:2, cols] + nxt * cw_ref[2:3, cols]
        y_s[:, cols] = (bg * conv).astype(BF)
    o_ref[...] = x + _gate(mod_ref, 0) * _mm(y_s[...], wout_ref[...])


def conv_mixer(x, mod, w, L):
    M = x.shape[0]
    shared_mod = mod.shape[0] == 1
    tm = min(M, 1024) if shared_mod else L
    row = pl.BlockSpec((tm, D), lambda i: (i, 0))
    return pl.pallas_call(
        functools.partial(_conv_kernel, L=L),
        grid=(M // tm,),
        in_specs=[row, _mod_spec(tm, M if shared_mod else L), _const_spec(w["win"]),
                  _const_spec(w["cw"]), _const_spec(w["wout"])],
        out_specs=row,
        out_shape=jax.ShapeDtypeStruct((M, D), F32),
        scratch_shapes=[pltpu.VMEM((tm, D), BF)],
        compiler_params=_cp("arbitrary"),
        name="conv_mix",
    )(x, mod, w["win"], w["cw"], w["wout"])


def _mla_ctx_kernel(x_ref, mod_ref, wdq_ref, qn_ref, wuq_ref, wdkv_ref, kvn_ref, wukv_ref, wo_ref,
                    o_ref, ckv_out, kr_out, o_s):
    x = x_ref[...]
    h = _modulate(x, mod_ref, 0).astype(BF)
    cq = (_rms(_mm(h, wdq_ref[...])) * qn_ref[...]).astype(BF)
    q = _mm(cq, wuq_ref[...]).astype(BF)
    kvc = _mm(h, wdkv_ref[...])
    ckv = _rms(kvc[:, :KV_LORA]) * kvn_ref[...]
    kr = kvc[:, KV_LORA:]
    ckv_out[...] = ckv
    kr_out[0] = kr.T[:QK_ROPE, :]
    kv = _mm(ckv.astype(BF), wukv_ref[...]).astype(BF)
    krb = kr.astype(BF)
    nn = MLA_H * QK_NOPE
    for hh in range(MLA_H):
        qh = q[:, hh * MLA_QW:(hh + 1) * MLA_QW]
        kh = jnp.concatenate([kv[:, hh * QK_NOPE:(hh + 1) * QK_NOPE], krb], axis=1)
        s = _mm_nt(qh, kh) * MLA_SCALE
        m = jnp.max(s, axis=-1, keepdims=True)
        e = jnp.exp(s - m)
        inv = 1.0 / jnp.sum(e, axis=-1, keepdims=True)
        vh = kv[:, nn + hh * V_HEAD:nn + (hh + 1) * V_HEAD]
        o_s[:, hh * V_HEAD:(hh + 1) * V_HEAD] = (_mm(e.astype(BF), vh) * inv).astype(BF)
    o_ref[...] = x + _gate(mod_ref, 0) * _mm(o_s[...], wo_ref[...])


def mla_ctx_mixer(x, mod, w, L):
    M = x.shape[0]
    row = lambda n: pl.BlockSpec((L, n), lambda i: (i, 0))
    ws = [w["wdq"], w["qnorm"], w["wuq"], w["wdkv"], w["kvnorm"], w["wukv"], w["wo"]]
    return pl.pallas_call(
        _mla_ctx_kernel,
        grid=(M // L,),
        in_specs=[row(D), _mod_spec(L, M)] + [_const_spec(a) for a in ws],
        out_specs=[row(D), row(KV_LORA), pl.BlockSpec((1, QK_ROPE, L), lambda i: (i, 0, 0))],
        out_shape=[jax.ShapeDtypeStruct((M, D), F32), jax.ShapeDtypeStruct((M, KV_LORA), F32),
                   jax.ShapeDtypeStruct((M // L, QK_ROPE, L), F32)],
        scratch_shapes=[pltpu.VMEM((L, MLA_H * V_HEAD), BF)],
        compiler_params=_cp("arbitrary"),
        name="mla_ctx",
    )(x, mod, *ws)


def _mla_lat_kernel(x_ref, mod_ref, wdq_ref, qn_ref, wuq_ref, wuqsw_ref, wdkv_ref, wdkvsw_ref,
                    kvn_ref, wukv_ref, cq_ref, sq_ref, ck_ref, sk_ref, ckvc_ref, krc_ref, wo_ref,
                    o_ref, q_s, kn_s, v_s, kr_s, knc_s, vc_s, o_s, *, L, QB):
    n = pl.program_id(1)
    RC = 256
    nn = MLA_H * QK_NOPE

    @pl.when(n == 0)
    def _():
        for c in range(L // RC):
            rows = slice(c * RC, (c + 1) * RC)
            h = _modulate(x_ref[rows, :], mod_ref, 0).astype(BF)
            cq = (_rms(_mm(h, wdq_ref[...])) * qn_ref[...]).astype(BF)
            cosq = jnp.concatenate([cq_ref[rows, :]] * MLA_H, axis=1)
            sinq = jnp.concatenate([sq_ref[rows, :]] * MLA_H, axis=1)
            q = _mm(cq, wuq_ref[...]) * cosq + _mm(cq, wuqsw_ref[...]) * sinq
            q_s[rows, :] = q.astype(BF)
            kvc = _mm(h, wdkv_ref[...])
            ksw = _mm(h, wdkvsw_ref[...])
            ckv = _rms(kvc[:, :KV_LORA]) * kvn_ref[...]
            kr_s[rows, :] = (kvc[:, KV_LORA:] * ck_ref[rows, :] + ksw * sk_ref[rows, :]).astype(BF)
            kv = _mm(ckv.astype(BF), wukv_ref[...])
            kn_s[rows, :] = kv[:, :nn].astype(BF)
            v_s[rows, :] = kv[:, nn:].astype(BF)
        kvp = _mm(ckvc_ref[0].astype(BF), wukv_ref[...])
        knc_s[...] = kvp[:, :nn].astype(BF)
        vc_s[...] = kvp[:, nn:].astype(BF)

    r0 = pl.multiple_of(n * QB, QB)
    krc = krc_ref[0].astype(BF)
    for hh in range(MLA_H):
        ns = slice(hh * QK_NOPE, (hh + 1) * QK_NOPE)
        qh = q_s[pl.ds(r0, QB), hh * MLA_QW:(hh + 1) * MLA_QW]
        k1 = jnp.concatenate([kn_s[:, ns], kr_s[...]], axis=1)
        k2 = jnp.concatenate([knc_s[:, ns], krc], axis=1)
        s1 = _mm_nt(qh, k1) * MLA_SCALE
        s2 = _mm_nt(qh, k2) * MLA_SCALE
        m = jnp.maximum(jnp.max(s1, axis=-1, keepdims=True), jnp.max(s2, axis=-1, keepdims=True))
        e1 = jnp.exp(s1 - m)
        e2 = jnp.exp(s2 - m)
        den = jnp.sum(e1, axis=-1, keepdims=True) + jnp.sum(e2, axis=-1, keepdims=True)
        vs = slice(hh * V_HEAD, (hh + 1) * V_HEAD)
        o = _mm(e1.astype(BF), v_s[:, vs]) + _mm(e2.astype(BF), vc_s[:, vs])
        o_s[:, vs] = (o * (1.0 / den)).astype(BF)
    o_ref[...] = x_ref[pl.ds(r0, QB), :] + _gate(mod_ref, 0) * _mm(o_s[...], wo_ref[...])


def mla_lat_mixer(x, mod, w, L, ckv_c, kr_c128, tabs):
    M = x.shape[0]
    B = M // L
    P = ckv_c.shape[1]
    QB = 256
    nb = L // QB
    ws1 = [w["wdq"], w["qnorm"], w["wuq"], w["wuqsw"], w["wdkv"], w["wdkvsw"], w["kvnorm"], w["wukv"]]
    nv = MLA_H * V_HEAD
    return pl.pallas_call(
        functools.partial(_mla_lat_kernel, L=L, QB=QB),
        grid=(B, nb),
        in_specs=[pl.BlockSpec((L, D), lambda b, n: (b, 0)),
                  pl.BlockSpec((1, 6, D), lambda b, n: (b, 0, 0))]
                 + [_const_spec(a) for a in ws1] + [_const_spec(a) for a in tabs]
                 + [pl.BlockSpec((1, P, KV_LORA), lambda b, n: (b, 0, 0)),
                    pl.BlockSpec((1, P, LANES), lambda b, n: (b, 0, 0)),
                    _const_spec(w["wo"])],
        out_specs=pl.BlockSpec((QB, D), lambda b, n: (b * nb + n, 0)),
        out_shape=jax.ShapeDtypeStruct((M, D), F32),
        scratch_shapes=[pltpu.VMEM((L, MLA_H * MLA_QW), BF), pltpu.VMEM((L, nv), BF),
                        pltpu.VMEM((L, nv), BF), pltpu.VMEM((L, LANES), BF),
                        pltpu.VMEM((P, nv), BF), pltpu.VMEM((P, nv), BF), pltpu.VMEM((QB, nv), BF)],
        compiler_params=_cp("arbitrary", "arbitrary"),
        name="mla_lat",
    )(x, mod, *ws1, *tabs, ckv_c, kr_c128, w["wo"])


def _swiglu_partial(xb, wg_ref, wu_ref, wd_ref):
    hg = _mm(xb, wg_ref[...].astype(BF))
    hu = _mm(xb, wu_ref[...].astype(BF))
    a = (_silu(hg) * hu).astype(BF)
    return _mm(a, wd_ref[...].astype(BF))


def _dense_ffn_kernel(x_ref, mod_ref, wga_ref, wgb_ref, wua_ref, wub_ref, wda_ref, wdb_ref, o_ref, xm_s,
                      *, odd_blocks):
    f = pl.program_id(1)
    last = pl.num_programs(1) - 1

    @pl.when(f == 0)
    def _():
        xm_s[...] = _modulate(x_ref[...], mod_ref, 1).astype(BF)
        o_ref[...] = jnp.zeros_like(o_ref)

    def pair():
        xb = xm_s[...]
        acts = []
        for wg_ref, wu_ref in ((wga_ref, wua_ref), (wgb_ref, wub_ref)):
            hg = _mm(xb, wg_ref[...].astype(BF))
            hu = _mm(xb, wu_ref[...].astype(BF))
            acts.append((_silu(hg) * hu).astype(BF))
        wd = jnp.concatenate([wda_ref[...].astype(BF), wdb_ref[...].astype(BF)], axis=0)
        return _mm(jnp.concatenate(acts, axis=1), wd)

    if odd_blocks:
        @pl.when(f < last)
        def _():
            o_ref[...] += pair()

        @pl.when(f == last)
        def _():
            o_ref[...] += _swiglu_partial(xm_s[...], wga_ref, wua_ref, wda_ref)
    else:
        o_ref[...] += pair()

    @pl.when(f == last)
    def _():
        o_ref[...] = x_ref[...] + _gate(mod_ref, 1) * o_ref[...]


def dense_ffn(x, mod, wg, wu, wd, j, L):
    M = x.shape[0]
    F = wg.shape[-1]
    tm, tf = DENSE_TM, DENSE_TF
    nfb = F // tf
    blk_a = lambda f: 2 * f
    blk_b = lambda f: jnp.minimum(2 * f + 1, nfb - 1)
    col = lambda blk: pl.BlockSpec((None, D, tf), lambda i, f: (j, 0, blk(f)))
    row = lambda blk: pl.BlockSpec((None, tf, D), lambda i, f: (j, blk(f), 0))
    return pl.pallas_call(
        functools.partial(_dense_ffn_kernel, odd_blocks=nfb % 2 == 1),
        grid=(M // tm, (nfb + 1) // 2),
        in_specs=[pl.BlockSpec((tm, D), lambda i, f: (i, 0)),
                  pl.BlockSpec((1, 6, D), lambda i, f: ((i * tm) // (L if mod.shape[0] > 1 else M), 0, 0)),
                  col(blk_a), col(blk_b), col(blk_a), col(blk_b), row(blk_a), row(blk_b)],
        out_specs=pl.BlockSpec((tm, D), lambda i, f: (i, 0)),
        out_shape=jax.ShapeDtypeStruct((M, D), F32),
        scratch_shapes=[pltpu.VMEM((tm, D), BF)],
        compiler_params=_cp("arbitrary", "arbitrary"),
        name="dense_ffn",
    )(x, mod, wg, wg, wu, wu, wd, wd)


def _router_kernel(xp_ref, xs_ref, mod_ref, wr_ref, xm_ref, idx_ref, w_ref, cnt_ref, cnt_s, *, np_tiles):
    i = pl.program_id(0)
    tm = xm_ref.shape[0]

    @pl.when(i == 0)
    def _():
        cnt_s[...] = jnp.zeros_like(cnt_s)

    x = jnp.where(i < np_tiles, xp_ref[...], xs_ref[...])
    xmod = _modulate(x, mod_ref, 1)
    xm_ref[...] = xmod
    xm = xmod.astype(BF)
    lane = lax.broadcasted_iota(jnp.int32, idx_ref.shape, 1)
    logits = jnp.where(lane < N_EXPERTS, _mm(xm, wr_ref[...]), -jnp.inf)
    m1 = jnp.max(logits, axis=-1, keepdims=True)
    i1 = jnp.min(jnp.where(logits == m1, lane, LANES), axis=-1, keepdims=True)
    rest = jnp.where(lane == i1, -jnp.inf, logits)
    m2 = jnp.max(rest, axis=-1, keepdims=True)
    i2 = jnp.min(jnp.where(rest == m2, lane, LANES), axis=-1, keepdims=True)
    e = jnp.exp(m2 - m1)
    w_ref[...] = jnp.where(lane == 0, 1.0 / (1.0 + e), e / (1.0 + e))
    oh1 = jnp.where(lane == i1, 1.0, 0.0)
    oh2 = jnp.where(lane == i2, 1.0, 0.0)
    rr = lax.broadcasted_iota(jnp.int32, (tm, tm), 0)
    cc = lax.broadcasted_iota(jnp.int32, (tm, tm), 1)
    below = jnp.where(cc < rr, 1.0, 0.0).astype(BF)
    run = cnt_s[0:1, :]
    tot1 = jnp.sum(oh1, axis=0, keepdims=True)
    tot2 = jnp.sum(oh2, axis=0, keepdims=True)
    r1 = jnp.sum(oh1 * (run + _mm(below, oh1.astype(BF))), axis=-1, keepdims=True)
    r2 = jnp.sum(oh2 * (run + tot1 + _mm(below, oh2.astype(BF))), axis=-1, keepdims=True)
    idx_ref[...] = jnp.where(lane == 0, i1, jnp.where(lane == 1, i2, jnp.where(
        lane == 2, r1.astype(jnp.int32), r2.astype(jnp.int32))))
    total = run + tot1 + tot2
    cnt_s[...] = jnp.broadcast_to(total, cnt_s.shape)
    cnt_ref[...] = jnp.broadcast_to(total, cnt_ref.shape)


def route(xp, xs, mod3, wr, Ls):
    Mp, Ms = xp.shape[0], xs.shape[0]
    T = Mp + Ms
    tm = ROUTE_TM
    npt = Mp // tm
    row = lambda n: pl.BlockSpec((tm, n), lambda i: (i, 0))
    return pl.pallas_call(
        functools.partial(_router_kernel, np_tiles=npt),
        grid=(T // tm,),
        in_specs=[pl.BlockSpec((tm, D), lambda i: (jnp.minimum(i, npt - 1), 0)),
                  pl.BlockSpec((tm, D), lambda i: (jnp.maximum(i - npt, 0), 0)),
                  pl.BlockSpec((1, 6, D), lambda i: (jnp.where(i < npt, 0, 1 + ((i - npt) * tm) // Ls), 0, 0)),
                  _const_spec(wr)],
        out_specs=[row(D), row(LANES), row(LANES), pl.BlockSpec((SUBLANES, LANES), lambda i: (0, 0))],
        out_shape=[jax.ShapeDtypeStruct((T, D), F32), jax.ShapeDtypeStruct((T, LANES), jnp.int32),
                   jax.ShapeDtypeStruct((T, LANES), F32), jax.ShapeDtypeStruct((SUBLANES, LANES), F32)],
        scratch_shapes=[pltpu.VMEM((SUBLANES, LANES), F32)],
        compiler_params=_cp("arbitrary"),
        name="moe_route",
    )(xp, xs, mod3, wr)


def _moe_ffn_kernel(vb_ref, ve_ref, vm_ref, nv_ref, xa_ref, xb_ref, wg_ref, wu_ref, wd_ref, o_ref, x_s,
                    *, half_blocks):
    v = pl.program_id(0)
    f = pl.program_id(1)
    mode = vm_ref[v]
    H = MOE_TM

    @pl.when(jnp.logical_and(v < nv_ref[0], f == 0))
    def _():
        x_s[...] = jnp.where(vb_ref[v] < half_blocks, xa_ref[...], xb_ref[...]).astype(BF)

    def run(rows):
        @pl.when(f == 0)
        def _():
            o_ref[rows, :] = jnp.zeros((rows.stop - rows.start, D), F32)

        o_ref[rows, :] += _swiglu_partial(x_s[rows, :], wg_ref, wu_ref, wd_ref)

    @pl.when(mode == MOE_FULL)
    def _():
        run(slice(0, 2 * H))

    @pl.when(jnp.logical_or(mode == MOE_LO, mode == MOE_LO_ZERO_HI))
    def _():
        run(slice(0, H))

    @pl.when(mode == MOE_HI)
    def _():
        run(slice(H, 2 * H))

    @pl.when(jnp.logical_and(mode == MOE_LO_ZERO_HI, f == 0))
    def _():
        o_ref[H:2 * H, :] = jnp.zeros((H, D), F32)

    @pl.when(jnp.logical_and(mode == MOE_ZERO, f == 0))
    def _():
        o_ref[...] = jnp.zeros_like(o_ref)


def moe_ffn(xa, xb, tile_expert, n_valid, wg, wu, wd, j):
    R = 2 * xa.shape[0]
    F = wg.shape[-1]
    bm, tf = 2 * MOE_TM, MOE_TF
    nf = F // tf
    nb = R // bm
    nhb = nb // 2
    n_vis = nb + N_EXPERTS - 1
    blocks = jnp.arange(nb, dtype=jnp.int32)
    t0, t1 = tile_expert[0::2], tile_expert[1::2]
    valid0 = 2 * blocks < n_valid[0]
    valid1 = 2 * blocks + 1 < n_valid[0]
    two = jnp.logical_and(valid1, t0 != t1)
    cnt = jnp.where(valid0, 1 + two.astype(jnp.int32), 0)
    cum = jnp.cumsum(cnt)
    nv = cum[-1:].astype(jnp.int32)
    v_eff = jnp.minimum(jnp.arange(n_vis, dtype=jnp.int32), nv[0] - 1)
    vb = jnp.sum((v_eff[:, None] >= cum[None, :]).astype(jnp.int32), axis=1)
    pick = lambda a: jnp.sum(jnp.where(vb[:, None] == blocks[None, :], a[None, :].astype(jnp.int32), 0), axis=1)
    k = v_eff - pick(cum - cnt)
    two_v, t0_v, t1_v, valid1_v = pick(two), pick(t0), pick(t1), pick(valid1)
    vm = jnp.where(two_v == 1, jnp.where(k == 0, MOE_LO, MOE_HI),
                   jnp.where(valid1_v == 1, MOE_FULL, MOE_LO_ZERO_HI))
    ve = jnp.where(jnp.logical_and(two_v == 1, k == 1), t1_v, t0_v).astype(jnp.int32)
    spare = jnp.arange(n_vis, dtype=jnp.int32) - nv[0]
    n_used = jnp.sum(valid0.astype(jnp.int32))
    fill = jnp.logical_and(spare >= 0, n_used + spare < nb)
    idle = jnp.logical_and(spare >= 0, jnp.logical_not(fill))
    vb = jnp.where(fill, n_used + spare, jnp.where(jnp.logical_and(idle, n_used < nb), nb - 1, vb))
    vm = jnp.where(fill, MOE_ZERO, jnp.where(idle, MOE_IDLE, vm)).astype(jnp.int32)

    def fidx(v, f, nv):
        return jnp.where(v < nv[0], f, nf - 1)

    grid_spec = pltpu.PrefetchScalarGridSpec(
        num_scalar_prefetch=4,
        grid=(n_vis, nf),
        in_specs=[pl.BlockSpec((bm, D), lambda v, f, vb, ve, vm, nv: (jnp.minimum(vb[v], nhb - 1), 0)),
                  pl.BlockSpec((bm, D), lambda v, f, vb, ve, vm, nv: (jnp.maximum(vb[v] - nhb, 0), 0)),
                  pl.BlockSpec((None, None, D, tf), lambda v, f, vb, ve, vm, nv: (j, ve[v], 0, fidx(v, f, nv))),
                  pl.BlockSpec((None, None, D, tf), lambda v, f, vb, ve, vm, nv: (j, ve[v], 0, fidx(v, f, nv))),
                  pl.BlockSpec((None, None, tf, D), lambda v, f, vb, ve, vm, nv: (j, ve[v], fidx(v, f, nv), 0))],
        out_specs=pl.BlockSpec((bm, D), lambda v, f, vb, ve, vm, nv: (vb[v], 0)),
        scratch_shapes=[pltpu.VMEM((bm, D), BF)],
    )
    return pl.pallas_call(
        functools.partial(_moe_ffn_kernel, half_blocks=nhb),
        grid_spec=grid_spec,
        out_shape=jax.ShapeDtypeStruct((R, D), F32),
        compiler_params=_cp("arbitrary", "arbitrary"),
        name="moe_ffn",
    )(vb.astype(jnp.int32), ve, vm, nv, xa, xb, wg, wu, wd)


def _combine_kernel(x_ref, mod_ref, ya_ref, yb_ref, w_ref, fn_ref, o_ref, *, final):
    y = w_ref[:, 0:1] * ya_ref[...] + w_ref[:, 1:2] * yb_ref[...]
    o = x_ref[...] + _gate(mod_ref, 1) * y
    if final:
        o = _rms(o) * fn_ref[...]
    o_ref[...] = o


def moe_combine(x, mod, ya, yb, w, fn, L, final, row_off):
    M = x.shape[0]
    tm = ROW_TM
    off = row_off // tm
    row = pl.BlockSpec((tm, D), lambda i: (i, 0))
    row_o = lambda n: pl.BlockSpec((tm, n), lambda i: (i + off, 0))
    return pl.pallas_call(
        functools.partial(_combine_kernel, final=final),
        grid=(M // tm,),
        in_specs=[row, _mod_spec(tm, L if mod.shape[0] > 1 else M), row_o(D), row_o(D), row_o(LANES),
                  _const_spec(fn)],
        out_specs=row,
        out_shape=jax.ShapeDtypeStruct((M, D), F32),
        compiler_params=_cp("arbitrary"),
        name="moe_combine",
    )(x, mod, ya, yb, w, fn)


def moe_layer(xp, xs, mod3, wr, wg, wu, wd, j, fn, Lp, Ls, final):
    Mp = xp.shape[0]
    xm, idx, wts, cnt = route(xp, xs, mod3, wr, Ls)
    T = xm.shape[0]
    tm = MOE_TM
    R = 2 * T + N_EXPERTS * tm
    counts = cnt[0, :N_EXPERTS].astype(jnp.int32)
    tiles_per = (counts + tm - 1) // tm
    tile_end = jnp.cumsum(tiles_per)
    offs = (tile_end - tiles_per) * tm
    experts = jnp.arange(N_EXPERTS, dtype=jnp.int32)
    dest = jnp.sum(jnp.where(idx[:, 0:2, None] == experts, offs, 0), axis=-1) + idx[:, 2:4]
    tok = jnp.arange(2 * T, dtype=jnp.int32) // 2
    src = (jnp.arange(R, dtype=jnp.int32) % T).at[dest.reshape(-1)].set(
        tok, unique_indices=True, mode="promise_in_bounds")
    n_tiles = R // tm
    n_valid = tile_end[-1:].astype(jnp.int32)
    t_ids = jnp.arange(n_tiles, dtype=jnp.int32)
    te = jnp.sum((t_ids[:, None] >= tile_end[None, :]).astype(jnp.int32), axis=1)
    te_last = jnp.sum((n_valid - 1 >= tile_end).astype(jnp.int32))
    te = jnp.where(t_ids < n_valid[0], te, te_last).astype(jnp.int32)
    xa = xm.at[src[:R // 2]].get(mode="promise_in_bounds")
    xb = xm.at[src[R // 2:]].get(mode="promise_in_bounds")
    y_sorted = moe_ffn(xa, xb, te, n_valid, wg, wu, wd, j)
    ya = y_sorted.at[dest[:, 0]].get(mode="promise_in_bounds")
    yb = y_sorted.at[dest[:, 1]].get(mode="promise_in_bounds")
    xp_new = moe_combine(xp, mod3[0:1], ya, yb, wts, fn, Lp, final, 0)
    xs_new = moe_combine(xs, mod3[1:], ya, yb, wts, fn, Ls, final, Mp)
    return xp_new, xs_new


def _rope_partner_perm(n):
    c = np.arange(n)
    return np.where((c % 32) < 16, c + 16, c - 16)


def _rope_tables(L):
    nf = 16
    inv = ROPE_THETA ** (-np.arange(nf, dtype=np.float32) / nf)
    pos = np.arange(L)
    ang = np.stack([(pos // GRID_W).astype(np.float32), (pos % GRID_W).astype(np.float32)],
                   axis=-1)[:, :, None] * inv
    cos = jnp.cos(jnp.asarray(ang, F32))
    sin = jnp.sin(jnp.asarray(ang, F32))
    c64 = jnp.stack([cos, cos], axis=2).reshape(L, 64)
    s64 = jnp.stack([-sin, sin], axis=2).reshape(L, 64)
    return c64, s64


def kernel(x_prompt, x_sample, state_l0_gla, cache_l1_k, cache_l1_v, cache_l3_ckv, cache_l3_krope, c, c_ctx, w_mod, b_mod, gla_wq, gla_wk, gla_wv, gla_wg, gla_wgk1, gla_wgk2, gla_bgk, gla_norm, gla_wo, swa_wqkv, swa_sink, swa_wo, conv_win, conv_w, conv_wout, mla_wdq, mla_qnorm, mla_wuq, mla_wdkv, mla_kvnorm, mla_wukv, mla_wo, dense_w_gate, dense_w_up, dense_w_down, moe_router, moe_w_gate, moe_w_up, moe_w_down, final_norm):
    Bp, Lp, _ = x_prompt.shape
    Bs, Ls, _ = x_sample.shape
    P = cache_l1_k.shape[1]
    xp = x_prompt.reshape(Bp * Lp, D)
    xs = x_sample.reshape(Bs * Ls, D)

    cvec = jnp.zeros((8, D), F32).at[0].set(c_ctx).at[1:1 + Bs].set(c)
    mod_all = adaln_all(cvec, w_mod, b_mod).reshape(w_mod.shape[0], 8, 6, D)

    wlr = jnp.zeros((D, 128), F32).at[:, :GLA_RANK].set(gla_wgk1[0]).at[:, GLA_RANK:2 * GLA_RANK].set(gla_wgk1[1])
    wgk2 = jnp.zeros((2, 128, GLA_H * GLA_DK), F32)
    wgk2 = wgk2.at[0, :GLA_RANK].set(gla_wgk2[0]).at[1, GLA_RANK:2 * GLA_RANK].set(gla_wgk2[1])
    gla_w = dict(wp=jnp.concatenate([gla_wq, gla_wk, gla_wv, gla_wg], axis=1).astype(BF),
                 wlr=wlr.astype(BF), wgk2=wgk2.astype(BF), bgk=gla_bgk,
                 gnorm=gla_norm.reshape(1, GLA_DV), wo=gla_wo.astype(BF))
    modp, mods = mod_all[0, 0:1], mod_all[0, 1:1 + Bs]
    xp, new_state = gla_mixer(xp, modp, gla_w, Lp, None)
    xs, _ = gla_mixer(xs, mods, gla_w, Ls, state_l0_gla)
    xp = dense_ffn(xp, modp, dense_w_gate, dense_w_up, dense_w_down, 0, Lp)
    xs = dense_ffn(xs, mods, dense_w_gate, dense_w_up, dense_w_down, 0, Ls)

    c64, s64 = _rope_tables(Ls)
    cos128 = jnp.concatenate([c64, c64], axis=1)
    sin128 = jnp.concatenate([s64, s64], axis=1)
    perm = _rope_partner_perm(SWA_NQ + SWA_NK)
    swa_w = dict(wqkv=swa_wqkv.astype(BF), wsw=swa_wqkv[:, perm].astype(BF), wo=swa_wo.astype(BF),
                 sink=swa_sink)
    modp, mods = mod_all[1, 0:1], mod_all[1, 1:1 + Bs]
    xp, k1, v1 = swa_ctx_mixer(xp, modp, swa_w, Lp)
    xs = swa_lat_mixer(xs, mods, swa_w, Ls, cache_l1_k.reshape(Bs, P, SWA_NK),
                       cache_l1_v.reshape(Bs, P, SWA_NK), cos128, sin128)
    wr = jnp.zeros((moe_router.shape[0], D, LANES), F32).at[:, :, :N_EXPERTS].set(moe_router).astype(BF)
    xp, xs = moe_layer(xp, xs, mod_all[1, 0:1 + Bs], wr[0], moe_w_gate, moe_w_up, moe_w_down, 0,
                       final_norm.reshape(1, D), Lp, Ls, False)

    conv_wts = dict(win=conv_win.astype(BF), cw=conv_w, wout=conv_wout.astype(BF))
    modp, mods = mod_all[2, 0:1], mod_all[2, 1:1 + Bs]
    xp = conv_mixer(xp, modp, conv_wts, Lp)
    xs = conv_mixer(xs, mods, conv_wts, Ls)
    xp = dense_ffn(xp, modp, dense_w_gate, dense_w_up, dense_w_down, 1, Lp)
    xs = dense_ffn(xs, mods, dense_w_gate, dense_w_up, dense_w_down, 1, Ls)

    wuq3 = mla_wuq.reshape(Q_LORA, MLA_H, QK_NOPE + QK_ROPE)
    zpad = jnp.zeros((Q_LORA, MLA_H, MLA_QW - QK_NOPE - QK_ROPE), F32)
    wuq = jnp.concatenate([wuq3, zpad], axis=2).reshape(Q_LORA, MLA_H * MLA_QW)
    rperm = _rope_partner_perm(QK_ROPE)
    wuq_sw = jnp.concatenate([jnp.zeros((Q_LORA, MLA_H, QK_NOPE), F32), wuq3[:, :, QK_NOPE:][:, :, rperm], zpad],
                             axis=2).reshape(Q_LORA, MLA_H * MLA_QW)
    kpad = jnp.zeros((D, 128 - QK_ROPE), F32)
    wdkv = jnp.concatenate([mla_wdkv, kpad], axis=1)
    wdkv_sw = jnp.concatenate([mla_wdkv[:, KV_LORA:][:, rperm], kpad], axis=1)
    wukv3 = mla_wukv.reshape(KV_LORA, MLA_H, QK_NOPE + V_HEAD)
    wukv = jnp.concatenate([wukv3[:, :, :QK_NOPE].reshape(KV_LORA, MLA_H * QK_NOPE),
                            wukv3[:, :, QK_NOPE:].reshape(KV_LORA, MLA_H * V_HEAD)], axis=1)
    mla_w = dict(wdq=mla_wdq.astype(BF), qnorm=mla_qnorm.reshape(1, Q_LORA), wuq=wuq.astype(BF),
                 wuqsw=wuq_sw.astype(BF), wdkv=wdkv.astype(BF), wdkvsw=wdkv_sw.astype(BF),
                 kvnorm=mla_kvnorm.reshape(1, KV_LORA), wukv=wukv.astype(BF), wo=mla_wo.astype(BF))
    ones = jnp.ones((Ls, QK_NOPE), F32)
    z128 = jnp.zeros((Ls, QK_NOPE), F32)
    z64 = jnp.zeros((Ls, 64), F32)
    tabs = [jnp.concatenate([ones, c64, z64], axis=1), jnp.concatenate([z128, s64, z64], axis=1),
            jnp.concatenate([c64, z64], axis=1), jnp.concatenate([s64, z64], axis=1)]
    modp, mods = mod_all[3, 0:1], mod_all[3, 1:1 + Bs]
    xp, ckv3, kr3 = mla_ctx_mixer(xp, modp, mla_w, Lp)
    kr_c128 = jnp.concatenate([cache_l3_krope, jnp.zeros((Bs, P, 128 - QK_ROPE), F32)], axis=2)
    xs = mla_lat_mixer(xs, mods, mla_w, Ls, cache_l3_ckv, kr_c128, tabs)
    xp, xs = moe_layer(xp, xs, mod_all[3, 0:1 + Bs], wr[1], moe_w_gate, moe_w_up, moe_w_down, 1,
                       final_norm.reshape(1, D), Lp, Ls, True)

    return (xp.reshape(Bp, Lp, D), xs.reshape(Bs, Ls, D), new_state,
            k1.reshape(Bp, SWA_KVH, SWA_HD, Lp).transpose(0, 3, 1, 2),
            v1.reshape(Bp, SWA_KVH, SWA_HD, Lp).transpose(0, 3, 1, 2),
            ckv3.reshape(Bp, Lp, KV_LORA), kr3.transpose(0, 2, 1))
```

```python
import functools

import numpy as np
import jax
import jax.numpy as jnp
from jax import lax
from jax.experimental import pallas as pl
from jax.experimental.pallas import tpu as pltpu

BF = jnp.bfloat16
F32 = jnp.float32

D = 1024
EPS = 1e-6
NEG = -1e30
ROPE_THETA = 10000.0
GRID_W = 64

GLA_H, GLA_DK, GLA_DV, GLA_RANK, GLA_CHUNK = 4, 128, 256, 16, 64
GLA_SCALE = GLA_DK ** -0.5
GLA_INV_NORMALIZER = 1.0 / 16.0

SWA_H, SWA_KVH, SWA_HD, SWA_WINDOW, SWA_BLOCK = 16, 4, 64, 128, 256
SWA_SCALE = SWA_HD ** -0.5
SWA_NQ = SWA_H * SWA_HD
SWA_NK = SWA_KVH * SWA_HD

MLA_H, Q_LORA, KV_LORA, QK_NOPE, QK_ROPE, V_HEAD = 8, 384, 256, 128, 64, 128
MLA_SCALE = (QK_NOPE + QK_ROPE) ** -0.5
MLA_QW = 256

N_EXPERTS = 8
MOE_TM = 512
MOE_TF = 512
MOE_FULL, MOE_LO, MOE_HI, MOE_LO_ZERO_HI, MOE_ZERO, MOE_IDLE = range(6)
ROUTE_TM = 1024
DENSE_TM = 1024
DENSE_TF = 256

LANES = 128
SUBLANES = 8
V7X_VMEM_BYTES = 64 * 1024 * 1024
VMEM_LIMIT = V7X_VMEM_BYTES * 7 // 8
ROW_TM = 512


def _cp(*sem):
    return pltpu.CompilerParams(dimension_semantics=sem, vmem_limit_bytes=VMEM_LIMIT)


def _mm(a, b):
    return jnp.dot(a, b, preferred_element_type=F32)


def _mm_nt(a, b):
    return lax.dot_general(a, b, (((1,), (1,)), ((), ())), preferred_element_type=F32)


def _mm_tn(a, b):
    return lax.dot_general(a, b, (((0,), (0,)), ((), ())), preferred_element_type=F32)


def _rms(x):
    return x * lax.rsqrt(jnp.mean(x * x, axis=-1, keepdims=True) + EPS)


def _silu(x):
    return x * (1.0 / (1.0 + jnp.exp(-x)))


def _modulate(x, mod_ref, j):
    shift = mod_ref[0, 3 * j:3 * j + 1, :]
    scale = mod_ref[0, 3 * j + 1:3 * j + 2, :]
    return _rms(x) * (1.0 + scale) + shift


def _gate(mod_ref, j):
    return mod_ref[0, 3 * j + 2:3 * j + 3, :]


def _const_spec(a):
    nd = a.ndim
    return pl.BlockSpec(a.shape, lambda *_: (0,) * nd)


def _mod_spec(tm, rows_per_mod):
    return pl.BlockSpec((1, 6, D), lambda i, *_: ((i * tm) // rows_per_mod, 0, 0))


def _adaln_kernel(c_ref, w_ref, b_ref, o_ref):
    s = _silu(c_ref[...]).astype(BF)
    o_ref[0] = _mm(s, w_ref[0].astype(BF)) + b_ref[0]


def adaln_all(cvec8, w_mod, b_mod):
    nl, d, n = w_mod.shape
    tn = 1536
    return pl.pallas_call(
        _adaln_kernel,
        grid=(nl, n // tn),
        in_specs=[pl.BlockSpec((8, d), lambda l, j: (0, 0)),
                  pl.BlockSpec((1, d, tn), lambda l, j: (l, 0, j)),
                  pl.BlockSpec((1, 1, tn), lambda l, j: (l, 0, j))],
        out_specs=pl.BlockSpec((1, 8, tn), lambda l, j: (l, 0, j)),
        out_shape=jax.ShapeDtypeStruct((nl, 8, n), F32),
        compiler_params=_cp("arbitrary", "arbitrary"),
        name="adaln",
    )(cvec8, w_mod, b_mod.reshape(nl, 1, n))


def _split3(x):
    hi = x.astype(BF)
    rem = x - hi.astype(F32)
    mid = rem.astype(BF)
    lo = (rem - mid.astype(F32)).astype(BF)
    return hi, mid, lo


def _mm3(a, pieces):
    return _mm(a, pieces[0]) + _mm(a, pieces[1]) + _mm(a, pieces[2])


def _gla_pre_kernel(x_ref, mod_ref, wp_ref, wlr_ref, wgk2_ref, bgk_ref,
                    qt_ref, kt_ref, kd_ref, v_ref, g_ref, dec_ref):
    tm = x_ref.shape[0]
    C = GLA_CHUNK
    G = 256
    h = _modulate(x_ref[...], mod_ref, 0).astype(BF)
    nk = GLA_H * GLA_DK
    nv = GLA_H * GLA_DV
    q = _mm(h, wp_ref[:, 0:nk]) * GLA_SCALE
    k = _mm(h, wp_ref[:, nk:2 * nk])
    v_ref[...] = _mm(h, wp_ref[:, 2 * nk:2 * nk + nv]).astype(BF)
    g_ref[...] = _mm(h, wp_ref[:, 2 * nk + nv:2 * nk + 2 * nv])
    lr = _mm(h, wlr_ref[...]).astype(BF)
    ii = lax.broadcasted_iota(jnp.int32, (G, G), 0)
    jj = lax.broadcasted_iota(jnp.int32, (G, G), 1)
    same = (ii // C) == (jj // C)
    ci = lax.broadcasted_iota(jnp.int32, (tm // C, tm), 0)
    cj = lax.broadcasted_iota(jnp.int32, (tm // C, tm), 1)
    sel = jnp.where((cj // C) == ci, 1.0, 0.0).astype(BF)
    for r in range(2):
        z = _mm(lr, wgk2_ref[r]) + bgk_ref[r:r + 1, :]
        gk = (jnp.minimum(z, 0.0) - jnp.log1p(jnp.exp(-jnp.abs(z)))) * GLA_INV_NORMALIZER
        pieces = _split3(gk)
        tri = jnp.where(jnp.logical_and(same, (jj <= ii) if r == 0 else (jj >= ii)), 1.0, 0.0).astype(BF)
        dec_ref[r] = jnp.exp(_mm3(sel, pieces))
        for gi in range(tm // G):
            rows = slice(gi * G, (gi + 1) * G)
            pg = tuple(p[rows] for p in pieces)
            b = _mm3(tri, pg)
            b3 = b.reshape(G // C, C, nk)
            edge = b3[:, C - 1:C, :] if r == 0 else b3[:, 0:1, :]
            tot = jnp.broadcast_to(edge, b3.shape).reshape(G, nk)
            qt_ref[r, rows, :] = (q[rows] * jnp.exp(b)).astype(BF)
            kt_ref[r, rows, :] = (k[rows] * jnp.exp(-b)).astype(BF)
            kd_ref[r, rows, :] = (k[rows] * jnp.exp(tot - b)).astype(BF)


def _gla_scan_kernel(*refs, L, S, has_h0, emit_state):
    qt_ref, kt_ref, kd_ref, v_ref, g_ref, dec_ref, gn_ref = refs[:7]
    pos = 7
    h0_ref = None
    if has_h0:
        h0_ref = refs[pos]
        pos += 1
    y_ref = refs[pos]
    pos += 1
    st_ref = None
    if emit_state:
        st_ref = refs[pos]
        pos += 1
    st_s, of_s, ob_s = refs[pos], refs[pos + 1], refs[pos + 2]

    C = GLA_CHUNK
    n_chunks = L // C
    ii = lax.broadcasted_iota(jnp.int32, (C, C), 0)
    jj = lax.broadcasted_iota(jnp.int32, (C, C), 1)
    keeps = ((jj <= ii), (jj >= ii))
    chains = [(sq, r, hd) for sq in range(S) for r in range(2) for hd in range(GLA_H)]
    slot = lambda sq, r, hd: (sq * 2 + r) * GLA_H + hd
    for sq, r, hd in chains:
        if has_h0:
            st_s[slot(sq, r, hd)] = h0_ref[sq, r, hd].T
        else:
            st_s[slot(sq, r, hd)] = jnp.zeros((GLA_DV, GLA_DK), F32)

    def step(i):
        for sq, r, hd in chains:
            n = i if r == 0 else n_chunks - 1 - i
            start = sq * L + n * C
            if not isinstance(start, int):
                start = pl.multiple_of(start, C)
            rows = pl.ds(start, C)
            o_s = of_s if r == 0 else ob_s
            kc = slice(hd * GLA_DK, (hd + 1) * GLA_DK)
            vc = slice(hd * GLA_DV, (hd + 1) * GLA_DV)
            qt = qt_ref[r, rows, kc]
            v = v_ref[rows, vc]
            att = jnp.where(keeps[r], _mm_nt(qt, kt_ref[r, rows, kc]), 0.0).astype(BF)
            st = st_s[slot(sq, r, hd)]
            o_s[rows, vc] = _mm_nt(qt, st.astype(BF)) + _mm(att, v)
            st_s[slot(sq, r, hd)] = st * dec_ref[r, sq, n, :, kc] + _mm_tn(v, kd_ref[r, rows, kc])

    if n_chunks <= 4:
        for i in range(n_chunks):
            step(i)
    else:
        def body(i, carry):
            step(i)
            return carry
        lax.fori_loop(0, n_chunks, body, 0)

    RC = 256
    for c in range(S * L // RC):
        rows = slice(c * RC, (c + 1) * RC)
        for hd in range(GLA_H):
            vc = slice(hd * GLA_DV, (hd + 1) * GLA_DV)
            y = _rms(of_s[rows, vc] + ob_s[rows, vc]) * gn_ref[...]
            y_ref[rows, vc] = (y * _silu(g_ref[rows, vc])).astype(BF)
    if emit_state:
        for sq, r, hd in chains:
            st_ref[sq, r, hd] = st_s[slot(sq, r, hd)].T


def _residual_out_kernel(y_ref, x_ref, mod_ref, w_ref, o_ref, *, gate_j):
    o_ref[...] = x_ref[...] + _gate(mod_ref, gate_j) * _mm(y_ref[...], w_ref[...])


def gla_mixer(x, mod, w, L, h0):
    M = x.shape[0]
    B = M // L
    tm = ROW_TM
    nk, nv = GLA_H * GLA_DK, GLA_H * GLA_DV
    row = lambda n: pl.BlockSpec((tm, n), lambda i: (i, 0))
    dirs = lambda: pl.BlockSpec((2, tm, nk), lambda i: (0, i, 0))
    n_chunks = L // GLA_CHUNK
    qt, kt, kd, v, g, dec = pl.pallas_call(
        _gla_pre_kernel,
        grid=(M // tm,),
        in_specs=[row(D), _mod_spec(tm, L if mod.shape[0] > 1 else M),
                  _const_spec(w["wp"]), _const_spec(w["wlr"]), _const_spec(w["wgk2"]),
                  _const_spec(w["bgk"])],
        out_specs=[dirs(), dirs(), dirs(), row(nv), row(nv),
                   pl.BlockSpec((2, tm // GLA_CHUNK, nk), lambda i: (0, i, 0))],
        out_shape=[jax.ShapeDtypeStruct((2, M, nk), BF), jax.ShapeDtypeStruct((2, M, nk), BF),
                   jax.ShapeDtypeStruct((2, M, nk), BF),
                   jax.ShapeDtypeStruct((M, nv), BF), jax.ShapeDtypeStruct((M, nv), F32),
                   jax.ShapeDtypeStruct((2, M // GLA_CHUNK, nk), F32)],
        compiler_params=_cp("arbitrary"),
        name="gla_pre",
    )(x, mod, w["wp"], w["wlr"], w["wgk2"], w["bgk"])
    dec = dec.reshape(2, B, n_chunks, 1, nk)

    has_h0 = h0 is not None
    emit_state = not has_h0
    S = max(s for s in (1, 2, 4) if B % s == 0 and s * L <= 1024)
    SL = S * L
    dir_spec = lambda: pl.BlockSpec((2, SL, nk), lambda b: (0, b, 0))
    in_specs = [dir_spec(), dir_spec(), dir_spec(),
                pl.BlockSpec((SL, nv), lambda b: (b, 0)),
                pl.BlockSpec((SL, nv), lambda b: (b, 0)),
                pl.BlockSpec((2, S, n_chunks, 1, nk), lambda b: (0, b, 0, 0, 0)),
                pl.BlockSpec((1, GLA_DV), lambda b: (0, 0))]
    args = [qt, kt, kd, v, g, dec, w["gnorm"]]
    st_spec = pl.BlockSpec((S, 2, GLA_H, GLA_DK, GLA_DV), lambda b: (b, 0, 0, 0, 0))
    if has_h0:
        in_specs.append(st_spec)
        args.append(h0)
    out_specs = [pl.BlockSpec((SL, nv), lambda b: (b, 0))]
    out_shape = [jax.ShapeDtypeStruct((M, nv), BF)]
    if emit_state:
        out_specs.append(st_spec)
        out_shape.append(jax.ShapeDtypeStruct((B, 2, GLA_H, GLA_DK, GLA_DV), F32))
    res = pl.pallas_call(
        functools.partial(_gla_scan_kernel, L=L, S=S, has_h0=has_h0, emit_state=emit_state),
        grid=(B // S,),
        in_specs=in_specs,
        out_specs=out_specs,
        out_shape=out_shape,
        scratch_shapes=[pltpu.VMEM((S * 2 * GLA_H, GLA_DV, GLA_DK), F32), pltpu.VMEM((SL, nv), F32),
                        pltpu.VMEM((SL, nv), F32)],
        compiler_params=_cp("arbitrary"),
        name="gla_scan",
    )(*args)
    y = res[0]
    state = res[1] if emit_state else None

    x_new = pl.pallas_call(
        functools.partial(_residual_out_kernel, gate_j=0),
        grid=(M // tm,),
        in_specs=[row(nv), row(D), _mod_spec(tm, L if mod.shape[0] > 1 else M), _const_spec(w["wo"])],
        out_specs=row(D),
        out_shape=jax.ShapeDtypeStruct((M, D), F32),
        compiler_params=_cp("arbitrary"),
        name="gla_post",
    )(y, x, mod, w["wo"])
    return x_new, state


def _softmax_sink_heads(q_of, k_of, v_of, sink_ref, o_s):
    for hq in range(SWA_H):
        kh = hq // (SWA_H // SWA_KVH)
        s = _mm_nt(q_of(hq), k_of(kh))
        sink = sink_ref[hq]
        m = jnp.maximum(jnp.max(s, axis=-1, keepdims=True), sink)
        e = jnp.exp(s - m)
        p = e / (jnp.sum(e, axis=-1, keepdims=True) + jnp.exp(sink - m))
        o_s[:, hq * SWA_HD:(hq + 1) * SWA_HD] = _mm(p.astype(BF), v_of(kh)).astype(BF)


def _swa_ctx_kernel(sink_ref, x_ref, mod_ref, wqkv_ref, wo_ref, o_ref, k_out, v_out, o_s):
    x = x_ref[...]
    h = _modulate(x, mod_ref, 0).astype(BF)
    qkv = _mm(h, wqkv_ref[...])
    k_out[0] = qkv[:, SWA_NQ:SWA_NQ + SWA_NK].T
    v_out[0] = qkv[:, SWA_NQ + SWA_NK:].T
    q_of = lambda hq: (qkv[:, hq * SWA_HD:(hq + 1) * SWA_HD] * SWA_SCALE).astype(BF)
    k_of = lambda kh: qkv[:, SWA_NQ + kh * SWA_HD:SWA_NQ + (kh + 1) * SWA_HD].astype(BF)
    v_of = lambda kh: qkv[:, SWA_NQ + SWA_NK + kh * SWA_HD:SWA_NQ + SWA_NK + (kh + 1) * SWA_HD].astype(BF)
    _softmax_sink_heads(q_of, k_of, v_of, sink_ref, o_s)
    o_ref[...] = x + _gate(mod_ref, 0) * _mm(o_s[...], wo_ref[...])


def swa_ctx_mixer(x, mod, w, L):
    M = x.shape[0]
    row = lambda n: pl.BlockSpec((L, n), lambda i: (i, 0))
    return pl.pallas_call(
        _swa_ctx_kernel,
        grid=(M // L,),
        in_specs=[pl.BlockSpec(memory_space=pltpu.SMEM), row(D), _mod_spec(L, M),
                  _const_spec(w["wqkv"]), _const_spec(w["wo"])],
        out_specs=[row(D), pl.BlockSpec((1, SWA_NK, L), lambda i: (i, 0, 0)),
                   pl.BlockSpec((1, SWA_NK, L), lambda i: (i, 0, 0))],
        out_shape=[jax.ShapeDtypeStruct((M, D), F32), jax.ShapeDtypeStruct((M // L, SWA_NK, L), F32),
                   jax.ShapeDtypeStruct((M // L, SWA_NK, L), F32)],
        scratch_shapes=[pltpu.VMEM((L, SWA_NQ), BF)],
        compiler_params=_cp("arbitrary"),
        name="swa_ctx",
    )(w["sink"], x, mod, w["wqkv"], w["wo"])


def _swa_lat_kernel(sink_ref, x_ref, mod_ref, wqkv_ref, wsw_ref, cos_ref, sin_ref, kc_ref, vc_ref,
                    wo_ref, o_ref, q_s, k_s, v_s, o_s, *, L):
    n = pl.program_id(1)
    RC = 256

    @pl.when(n == 0)
    def _():
        for c in range(L // RC):
            rows = slice(c * RC, (c + 1) * RC)
            h = _modulate(x_ref[rows, :], mod_ref, 0).astype(BF)
            qkv = _mm(h, wqkv_ref[...])
            sw = _mm(h, wsw_ref[...])
            cos = cos_ref[rows, :]
            sin = sin_ref[rows, :]
            cq = jnp.concatenate([cos] * (SWA_NQ // 128), axis=1)
            sq = jnp.concatenate([sin] * (SWA_NQ // 128), axis=1)
            ck = jnp.concatenate([cos] * (SWA_NK // 128), axis=1)
            sk = jnp.concatenate([sin] * (SWA_NK // 128), axis=1)
            q = (qkv[:, :SWA_NQ] * SWA_SCALE) * cq + (sw[:, :SWA_NQ] * SWA_SCALE) * sq
            q_s[rows, :] = q.astype(BF)
            k = qkv[:, SWA_NQ:SWA_NQ + SWA_NK] * ck + sw[:, SWA_NQ:SWA_NQ + SWA_NK] * sk
            k_s[rows, :] = k.astype(BF)
            v_s[rows, :] = qkv[:, SWA_NQ + SWA_NK:].astype(BF)

    QB = SWA_BLOCK
    KW = QB + 2 * SWA_WINDOW
    r0 = pl.multiple_of(n * QB, QB)
    ws = pl.multiple_of(jnp.clip(n * QB - SWA_WINDOW, 0, L - KW), SWA_WINDOW)
    qpos = r0 + lax.broadcasted_iota(jnp.int32, (QB, KW), 0)
    kpos = ws + lax.broadcasted_iota(jnp.int32, (QB, KW), 1)
    valid = jnp.abs(kpos - qpos) <= SWA_WINDOW
    for hq in range(SWA_H):
        kh = hq // (SWA_H // SWA_KVH)
        hs = slice(kh * SWA_HD, (kh + 1) * SWA_HD)
        q = q_s[pl.ds(r0, QB), hq * SWA_HD:(hq + 1) * SWA_HD]
        s1 = jnp.where(valid, _mm_nt(q, k_s[pl.ds(ws, KW), hs]), NEG)
        s2 = _mm_nt(q, kc_ref[0, :, hs].astype(BF))
        sink = sink_ref[hq]
        m = jnp.maximum(jnp.maximum(jnp.max(s1, axis=-1, keepdims=True),
                                    jnp.max(s2, axis=-1, keepdims=True)), sink)
        e1 = jnp.exp(s1 - m)
        e2 = jnp.exp(s2 - m)
        den = (jnp.sum(e1, axis=-1, keepdims=True) + jnp.sum(e2, axis=-1, keepdims=True)
               + jnp.exp(sink - m))
        o = (_mm(e1.astype(BF), v_s[pl.ds(ws, KW), hs])
             + _mm(e2.astype(BF), vc_ref[0, :, hs].astype(BF)))
        o_s[:, hq * SWA_HD:(hq + 1) * SWA_HD] = (o * (1.0 / den)).astype(BF)
    o_ref[...] = x_ref[pl.ds(r0, QB), :] + _gate(mod_ref, 0) * _mm(o_s[...], wo_ref[...])


def swa_lat_mixer(x, mod, w, L, kc, vc, cos128, sin128):
    M = x.shape[0]
    B = M // L
    P = kc.shape[1]
    nb = L // SWA_BLOCK
    return pl.pallas_call(
        functools.partial(_swa_lat_kernel, L=L),
        grid=(B, nb),
        in_specs=[pl.BlockSpec(memory_space=pltpu.SMEM),
                  pl.BlockSpec((L, D), lambda b, n: (b, 0)),
                  pl.BlockSpec((1, 6, D), lambda b, n: (b, 0, 0)),
                  _const_spec(w["wqkv"]), _const_spec(w["wsw"]),
                  _const_spec(cos128), _const_spec(sin128),
                  pl.BlockSpec((1, P, SWA_NK), lambda b, n: (b, 0, 0)),
                  pl.BlockSpec((1, P, SWA_NK), lambda b, n: (b, 0, 0)),
                  _const_spec(w["wo"])],
        out_specs=pl.BlockSpec((SWA_BLOCK, D), lambda b, n: (b * nb + n, 0)),
        out_shape=jax.ShapeDtypeStruct((M, D), F32),
        scratch_shapes=[pltpu.VMEM((L, SWA_NQ), BF), pltpu.VMEM((L, SWA_NK), BF),
                        pltpu.VMEM((L, SWA_NK), BF), pltpu.VMEM((SWA_BLOCK, SWA_NQ), BF)],
        compiler_params=_cp("arbitrary", "arbitrary"),
        name="swa_lat",
    )(w["sink"], x, mod, w["wqkv"], w["wsw"], cos128, sin128, kc, vc, w["wo"])


def _conv_kernel(x_ref, mod_ref, win_ref, cw_ref, wout_ref, o_ref, y_s, *, L):
    x = x_ref[...]
    rows = x.shape[0]
    h = _modulate(x, mod_ref, 0).astype(BF)
    CC = 256
    row = lax.broadcasted_iota(jnp.int32, (rows, CC), 0) % L
    for c in range(D // CC):
        cols = slice(c * CC, (c + 1) * CC)
        bg = _mm(h, win_ref[:, c * CC:(c + 1) * CC])
        cg = _mm(h, win_ref[:, D + c * CC:D + (c + 1) * CC])
        u = _mm(h, win_ref[:, 2 * D + c * CC:2 * D + (c + 1) * CC])
        cu = cg * u
        prev = jnp.where(row == 0, 0.0, pltpu.roll(cu, 1, 0))
        nxt = jnp.where(row == L - 1, 0.0, pltpu.roll(cu, rows - 1, 0))
        conv = prev * cw_ref[0:1, cols] + cu * cw_ref[1:2, cols] + nxt * cw_ref[2:3, cols]
        y_s[:, cols] = (bg * conv).astype(BF)
    o_ref[...] = x + _gate(mod_ref, 0) * _mm(y_s[...], wout_ref[...])


def conv_mixer(x, mod, w, L):
    M = x.shape[0]
    shared_mod = mod.shape[0] == 1
    tm = min(M, 1024) if shared_mod else L
    row = pl.BlockSpec((tm, D), lambda i: (i, 0))
    return pl.pallas_call(
        functools.partial(_conv_kernel, L=L),
        grid=(M // tm,),
        in_specs=[row, _mod_spec(tm, M if shared_mod else L), _const_spec(w["win"]),
                  _const_spec(w["cw"]), _const_spec(w["wout"])],
        out_specs=row,
        out_shape=jax.ShapeDtypeStruct((M, D), F32),
        scratch_shapes=[pltpu.VMEM((tm, D), BF)],
        compiler_params=_cp("arbitrary"),
        name="conv_mix",
    )(x, mod, w["win"], w["cw"], w["wout"])


def _mla_ctx_kernel(x_ref, mod_ref, wdq_ref, qn_ref, wuq_ref, wdkv_ref, kvn_ref, wukv_ref, wo_ref,
                    o_ref, ckv_out, kr_out, o_s):
    x = x_ref[...]
    h = _modulate(x, mod_ref, 0).astype(BF)
    cq = (_rms(_mm(h, wdq_ref[...])) * qn_ref[...]).astype(BF)
    q = _mm(cq, wuq_ref[...]).astype(BF)
    kvc = _mm(h, wdkv_ref[...])
    ckv = _rms(kvc[:, :KV_LORA]) * kvn_ref[...]
    kr = kvc[:, KV_LORA:]
    ckv_out[...] = ckv
    kr_out[0] = kr.T[:QK_ROPE, :]
    kv = _mm(ckv.astype(BF), wukv_ref[...]).astype(BF)
    krb = kr.astype(BF)
    nn = MLA_H * QK_NOPE
    for hh in range(MLA_H):
        qh = q[:, hh * MLA_QW:(hh + 1) * MLA_QW]
        kh = jnp.concatenate([kv[:, hh * QK_NOPE:(hh + 1) * QK_NOPE], krb], axis=1)
        s = _mm_nt(qh, kh) * MLA_SCALE
        m = jnp.max(s, axis=-1, keepdims=True)
        e = jnp.exp(s - m)
        inv = 1.0 / jnp.sum(e, axis=-1, keepdims=True)
        vh = kv[:, nn + hh * V_HEAD:nn + (hh + 1) * V_HEAD]
        o_s[:, hh * V_HEAD:(hh + 1) * V_HEAD] = (_mm(e.astype(BF), vh) * inv).astype(BF)
    o_ref[...] = x + _gate(mod_ref, 0) * _mm(o_s[...], wo_ref[...])


def mla_ctx_mixer(x, mod, w, L):
    M = x.shape[0]
    row = lambda n: pl.BlockSpec((L, n), lambda i: (i, 0))
    ws = [w["wdq"], w["qnorm"], w["wuq"], w["wdkv"], w["kvnorm"], w["wukv"], w["wo"]]
    return pl.pallas_call(
        _mla_ctx_kernel,
        grid=(M // L,),
        in_specs=[row(D), _mod_spec(L, M)] + [_const_spec(a) for a in ws],
        out_specs=[row(D), row(KV_LORA), pl.BlockSpec((1, QK_ROPE, L), lambda i: (i, 0, 0))],
        out_shape=[jax.ShapeDtypeStruct((M, D), F32), jax.ShapeDtypeStruct((M, KV_LORA), F32),
                   jax.ShapeDtypeStruct((M // L, QK_ROPE, L), F32)],
        scratch_shapes=[pltpu.VMEM((L, MLA_H * V_HEAD), BF)],
        compiler_params=_cp("arbitrary"),
        name="mla_ctx",
    )(x, mod, *ws)


def _mla_lat_kernel(x_ref, mod_ref, wdq_ref, qn_ref, wuq_ref, wuqsw_ref, wdkv_ref, wdkvsw_ref,
                    kvn_ref, wukv_ref, cq_ref, sq_ref, ck_ref, sk_ref, ckvc_ref, krc_ref, wo_ref,
                    o_ref, q_s, kn_s, v_s, kr_s, knc_s, vc_s, o_s, *, L, QB):
    n = pl.program_id(1)
    RC = 256
    nn = MLA_H * QK_NOPE

    @pl.when(n == 0)
    def _():
        for c in range(L // RC):
            rows = slice(c * RC, (c + 1) * RC)
            h = _modulate(x_ref[rows, :], mod_ref, 0).astype(BF)
            cq = (_rms(_mm(h, wdq_ref[...])) * qn_ref[...]).astype(BF)
            cosq = jnp.concatenate([cq_ref[rows, :]] * MLA_H, axis=1)
            sinq = jnp.concatenate([sq_ref[rows, :]] * MLA_H, axis=1)
            q = _mm(cq, wuq_ref[...]) * cosq + _mm(cq, wuqsw_ref[...]) * sinq
            q_s[rows, :] = q.astype(BF)
            kvc = _mm(h, wdkv_ref[...])
            ksw = _mm(h, wdkvsw_ref[...])
            ckv = _rms(kvc[:, :KV_LORA]) * kvn_ref[...]
            kr_s[rows, :] = (kvc[:, KV_LORA:] * ck_ref[rows, :] + ksw * sk_ref[rows, :]).astype(BF)
            kv = _mm(ckv.astype(BF), wukv_ref[...])
            kn_s[rows, :] = kv[:, :nn].astype(BF)
            v_s[rows, :] = kv[:, nn:].astype(BF)
        kvp = _mm(ckvc_ref[0].astype(BF), wukv_ref[...])
        knc_s[...] = kvp[:, :nn].astype(BF)
        vc_s[...] = kvp[:, nn:].astype(BF)

    r0 = pl.multiple_of(n * QB, QB)
    krc = krc_ref[0].astype(BF)
    for hh in range(MLA_H):
        ns = slice(hh * QK_NOPE, (hh + 1) * QK_NOPE)
        qh = q_s[pl.ds(r0, QB), hh * MLA_QW:(hh + 1) * MLA_QW]
        k1 = jnp.concatenate([kn_s[:, ns], kr_s[...]], axis=1)
        k2 = jnp.concatenate([knc_s[:, ns], krc], axis=1)
        s1 = _mm_nt(qh, k1) * MLA_SCALE
        s2 = _mm_nt(qh, k2) * MLA_SCALE
        m = jnp.maximum(jnp.max(s1, axis=-1, keepdims=True), jnp.max(s2, axis=-1, keepdims=True))
        e1 = jnp.exp(s1 - m)
        e2 = jnp.exp(s2 - m)
        den = jnp.sum(e1, axis=-1, keepdims=True) + jnp.sum(e2, axis=-1, keepdims=True)
        vs = slice(hh * V_HEAD, (hh + 1) * V_HEAD)
        o = _mm(e1.astype(BF), v_s[:, vs]) + _mm(e2.astype(BF), vc_s[:, vs])
        o_s[:, vs] = (o * (1.0 / den)).astype(BF)
    o_ref[...] = x_ref[pl.ds(r0, QB), :] + _gate(mod_ref, 0) * _mm(o_s[...], wo_ref[...])


def mla_lat_mixer(x, mod, w, L, ckv_c, kr_c128, tabs):
    M = x.shape[0]
    B = M // L
    P = ckv_c.shape[1]
    QB = 512
    nb = L // QB
    ws1 = [w["wdq"], w["qnorm"], w["wuq"], w["wuqsw"], w["wdkv"], w["wdkvsw"], w["kvnorm"], w["wukv"]]
    nv = MLA_H * V_HEAD
    return pl.pallas_call(
        functools.partial(_mla_lat_kernel, L=L, QB=QB),
        grid=(B, nb),
        in_specs=[pl.BlockSpec((L, D), lambda b, n: (b, 0)),
                  pl.BlockSpec((1, 6, D), lambda b, n: (b, 0, 0))]
                 + [_const_spec(a) for a in ws1] + [_const_spec(a) for a in tabs]
                 + [pl.BlockSpec((1, P, KV_LORA), lambda b, n: (b, 0, 0)),
                    pl.BlockSpec((1, P, LANES), lambda b, n: (b, 0, 0)),
                    _const_spec(w["wo"])],
        out_specs=pl.BlockSpec((QB, D), lambda b, n: (b * nb + n, 0)),
        out_shape=jax.ShapeDtypeStruct((M, D), F32),
        scratch_shapes=[pltpu.VMEM((L, MLA_H * MLA_QW), BF), pltpu.VMEM((L, nv), BF),
                        pltpu.VMEM((L, nv), BF), pltpu.VMEM((L, LANES), BF),
                        pltpu.VMEM((P, nv), BF), pltpu.VMEM((P, nv), BF), pltpu.VMEM((QB, nv), BF)],
        compiler_params=_cp("arbitrary", "arbitrary"),
        name="mla_lat",
    )(x, mod, *ws1, *tabs, ckv_c, kr_c128, w["wo"])


def _swiglu_partial(xb, wg_ref, wu_ref, wd_ref):
    hg = _mm(xb, wg_ref[...].astype(BF))
    hu = _mm(xb, wu_ref[...].astype(BF))
    a = (_silu(hg) * hu).astype(BF)
    return _mm(a, wd_ref[...].astype(BF))


def _dense_ffn_kernel(x_ref, mod_ref, wga_ref, wgb_ref, wua_ref, wub_ref, wda_ref, wdb_ref, o_ref, xm_s,
                      *, odd_blocks):
    f = pl.program_id(1)
    last = pl.num_programs(1) - 1

    @pl.when(f == 0)
    def _():
        xm_s[...] = _modulate(x_ref[...], mod_ref, 1).astype(BF)
        o_ref[...] = jnp.zeros_like(o_ref)

    def pair():
        xb = xm_s[...]
        acts = []
        for wg_ref, wu_ref in ((wga_ref, wua_ref), (wgb_ref, wub_ref)):
            hg = _mm(xb, wg_ref[...].astype(BF))
            hu = _mm(xb, wu_ref[...].astype(BF))
            acts.append((_silu(hg) * hu).astype(BF))
        wd = jnp.concatenate([wda_ref[...].astype(BF), wdb_ref[...].astype(BF)], axis=0)
        return _mm(jnp.concatenate(acts, axis=1), wd)

    if odd_blocks:
        @pl.when(f < last)
        def _():
            o_ref[...] += pair()

        @pl.when(f == last)
        def _():
            o_ref[...] += _swiglu_partial(xm_s[...], wga_ref, wua_ref, wda_ref)
    else:
        o_ref[...] += pair()

    @pl.when(f == last)
    def _():
        o_ref[...] = x_ref[...] + _gate(mod_ref, 1) * o_ref[...]


def dense_ffn(x, mod, wg, wu, wd, j, L):
    M = x.shape[0]
    F = wg.shape[-1]
    tm, tf = DENSE_TM, DENSE_TF
    nfb = F // tf
    blk_a = lambda f: 2 * f
    blk_b = lambda f: jnp.minimum(2 * f + 1, nfb - 1)
    col = lambda blk: pl.BlockSpec((None, D, tf), lambda i, f: (j, 0, blk(f)))
    row = lambda blk: pl.BlockSpec((None, tf, D), lambda i, f: (j, blk(f), 0))
    return pl.pallas_call(
        functools.partial(_dense_ffn_kernel, odd_blocks=nfb % 2 == 1),
        grid=(M // tm, (nfb + 1) // 2),
        in_specs=[pl.BlockSpec((tm, D), lambda i, f: (i, 0)),
                  pl.BlockSpec((1, 6, D), lambda i, f: ((i * tm) // (L if mod.shape[0] > 1 else M), 0, 0)),
                  col(blk_a), col(blk_b), col(blk_a), col(blk_b), row(blk_a), row(blk_b)],
        out_specs=pl.BlockSpec((tm, D), lambda i, f: (i, 0)),
        out_shape=jax.ShapeDtypeStruct((M, D), F32),
        scratch_shapes=[pltpu.VMEM((tm, D), BF)],
        compiler_params=_cp("arbitrary", "arbitrary"),
        name="dense_ffn",
    )(x, mod, wg, wg, wu, wu, wd, wd)


def _router_kernel(xp_ref, xs_ref, mod_ref, wr_ref, xm_ref, idx_ref, w_ref, cnt_ref, cnt_s, *, np_tiles):
    i = pl.program_id(0)
    tm = xm_ref.shape[0]

    @pl.when(i == 0)
    def _():
        cnt_s[...] = jnp.zeros_like(cnt_s)

    x = jnp.where(i < np_tiles, xp_ref[...], xs_ref[...])
    xmod = _modulate(x, mod_ref, 1)
    xm_ref[...] = xmod
    xm = xmod.astype(BF)
    lane = lax.broadcasted_iota(jnp.int32, idx_ref.shape, 1)
    logits = jnp.where(lane < N_EXPERTS, _mm(xm, wr_ref[...]), -jnp.inf)
    m1 = jnp.max(logits, axis=-1, keepdims=True)
    i1 = jnp.min(jnp.where(logits == m1, lane, LANES), axis=-1, keepdims=True)
    rest = jnp.where(lane == i1, -jnp.inf, logits)
    m2 = jnp.max(rest, axis=-1, keepdims=True)
    i2 = jnp.min(jnp.where(rest == m2, lane, LANES), axis=-1, keepdims=True)
    e = jnp.exp(m2 - m1)
    w_ref[...] = jnp.where(lane == 0, 1.0 / (1.0 + e), e / (1.0 + e))
    oh1 = jnp.where(lane == i1, 1.0, 0.0)
    oh2 = jnp.where(lane == i2, 1.0, 0.0)
    rr = lax.broadcasted_iota(jnp.int32, (tm, tm), 0)
    cc = lax.broadcasted_iota(jnp.int32, (tm, tm), 1)
    below = jnp.where(cc < rr, 1.0, 0.0).astype(BF)
    run = cnt_s[0:1, :]
    tot1 = jnp.sum(oh1, axis=0, keepdims=True)
    tot2 = jnp.sum(oh2, axis=0, keepdims=True)
    r1 = jnp.sum(oh1 * (run + _mm(below, oh1.astype(BF))), axis=-1, keepdims=True)
    r2 = jnp.sum(oh2 * (run + tot1 + _mm(below, oh2.astype(BF))), axis=-1, keepdims=True)
    idx_ref[...] = jnp.where(lane == 0, i1, jnp.where(lane == 1, i2, jnp.where(
        lane == 2, r1.astype(jnp.int32), r2.astype(jnp.int32))))
    total = run + tot1 + tot2
    cnt_s[...] = jnp.broadcast_to(total, cnt_s.shape)
    cnt_ref[...] = jnp.broadcast_to(total, cnt_ref.shape)


def route(xp, xs, mod3, wr, Ls):
    Mp, Ms = xp.shape[0], xs.shape[0]
    T = Mp + Ms
    tm = ROUTE_TM
    npt = Mp // tm
    row = lambda n: pl.BlockSpec((tm, n), lambda i: (i, 0))
    return pl.pallas_call(
        functools.partial(_router_kernel, np_tiles=npt),
        grid=(T // tm,),
        in_specs=[pl.BlockSpec((tm, D), lambda i: (jnp.minimum(i, npt - 1), 0)),
                  pl.BlockSpec((tm, D), lambda i: (jnp.maximum(i - npt, 0), 0)),
                  pl.BlockSpec((1, 6, D), lambda i: (jnp.where(i < npt, 0, 1 + ((i - npt) * tm) // Ls), 0, 0)),
                  _const_spec(wr)],
        out_specs=[row(D), row(LANES), row(LANES), pl.BlockSpec((SUBLANES, LANES), lambda i: (0, 0))],
        out_shape=[jax.ShapeDtypeStruct((T, D), F32), jax.ShapeDtypeStruct((T, LANES), jnp.int32),
                   jax.ShapeDtypeStruct((T, LANES), F32), jax.ShapeDtypeStruct((SUBLANES, LANES), F32)],
        scratch_shapes=[pltpu.VMEM((SUBLANES, LANES), F32)],
        compiler_params=_cp("arbitrary"),
        name="moe_route",
    )(xp, xs, mod3, wr)


def _moe_ffn_kernel(vb_ref, ve_ref, vm_ref, nv_ref, xa_ref, xb_ref, wg_ref, wu_ref, wd_ref, o_ref, x_s,
                    *, half_blocks):
    v = pl.program_id(0)
    f = pl.program_id(1)
    mode = vm_ref[v]
    H = MOE_TM

    @pl.when(jnp.logical_and(v < nv_ref[0], f == 0))
    def _():
        x_s[...] = jnp.where(vb_ref[v] < half_blocks, xa_ref[...], xb_ref[...]).astype(BF)

    def run(rows):
        @pl.when(f == 0)
        def _():
            o_ref[rows, :] = jnp.zeros((rows.stop - rows.start, D), F32)

        o_ref[rows, :] += _swiglu_partial(x_s[rows, :], wg_ref, wu_ref, wd_ref)

    @pl.when(mode == MOE_FULL)
    def _():
        run(slice(0, 2 * H))

    @pl.when(jnp.logical_or(mode == MOE_LO, mode == MOE_LO_ZERO_HI))
    def _():
        run(slice(0, H))

    @pl.when(mode == MOE_HI)
    def _():
        run(slice(H, 2 * H))

    @pl.when(jnp.logical_and(mode == MOE_LO_ZERO_HI, f == 0))
    def _():
        o_ref[H:2 * H, :] = jnp.zeros((H, D), F32)

    @pl.when(jnp.logical_and(mode == MOE_ZERO, f == 0))
    def _():
        o_ref[...] = jnp.zeros_like(o_ref)


def moe_ffn(xa, xb, tile_expert, n_valid, wg, wu, wd, j):
    R = 2 * xa.shape[0]
    F = wg.shape[-1]
    bm, tf = 2 * MOE_TM, MOE_TF
    nf = F // tf
    nb = R // bm
    nhb = nb // 2
    n_vis = nb + N_EXPERTS - 1
    blocks = jnp.arange(nb, dtype=jnp.int32)
    t0, t1 = tile_expert[0::2], tile_expert[1::2]
    valid0 = 2 * blocks < n_valid[0]
    valid1 = 2 * blocks + 1 < n_valid[0]
    two = jnp.logical_and(valid1, t0 != t1)
    cnt = jnp.where(valid0, 1 + two.astype(jnp.int32), 0)
    cum = jnp.cumsum(cnt)
    nv = cum[-1:].astype(jnp.int32)
    v_eff = jnp.minimum(jnp.arange(n_vis, dtype=jnp.int32), nv[0] - 1)
    vb = jnp.sum((v_eff[:, None] >= cum[None, :]).astype(jnp.int32), axis=1)
    pick = lambda a: jnp.sum(jnp.where(vb[:, None] == blocks[None, :], a[None, :].astype(jnp.int32), 0), axis=1)
    k = v_eff - pick(cum - cnt)
    two_v, t0_v, t1_v, valid1_v = pick(two), pick(t0), pick(t1), pick(valid1)
    vm = jnp.where(two_v == 1, jnp.where(k == 0, MOE_LO, MOE_HI),
                   jnp.where(valid1_v == 1, MOE_FULL, MOE_LO_ZERO_HI))
    ve = jnp.where(jnp.logical_and(two_v == 1, k == 1), t1_v, t0_v).astype(jnp.int32)
    spare = jnp.arange(n_vis, dtype=jnp.int32) - nv[0]
    n_used = jnp.sum(valid0.astype(jnp.int32))
    fill = jnp.logical_and(spare >= 0, n_used + spare < nb)
    idle = jnp.logical_and(spare >= 0, jnp.logical_not(fill))
    vb = jnp.where(fill, n_used + spare, jnp.where(jnp.logical_and(idle, n_used < nb), nb - 1, vb))
    vm = jnp.where(fill, MOE_ZERO, jnp.where(idle, MOE_IDLE, vm)).astype(jnp.int32)

    def fidx(v, f, nv):
        return jnp.where(v < nv[0], f, nf - 1)

    grid_spec = pltpu.PrefetchScalarGridSpec(
        num_scalar_prefetch=4,
        grid=(n_vis, nf),
        in_specs=[pl.BlockSpec((bm, D), lambda v, f, vb, ve, vm, nv: (jnp.minimum(vb[v], nhb - 1), 0)),
                  pl.BlockSpec((bm, D), lambda v, f, vb, ve, vm, nv: (jnp.maximum(vb[v] - nhb, 0), 0)),
                  pl.BlockSpec((None, None, D, tf), lambda v, f, vb, ve, vm, nv: (j, ve[v], 0, fidx(v, f, nv))),
                  pl.BlockSpec((None, None, D, tf), lambda v, f, vb, ve, vm, nv: (j, ve[v], 0, fidx(v, f, nv))),
                  pl.BlockSpec((None, None, tf, D), lambda v, f, vb, ve, vm, nv: (j, ve[v], fidx(v, f, nv), 0))],
        out_specs=pl.BlockSpec((bm, D), lambda v, f, vb, ve, vm, nv: (vb[v], 0)),
        scratch_shapes=[pltpu.VMEM((bm, D), BF)],
    )
    return pl.pallas_call(
        functools.partial(_moe_ffn_kernel, half_blocks=nhb),
        grid_spec=grid_spec,
        out_shape=jax.ShapeDtypeStruct((R, D), F32),
        compiler_params=_cp("arbitrary", "arbitrary"),
        name="moe_ffn",
    )(vb.astype(jnp.int32), ve, vm, nv, xa, xb, wg, wu, wd)


def _combine_kernel(x_ref, mod_ref, ya_ref, yb_ref, w_ref, fn_ref, o_ref, *, final):
    y = w_ref[:, 0:1] * ya_ref[...] + w_ref[:, 1:2] * yb_ref[...]
    o = x_ref[...] + _gate(mod_ref, 1) * y
    if final:
        o = _rms(o) * fn_ref[...]
    o_ref[...] = o


def moe_combine(x, mod, ya, yb, w, fn, L, final, row_off):
    M = x.shape[0]
    tm = ROW_TM
    off = row_off // tm
    row = pl.BlockSpec((tm, D), lambda i: (i, 0))
    row_o = lambda n: pl.BlockSpec((tm, n), lambda i: (i + off, 0))
    return pl.pallas_call(
        functools.partial(_combine_kernel, final=final),
        grid=(M // tm,),
        in_specs=[row, _mod_spec(tm, L if mod.shape[0] > 1 else M), row_o(D), row_o(D), row_o(LANES),
                  _const_spec(fn)],
        out_specs=row,
        out_shape=jax.ShapeDtypeStruct((M, D), F32),
        compiler_params=_cp("arbitrary"),
        name="moe_combine",
    )(x, mod, ya, yb, w, fn)


def moe_layer(xp, xs, mod3, wr, wg, wu, wd, j, fn, Lp, Ls, final):
    Mp = xp.shape[0]
    xm, idx, wts, cnt = route(xp, xs, mod3, wr, Ls)
    T = xm.shape[0]
    tm = MOE_TM
    R = 2 * T + N_EXPERTS * tm
    counts = cnt[0, :N_EXPERTS].astype(jnp.int32)
    tiles_per = (counts + tm - 1) // tm
    tile_end = jnp.cumsum(tiles_per)
    offs = (tile_end - tiles_per) * tm
    experts = jnp.arange(N_EXPERTS, dtype=jnp.int32)
    dest = jnp.sum(jnp.where(idx[:, 0:2, None] == experts, offs, 0), axis=-1) + idx[:, 2:4]
    tok = jnp.arange(2 * T, dtype=jnp.int32) // 2
    src = (jnp.arange(R, dtype=jnp.int32) % T).at[dest.reshape(-1)].set(
        tok, unique_indices=True, mode="promise_in_bounds")
    n_tiles = R // tm
    n_valid = tile_end[-1:].astype(jnp.int32)
    t_ids = jnp.arange(n_tiles, dtype=jnp.int32)
    te = jnp.sum((t_ids[:, None] >= tile_end[None, :]).astype(jnp.int32), axis=1)
    te_last = jnp.sum((n_valid - 1 >= tile_end).astype(jnp.int32))
    te = jnp.where(t_ids < n_valid[0], te, te_last).astype(jnp.int32)
    xa = xm.at[src[:R // 2]].get(mode="promise_in_bounds")
    xb = xm.at[src[R // 2:]].get(mode="promise_in_bounds")
    y_sorted = moe_ffn(xa, xb, te, n_valid, wg, wu, wd, j)
    ya = y_sorted.at[dest[:, 0]].get(mode="promise_in_bounds")
    yb = y_sorted.at[dest[:, 1]].get(mode="promise_in_bounds")
    xp_new = moe_combine(xp, mod3[0:1], ya, yb, wts, fn, Lp, final, 0)
    xs_new = moe_combine(xs, mod3[1:], ya, yb, wts, fn, Ls, final, Mp)
    return xp_new, xs_new


def _rope_partner_perm(n):
    c = np.arange(n)
    return np.where((c % 32) < 16, c + 16, c - 16)


def _rope_tables(L):
    nf = 16
    inv = ROPE_THETA ** (-np.arange(nf, dtype=np.float32) / nf)
    pos = np.arange(L)
    ang = np.stack([(pos // GRID_W).astype(np.float32), (pos % GRID_W).astype(np.float32)],
                   axis=-1)[:, :, None] * inv
    cos = jnp.cos(jnp.asarray(ang, F32))
    sin = jnp.sin(jnp.asarray(ang, F32))
    c64 = jnp.stack([cos, cos], axis=2).reshape(L, 64)
    s64 = jnp.stack([-sin, sin], axis=2).reshape(L, 64)
    return c64, s64


def kernel(x_prompt, x_sample, state_l0_gla, cache_l1_k, cache_l1_v, cache_l3_ckv, cache_l3_krope, c, c_ctx, w_mod, b_mod, gla_wq, gla_wk, gla_wv, gla_wg, gla_wgk1, gla_wgk2, gla_bgk, gla_norm, gla_wo, swa_wqkv, swa_sink, swa_wo, conv_win, conv_w, conv_wout, mla_wdq, mla_qnorm, mla_wuq, mla_wdkv, mla_kvnorm, mla_wukv, mla_wo, dense_w_gate, dense_w_up, dense_w_down, moe_router, moe_w_gate, moe_w_up, moe_w_down, final_norm):
    Bp, Lp, _ = x_prompt.shape
    Bs, Ls, _ = x_sample.shape
    P = cache_l1_k.shape[1]
    xp = x_prompt.reshape(Bp * Lp, D)
    xs = x_sample.reshape(Bs * Ls, D)

    cvec = jnp.zeros((8, D), F32).at[0].set(c_ctx).at[1:1 + Bs].set(c)
    mod_all = adaln_all(cvec, w_mod, b_mod).reshape(w_mod.shape[0], 8, 6, D)

    wlr = jnp.zeros((D, 128), F32).at[:, :GLA_RANK].set(gla_wgk1[0]).at[:, GLA_RANK:2 * GLA_RANK].set(gla_wgk1[1])
    wgk2 = jnp.zeros((2, 128, GLA_H * GLA_DK), F32)
    wgk2 = wgk2.at[0, :GLA_RANK].set(gla_wgk2[0]).at[1, GLA_RANK:2 * GLA_RANK].set(gla_wgk2[1])
    gla_w = dict(wp=jnp.concatenate([gla_wq, gla_wk, gla_wv, gla_wg], axis=1).astype(BF),
                 wlr=wlr.astype(BF), wgk2=wgk2.astype(BF), bgk=gla_bgk,
                 gnorm=gla_norm.reshape(1, GLA_DV), wo=gla_wo.astype(BF))
    modp, mods = mod_all[0, 0:1], mod_all[0, 1:1 + Bs]
    xp, new_state = gla_mixer(xp, modp, gla_w, Lp, None)
    xs, _ = gla_mixer(xs, mods, gla_w, Ls, state_l0_gla)
    xp = dense_ffn(xp, modp, dense_w_gate, dense_w_up, dense_w_down, 0, Lp)
    xs = dense_ffn(xs, mods, dense_w_gate, dense_w_up, dense_w_down, 0, Ls)

    c64, s64 = _rope_tables(Ls)
    cos128 = jnp.concatenate([c64, c64], axis=1)
    sin128 = jnp.concatenate([s64, s64], axis=1)
    perm = _rope_partner_perm(SWA_NQ + SWA_NK)
    swa_w = dict(wqkv=swa_wqkv.astype(BF), wsw=swa_wqkv[:, perm].astype(BF), wo=swa_wo.astype(BF),
                 sink=swa_sink)
    modp, mods = mod_all[1, 0:1], mod_all[1, 1:1 + Bs]
    xp, k1, v1 = swa_ctx_mixer(xp, modp, swa_w, Lp)
    xs = swa_lat_mixer(xs, mods, swa_w, Ls, cache_l1_k.reshape(Bs, P, SWA_NK),
                       cache_l1_v.reshape(Bs, P, SWA_NK), cos128, sin128)
    wr = jnp.zeros((moe_router.shape[0], D, LANES), F32).at[:, :, :N_EXPERTS].set(moe_router).astype(BF)
    xp, xs = moe_layer(xp, xs, mod_all[1, 0:1 + Bs], wr[0], moe_w_gate, moe_w_up, moe_w_down, 0,
                       final_norm.reshape(1, D), Lp, Ls, False)

    conv_wts = dict(win=conv_win.astype(BF), cw=conv_w, wout=conv_wout.astype(BF))
    modp, mods = mod_all[2, 0:1], mod_all[2, 1:1 + Bs]
    xp = conv_mixer(xp, modp, conv_wts, Lp)
    xs = conv_mixer(xs, mods, conv_wts, Ls)
    xp = dense_ffn(xp, modp, dense_w_gate, dense_w_up, dense_w_down, 1, Lp)
    xs = dense_ffn(xs, mods, dense_w_gate, dense_w_up, dense_w_down, 1, Ls)

    wuq3 = mla_wuq.reshape(Q_LORA, MLA_H, QK_NOPE + QK_ROPE)
    zpad = jnp.zeros((Q_LORA, MLA_H, MLA_QW - QK_NOPE - QK_ROPE), F32)
    wuq = jnp.concatenate([wuq3, zpad], axis=2).reshape(Q_LORA, MLA_H * MLA_QW)
    rperm = _rope_partner_perm(QK_ROPE)
    wuq_sw = jnp.concatenate([jnp.zeros((Q_LORA, MLA_H, QK_NOPE), F32), wuq3[:, :, QK_NOPE:][:, :, rperm], zpad],
                             axis=2).reshape(Q_LORA, MLA_H * MLA_QW)
    kpad = jnp.zeros((D, 128 - QK_ROPE), F32)
    wdkv = jnp.concatenate([mla_wdkv, kpad], axis=1)
    wdkv_sw = jnp.concatenate([mla_wdkv[:, KV_LORA:][:, rperm], kpad], axis=1)
    wukv3 = mla_wukv.reshape(KV_LORA, MLA_H, QK_NOPE + V_HEAD)
    wukv = jnp.concatenate([wukv3[:, :, :QK_NOPE].reshape(KV_LORA, MLA_H * QK_NOPE),
                            wukv3[:, :, QK_NOPE:].reshape(KV_LORA, MLA_H * V_HEAD)], axis=1)
    mla_w = dict(wdq=mla_wdq.astype(BF), qnorm=mla_qnorm.reshape(1, Q_LORA), wuq=wuq.astype(BF),
                 wuqsw=wuq_sw.astype(BF), wdkv=wdkv.astype(BF), wdkvsw=wdkv_sw.astype(BF),
                 kvnorm=mla_kvnorm.reshape(1, KV_LORA), wukv=wukv.astype(BF), wo=mla_wo.astype(BF))
    ones = jnp.ones((Ls, QK_NOPE), F32)
    z128 = jnp.zeros((Ls, QK_NOPE), F32)
    z64 = jnp.zeros((Ls, 64), F32)
    tabs = [jnp.concatenate([ones, c64, z64], axis=1), jnp.concatenate([z128, s64, z64], axis=1),
            jnp.concatenate([c64, z64], axis=1), jnp.concatenate([s64, z64], axis=1)]
    modp, mods = mod_all[3, 0:1], mod_all[3, 1:1 + Bs]
    xp, ckv3, kr3 = mla_ctx_mixer(xp, modp, mla_w, Lp)
    kr_c128 = jnp.concatenate([cache_l3_krope, jnp.zeros((Bs, P, 128 - QK_ROPE), F32)], axis=2)
    xs = mla_lat_mixer(xs, mods, mla_w, Ls, cache_l3_ckv, kr_c128, tabs)
    xp, xs = moe_layer(xp, xs, mod_all[3, 0:1 + Bs], wr[1], moe_w_gate, moe_w_up, moe_w_down, 1,
                       final_norm.reshape(1, D), Lp, Ls, True)

    return (xp.reshape(Bp, Lp, D), xs.reshape(Bs, Ls, D), new_state,
            k1.reshape(Bp, SWA_KVH, SWA_HD, Lp).transpose(0, 3, 1, 2),
            v1.reshape(Bp, SWA_KVH, SWA_HD, Lp).transpose(0, 3, 1, 2),
            ckv3.reshape(Bp, Lp, KV_LORA), kr3.transpose(0, 2, 1))
```

```python
import functools

import numpy as np
import jax
import jax.numpy as jnp
from jax import lax
from jax.experimental import pallas as pl
from jax.experimental.pallas import tpu as pltpu

BF = jnp.bfloat16
F32 = jnp.float32

D = 1024
EPS = 1e-6
NEG = -1e30
ROPE_THETA = 10000.0
GRID_W = 64

GLA_H, GLA_DK, GLA_DV, GLA_RANK, GLA_CHUNK = 4, 128, 256, 16, 64
GLA_SCALE = GLA_DK ** -0.5
GLA_INV_NORMALIZER = 1.0 / 16.0

SWA_H, SWA_KVH, SWA_HD, SWA_WINDOW, SWA_BLOCK = 16, 4, 64, 128, 256
SWA_SCALE = SWA_HD ** -0.5
SWA_NQ = SWA_H * SWA_HD
SWA_NK = SWA_KVH * SWA_HD

MLA_H, Q_LORA, KV_LORA, QK_NOPE, QK_ROPE, V_HEAD = 8, 384, 256, 128, 64, 128
MLA_SCALE = (QK_NOPE + QK_ROPE) ** -0.5
MLA_QW = 256

N_EXPERTS = 8
MOE_TM = 512
MOE_TF = 512
MOE_FULL, MOE_LO, MOE_HI, MOE_LO_ZERO_HI, MOE_ZERO, MOE_IDLE = range(6)
ROUTE_TM = 1024
DENSE_TM = 1024
DENSE_TF = 256

LANES = 128
SUBLANES = 8
V7X_VMEM_BYTES = 64 * 1024 * 1024
VMEM_LIMIT = V7X_VMEM_BYTES * 7 // 8
ROW_TM = 512


def _cp(*sem):
    return pltpu.CompilerParams(dimension_semantics=sem, vmem_limit_bytes=VMEM_LIMIT)


def _mm(a, b):
    return jnp.dot(a, b, preferred_element_type=F32)


def _mm_nt(a, b):
    return lax.dot_general(a, b, (((1,), (1,)), ((), ())), preferred_element_type=F32)


def _mm_tn(a, b):
    return lax.dot_general(a, b, (((0,), (0,)), ((), ())), preferred_element_type=F32)


def _rms(x):
    return x * lax.rsqrt(jnp.mean(x * x, axis=-1, keepdims=True) + EPS)


def _silu(x):
    return x * (1.0 / (1.0 + jnp.exp(-x)))


def _modulate(x, mod_ref, j):
    shift = mod_ref[0, 3 * j:3 * j + 1, :]
    scale = mod_ref[0, 3 * j + 1:3 * j + 2, :]
    return _rms(x) * (1.0 + scale) + shift


def _gate(mod_ref, j):
    return mod_ref[0, 3 * j + 2:3 * j + 3, :]


def _const_spec(a):
    nd = a.ndim
    return pl.BlockSpec(a.shape, lambda *_: (0,) * nd)


def _mod_spec(tm, rows_per_mod):
    return pl.BlockSpec((1, 6, D), lambda i, *_: ((i * tm) // rows_per_mod, 0, 0))


def _adaln_kernel(c_ref, w_ref, b_ref, o_ref):
    s = _silu(c_ref[...]).astype(BF)
    o_ref[0] = _mm(s, w_ref[0].astype(BF)) + b_ref[0]


def adaln_all(cvec8, w_mod, b_mod):
    nl, d, n = w_mod.shape
    tn = 1536
    return pl.pallas_call(
        _adaln_kernel,
        grid=(nl, n // tn),
        in_specs=[pl.BlockSpec((8, d), lambda l, j: (0, 0)),
                  pl.BlockSpec((1, d, tn), lambda l, j: (l, 0, j)),
                  pl.BlockSpec((1, 1, tn), lambda l, j: (l, 0, j))],
        out_specs=pl.BlockSpec((1, 8, tn), lambda l, j: (l, 0, j)),
        out_shape=jax.ShapeDtypeStruct((nl, 8, n), F32),
        compiler_params=_cp("arbitrary", "arbitrary"),
        name="adaln",
    )(cvec8, w_mod, b_mod.reshape(nl, 1, n))


def _split3(x):
    hi = x.astype(BF)
    rem = x - hi.astype(F32)
    mid = rem.astype(BF)
    lo = (rem - mid.astype(F32)).astype(BF)
    return hi, mid, lo


def _mm3(a, pieces):
    return _mm(a, pieces[0]) + _mm(a, pieces[1]) + _mm(a, pieces[2])


def _gla_pre_kernel(x_ref, mod_ref, wp_ref, wlr_ref, wgk2_ref, bgk_ref,
                    qt_ref, kt_ref, kd_ref, v_ref, g_ref, dec_ref):
    tm = x_ref.shape[0]
    C = GLA_CHUNK
    G = 256
    h = _modulate(x_ref[...], mod_ref, 0).astype(BF)
    nk = GLA_H * GLA_DK
    nv = GLA_H * GLA_DV
    q = _mm(h, wp_ref[:, 0:nk]) * GLA_SCALE
    k = _mm(h, wp_ref[:, nk:2 * nk])
    v_ref[...] = _mm(h, wp_ref[:, 2 * nk:2 * nk + nv]).astype(BF)
    g_ref[...] = _mm(h, wp_ref[:, 2 * nk + nv:2 * nk + 2 * nv])
    lr = _mm(h, wlr_ref[...]).astype(BF)
    ii = lax.broadcasted_iota(jnp.int32, (G, G), 0)
    jj = lax.broadcasted_iota(jnp.int32, (G, G), 1)
    same = (ii // C) == (jj // C)
    ci = lax.broadcasted_iota(jnp.int32, (tm // C, tm), 0)
    cj = lax.broadcasted_iota(jnp.int32, (tm // C, tm), 1)
    sel = jnp.where((cj // C) == ci, 1.0, 0.0).astype(BF)
    for r in range(2):
        z = _mm(lr, wgk2_ref[r]) + bgk_ref[r:r + 1, :]
        gk = (jnp.minimum(z, 0.0) - jnp.log1p(jnp.exp(-jnp.abs(z)))) * GLA_INV_NORMALIZER
        pieces = _split3(gk)
        tri = jnp.where(jnp.logical_and(same, (jj <= ii) if r == 0 else (jj >= ii)), 1.0, 0.0).astype(BF)
        dec_ref[r] = jnp.exp(_mm3(sel, pieces))
        for gi in range(tm // G):
            rows = slice(gi * G, (gi + 1) * G)
            pg = tuple(p[rows] for p in pieces)
            b = _mm3(tri, pg)
            b3 = b.reshape(G // C, C, nk)
            edge = b3[:, C - 1:C, :] if r == 0 else b3[:, 0:1, :]
            tot = jnp.broadcast_to(edge, b3.shape).reshape(G, nk)
            qt_ref[r, rows, :] = (q[rows] * jnp.exp(b)).astype(BF)
            kt_ref[r, rows, :] = (k[rows] * jnp.exp(-b)).astype(BF)
            kd_ref[r, rows, :] = (k[rows] * jnp.exp(tot - b)).astype(BF)


def _gla_scan_kernel(*refs, L, S, has_h0, emit_state):
    qt_ref, kt_ref, kd_ref, v_ref, g_ref, dec_ref, gn_ref = refs[:7]
    pos = 7
    h0_ref = None
    if has_h0:
        h0_ref = refs[pos]
        pos += 1
    y_ref = refs[pos]
    pos += 1
    st_ref = None
    if emit_state:
        st_ref = refs[pos]
        pos += 1
    st_s, of_s, ob_s = refs[pos], refs[pos + 1], refs[pos + 2]

    C = GLA_CHUNK
    n_chunks = L // C
    ii = lax.broadcasted_iota(jnp.int32, (C, C), 0)
    jj = lax.broadcasted_iota(jnp.int32, (C, C), 1)
    keeps = ((jj <= ii), (jj >= ii))
    chains = [(sq, r, hd) for sq in range(S) for r in range(2) for hd in range(GLA_H)]
    slot = lambda sq, r, hd: (sq * 2 + r) * GLA_H + hd
    for sq, r, hd in chains:
        if has_h0:
            st_s[slot(sq, r, hd)] = h0_ref[sq, r, hd].T
        else:
            st_s[slot(sq, r, hd)] = jnp.zeros((GLA_DV, GLA_DK), F32)

    def step(i):
        for sq, r, hd in chains:
            n = i if r == 0 else n_chunks - 1 - i
            start = sq * L + n * C
            if not isinstance(start, int):
                start = pl.multiple_of(start, C)
            rows = pl.ds(start, C)
            o_s = of_s if r == 0 else ob_s
            kc = slice(hd * GLA_DK, (hd + 1) * GLA_DK)
            vc = slice(hd * GLA_DV, (hd + 1) * GLA_DV)
            qt = qt_ref[r, rows, kc]
            v = v_ref[rows, vc]
            st = st_s[slot(sq, r, hd)]
            if batched_intra:
                o_s[rows, vc] += _mm_nt(qt, st.astype(BF))
            else:
                att = jnp.where(keeps[r], _mm_nt(qt, kt_ref[r, rows, kc]), 0.0).astype(BF)
                o_s[rows, vc] = _mm_nt(qt, st.astype(BF)) + _mm(att, v)
            st_s[slot(sq, r, hd)] = st * dec_ref[r, sq, n, :, kc] + _mm_tn(v, kd_ref[r, rows, kc])

    batched_intra = n_chunks <= 4
    if batched_intra:
        fi = lax.broadcasted_iota(jnp.int32, (L, L), 0)
        fj = lax.broadcasted_iota(jnp.int32, (L, L), 1)
        same = (fi // C) == (fj // C)
        full_keeps = (jnp.logical_and(same, fj <= fi), jnp.logical_and(same, fj >= fi))
        for sq, r, hd in chains:
            rows = slice(sq * L, (sq + 1) * L)
            kc = slice(hd * GLA_DK, (hd + 1) * GLA_DK)
            vc = slice(hd * GLA_DV, (hd + 1) * GLA_DV)
            att = jnp.where(full_keeps[r], _mm_nt(qt_ref[r, rows, kc], kt_ref[r, rows, kc]), 0.0).astype(BF)
            (of_s if r == 0 else ob_s)[rows, vc] = _mm(att, v_ref[rows, vc])

    if n_chunks <= 4:
        for i in range(n_chunks):
            step(i)
    else:
        def body(i, carry):
            step(i)
            return carry
        lax.fori_loop(0, n_chunks, body, 0)

    RC = 256
    for c in range(S * L // RC):
        rows = slice(c * RC, (c + 1) * RC)
        for hd in range(GLA_H):
            vc = slice(hd * GLA_DV, (hd + 1) * GLA_DV)
            y = _rms(of_s[rows, vc] + ob_s[rows, vc]) * gn_ref[...]
            y_ref[rows, vc] = (y * _silu(g_ref[rows, vc])).astype(BF)
    if emit_state:
        for sq, r, hd in chains:
            st_ref[sq, r, hd] = st_s[slot(sq, r, hd)].T


def _residual_out_kernel(y_ref, x_ref, mod_ref, w_ref, o_ref, *, gate_j):
    o_ref[...] = x_ref[...] + _gate(mod_ref, gate_j) * _mm(y_ref[...], w_ref[...])


def gla_mixer(x, mod, w, L, h0):
    M = x.shape[0]
    B = M // L
    tm = ROW_TM
    nk, nv = GLA_H * GLA_DK, GLA_H * GLA_DV
    row = lambda n: pl.BlockSpec((tm, n), lambda i: (i, 0))
    dirs = lambda: pl.BlockSpec((2, tm, nk), lambda i: (0, i, 0))
    n_chunks = L // GLA_CHUNK
    qt, kt, kd, v, g, dec = pl.pallas_call(
        _gla_pre_kernel,
        grid=(M // tm,),
        in_specs=[row(D), _mod_spec(tm, L if mod.shape[0] > 1 else M),
                  _const_spec(w["wp"]), _const_spec(w["wlr"]), _const_spec(w["wgk2"]),
                  _const_spec(w["bgk"])],
        out_specs=[dirs(), dirs(), dirs(), row(nv), row(nv),
                   pl.BlockSpec((2, tm // GLA_CHUNK, nk), lambda i: (0, i, 0))],
        out_shape=[jax.ShapeDtypeStruct((2, M, nk), BF), jax.ShapeDtypeStruct((2, M, nk), BF),
                   jax.ShapeDtypeStruct((2, M, nk), BF),
                   jax.ShapeDtypeStruct((M, nv), BF), jax.ShapeDtypeStruct((M, nv), F32),
                   jax.ShapeDtypeStruct((2, M // GLA_CHUNK, nk), F32)],
        compiler_params=_cp("arbitrary"),
        name="gla_pre",
    )(x, mod, w["wp"], w["wlr"], w["wgk2"], w["bgk"])
    dec = dec.reshape(2, B, n_chunks, 1, nk)

    has_h0 = h0 is not None
    emit_state = not has_h0
    S = max(s for s in (1, 2, 4) if B % s == 0 and s * L <= 1024)
    SL = S * L
    dir_spec = lambda: pl.BlockSpec((2, SL, nk), lambda b: (0, b, 0))
    in_specs = [dir_spec(), dir_spec(), dir_spec(),
                pl.BlockSpec((SL, nv), lambda b: (b, 0)),
                pl.BlockSpec((SL, nv), lambda b: (b, 0)),
                pl.BlockSpec((2, S, n_chunks, 1, nk), lambda b: (0, b, 0, 0, 0)),
                pl.BlockSpec((1, GLA_DV), lambda b: (0, 0))]
    args = [qt, kt, kd, v, g, dec, w["gnorm"]]
    st_spec = pl.BlockSpec((S, 2, GLA_H, GLA_DK, GLA_DV), lambda b: (b, 0, 0, 0, 0))
    if has_h0:
        in_specs.append(st_spec)
        args.append(h0)
    out_specs = [pl.BlockSpec((SL, nv), lambda b: (b, 0))]
    out_shape = [jax.ShapeDtypeStruct((M, nv), BF)]
    if emit_state:
        out_specs.append(st_spec)
        out_shape.append(jax.ShapeDtypeStruct((B, 2, GLA_H, GLA_DK, GLA_DV), F32))
    res = pl.pallas_call(
        functools.partial(_gla_scan_kernel, L=L, S=S, has_h0=has_h0, emit_state=emit_state),
        grid=(B // S,),
        in_specs=in_specs,
        out_specs=out_specs,
        out_shape=out_shape,
        scratch_shapes=[pltpu.VMEM((S * 2 * GLA_H, GLA_DV, GLA_DK), F32), pltpu.VMEM((SL, nv), F32),
                        pltpu.VMEM((SL, nv), F32)],
        compiler_params=_cp("arbitrary"),
        name="gla_scan",
    )(*args)
    y = res[0]
    state = res[1] if emit_state else None

    x_new = pl.pallas_call(
        functools.partial(_residual_out_kernel, gate_j=0),
        grid=(M // tm,),
        in_specs=[row(nv), row(D), _mod_spec(tm, L if mod.shape[0] > 1 else M), _const_spec(w["wo"])],
        out_specs=row(D),
        out_shape=jax.ShapeDtypeStruct((M, D), F32),
        compiler_params=_cp("arbitrary"),
        name="gla_post",
    )(y, x, mod, w["wo"])
    return x_new, state


def _softmax_sink_heads(q_of, k_of, v_of, sink_ref, o_s):
    for hq in range(SWA_H):
        kh = hq // (SWA_H // SWA_KVH)
        s = _mm_nt(q_of(hq), k_of(kh))
        sink = sink_ref[hq]
        m = jnp.maximum(jnp.max(s, axis=-1, keepdims=True), sink)
        e = jnp.exp(s - m)
        p = e / (jnp.sum(e, axis=-1, keepdims=True) + jnp.exp(sink - m))
        o_s[:, hq * SWA_HD:(hq + 1) * SWA_HD] = _mm(p.astype(BF), v_of(kh)).astype(BF)


def _swa_ctx_kernel(sink_ref, x_ref, mod_ref, wqkv_ref, wo_ref, o_ref, k_out, v_out, o_s):
    x = x_ref[...]
    h = _modulate(x, mod_ref, 0).astype(BF)
    qkv = _mm(h, wqkv_ref[...])
    k_out[0] = qkv[:, SWA_NQ:SWA_NQ + SWA_NK].T
    v_out[0] = qkv[:, SWA_NQ + SWA_NK:].T
    q_of = lambda hq: (qkv[:, hq * SWA_HD:(hq + 1) * SWA_HD] * SWA_SCALE).astype(BF)
    k_of = lambda kh: qkv[:, SWA_NQ + kh * SWA_HD:SWA_NQ + (kh + 1) * SWA_HD].astype(BF)
    v_of = lambda kh: qkv[:, SWA_NQ + SWA_NK + kh * SWA_HD:SWA_NQ + SWA_NK + (kh + 1) * SWA_HD].astype(BF)
    _softmax_sink_heads(q_of, k_of, v_of, sink_ref, o_s)
    o_ref[...] = x + _gate(mod_ref, 0) * _mm(o_s[...], wo_ref[...])


def swa_ctx_mixer(x, mod, w, L):
    M = x.shape[0]
    row = lambda n: pl.BlockSpec((L, n), lambda i: (i, 0))
    return pl.pallas_call(
        _swa_ctx_kernel,
        grid=(M // L,),
        in_specs=[pl.BlockSpec(memory_space=pltpu.SMEM), row(D), _mod_spec(L, M),
                  _const_spec(w["wqkv"]), _const_spec(w["wo"])],
        out_specs=[row(D), pl.BlockSpec((1, SWA_NK, L), lambda i: (i, 0, 0)),
                   pl.BlockSpec((1, SWA_NK, L), lambda i: (i, 0, 0))],
        out_shape=[jax.ShapeDtypeStruct((M, D), F32), jax.ShapeDtypeStruct((M // L, SWA_NK, L), F32),
                   jax.ShapeDtypeStruct((M // L, SWA_NK, L), F32)],
        scratch_shapes=[pltpu.VMEM((L, SWA_NQ), BF)],
        compiler_params=_cp("arbitrary"),
        name="swa_ctx",
    )(w["sink"], x, mod, w["wqkv"], w["wo"])


def _swa_lat_kernel(sink_ref, x_ref, mod_ref, wqkv_ref, wsw_ref, cos_ref, sin_ref, kc_ref, vc_ref,
                    wo_ref, o_ref, q_s, k_s, v_s, o_s, *, L):
    n = pl.program_id(1)
    RC = 256

    @pl.when(n == 0)
    def _():
        for c in range(L // RC):
            rows = slice(c * RC, (c + 1) * RC)
            h = _modulate(x_ref[rows, :], mod_ref, 0).astype(BF)
            qkv = _mm(h, wqkv_ref[...])
            sw = _mm(h, wsw_ref[...])
            cos = cos_ref[rows, :]
            sin = sin_ref[rows, :]
            cq = jnp.concatenate([cos] * (SWA_NQ // 128), axis=1)
            sq = jnp.concatenate([sin] * (SWA_NQ // 128), axis=1)
            ck = jnp.concatenate([cos] * (SWA_NK // 128), axis=1)
            sk = jnp.concatenate([sin] * (SWA_NK // 128), axis=1)
            q = (qkv[:, :SWA_NQ] * SWA_SCALE) * cq + (sw[:, :SWA_NQ] * SWA_SCALE) * sq
            q_s[rows, :] = q.astype(BF)
            k = qkv[:, SWA_NQ:SWA_NQ + SWA_NK] * ck + sw[:, SWA_NQ:SWA_NQ + SWA_NK] * sk
            k_s[rows, :] = k.astype(BF)
            v_s[rows, :] = qkv[:, SWA_NQ + SWA_NK:].astype(BF)

    QB = SWA_BLOCK
    KW = QB + 2 * SWA_WINDOW
    r0 = pl.multiple_of(n * QB, QB)
    ws = pl.multiple_of(jnp.clip(n * QB - SWA_WINDOW, 0, L - KW), SWA_WINDOW)
    qpos = r0 + lax.broadcasted_iota(jnp.int32, (QB, KW), 0)
    kpos = ws + lax.broadcasted_iota(jnp.int32, (QB, KW), 1)
    valid = jnp.abs(kpos - qpos) <= SWA_WINDOW
    for hq in range(SWA_H):
        kh = hq // (SWA_H // SWA_KVH)
        hs = slice(kh * SWA_HD, (kh + 1) * SWA_HD)
        q = q_s[pl.ds(r0, QB), hq * SWA_HD:(hq + 1) * SWA_HD]
        s1 = jnp.where(valid, _mm_nt(q, k_s[pl.ds(ws, KW), hs]), NEG)
        s2 = _mm_nt(q, kc_ref[0, :, hs].astype(BF))
        sink = sink_ref[hq]
        m = jnp.maximum(jnp.maximum(jnp.max(s1, axis=-1, keepdims=True),
                                    jnp.max(s2, axis=-1, keepdims=True)), sink)
        e1 = jnp.exp(s1 - m)
        e2 = jnp.exp(s2 - m)
        den = (jnp.sum(e1, axis=-1, keepdims=True) + jnp.sum(e2, axis=-1, keepdims=True)
               + jnp.exp(sink - m))
        o = (_mm(e1.astype(BF), v_s[pl.ds(ws, KW), hs])
             + _mm(e2.astype(BF), vc_ref[0, :, hs].astype(BF)))
        o_s[:, hq * SWA_HD:(hq + 1) * SWA_HD] = (o * (1.0 / den)).astype(BF)
    o_ref[...] = x_ref[pl.ds(r0, QB), :] + _gate(mod_ref, 0) * _mm(o_s[...], wo_ref[...])


def swa_lat_mixer(x, mod, w, L, kc, vc, cos128, sin128):
    M = x.shape[0]
    B = M // L
    P = kc.shape[1]
    nb = L // SWA_BLOCK
    return pl.pallas_call(
        functools.partial(_swa_lat_kernel, L=L),
        grid=(B, nb),
        in_specs=[pl.BlockSpec(memory_space=pltpu.SMEM),
                  pl.BlockSpec((L, D), lambda b, n: (b, 0)),
                  pl.BlockSpec((1, 6, D), lambda b, n: (b, 0, 0)),
                  _const_spec(w["wqkv"]), _const_spec(w["wsw"]),
                  _const_spec(cos128), _const_spec(sin128),
                  pl.BlockSpec((1, P, SWA_NK), lambda b, n: (b, 0, 0)),
                  pl.BlockSpec((1, P, SWA_NK), lambda b, n: (b, 0, 0)),
                  _const_spec(w["wo"])],
        out_specs=pl.BlockSpec((SWA_BLOCK, D), lambda b, n: (b * nb + n, 0)),
        out_shape=jax.ShapeDtypeStruct((M, D), F32),
        scratch_shapes=[pltpu.VMEM((L, SWA_NQ), BF), pltpu.VMEM((L, SWA_NK), BF),
                        pltpu.VMEM((L, SWA_NK), BF), pltpu.VMEM((SWA_BLOCK, SWA_NQ), BF)],
        compiler_params=_cp("arbitrary", "arbitrary"),
        name="swa_lat",
    )(w["sink"], x, mod, w["wqkv"], w["wsw"], cos128, sin128, kc, vc, w["wo"])


def _conv_kernel(x_ref, mod_ref, win_ref, cw_ref, wout_ref, o_ref, y_s, *, L):
    x = x_ref[...]
    rows = x.shape[0]
    h = _modulate(x, mod_ref, 0).astype(BF)
    CC = 256
    row = lax.broadcasted_iota(jnp.int32, (rows, CC), 0) % L
    for c in range(D // CC):
        cols = slice(c * CC, (c + 1) * CC)
        bg = _mm(h, win_ref[:, c * CC:(c + 1) * CC])
        cg = _mm(h, win_ref[:, D + c * CC:D + (c + 1) * CC])
        u = _mm(h, win_ref[:, 2 * D + c * CC:2 * D + (c + 1) * CC])
        cu = cg * u
        prev = jnp.where(row == 0, 0.0, pltpu.roll(cu, 1, 0))
        nxt = jnp.where(row == L - 1, 0.0, pltpu.roll(cu, rows - 1, 0))
        conv = prev * cw_ref[0:1, cols] + cu * cw_ref[1:2, cols] + nxt * cw_ref[2:3, cols]
        y_s[:, cols] = (bg * conv).astype(BF)
    o_ref[...] = x + _gate(mod_ref, 0) * _mm(y_s[...], wout_ref[...])


def conv_mixer(x, mod, w, L):
    M = x.shape[0]
    shared_mod = mod.shape[0] == 1
    tm = min(M, 1024) if shared_mod else L
    row = pl.BlockSpec((tm, D), lambda i: (i, 0))
    return pl.pallas_call(
        functools.partial(_conv_kernel, L=L),
        grid=(M // tm,),
        in_specs=[row, _mod_spec(tm, M if shared_mod else L), _const_spec(w["win"]),
                  _const_spec(w["cw"]), _const_spec(w["wout"])],
        out_specs=row,
        out_shape=jax.ShapeDtypeStruct((M, D), F32),
        scratch_shapes=[pltpu.VMEM((tm, D), BF)],
        compiler_params=_cp("arbitrary"),
        name="conv_mix",
    )(x, mod, w["win"], w["cw"], w["wout"])


def _mla_ctx_kernel(x_ref, mod_ref, wdq_ref, qn_ref, wuq_ref, wdkv_ref, kvn_ref, wukv_ref, wo_ref,
                    o_ref, ckv_out, kr_out, o_s):
    x = x_ref[...]
    h = _modulate(x, mod_ref, 0).astype(BF)
    cq = (_rms(_mm(h, wdq_ref[...])) * qn_ref[...]).astype(BF)
    q = _mm(cq, wuq_ref[...]).astype(BF)
    kvc = _mm(h, wdkv_ref[...])
    ckv = _rms(kvc[:, :KV_LORA]) * kvn_ref[...]
    kr = kvc[:, KV_LORA:]
    ckv_out[...] = ckv
    kr_out[0] = kr.T[:QK_ROPE, :]
    kv = _mm(ckv.astype(BF), wukv_ref[...]).astype(BF)
    krb = kr.astype(BF)
    nn = MLA_H * QK_NOPE
    for hh in range(MLA_H):
        qh = q[:, hh * MLA_QW:(hh + 1) * MLA_QW]
        kh = jnp.concatenate([kv[:, hh * QK_NOPE:(hh + 1) * QK_NOPE], krb], axis=1)
        s = _mm_nt(qh, kh) * MLA_SCALE
        m = jnp.max(s, axis=-1, keepdims=True)
        e = jnp.exp(s - m)
        inv = 1.0 / jnp.sum(e, axis=-1, keepdims=True)
        vh = kv[:, nn + hh * V_HEAD:nn + (hh + 1) * V_HEAD]
        o_s[:, hh * V_HEAD:(hh + 1) * V_HEAD] = (_mm(e.astype(BF), vh) * inv).astype(BF)
    o_ref[...] = x + _gate(mod_ref, 0) * _mm(o_s[...], wo_ref[...])


def mla_ctx_mixer(x, mod, w, L):
    M = x.shape[0]
    row = lambda n: pl.BlockSpec((L, n), lambda i: (i, 0))
    ws = [w["wdq"], w["qnorm"], w["wuq"], w["wdkv"], w["kvnorm"], w["wukv"], w["wo"]]
    return pl.pallas_call(
        _mla_ctx_kernel,
        grid=(M // L,),
        in_specs=[row(D), _mod_spec(L, M)] + [_const_spec(a) for a in ws],
        out_specs=[row(D), row(KV_LORA), pl.BlockSpec((1, QK_ROPE, L), lambda i: (i, 0, 0))],
        out_shape=[jax.ShapeDtypeStruct((M, D), F32), jax.ShapeDtypeStruct((M, KV_LORA), F32),
                   jax.ShapeDtypeStruct((M // L, QK_ROPE, L), F32)],
        scratch_shapes=[pltpu.VMEM((L, MLA_H * V_HEAD), BF)],
        compiler_params=_cp("arbitrary"),
        name="mla_ctx",
    )(x, mod, *ws)


def _mla_lat_kernel(x_ref, mod_ref, wdq_ref, qn_ref, wuq_ref, wuqsw_ref, wdkv_ref, wdkvsw_ref,
                    kvn_ref, wukv_ref, cq_ref, sq_ref, ck_ref, sk_ref, ckvc_ref, krc_ref, wo_ref,
                    o_ref, q_s, kn_s, v_s, kr_s, knc_s, vc_s, o_s, *, L, QB):
    n = pl.program_id(1)
    RC = 256
    nn = MLA_H * QK_NOPE

    @pl.when(n == 0)
    def _():
        for c in range(L // RC):
            rows = slice(c * RC, (c + 1) * RC)
            h = _modulate(x_ref[rows, :], mod_ref, 0).astype(BF)
            cq = (_rms(_mm(h, wdq_ref[...])) * qn_ref[...]).astype(BF)
            cosq = jnp.concatenate([cq_ref[rows, :]] * MLA_H, axis=1)
            sinq = jnp.concatenate([sq_ref[rows, :]] * MLA_H, axis=1)
            q = _mm(cq, wuq_ref[...]) * cosq + _mm(cq, wuqsw_ref[...]) * sinq
            q_s[rows, :] = q.astype(BF)
            kvc = _mm(h, wdkv_ref[...])
            ksw = _mm(h, wdkvsw_ref[...])
            ckv = _rms(kvc[:, :KV_LORA]) * kvn_ref[...]
            kr_s[rows, :] = (kvc[:, KV_LORA:] * ck_ref[rows, :] + ksw * sk_ref[rows, :]).astype(BF)
            kv = _mm(ckv.astype(BF), wukv_ref[...])
            kn_s[rows, :] = kv[:, :nn].astype(BF)
            v_s[rows, :] = kv[:, nn:].astype(BF)
        kvp = _mm(ckvc_ref[0].astype(BF), wukv_ref[...])
        knc_s[...] = kvp[:, :nn].astype(BF)
        vc_s[...] = kvp[:, nn:].astype(BF)

    r0 = pl.multiple_of(n * QB, QB)
    krc = krc_ref[0].astype(BF)
    for hh in range(MLA_H):
        ns = slice(hh * QK_NOPE, (hh + 1) * QK_NOPE)
        qh = q_s[pl.ds(r0, QB), hh * MLA_QW:(hh + 1) * MLA_QW]
        k1 = jnp.concatenate([kn_s[:, ns], kr_s[...]], axis=1)
        k2 = jnp.concatenate([knc_s[:, ns], krc], axis=1)
        s1 = _mm_nt(qh, k1) * MLA_SCALE
        s2 = _mm_nt(qh, k2) * MLA_SCALE
        m = jnp.maximum(jnp.max(s1, axis=-1, keepdims=True), jnp.max(s2, axis=-1, keepdims=True))
        e1 = jnp.exp(s1 - m)
        e2 = jnp.exp(s2 - m)
        den = jnp.sum(e1, axis=-1, keepdims=True) + jnp.sum(e2, axis=-1, keepdims=True)
        vs = slice(hh * V_HEAD, (hh + 1) * V_HEAD)
        o = _mm(e1.astype(BF), v_s[:, vs]) + _mm(e2.astype(BF), vc_s[:, vs])
        o_s[:, vs] = (o * (1.0 / den)).astype(BF)
    o_ref[...] = x_ref[pl.ds(r0, QB), :] + _gate(mod_ref, 0) * _mm(o_s[...], wo_ref[...])


def mla_lat_mixer(x, mod, w, L, ckv_c, kr_c128, tabs):
    M = x.shape[0]
    B = M // L
    P = ckv_c.shape[1]
    QB = 512
    nb = L // QB
    ws1 = [w["wdq"], w["qnorm"], w["wuq"], w["wuqsw"], w["wdkv"], w["wdkvsw"], w["kvnorm"], w["wukv"]]
    nv = MLA_H * V_HEAD
    return pl.pallas_call(
        functools.partial(_mla_lat_kernel, L=L, QB=QB),
        grid=(B, nb),
        in_specs=[pl.BlockSpec((L, D), lambda b, n: (b, 0)),
                  pl.BlockSpec((1, 6, D), lambda b, n: (b, 0, 0))]
                 + [_const_spec(a) for a in ws1] + [_const_spec(a) for a in tabs]
                 + [pl.BlockSpec((1, P, KV_LORA), lambda b, n: (b, 0, 0)),
                    pl.BlockSpec((1, P, LANES), lambda b, n: (b, 0, 0)),
                    _const_spec(w["wo"])],
        out_specs=pl.BlockSpec((QB, D), lambda b, n: (b * nb + n, 0)),
        out_shape=jax.ShapeDtypeStruct((M, D), F32),
        scratch_shapes=[pltpu.VMEM((L, MLA_H * MLA_QW), BF), pltpu.VMEM((L, nv), BF),
                        pltpu.VMEM((L, nv), BF), pltpu.VMEM((L, LANES), BF),
                        pltpu.VMEM((P, nv), BF), pltpu.VMEM((P, nv), BF), pltpu.VMEM((QB, nv), BF)],
        compiler_params=_cp("arbitrary", "arbitrary"),
        name="mla_lat",
    )(x, mod, *ws1, *tabs, ckv_c, kr_c128, w["wo"])


def _swiglu_partial(xb, wg_ref, wu_ref, wd_ref):
    hg = _mm(xb, wg_ref[...].astype(BF))
    hu = _mm(xb, wu_ref[...].astype(BF))
    a = (_silu(hg) * hu).astype(BF)
    return _mm(a, wd_ref[...].astype(BF))


def _dense_ffn_kernel(x_ref, mod_ref, wga_ref, wgb_ref, wua_ref, wub_ref, wda_ref, wdb_ref, o_ref, xm_s,
                      *, odd_blocks):
    f = pl.program_id(1)
    last = pl.num_programs(1) - 1

    @pl.when(f == 0)
    def _():
        xm_s[...] = _modulate(x_ref[...], mod_ref, 1).astype(BF)
        o_ref[...] = jnp.zeros_like(o_ref)

    def pair():
        xb = xm_s[...]
        acts = []
        for wg_ref, wu_ref in ((wga_ref, wua_ref), (wgb_ref, wub_ref)):
            hg = _mm(xb, wg_ref[...].astype(BF))
            hu = _mm(xb, wu_ref[...].astype(BF))
            acts.append((_silu(hg) * hu).astype(BF))
        wd = jnp.concatenate([wda_ref[...].astype(BF), wdb_ref[...].astype(BF)], axis=0)
        return _mm(jnp.concatenate(acts, axis=1), wd)

    if odd_blocks:
        @pl.when(f < last)
        def _():
            o_ref[...] += pair()

        @pl.when(f == last)
        def _():
            o_ref[...] += _swiglu_partial(xm_s[...], wga_ref, wua_ref, wda_ref)
    else:
        o_ref[...] += pair()

    @pl.when(f == last)
    def _():
        o_ref[...] = x_ref[...] + _gate(mod_ref, 1) * o_ref[...]


def dense_ffn(x, mod, wg, wu, wd, j, L):
    M = x.shape[0]
    F = wg.shape[-1]
    tm, tf = DENSE_TM, DENSE_TF
    nfb = F // tf
    blk_a = lambda f: 2 * f
    blk_b = lambda f: jnp.minimum(2 * f + 1, nfb - 1)
    col = lambda blk: pl.BlockSpec((None, D, tf), lambda i, f: (j, 0, blk(f)))
    row = lambda blk: pl.BlockSpec((None, tf, D), lambda i, f: (j, blk(f), 0))
    return pl.pallas_call(
        functools.partial(_dense_ffn_kernel, odd_blocks=nfb % 2 == 1),
        grid=(M // tm, (nfb + 1) // 2),
        in_specs=[pl.BlockSpec((tm, D), lambda i, f: (i, 0)),
                  pl.BlockSpec((1, 6, D), lambda i, f: ((i * tm) // (L if mod.shape[0] > 1 else M), 0, 0)),
                  col(blk_a), col(blk_b), col(blk_a), col(blk_b), row(blk_a), row(blk_b)],
        out_specs=pl.BlockSpec((tm, D), lambda i, f: (i, 0)),
        out_shape=jax.ShapeDtypeStruct((M, D), F32),
        scratch_shapes=[pltpu.VMEM((tm, D), BF)],
        compiler_params=_cp("arbitrary", "arbitrary"),
        name="dense_ffn",
    )(x, mod, wg, wg, wu, wu, wd, wd)


def _router_kernel(xp_ref, xs_ref, mod_ref, wr_ref, xm_ref, idx_ref, w_ref, cnt_ref, cnt_s, *, np_tiles):
    i = pl.program_id(0)
    tm = xm_ref.shape[0]

    @pl.when(i == 0)
    def _():
        cnt_s[...] = jnp.zeros_like(cnt_s)

    x = jnp.where(i < np_tiles, xp_ref[...], xs_ref[...])
    xmod = _modulate(x, mod_ref, 1)
    xm_ref[...] = xmod
    xm = xmod.astype(BF)
    lane = lax.broadcasted_iota(jnp.int32, idx_ref.shape, 1)
    logits = jnp.where(lane < N_EXPERTS, _mm(xm, wr_ref[...]), -jnp.inf)
    m1 = jnp.max(logits, axis=-1, keepdims=True)
    i1 = jnp.min(jnp.where(logits == m1, lane, LANES), axis=-1, keepdims=True)
    rest = jnp.where(lane == i1, -jnp.inf, logits)
    m2 = jnp.max(rest, axis=-1, keepdims=True)
    i2 = jnp.min(jnp.where(rest == m2, lane, LANES), axis=-1, keepdims=True)
    e = jnp.exp(m2 - m1)
    w_ref[...] = jnp.where(lane == 0, 1.0 / (1.0 + e), e / (1.0 + e))
    oh1 = jnp.where(lane == i1, 1.0, 0.0)
    oh2 = jnp.where(lane == i2, 1.0, 0.0)
    rr = lax.broadcasted_iota(jnp.int32, (tm, tm), 0)
    cc = lax.broadcasted_iota(jnp.int32, (tm, tm), 1)
    below = jnp.where(cc < rr, 1.0, 0.0).astype(BF)
    run = cnt_s[0:1, :]
    tot1 = jnp.sum(oh1, axis=0, keepdims=True)
    tot2 = jnp.sum(oh2, axis=0, keepdims=True)
    r1 = jnp.sum(oh1 * (run + _mm(below, oh1.astype(BF))), axis=-1, keepdims=True)
    r2 = jnp.sum(oh2 * (run + tot1 + _mm(below, oh2.astype(BF))), axis=-1, keepdims=True)
    idx_ref[...] = jnp.where(lane == 0, i1, jnp.where(lane == 1, i2, jnp.where(
        lane == 2, r1.astype(jnp.int32), r2.astype(jnp.int32))))
    total = run + tot1 + tot2
    cnt_s[...] = jnp.broadcast_to(total, cnt_s.shape)
    cnt_ref[...] = jnp.broadcast_to(total, cnt_ref.shape)


def route(xp, xs, mod3, wr, Ls):
    Mp, Ms = xp.shape[0], xs.shape[0]
    T = Mp + Ms
    tm = ROUTE_TM
    npt = Mp // tm
    row = lambda n: pl.BlockSpec((tm, n), lambda i: (i, 0))
    return pl.pallas_call(
        functools.partial(_router_kernel, np_tiles=npt),
        grid=(T // tm,),
        in_specs=[pl.BlockSpec((tm, D), lambda i: (jnp.minimum(i, npt - 1), 0)),
                  pl.BlockSpec((tm, D), lambda i: (jnp.maximum(i - npt, 0), 0)),
                  pl.BlockSpec((1, 6, D), lambda i: (jnp.where(i < npt, 0, 1 + ((i - npt) * tm) // Ls), 0, 0)),
                  _const_spec(wr)],
        out_specs=[row(D), row(LANES), row(LANES), pl.BlockSpec((SUBLANES, LANES), lambda i: (0, 0))],
        out_shape=[jax.ShapeDtypeStruct((T, D), F32), jax.ShapeDtypeStruct((T, LANES), jnp.int32),
                   jax.ShapeDtypeStruct((T, LANES), F32), jax.ShapeDtypeStruct((SUBLANES, LANES), F32)],
        scratch_shapes=[pltpu.VMEM((SUBLANES, LANES), F32)],
        compiler_params=_cp("arbitrary"),
        name="moe_route",
    )(xp, xs, mod3, wr)


def _moe_ffn_kernel(vb_ref, ve_ref, vm_ref, nv_ref, xa_ref, xb_ref, wg_ref, wu_ref, wd_ref, o_ref, x_s,
                    *, half_blocks):
    v = pl.program_id(0)
    f = pl.program_id(1)
    mode = vm_ref[v]
    H = MOE_TM

    @pl.when(jnp.logical_and(v < nv_ref[0], f == 0))
    def _():
        x_s[...] = jnp.where(vb_ref[v] < half_blocks, xa_ref[...], xb_ref[...]).astype(BF)

    def run(rows):
        @pl.when(f == 0)
        def _():
            o_ref[rows, :] = jnp.zeros((rows.stop - rows.start, D), F32)

        o_ref[rows, :] += _swiglu_partial(x_s[rows, :], wg_ref, wu_ref, wd_ref)

    @pl.when(mode == MOE_FULL)
    def _():
        run(slice(0, 2 * H))

    @pl.when(jnp.logical_or(mode == MOE_LO, mode == MOE_LO_ZERO_HI))
    def _():
        run(slice(0, H))

    @pl.when(mode == MOE_HI)
    def _():
        run(slice(H, 2 * H))

    @pl.when(jnp.logical_and(mode == MOE_LO_ZERO_HI, f == 0))
    def _():
        o_ref[H:2 * H, :] = jnp.zeros((H, D), F32)

    @pl.when(jnp.logical_and(mode == MOE_ZERO, f == 0))
    def _():
        o_ref[...] = jnp.zeros_like(o_ref)


def moe_ffn(xa, xb, tile_expert, n_valid, wg, wu, wd, j):
    R = 2 * xa.shape[0]
    F = wg.shape[-1]
    bm, tf = 2 * MOE_TM, MOE_TF
    nf = F // tf
    nb = R // bm
    nhb = nb // 2
    n_vis = nb + N_EXPERTS - 1
    blocks = jnp.arange(nb, dtype=jnp.int32)
    t0, t1 = tile_expert[0::2], tile_expert[1::2]
    valid0 = 2 * blocks < n_valid[0]
    valid1 = 2 * blocks + 1 < n_valid[0]
    two = jnp.logical_and(valid1, t0 != t1)
    cnt = jnp.where(valid0, 1 + two.astype(jnp.int32), 0)
    cum = jnp.cumsum(cnt)
    nv = cum[-1:].astype(jnp.int32)
    v_eff = jnp.minimum(jnp.arange(n_vis, dtype=jnp.int32), nv[0] - 1)
    vb = jnp.sum((v_eff[:, None] >= cum[None, :]).astype(jnp.int32), axis=1)
    pick = lambda a: jnp.sum(jnp.where(vb[:, None] == blocks[None, :], a[None, :].astype(jnp.int32), 0), axis=1)
    k = v_eff - pick(cum - cnt)
    two_v, t0_v, t1_v, valid1_v = pick(two), pick(t0), pick(t1), pick(valid1)
    vm = jnp.where(two_v == 1, jnp.where(k == 0, MOE_LO, MOE_HI),
                   jnp.where(valid1_v == 1, MOE_FULL, MOE_LO_ZERO_HI))
    ve = jnp.where(jnp.logical_and(two_v == 1, k == 1), t1_v, t0_v).astype(jnp.int32)
    spare = jnp.arange(n_vis, dtype=jnp.int32) - nv[0]
    n_used = jnp.sum(valid0.astype(jnp.int32))
    fill = jnp.logical_and(spare >= 0, n_used + spare < nb)
    idle = jnp.logical_and(spare >= 0, jnp.logical_not(fill))
    vb = jnp.where(fill, n_used + spare, jnp.where(jnp.logical_and(idle, n_used < nb), nb - 1, vb))
    vm = jnp.where(fill, MOE_ZERO, jnp.where(idle, MOE_IDLE, vm)).astype(jnp.int32)

    def fidx(v, f, nv):
        return jnp.where(v < nv[0], f, nf - 1)

    grid_spec = pltpu.PrefetchScalarGridSpec(
        num_scalar_prefetch=4,
        grid=(n_vis, nf),
        in_specs=[pl.BlockSpec((bm, D), lambda v, f, vb, ve, vm, nv: (jnp.minimum(vb[v], nhb - 1), 0)),
                  pl.BlockSpec((bm, D), lambda v, f, vb, ve, vm, nv: (jnp.maximum(vb[v] - nhb, 0), 0)),
                  pl.BlockSpec((None, None, D, tf), lambda v, f, vb, ve, vm, nv: (j, ve[v], 0, fidx(v, f, nv))),
                  pl.BlockSpec((None, None, D, tf), lambda v, f, vb, ve, vm, nv: (j, ve[v], 0, fidx(v, f, nv))),
                  pl.BlockSpec((None, None, tf, D), lambda v, f, vb, ve, vm, nv: (j, ve[v], fidx(v, f, nv), 0))],
        out_specs=pl.BlockSpec((bm, D), lambda v, f, vb, ve, vm, nv: (vb[v], 0)),
        scratch_shapes=[pltpu.VMEM((bm, D), BF)],
    )
    return pl.pallas_call(
        functools.partial(_moe_ffn_kernel, half_blocks=nhb),
        grid_spec=grid_spec,
        out_shape=jax.ShapeDtypeStruct((R, D), F32),
        compiler_params=_cp("arbitrary", "arbitrary"),
        name="moe_ffn",
    )(vb.astype(jnp.int32), ve, vm, nv, xa, xb, wg, wu, wd)


def _combine_kernel(x_ref, mod_ref, ya_ref, yb_ref, w_ref, fn_ref, o_ref, *, final):
    y = w_ref[:, 0:1] * ya_ref[...] + w_ref[:, 1:2] * yb_ref[...]
    o = x_ref[...] + _gate(mod_ref, 1) * y
    if final:
        o = _rms(o) * fn_ref[...]
    o_ref[...] = o


def moe_combine(x, mod, ya, yb, w, fn, L, final, row_off):
    M = x.shape[0]
    tm = ROW_TM
    off = row_off // tm
    row = pl.BlockSpec((tm, D), lambda i: (i, 0))
    row_o = lambda n: pl.BlockSpec((tm, n), lambda i: (i + off, 0))
    return pl.pallas_call(
        functools.partial(_combine_kernel, final=final),
        grid=(M // tm,),
        in_specs=[row, _mod_spec(tm, L if mod.shape[0] > 1 else M), row_o(D), row_o(D), row_o(LANES),
                  _const_spec(fn)],
        out_specs=row,
        out_shape=jax.ShapeDtypeStruct((M, D), F32),
        compiler_params=_cp("arbitrary"),
        name="moe_combine",
    )(x, mod, ya, yb, w, fn)


def moe_layer(xp, xs, mod3, wr, wg, wu, wd, j, fn, Lp, Ls, final):
    Mp = xp.shape[0]
    xm, idx, wts, cnt = route(xp, xs, mod3, wr, Ls)
    T = xm.shape[0]
    tm = MOE_TM
    R = 2 * T + N_EXPERTS * tm
    counts = cnt[0, :N_EXPERTS].astype(jnp.int32)
    tiles_per = (counts + tm - 1) // tm
    tile_end = jnp.cumsum(tiles_per)
    offs = (tile_end - tiles_per) * tm
    experts = jnp.arange(N_EXPERTS, dtype=jnp.int32)
    dest = jnp.sum(jnp.where(idx[:, 0:2, None] == experts, offs, 0), axis=-1) + idx[:, 2:4]
    tok = jnp.arange(2 * T, dtype=jnp.int32) // 2
    src = (jnp.arange(R, dtype=jnp.int32) % T).at[dest.reshape(-1)].set(
        tok, unique_indices=True, mode="promise_in_bounds")
    n_tiles = R // tm
    n_valid = tile_end[-1:].astype(jnp.int32)
    t_ids = jnp.arange(n_tiles, dtype=jnp.int32)
    te = jnp.sum((t_ids[:, None] >= tile_end[None, :]).astype(jnp.int32), axis=1)
    te_last = jnp.sum((n_valid - 1 >= tile_end).astype(jnp.int32))
    te = jnp.where(t_ids < n_valid[0], te, te_last).astype(jnp.int32)
    xa = xm.at[src[:R // 2]].get(mode="promise_in_bounds")
    xb = xm.at[src[R // 2:]].get(mode="promise_in_bounds")
    y_sorted = moe_ffn(xa, xb, te, n_valid, wg, wu, wd, j)
    ya = y_sorted.at[dest[:, 0]].get(mode="promise_in_bounds")
    yb = y_sorted.at[dest[:, 1]].get(mode="promise_in_bounds")
    xp_new = moe_combine(xp, mod3[0:1], ya, yb, wts, fn, Lp, final, 0)
    xs_new = moe_combine(xs, mod3[1:], ya, yb, wts, fn, Ls, final, Mp)
    return xp_new, xs_new


def _rope_partner_perm(n):
    c = np.arange(n)
    return np.where((c % 32) < 16, c + 16, c - 16)


def _rope_tables(L):
    nf = 16
    inv = ROPE_THETA ** (-np.arange(nf, dtype=np.float32) / nf)
    pos = np.arange(L)
    ang = np.stack([(pos // GRID_W).astype(np.float32), (pos % GRID_W).astype(np.float32)],
                   axis=-1)[:, :, None] * inv
    cos = jnp.cos(jnp.asarray(ang, F32))
    sin = jnp.sin(jnp.asarray(ang, F32))
    c64 = jnp.stack([cos, cos], axis=2).reshape(L, 64)
    s64 = jnp.stack([-sin, sin], axis=2).reshape(L, 64)
    return c64, s64


def kernel(x_prompt, x_sample, state_l0_gla, cache_l1_k, cache_l1_v, cache_l3_ckv, cache_l3_krope, c, c_ctx, w_mod, b_mod, gla_wq, gla_wk, gla_wv, gla_wg, gla_wgk1, gla_wgk2, gla_bgk, gla_norm, gla_wo, swa_wqkv, swa_sink, swa_wo, conv_win, conv_w, conv_wout, mla_wdq, mla_qnorm, mla_wuq, mla_wdkv, mla_kvnorm, mla_wukv, mla_wo, dense_w_gate, dense_w_up, dense_w_down, moe_router, moe_w_gate, moe_w_up, moe_w_down, final_norm):
    Bp, Lp, _ = x_prompt.shape
    Bs, Ls, _ = x_sample.shape
    P = cache_l1_k.shape[1]
    xp = x_prompt.reshape(Bp * Lp, D)
    xs = x_sample.reshape(Bs * Ls, D)

    cvec = jnp.zeros((8, D), F32).at[0].set(c_ctx).at[1:1 + Bs].set(c)
    mod_all = adaln_all(cvec, w_mod, b_mod).reshape(w_mod.shape[0], 8, 6, D)

    wlr = jnp.zeros((D, 128), F32).at[:, :GLA_RANK].set(gla_wgk1[0]).at[:, GLA_RANK:2 * GLA_RANK].set(gla_wgk1[1])
    wgk2 = jnp.zeros((2, 128, GLA_H * GLA_DK), F32)
    wgk2 = wgk2.at[0, :GLA_RANK].set(gla_wgk2[0]).at[1, GLA_RANK:2 * GLA_RANK].set(gla_wgk2[1])
    gla_w = dict(wp=jnp.concatenate([gla_wq, gla_wk, gla_wv, gla_wg], axis=1).astype(BF),
                 wlr=wlr.astype(BF), wgk2=wgk2.astype(BF), bgk=gla_bgk,
                 gnorm=gla_norm.reshape(1, GLA_DV), wo=gla_wo.astype(BF))
    modp, mods = mod_all[0, 0:1], mod_all[0, 1:1 + Bs]
    xp, new_state = gla_mixer(xp, modp, gla_w, Lp, None)
    xs, _ = gla_mixer(xs, mods, gla_w, Ls, state_l0_gla)
    xp = dense_ffn(xp, modp, dense_w_gate, dense_w_up, dense_w_down, 0, Lp)
    xs = dense_ffn(xs, mods, dense_w_gate, dense_w_up, dense_w_down, 0, Ls)

    c64, s64 = _rope_tables(Ls)
    cos128 = jnp.concatenate([c64, c64], axis=1)
    sin128 = jnp.concatenate([s64, s64], axis=1)
    perm = _rope_partner_perm(SWA_NQ + SWA_NK)
    swa_w = dict(wqkv=swa_wqkv.astype(BF), wsw=swa_wqkv[:, perm].astype(BF), wo=swa_wo.astype(BF),
                 sink=swa_sink)
    modp, mods = mod_all[1, 0:1], mod_all[1, 1:1 + Bs]
    xp, k1, v1 = swa_ctx_mixer(xp, modp, swa_w, Lp)
    xs = swa_lat_mixer(xs, mods, swa_w, Ls, cache_l1_k.reshape(Bs, P, SWA_NK),
                       cache_l1_v.reshape(Bs, P, SWA_NK), cos128, sin128)
    wr = jnp.zeros((moe_router.shape[0], D, LANES), F32).at[:, :, :N_EXPERTS].set(moe_router).astype(BF)
    xp, xs = moe_layer(xp, xs, mod_all[1, 0:1 + Bs], wr[0], moe_w_gate, moe_w_up, moe_w_down, 0,
                       final_norm.reshape(1, D), Lp, Ls, False)

    conv_wts = dict(win=conv_win.astype(BF), cw=conv_w, wout=conv_wout.astype(BF))
    modp, mods = mod_all[2, 0:1], mod_all[2, 1:1 + Bs]
    xp = conv_mixer(xp, modp, conv_wts, Lp)
    xs = conv_mixer(xs, mods, conv_wts, Ls)
    xp = dense_ffn(xp, modp, dense_w_gate, dense_w_up, dense_w_down, 1, Lp)
    xs = dense_ffn(xs, mods, dense_w_gate, dense_w_up, dense_w_down, 1, Ls)

    wuq3 = mla_wuq.reshape(Q_LORA, MLA_H, QK_NOPE + QK_ROPE)
    zpad = jnp.zeros((Q_LORA, MLA_H, MLA_QW - QK_NOPE - QK_ROPE), F32)
    wuq = jnp.concatenate([wuq3, zpad], axis=2).reshape(Q_LORA, MLA_H * MLA_QW)
    rperm = _rope_partner_perm(QK_ROPE)
    wuq_sw = jnp.concatenate([jnp.zeros((Q_LORA, MLA_H, QK_NOPE), F32), wuq3[:, :, QK_NOPE:][:, :, rperm], zpad],
                             axis=2).reshape(Q_LORA, MLA_H * MLA_QW)
    kpad = jnp.zeros((D, 128 - QK_ROPE), F32)
    wdkv = jnp.concatenate([mla_wdkv, kpad], axis=1)
    wdkv_sw = jnp.concatenate([mla_wdkv[:, KV_LORA:][:, rperm], kpad], axis=1)
    wukv3 = mla_wukv.reshape(KV_LORA, MLA_H, QK_NOPE + V_HEAD)
    wukv = jnp.concatenate([wukv3[:, :, :QK_NOPE].reshape(KV_LORA, MLA_H * QK_NOPE),
                            wukv3[:, :, QK_NOPE:].reshape(KV_LORA, MLA_H * V_HEAD)], axis=1)
    mla_w = dict(wdq=mla_wdq.astype(BF), qnorm=mla_qnorm.reshape(1, Q_LORA), wuq=wuq.astype(BF),
                 wuqsw=wuq_sw.astype(BF), wdkv=wdkv.astype(BF), wdkvsw=wdkv_sw.astype(BF),
                 kvnorm=mla_kvnorm.reshape(1, KV_LORA), wukv=wukv.astype(BF), wo=mla_wo.astype(BF))
    ones = jnp.ones((Ls, QK_NOPE), F32)
    z128 = jnp.zeros((Ls, QK_NOPE), F32)
    z64 = jnp.zeros((Ls, 64), F32)
    tabs = [jnp.concatenate([ones, c64, z64], axis=1), jnp.concatenate([z128, s64, z64], axis=1),
            jnp.concatenate([c64, z64], axis=1), jnp.concatenate([s64, z64], axis=1)]
    modp, mods = mod_all[3, 0:1], mod_all[3, 1:1 + Bs]
    xp, ckv3, kr3 = mla_ctx_mixer(xp, modp, mla_w, Lp)
    kr_c128 = jnp.concatenate([cache_l3_krope, jnp.zeros((Bs, P, 128 - QK_ROPE), F32)], axis=2)
    xs = mla_lat_mixer(xs, mods, mla_w, Ls, cache_l3_ckv, kr_c128, tabs)
    xp, xs = moe_layer(xp, xs, mod_all[3, 0:1 + Bs], wr[1], moe_w_gate, moe_w_up, moe_w_down, 1,
                       final_norm.reshape(1, D), Lp, Ls, True)

    return (xp.reshape(Bp, Lp, D), xs.reshape(Bs, Ls, D), new_state,
            k1.reshape(Bp, SWA_KVH, SWA_HD, Lp).transpose(0, 3, 1, 2),
            v1.reshape(Bp, SWA_KVH, SWA_HD, Lp).transpose(0, 3, 1, 2),
            ckv3.reshape(Bp, Lp, KV_LORA), kr3.transpose(0, 2, 1))
```
